```python
import math
import jax, jax.numpy as jnp
from jax import lax
import numpy as np

D_MODEL = 1024
BATCH = 8
SEQ = 2048
DEPTH = 1
DEC_BATCH = 128
DEC_SEQ = 4
PAST_LEN = 16384
PAGE_SIZE = 128

D_MIX = D_MODEL
SSD_WIDTH = D_MIX // 2
SSD_HEAD_DIM = 64
SSD_HEADS = SSD_WIDTH // SSD_HEAD_DIM
SSD_GROUPS = 2
HEADS_PER_GROUP = SSD_HEADS // SSD_GROUPS
D_STATE = 128
CONV_W = 4
CONV_DIM = SSD_WIDTH + 2 * SSD_GROUPS * D_STATE
SSD_CHUNK = 128
MLP_WIDTH = D_MIX - SSD_WIDTH
MLP_HEADS = 8
MLP_HEAD_DIM = MLP_WIDTH // MLP_HEADS
MLP_CHUNK = 128
IN_DIM = SSD_WIDTH + CONV_DIM + SSD_HEADS + 2 * MLP_WIDTH
N_EXPERTS = 32
TOP_K = 4
D_FF = D_MODEL
SWIGLU_LIMIT = 7.0
SWIGLU_ALPHA = 1.702
MOE_BLOCK = 128
PLE_DIM = 256
EPS = 1e-6

kernel_name = "hymba_ssd_chunkmlp_moe_step"


def _rms(x):
    xf = x.astype(jnp.float32)
    return (xf * lax.rsqrt(jnp.mean(xf * xf, axis=-1, keepdims=True) + EPS)).astype(x.dtype)


def rmsnorm(x, g):
    return _rms(x) * g


def layernorm(x, g, b):
    xf = x.astype(jnp.float32)
    mu = jnp.mean(xf, axis=-1, keepdims=True)
    var = jnp.mean(jnp.square(xf - mu), axis=-1, keepdims=True)
    return ((xf - mu) * lax.rsqrt(var + EPS)).astype(x.dtype) * g + b


def ssd_chunked(x, dt, A, B, C, h0, q):
    b, l, h, p = x.shape
    c = l // q
    B = jnp.repeat(B, HEADS_PER_GROUP, axis=2)
    C = jnp.repeat(C, HEADS_PER_GROUP, axis=2)
    xdt = (x * dt[..., None]).reshape(b, c, q, h, p)
    Bc = B.reshape(b, c, q, h, -1)
    Cc = C.reshape(b, c, q, h, -1)
    a = (dt * A).reshape(b, c, q, h).transpose(0, 3, 1, 2)
    a_cum = jnp.cumsum(a, axis=-1)
    seg = a_cum[..., :, None] - a_cum[..., None, :]
    causal = jnp.tril(jnp.ones((q, q), dtype=bool))
    Lmat = jnp.exp(jnp.where(causal, seg, -jnp.inf))
    scores = jnp.einsum('bcihn,bcjhn->bhcij', Cc, Bc) * Lmat
    y_diag = jnp.einsum('bhcij,bcjhp->bcihp', scores, xdt)
    decay_states = jnp.exp(a_cum[..., -1:] - a_cum)
    states = jnp.einsum('bcjhn,bhcj,bcjhp->bchpn', Bc, decay_states, xdt)
    chunk_decay = jnp.exp(a_cum[..., -1])

    def step(h_prev, inp):
        st, dec = inp
        return h_prev * dec[..., None, None] + st, h_prev

    h_final, h_prev = lax.scan(step, h0, (states.transpose(1, 0, 2, 3, 4),
                                          chunk_decay.transpose(2, 0, 1)))
    y_off = jnp.einsum('bcihn,cbhpn,bhci->bcihp', Cc, h_prev, jnp.exp(a_cum))
    return (y_diag + y_off).reshape(b, l, h, p), h_final


def chunk_mix(v, w_s, b_s):
    bsz, l, _ = v.shape
    q = min(MLP_CHUNK, l)
    c = l // q
    vh = v.reshape(bsz, c, q, MLP_HEADS, MLP_HEAD_DIM)
    w = jnp.tril(w_s[:, :q, :q])
    s = jnp.einsum('hij,bcjhd->bcihd', w, vh) + b_s[:, :q].T[None, None, :, :, None]
    return s.reshape(bsz, l, MLP_WIDTH)


def token_mixers(a, conv_prev, ssm_prev, prm):
    bsz, l, _ = a.shape
    proj = a @ prm['w_in']
    o1 = SSD_WIDTH
    o2 = o1 + CONV_DIM
    o3 = o2 + SSD_HEADS
    o4 = o3 + MLP_WIDTH
    z, xbc, dt_raw, u, v = jnp.split(proj, [o1, o2, o3, o4], axis=-1)
    xpad = jnp.concatenate([conv_prev.astype(xbc.dtype), xbc], axis=1)
    conv = prm['conv_b'] + sum(prm['conv_w'][k] * xpad[:, k:k + l] for k in range(CONV_W))
    new_conv = xpad[:, l:]
    xbc = jax.nn.silu(conv)
    xs, bm, cm = jnp.split(xbc, [SSD_WIDTH, SSD_WIDTH + SSD_GROUPS * D_STATE], axis=-1)
    xs = xs.reshape(bsz, l, SSD_HEADS, SSD_HEAD_DIM).astype(jnp.float32)
    bm = bm.reshape(bsz, l, SSD_GROUPS, D_STATE).astype(jnp.float32)
    cm = cm.reshape(bsz, l, SSD_GROUPS, D_STATE).astype(jnp.float32)
    dt = jax.nn.softplus(dt_raw.astype(jnp.float32) + prm['dt_bias'].astype(jnp.float32))
    A = -jnp.exp(prm['a_log'].astype(jnp.float32))
    y, ssm_new = ssd_chunked(xs, dt, A, bm, cm, ssm_prev.astype(jnp.float32), min(SSD_CHUNK, l))
    y = y + prm['d_skip'].astype(jnp.float32)[:, None] * xs
    y = y.reshape(bsz, l, SSD_WIDTH).astype(a.dtype) * jax.nn.silu(z)
    y = _rms(y.reshape(bsz, l, SSD_GROUPS, SSD_WIDTH // SSD_GROUPS)).reshape(bsz, l, SSD_WIDTH) * prm['ssd_norm_g']
    u = jax.nn.gelu(u)
    v = layernorm(jax.nn.gelu(v), prm['v_norm_g'], prm['v_norm_b'])
    m = rmsnorm(u * chunk_mix(v, prm['w_spatial'], prm['b_spatial']), prm['mlp_out_g'])
    out = jnp.concatenate([y, m], axis=-1) @ prm['w_out']
    return out, new_conv, ssm_new.astype(ssm_prev.dtype), v


def moe(x, w_router, b_router, w_up, b_up, w_down, b_down):
    T, D = x.shape
    logits = x.astype(jnp.float32) @ w_router.astype(jnp.float32) + b_router.astype(jnp.float32)
    top_v, top_i = lax.top_k(logits, TOP_K)
    gates = jax.nn.softmax(top_v, axis=-1).astype(x.dtype)
    TK = T * TOP_K
    flat_e = top_i.reshape(TK)
    order = jnp.argsort(flat_e)
    sorted_e = flat_e[order]
    sorted_tok = (order // TOP_K).astype(jnp.int32)
    sorted_gate = gates.reshape(TK)[order]
    counts = jnp.bincount(flat_e, length=N_EXPERTS)
    start = jnp.cumsum(counts) - counts
    padded = (counts + MOE_BLOCK - 1) // MOE_BLOCK * MOE_BLOCK
    pend = jnp.cumsum(padded)
    pstart = pend - padded
    dest = pstart[sorted_e] + jnp.arange(TK) - start[sorted_e]
    n_blocks = -(-TK // MOE_BLOCK) + N_EXPERTS
    tok_buf = jnp.full((n_blocks * MOE_BLOCK,), T, jnp.int32).at[dest].set(sorted_tok)
    gate_buf = jnp.zeros((n_blocks * MOE_BLOCK,), x.dtype).at[dest].set(sorted_gate)
    block_e = jnp.clip(jnp.searchsorted(pend, jnp.arange(n_blocks) * MOE_BLOCK, side='right'), 0, N_EXPERTS - 1)
    x_pad = jnp.concatenate([x, jnp.zeros((1, D), x.dtype)], axis=0)

    def run_block(args):
        tok, g, e = args
        h = x_pad[tok] @ w_up[e] + b_up[e]
        gate = jnp.minimum(h[:, ::2], SWIGLU_LIMIT)
        lin = jnp.clip(h[:, 1::2], -SWIGLU_LIMIT, SWIGLU_LIMIT)
        act = gate * jax.nn.sigmoid(SWIGLU_ALPHA * gate) * (lin + 1)
        return (act @ w_down[e] + b_down[e]) * g[:, None]

    y_blocks = lax.map(run_block, (tok_buf.reshape(n_blocks, MOE_BLOCK),
                                   gate_buf.reshape(n_blocks, MOE_BLOCK), block_e))
    out = jnp.zeros((T + 1, D), x.dtype).at[tok_buf].add(y_blocks.reshape(-1, D))
    return out[:T]


def decoder_layer(h, p, conv_prev, ssm_prev, prm):
    mix, new_conv, new_ssm, v_rows = token_mixers(rmsnorm(h, prm['norm_mix_g']), conv_prev, ssm_prev, prm)
    h = h + mix
    m = rmsnorm(h, prm['norm_moe_g'])
    h = h + moe(m.reshape(-1, D_MODEL), prm['w_router'], prm['b_router'], prm['w_up'], prm['b_up'],
                prm['w_down'], prm['b_down']).reshape(h.shape)
    gate = jax.nn.sigmoid(rmsnorm(h, prm['norm_ple_g']) @ prm['w_ple_gate'])
    h = h + (p @ prm['w_ple_proj']) * gate
    return h, new_conv, new_ssm, v_rows


def setup_inputs(seed: int = 0) -> dict:
    key = jax.random.key(seed)
    ks = jax.random.split(key, 32)
    f = jnp.float32
    nrm = lambda k, shape, s: jax.random.normal(k, shape, f) * s
    dt0 = jnp.exp(jax.random.uniform(ks[10], (DEPTH, SSD_HEADS), f) * (math.log(0.1) - math.log(0.001)) + math.log(0.001))
    return {
        "x_prompt": nrm(ks[0], (BATCH, SEQ, D_MODEL), 1.0),
        "x_sample": nrm(ks[1], (DEC_BATCH, DEC_SEQ, D_MODEL), 1.0),
        "state_ssm": nrm(ks[2], (DEPTH, DEC_BATCH, SSD_HEADS, SSD_HEAD_DIM, D_STATE), 0.5),
        "state_conv": nrm(ks[3], (DEPTH, DEC_BATCH, CONV_W - 1, CONV_DIM), 1.0),
        "p_prompt": nrm(ks[4], (DEPTH, BATCH, SEQ, PLE_DIM), 1.0),
        "p_sample": nrm(ks[5], (DEPTH, DEC_BATCH, DEC_SEQ, PLE_DIM), 1.0),
        "norm_mix_g": 1.0 + nrm(ks[6], (DEPTH, D_MODEL), 0.02),
        "w_in": nrm(ks[7], (DEPTH, D_MODEL, IN_DIM), D_MODEL ** -0.5),
        "conv_w": nrm(ks[8], (DEPTH, CONV_W, CONV_DIM), CONV_W ** -0.5),
        "conv_b": nrm(ks[9], (DEPTH, CONV_DIM), 0.02),
        "dt_bias": dt0 + jnp.log(-jnp.expm1(-dt0)),
        "a_log": jnp.log(jax.random.uniform(ks[11], (DEPTH, SSD_HEADS), f, 1.0, 16.0)),
        "d_skip": 1.0 + nrm(ks[12], (DEPTH, SSD_HEADS), 0.1),
        "ssd_norm_g": 1.0 + nrm(ks[13], (DEPTH, SSD_WIDTH), 0.02),
        "v_norm_g": 1.0 + nrm(ks[14], (DEPTH, MLP_WIDTH), 0.02),
        "v_norm_b": nrm(ks[15], (DEPTH, MLP_WIDTH), 0.02),
        "w_spatial": nrm(ks[16], (DEPTH, MLP_HEADS, MLP_CHUNK, MLP_CHUNK), MLP_CHUNK ** -0.5),
        "b_spatial": 1.0 + nrm(ks[17], (DEPTH, MLP_HEADS, MLP_CHUNK), 0.02),
        "mlp_out_g": 1.0 + nrm(ks[18], (DEPTH, MLP_WIDTH), 0.02),
        "w_out": nrm(ks[19], (DEPTH, D_MIX, D_MODEL), D_MIX ** -0.5),
        "norm_moe_g": 1.0 + nrm(ks[20], (DEPTH, D_MODEL), 0.02),
        "w_router": nrm(ks[21], (DEPTH, D_MODEL, N_EXPERTS), D_MODEL ** -0.5),
        "b_router": nrm(ks[22], (DEPTH, N_EXPERTS), 0.01),
        "w_up": nrm(ks[23], (DEPTH, N_EXPERTS, D_MODEL, 2 * D_FF), D_MODEL ** -0.5),
        "b_up": nrm(ks[24], (DEPTH, N_EXPERTS, 2 * D_FF), 0.02),
        "w_down": nrm(ks[25], (DEPTH, N_EXPERTS, D_FF, D_MODEL), D_FF ** -0.5),
        "b_down": nrm(ks[26], (DEPTH, N_EXPERTS, D_MODEL), 0.02),
        "norm_ple_g": 1.0 + nrm(ks[27], (DEPTH, D_MODEL), 0.02),
        "w_ple_gate": nrm(ks[28], (DEPTH, D_MODEL, D_MODEL), D_MODEL ** -0.5),
        "w_ple_proj": nrm(ks[29], (DEPTH, PLE_DIM, D_MODEL), PLE_DIM ** -0.5),
        "norm_final_g": 1.0 + nrm(ks[30], (D_MODEL,), 0.02),
    }


def reference(x_prompt, x_sample, state_ssm, state_conv, p_prompt, p_sample,
              norm_mix_g, w_in, conv_w, conv_b, dt_bias, a_log, d_skip, ssd_norm_g,
              v_norm_g, v_norm_b, w_spatial, b_spatial, mlp_out_g, w_out,
              norm_moe_g, w_router, b_router, w_up, b_up, w_down, b_down,
              norm_ple_g, w_ple_gate, w_ple_proj, norm_final_g):
    hp, hs = x_prompt, x_sample
    bp = x_prompt.shape[0]
    ssm_p, conv_p, ssm_s, conv_s, v_s = [], [], [], [], []
    for i in range(DEPTH):
        prm = {
            'norm_mix_g': norm_mix_g[i], 'w_in': w_in[i], 'conv_w': conv_w[i], 'conv_b': conv_b[i],
            'dt_bias': dt_bias[i], 'a_log': a_log[i], 'd_skip': d_skip[i], 'ssd_norm_g': ssd_norm_g[i],
            'v_norm_g': v_norm_g[i], 'v_norm_b': v_norm_b[i], 'w_spatial': w_spatial[i],
            'b_spatial': b_spatial[i], 'mlp_out_g': mlp_out_g[i], 'w_out': w_out[i],
            'norm_moe_g': norm_moe_g[i], 'w_router': w_router[i], 'b_router': b_router[i],
            'w_up': w_up[i], 'b_up': b_up[i], 'w_down': w_down[i], 'b_down': b_down[i],
            'norm_ple_g': norm_ple_g[i], 'w_ple_gate': w_ple_gate[i], 'w_ple_proj': w_ple_proj[i],
        }
        conv0 = jnp.zeros((bp, CONV_W - 1, CONV_DIM), x_prompt.dtype)
        ssm0 = jnp.zeros((bp, SSD_HEADS, SSD_HEAD_DIM, D_STATE), state_ssm.dtype)
        hp, cp, sp, _ = decoder_layer(hp, p_prompt[i], conv0, ssm0, prm)
        hs, cs, ss, vs = decoder_layer(hs, p_sample[i], state_conv[i], state_ssm[i], prm)
        ssm_p.append(sp)
        conv_p.append(cp)
        ssm_s.append(ss)
        conv_s.append(cs)
        v_s.append(vs)
    y_prompt = rmsnorm(hp, norm_final_g)
    y_sample = rmsnorm(hs, norm_final_g)
    new_ssm_prompt = jnp.stack(ssm_p)
    new_conv_prompt = jnp.stack(conv_p)
    new_ssm_sample = jnp.stack(ssm_s)
    new_conv_sample = jnp.stack(conv_s)
    new_chunk_v_sample = jnp.stack(v_s)
    return (y_prompt, y_sample, new_ssm_prompt, new_conv_prompt, new_ssm_sample, new_conv_sample, new_chunk_v_sample)
```

```python
import functools

import jax
import jax.numpy as jnp
from jax import lax
from jax.experimental import pallas as pl
from jax.experimental.pallas import tpu as pltpu

F32 = jnp.float32
BF16 = jnp.bfloat16
I32 = jnp.int32

EPS = 1e-6
D_MODEL = 1024
SSD_WIDTH = 512
SSD_HEADS = 8
HEAD_DIM = 64
SSD_GROUPS = 2
D_STATE = 128
CONV_W = 4
CONV_DIM = SSD_WIDTH + 2 * SSD_GROUPS * D_STATE
MLP_WIDTH = 512
MLP_HEADS = 8
N_EXPERTS = 32
TOP_K = 4
D_FF = 1024
SWIGLU_LIMIT = 7.0
SWIGLU_ALPHA = 1.702
LANES = 128
CHUNK = 128
DT_PAD = LANES
IN_PAD = SSD_WIDTH + CONV_DIM + 2 * MLP_WIDTH + DT_PAD
VMEM_LIMIT = 56 * 1024 * 1024


def _cparams(sem):
    return pltpu.CompilerParams(dimension_semantics=sem, vmem_limit_bytes=VMEM_LIMIT)


def _const_spec(shape):
    return pl.BlockSpec(shape, lambda *_: (0,) * len(shape))


def _rms(x):
    return x * lax.rsqrt(jnp.mean(x * x, axis=-1, keepdims=True) + EPS)


def _dot(a, b):
    return jnp.dot(a, b, preferred_element_type=F32)


def _dot_nt(a, b):
    return lax.dot_general(a, b, (((1,), (1,)), ((), ())), preferred_element_type=F32)


def _split3(x):
    hi = x.astype(BF16)
    r = x - hi.astype(F32)
    mid = r.astype(BF16)
    lo = (r - mid.astype(F32)).astype(BF16)
    return hi, mid, lo


def _sel_right(x, m01):
    hi, mid, lo = _split3(x)
    return _dot(hi, m01) + _dot(mid, m01) + _dot(lo, m01)


def _sel_left(m01, x):
    hi, mid, lo = _split3(x)
    return _dot(m01, hi) + _dot(m01, mid) + _dot(m01, lo)


def _softplus(x):
    return jnp.maximum(x, 0.0) + jnp.log1p(jnp.exp(-jnp.abs(x)))


def _inproj_call(xp, xs, g, w, tm):
    tp, ts = xp.shape[0], xs.shape[0]
    n_p, n_s = tp // tm, ts // tm
    t_all = tp + ts
    segs = ((0, 512), (512, 1536), (1536, 2048), (2048, 2560), (2560, IN_PAD))

    def body(xp_ref, xs_ref, g_ref, w_ref, *outs):
        def run(x_ref):
            xn = (_rms(x_ref[...]) * g_ref[...]).astype(BF16)
            for (a, b), o in zip(segs, outs):
                o[...] = _dot(xn, w_ref[:, a:b])

        i = pl.program_id(0)

        @pl.when(i < n_p)
        def _():
            run(xp_ref)

        @pl.when(i >= n_p)
        def _():
            run(xs_ref)

    widths = [b - a for a, b in segs]
    return pl.pallas_call(
        body,
        out_shape=[jax.ShapeDtypeStruct((t_all, wd), F32) for wd in widths],
        grid=(n_p + n_s,),
        in_specs=[
            pl.BlockSpec((tm, D_MODEL), lambda i: (jnp.minimum(i, n_p - 1), 0)),
            pl.BlockSpec((tm, D_MODEL), lambda i: (jnp.maximum(i - n_p, 0), 0)),
            _const_spec((1, D_MODEL)),
            _const_spec((D_MODEL, IN_PAD)),
        ],
        out_specs=[pl.BlockSpec((tm, wd), lambda i: (i, 0)) for wd in widths],
        compiler_params=_cparams(("arbitrary",)),
        name="in_proj",
    )(xp, xs, g, w)


def _mixer_front(conv, dtr, dtb, alog, alog_x, rexp, tril, seg_ones):
    xact = conv * jax.nn.sigmoid(conv)
    xs = xact[:, :SSD_WIDTH]
    bm = xact[:, SSD_WIDTH:SSD_WIDTH + 256]
    cm = xact[:, SSD_WIDTH + 256:]
    dt = _softplus(dtr + dtb)
    a = dt * (-jnp.exp(alog))
    dt_x = _sel_right(dt, rexp)
    a_x = dt_x * (-jnp.exp(alog_x))
    acum = _sel_left(tril, a)
    acum_x = _sel_left(tril, a_x)
    if seg_ones is None:
        r = acum_x.shape[0]
        tot_x = jnp.broadcast_to(acum_x[r - 1:r, :], acum_x.shape)
    else:
        tot_x = _sel_left(seg_ones, a_x)
    return xs, bm, cm, dt_x, acum, acum_x, tot_x


def _ssd_intra(cmb, bmb, acum, xdt, mask):
    r = acum.shape[0]
    acum_t = acum.T
    lane = lax.broadcasted_iota(I32, (r, LANES), 1)
    low = lane < HEAD_DIM
    outs = []
    for g in range(SSD_GROUPS):
        sg = _dot_nt(cmb[:, LANES * g:LANES * (g + 1)], bmb[:, LANES * g:LANES * (g + 1)])
        for k in (2 * g, 2 * g + 1):
            parts = []
            for h in (2 * k, 2 * k + 1):
                seg = acum[:, h:h + 1] - acum_t[h:h + 1, :]
                parts.append((sg * jnp.exp(jnp.where(mask, seg, -jnp.inf))).astype(BF16))
            lhs = jnp.concatenate(parts, axis=1)
            xd = xdt[:, LANES * k:LANES * (k + 1)]
            rhs = jnp.concatenate([jnp.where(low, xd, 0.0), jnp.where(low, 0.0, xd)], axis=0).astype(BF16)
            outs.append(_dot(lhs, rhs))
    return jnp.concatenate(outs, axis=1)


def _mixer_back(y, z, u, v, sng, vng, vnb, wsp_ref, bsp, mog):
    r = y.shape[0]
    yg = y * (z * jax.nn.sigmoid(z))
    halves = []
    for g in range(SSD_GROUPS):
        t = yg[:, 256 * g:256 * (g + 1)]
        halves.append(_rms(t))
    yn = jnp.concatenate(halves, axis=1) * sng
    ug = jax.nn.gelu(u)
    vg = jax.nn.gelu(v)
    mu = jnp.mean(vg, axis=-1, keepdims=True)
    var = jnp.mean(jnp.square(vg - mu), axis=-1, keepdims=True)
    v_ln = (vg - mu) * lax.rsqrt(var + EPS) * vng + vnb
    lane = lax.broadcasted_iota(I32, (r, LANES), 1)
    low = lane < HEAD_DIM
    outs = []
    for k in range(MLP_HEADS // 2):
        vd = v_ln[:, LANES * k:LANES * (k + 1)]
        rhs = jnp.concatenate([jnp.where(low, vd, 0.0), jnp.where(low, 0.0, vd)], axis=0).astype(BF16)
        outs.append(_dot(wsp_ref[k], rhs))
    s = jnp.concatenate(outs, axis=1) + bsp
    m = _rms(ug * s) * mog
    return jnp.concatenate([yn, m], axis=1).astype(BF16), v_ln


_MIXER_PARAM_SHAPES = (
    (CONV_W, CONV_DIM), (1, CONV_DIM), (1, LANES), (1, LANES), (1, SSD_WIDTH), (LANES, SSD_WIDTH),
    (CHUNK, CHUNK), (1, SSD_WIDTH), (1, SSD_WIDTH), (1, MLP_WIDTH), (1, MLP_WIDTH),
    (MLP_HEADS // 2, CHUNK, 2 * CHUNK), (CHUNK, MLP_WIDTH), (1, MLP_WIDTH),
)


def _prompt_mixer_body(z_ref, xbc_ref, u_ref, v_ref, dt_ref,
                       cw_ref, cb_ref, dtb_ref, alog_ref, alogx_ref, rexp_ref, tril_ref, dskip_ref,
                       sng_ref, vng_ref, vnb_ref, wsp_ref, bsp_ref, mog_ref,
                       cat_ref, ssm_ref, ext_scr, s_scr):
    c = pl.program_id(1)
    r = CHUNK

    @pl.when(c == 0)
    def _():
        ext_scr[0:8, :] = jnp.zeros((8, CONV_DIM), F32)
        s_scr[...] = jnp.zeros_like(s_scr)

    x = xbc_ref[...]
    ext_scr[8:8 + r, :] = x
    cw = cw_ref[...]
    conv = (cb_ref[...] + cw[3:4] * x + cw[2:3] * ext_scr[7:7 + r, :]
            + cw[1:2] * ext_scr[6:6 + r, :] + cw[0:1] * ext_scr[5:5 + r, :])
    ext_scr[0:8, :] = x[r - 8:r, :]

    xs, bm, cm, dt_x, acum, acum_x, tot_x = _mixer_front(
        conv, dt_ref[...], dtb_ref[...], alog_ref[...], alogx_ref[...], rexp_ref[...], tril_ref[...], None)
    bmb, cmb = bm.astype(BF16), cm.astype(BF16)
    xdt = xs * dt_x
    row = lax.broadcasted_iota(I32, (r, r), 0)
    col = lax.broadcasted_iota(I32, (r, r), 1)
    y_diag = _ssd_intra(cmb, bmb, acum, xdt, row >= col)

    s_prev = s_scr[...]
    s_prev_b = s_prev.astype(BF16)
    y_off = jnp.concatenate(
        [_dot_nt(cmb[:, LANES * g:LANES * (g + 1)], s_prev_b[256 * g:256 * (g + 1), :]) for g in range(SSD_GROUPS)],
        axis=1)
    y = y_diag + y_off * jnp.exp(acum_x) + dskip_ref[...] * xs

    w_t = (xdt * jnp.exp(tot_x - acum_x)).T.astype(BF16)
    states = jnp.concatenate(
        [_dot(w_t[256 * g:256 * (g + 1), :], bmb[:, LANES * g:LANES * (g + 1)]) for g in range(SSD_GROUPS)], axis=0)
    s_new = s_prev * jnp.exp(tot_x).T + states
    s_scr[...] = s_new

    cat, _ = _mixer_back(y, z_ref[...], u_ref[...], v_ref[...], sng_ref[...], vng_ref[...], vnb_ref[...],
                         wsp_ref, bsp_ref[...], mog_ref[...])
    cat_ref[...] = cat

    @pl.when(c == pl.num_programs(1) - 1)
    def _():
        ssm_ref[0] = s_new


def _prompt_mixer_call(z, xbc, u, v, dtr, params, nb, nc):
    row = lambda b, c: (b * nc + c, 0)
    in_specs = [
        pl.BlockSpec((CHUNK, SSD_WIDTH), row), pl.BlockSpec((CHUNK, CONV_DIM), row),
        pl.BlockSpec((CHUNK, MLP_WIDTH), row), pl.BlockSpec((CHUNK, MLP_WIDTH), row),
        pl.BlockSpec((CHUNK, DT_PAD), row),
    ] + [_const_spec(s) for s in _MIXER_PARAM_SHAPES]
    return pl.pallas_call(
        _prompt_mixer_body,
        out_shape=[jax.ShapeDtypeStruct((nb * nc * CHUNK, D_MODEL), BF16),
                   jax.ShapeDtypeStruct((nb, SSD_WIDTH, D_STATE), F32)],
        grid=(nb, nc),
        in_specs=in_specs,
        out_specs=[pl.BlockSpec((CHUNK, D_MODEL), row),
                   pl.BlockSpec((1, SSD_WIDTH, D_STATE), lambda b, c: (b, 0, 0))],
        scratch_shapes=[pltpu.VMEM((CHUNK + 8, CONV_DIM), F32), pltpu.VMEM((SSD_WIDTH, D_STATE), F32)],
        compiler_params=_cparams(("arbitrary", "arbitrary")),
        name="prompt_mixer",
    )(z, xbc, u, v, dtr, *params)


def _sample_mixer_body(seq_len, z_ref, x0_ref, x1_ref, x2_ref, x3_ref, u_ref, v_ref, dt_ref, h_ref,
                       cw_ref, cb_ref, dtb_ref, alog_ref, alogx_ref, rexp_ref, tril_ref, dskip_ref,
                       sng_ref, vng_ref, vnb_ref, wsp_ref, bsp_ref, mog_ref, segones_ref,
                       cat_ref, vout_ref, hout_ref, cm_scr, bm_scr, wt_scr, dtt_scr, yoff_scr):
    r = CHUNK
    shift = seq_len.bit_length() - 1
    cw = cw_ref[...]
    conv = (cb_ref[...] + cw[3:4] * x0_ref[...] + cw[2:3] * x1_ref[...]
            + cw[1:2] * x2_ref[...] + cw[0:1] * x3_ref[...])
    xs, bm, cm, dt_x, acum, acum_x, tot_x = _mixer_front(
        conv, dt_ref[...], dtb_ref[...], alog_ref[...], alogx_ref[...], rexp_ref[...], tril_ref[...],
        segones_ref[...])
    bmb, cmb = bm.astype(BF16), cm.astype(BF16)
    xdt = xs * dt_x
    row = lax.broadcasted_iota(I32, (r, r), 0)
    col = lax.broadcasted_iota(I32, (r, r), 1)
    same = lax.shift_right_logical(row, shift) == lax.shift_right_logical(col, shift)
    y_diag = _ssd_intra(cmb, bmb, acum, xdt, same & (row >= col))

    cm_scr[...] = cm
    bm_scr[...] = bmb
    wt_scr[...] = (xdt * jnp.exp(tot_x - acum_x)).T
    dtt_scr[...] = jnp.exp(tot_x).T
    ones_b = jnp.ones((LANES, LANES), BF16)
    seqs_per_slab = 8 // seq_len

    def slab(j, carry):
        rows = pl.ds(pl.multiple_of(8 * j, 8), 8)
        cms = cm_scr[rows, :].astype(BF16)
        sub = lax.broadcasted_iota(I32, (8, 256), 0)
        lane = lax.broadcasted_iota(I32, (256, LANES), 1)
        for g in range(SSD_GROUPS):
            q_rows = slice(256 * g, 256 * (g + 1))
            acc = jnp.zeros((8, 256), F32)
            for q in range(seqs_per_slab):
                s = seqs_per_slab * j + q
                y_s = _dot_nt(cms[:, LANES * g:LANES * (g + 1)], h_ref[s, q_rows, :].astype(BF16))
                acc = jnp.where(lax.shift_right_logical(sub, shift) == q, y_s, acc)
            yoff_scr[rows, 256 * g:256 * (g + 1)] = acc
            for q in range(seqs_per_slab):
                s = seqs_per_slab * j + q
                w_sel = jnp.where(lax.shift_right_logical(lane, shift) == s, wt_scr[q_rows, :], 0.0).astype(BF16)
                st = _dot(w_sel, bm_scr[:, LANES * g:LANES * (g + 1)])
                d_sel = jnp.where(lane == s * seq_len, dtt_scr[q_rows, :], 0.0)
                hout_ref[s, q_rows, :] = h_ref[s, q_rows, :] * _sel_right(d_sel, ones_b) + st
        return carry

    lax.fori_loop(0, r // 8, slab, 0)

    y = y_diag + yoff_scr[...] * jnp.exp(acum_x) + dskip_ref[...] * xs
    cat, v_ln = _mixer_back(y, z_ref[...], u_ref[...], v_ref[...], sng_ref[...], vng_ref[...], vnb_ref[...],
                            wsp_ref, bsp_ref[...], mog_ref[...])
    cat_ref[...] = cat
    vout_ref[...] = v_ln


def _sample_mixer_call(z, x_shift, u, v, dtr, h0, params, seg_ones, row0, seq_len):
    ts = x_shift[0].shape[0]
    n = ts // CHUNK
    spt = CHUNK // seq_len
    off = lambda i: (row0 + i, 0)
    loc = lambda i: (i, 0)
    st3 = lambda i: (i, 0, 0)
    in_specs = (
        [pl.BlockSpec((CHUNK, SSD_WIDTH), off)]
        + [pl.BlockSpec((CHUNK, CONV_DIM), loc)] * 4
        + [pl.BlockSpec((CHUNK, MLP_WIDTH), off), pl.BlockSpec((CHUNK, MLP_WIDTH), off),
           pl.BlockSpec((CHUNK, DT_PAD), off), pl.BlockSpec((spt, SSD_WIDTH, D_STATE), st3)]
        + [_const_spec(s) for s in _MIXER_PARAM_SHAPES] + [_const_spec((CHUNK, CHUNK))])
    return pl.pallas_call(
        functools.partial(_sample_mixer_body, seq_len),
        out_shape=[jax.ShapeDtypeStruct((ts, D_MODEL), BF16), jax.ShapeDtypeStruct((ts, MLP_WIDTH), F32),
                   jax.ShapeDtypeStruct(h0.shape, F32)],
        grid=(n,),
        in_specs=in_specs,
        out_specs=[pl.BlockSpec((CHUNK, D_MODEL), loc), pl.BlockSpec((CHUNK, MLP_WIDTH), loc),
                   pl.BlockSpec((spt, SSD_WIDTH, D_STATE), st3)],
        scratch_shapes=[pltpu.VMEM((CHUNK, 256), F32), pltpu.VMEM((CHUNK, 256), BF16),
                        pltpu.VMEM((SSD_WIDTH, CHUNK), F32), pltpu.VMEM((SSD_WIDTH, CHUNK), F32),
                        pltpu.VMEM((CHUNK, SSD_WIDTH), F32)],
        compiler_params=_cparams(("arbitrary",)),
        name="sample_mixer",
    )(z, *x_shift, u, v, dtr, h0, *params, seg_ones)


def _out_router_call(cat_p, cat_s, xp, xs, w_out, g_moe, wr_hi, wr_lo, b_r, tm):
    tp, ts = xp.shape[0], xs.shape[0]
    n_p, n_s = tp // tm, ts // tm
    t_all = tp + ts

    def body(cp_ref, cs_ref, xp_ref, xs_ref, wo_ref, g_ref, wh_ref, wl_ref, br_ref,
             h1_ref, m_ref, eid_ref, gate_ref):
        def run(c_ref, x_ref):
            h1 = x_ref[...] + _dot(c_ref[...], wo_ref[...])
            h1_ref[...] = h1
            m = _rms(h1) * g_ref[...]
            m_ref[...] = m
            m_hi = m.astype(BF16)
            m_lo = (m - m_hi.astype(F32)).astype(BF16)
            logits = _dot(m_hi, wh_ref[...]) + _dot(m_lo, wh_ref[...]) + _dot(m_hi, wl_ref[...]) + br_ref[...]
            lane = lax.broadcasted_iota(I32, logits.shape, 1).astype(F32)
            work = logits
            vals, ids = [], []
            for _ in range(TOP_K):
                mx = jnp.max(work, axis=-1, keepdims=True)
                idx = jnp.min(jnp.where(work == mx, lane, float(LANES)), axis=-1, keepdims=True)
                vals.append(mx)
                ids.append(idx)
                work = jnp.where(lane == idx, -jnp.inf, work)
            ex = [jnp.exp(vv - vals[0]) for vv in vals]
            den = ex[0] + ex[1] + ex[2] + ex[3]
            eid = jnp.zeros(logits.shape, I32)
            gate = jnp.zeros(logits.shape, F32)
            for k in range(TOP_K):
                eid = jnp.where(lane == k, ids[k].astype(I32), eid)
                gate = jnp.where(lane == k, ex[k] / den, gate)
            eid_ref[...] = eid
            gate_ref[...] = gate

        i = pl.program_id(0)

        @pl.when(i < n_p)
        def _():
            run(cp_ref, xp_ref)

        @pl.when(i >= n_p)
        def _():
            run(cs_ref, xs_ref)

    pmap = lambda i: (jnp.minimum(i, n_p - 1), 0)
    smap = lambda i: (jnp.maximum(i - n_p, 0), 0)
    omap = lambda i: (i, 0)
    return pl.pallas_call(
        body,
        out_shape=[jax.ShapeDtypeStruct((t_all, D_MODEL), F32), jax.ShapeDtypeStruct((t_all, D_MODEL), F32),
                   jax.ShapeDtypeStruct((t_all, LANES), I32), jax.ShapeDtypeStruct((t_all, LANES), F32)],
        grid=(n_p + n_s,),
        in_specs=[pl.BlockSpec((tm, D_MODEL), pmap), pl.BlockSpec((tm, D_MODEL), smap),
                  pl.BlockSpec((tm, D_MODEL), pmap), pl.BlockSpec((tm, D_MODEL), smap),
                  _const_spec((D_MODEL, D_MODEL)), _const_spec((1, D_MODEL)),
                  _const_spec((D_MODEL, LANES)), _const_spec((D_MODEL, LANES)), _const_spec((1, LANES))],
        out_specs=[pl.BlockSpec((tm, D_MODEL), omap), pl.BlockSpec((tm, D_MODEL), omap),
                   pl.BlockSpec((tm, LANES), omap), pl.BlockSpec((tm, LANES), omap)],
        compiler_params=_cparams(("arbitrary",)),
        name="out_router",
    )(cat_p, cat_s, xp, xs, w_out, g_moe, wr_hi, wr_lo, b_r)


def _route(eid, tm, nb):
    t = eid.shape[0]
    tk = t * TOP_K
    flat = eid.reshape(tk)
    order = jnp.argsort(flat, stable=True).astype(I32)
    counts = jnp.sum((flat[:, None] == jnp.arange(N_EXPERTS, dtype=I32)[None, :]).astype(I32), axis=0)
    nblk = (counts + tm - 1) // tm
    bend = jnp.cumsum(nblk)
    bstart = bend - nblk
    start = jnp.cumsum(counts) - counts
    nused = bend[-1]
    blk = jnp.arange(nb, dtype=I32)
    be = jnp.minimum(jnp.searchsorted(bend, blk, side="right").astype(I32), N_EXPERTS - 1)
    be = jnp.where(blk < nused, be, be[jnp.maximum(nused - 1, 0)])
    nval = jnp.where(blk < nused, jnp.clip(counts[be] - (blk - bstart[be]) * tm, 0, tm), 0).astype(I32)
    slot = jnp.arange(nb * tm, dtype=I32)
    sb = slot // tm
    j = slot - sb * tm
    src = jnp.clip(start[be[sb]] + (sb - bstart[be[sb]]) * tm + j, 0, tk - 1)
    valid = j < nval[sb]
    pair = jnp.where(valid, order[src], 0)
    tok = pair // TOP_K
    return be, nval, tok.reshape(nb, 1, tm), pair.reshape(nb, 1, tm)


def _moe_body(tm, be_ref, nval_ref, tok_ref, pair_ref, m_hbm, wup_ref, bup_ref, wdn_ref, bdn_ref, perm_ref,
              y_hbm, xbuf, ybuf, wup_b, wdn_b, gsem, ssem):
    i = pl.program_id(0)
    nv = nval_ref[i]

    @pl.when(i == 0)
    def _():
        xbuf[...] = jnp.zeros_like(xbuf)

    def gather(r):
        return pltpu.make_async_copy(m_hbm.at[pl.ds(tok_ref[0, 0, r], 1)], xbuf.at[pl.ds(r, 1)], gsem)

    def scatter(r):
        return pltpu.make_async_copy(ybuf.at[pl.ds(r, 1)], y_hbm.at[pl.ds(pair_ref[0, 0, r], 1)], ssem)

    @pl.when(nv > 0)
    def _():
        def g_start(r, c):
            gather(r).start()
            return c

        lax.fori_loop(0, nv, g_start, 0)

        prev = be_ref[jnp.maximum(i - 1, 0)]

        @pl.when((i == 0) | (be_ref[i] != prev))
        def _():
            for jb in range(2 * D_FF // 256):
                cols = slice(256 * jb, 256 * (jb + 1))
                wup_b[:, cols] = _dot(wup_ref[0, :, cols].astype(BF16), perm_ref[...]).astype(BF16)
            wdn_b[...] = wdn_ref[0].astype(BF16)

        def g_wait(r, c):
            gather(r).wait()
            return c

        lax.fori_loop(0, nv, g_wait, 0)

        x = xbuf[...].astype(BF16)
        acts = []
        for jb in range(D_FF // LANES):
            h = _dot(x, wup_b[:, 256 * jb:256 * (jb + 1)]) + bup_ref[0, :, 256 * jb:256 * (jb + 1)]
            gate = jnp.minimum(h[:, :LANES], SWIGLU_LIMIT)
            lin = jnp.clip(h[:, LANES:], -SWIGLU_LIMIT, SWIGLU_LIMIT)
            acts.append((gate * jax.nn.sigmoid(SWIGLU_ALPHA * gate) * (lin + 1.0)).astype(BF16))
        act = jnp.concatenate(acts, axis=1)
        ybuf[...] = _dot(act, wdn_b[...]) + bdn_ref[0]

        def s_start(r, c):
            scatter(r).start()
            return c

        lax.fori_loop(0, nv, s_start, 0)

        def s_wait(r, c):
            scatter(r).wait()
            return c

        lax.fori_loop(0, nv, s_wait, 0)


def _moe_call(m, be, nval, tok, pair, w_up, b_up_g, w_down, b_down, perm, tm, nb):
    t = m.shape[0]
    grid_spec = pltpu.PrefetchScalarGridSpec(
        num_scalar_prefetch=2,
        grid=(nb,),
        in_specs=[
            pl.BlockSpec((1, 1, tm), lambda i, be, nv: (i, 0, 0), memory_space=pltpu.SMEM),
            pl.BlockSpec((1, 1, tm), lambda i, be, nv: (i, 0, 0), memory_space=pltpu.SMEM),
            pl.BlockSpec(memory_space=pl.ANY),
            pl.BlockSpec((1, D_MODEL, 2 * D_FF), lambda i, be, nv: (be[i], 0, 0)),
            pl.BlockSpec((1, 1, 2 * D_FF), lambda i, be, nv: (be[i], 0, 0)),
            pl.BlockSpec((1, D_FF, D_MODEL), lambda i, be, nv: (be[i], 0, 0)),
            pl.BlockSpec((1, 1, D_MODEL), lambda i, be, nv: (be[i], 0, 0)),
            pl.BlockSpec((256, 256), lambda i, be, nv: (0, 0)),
        ],
        out_specs=pl.BlockSpec(memory_space=pl.ANY),
        scratch_shapes=[pltpu.VMEM((tm, D_MODEL), F32), pltpu.VMEM((tm, D_MODEL), F32),
                        pltpu.VMEM((D_MODEL, 2 * D_FF), BF16), pltpu.VMEM((D_FF, D_MODEL), BF16),
                        pltpu.SemaphoreType.DMA, pltpu.SemaphoreType.DMA],
    )
    return pl.pallas_call(
        functools.partial(_moe_body, tm),
        out_shape=jax.ShapeDtypeStruct((t * TOP_K, D_MODEL), F32),
        grid_spec=grid_spec,
        compiler_params=_cparams(("arbitrary",)),
        name="moe_experts",
    )(be, nval, tok, pair, m, w_up, b_up_g, w_down, b_down, perm)


def _ple_call(h1, y4, gates, pp, ps, g_ple, w_gate, w_proj, g_final, tm):
    tp, ts = pp.shape[0], ps.shape[0]
    n_p, n_s = tp // tm, ts // tm
    ple = pp.shape[1]

    def body(h1_ref, y4_ref, gt_ref, pp_ref, ps_ref, g_ref, wg_ref, wp_ref, gf_ref, yp_ref, ys_ref):
        def run(p_ref, o_ref):
            gt = gt_ref[...]
            moe = gt[:, 0:1] * y4_ref[:, 0:D_MODEL]
            for k in range(1, TOP_K):
                moe = moe + gt[:, k:k + 1] * y4_ref[:, D_MODEL * k:D_MODEL * (k + 1)]
            h2 = h1_ref[...] + moe
            a = (_rms(h2) * g_ref[...]).astype(BF16)
            gate = jax.nn.sigmoid(_dot(a, wg_ref[...]))
            pe = _dot(p_ref[...].astype(BF16), wp_ref[...])
            h3 = h2 + pe * gate
            o_ref[...] = _rms(h3) * gf_ref[...]

        i = pl.program_id(0)

        @pl.when(i < n_p)
        def _():
            run(pp_ref, yp_ref)

        @pl.when(i >= n_p)
        def _():
            run(ps_ref, ys_ref)

    pmap = lambda i: (jnp.minimum(i, n_p - 1), 0)
    smap = lambda i: (jnp.maximum(i - n_p, 0), 0)
    omap = lambda i: (i, 0)
    return pl.pallas_call(
        body,
        out_shape=[jax.ShapeDtypeStruct((tp, D_MODEL), F32), jax.ShapeDtypeStruct((ts, D_MODEL), F32)],
        grid=(n_p + n_s,),
        in_specs=[pl.BlockSpec((tm, D_MODEL), omap), pl.BlockSpec((tm, TOP_K * D_MODEL), omap),
                  pl.BlockSpec((tm, LANES), omap), pl.BlockSpec((tm, ple), pmap), pl.BlockSpec((tm, ple), smap),
                  _const_spec((1, D_MODEL)), _const_spec((D_MODEL, D_MODEL)), _const_spec((ple, D_MODEL)),
                  _const_spec((1, D_MODEL))],
        out_specs=[pl.BlockSpec((tm, D_MODEL), pmap), pl.BlockSpec((tm, D_MODEL), smap)],
        compiler_params=_cparams(("arbitrary",)),
        name="ple_final",
    )(h1, y4, gates, pp, ps, g_ple, w_gate, w_proj, g_final)


def _row(x, width=None):
    x = x.reshape(1, -1).astype(F32)
    if width is not None and x.shape[1] < width:
        x = jnp.pad(x, ((0, 0), (0, width - x.shape[1])))
    return x


def _mixer_params(conv_w, conv_b, dt_bias, a_log, d_skip, ssd_norm_g, v_norm_g, v_norm_b, w_spatial, b_spatial,
                  mlp_out_g, seq_len):
    n_seq = CHUNK // seq_len
    pos = jnp.arange(CHUNK) % seq_len
    same = (jnp.arange(CHUNK)[:, None] // seq_len) == (jnp.arange(CHUNK)[None, :] // seq_len)
    tril = (same & (jnp.arange(CHUNK)[:, None] >= jnp.arange(CHUNK)[None, :])).astype(BF16)
    rexp = (jnp.arange(LANES)[:, None] == (jnp.arange(SSD_WIDTH)[None, :] // HEAD_DIM)).astype(BF16)
    w_loc = jnp.tril(w_spatial[:, :seq_len, :seq_len])
    eye = jnp.eye(n_seq, dtype=F32)
    w_bd = jnp.einsum("st,hij->hsitj", eye, w_loc).reshape(MLP_HEADS, CHUNK, CHUNK)
    wsp = jnp.concatenate([w_bd[0::2], w_bd[1::2]], axis=2).astype(BF16)
    bsp = jnp.repeat(b_spatial[:, :seq_len].T[pos], MLP_WIDTH // MLP_HEADS, axis=1)
    params = (
        conv_w.astype(F32), _row(conv_b), _row(dt_bias, LANES), _row(a_log, LANES),
        _row(jnp.repeat(a_log, HEAD_DIM)), rexp, tril, _row(jnp.repeat(d_skip, HEAD_DIM)),
        _row(ssd_norm_g), _row(v_norm_g), _row(v_norm_b), wsp, bsp.astype(F32), _row(mlp_out_g),
    )
    return params, same.astype(BF16)


def _tile_rows(n):
    return 512 if n % 512 == 0 else CHUNK


def kernel(x_prompt, x_sample, state_ssm, state_conv, p_prompt, p_sample, norm_mix_g, w_in, conv_w, conv_b, dt_bias, a_log, d_skip, ssd_norm_g, v_norm_g, v_norm_b, w_spatial, b_spatial, mlp_out_g, w_out, norm_moe_g, w_router, b_router, w_up, b_up, w_down, b_down, norm_ple_g, w_ple_gate, w_ple_proj, norm_final_g):
    depth = norm_mix_g.shape[0]
    bp, lp, d = x_prompt.shape
    bs, ls, _ = x_sample.shape
    tp, ts = bp * lp, bs * ls
    assert depth == 1 and d == D_MODEL and lp % CHUNK == 0 and ts % CHUNK == 0 and 8 % ls == 0
    tm = _tile_rows(tp) if ts % _tile_rows(tp) == 0 else CHUNK
    t_all = tp + ts
    tm_moe = 256
    nb_moe = -(-t_all * TOP_K // tm_moe) + N_EXPERTS

    hp = x_prompt.reshape(tp, d)
    hs = x_sample.reshape(ts, d)
    ssm_p, conv_p, ssm_s, conv_s, v_s = [], [], [], [], []
    o1 = SSD_WIDTH
    o2 = o1 + CONV_DIM
    o3 = o2 + SSD_HEADS
    o4 = o3 + MLP_WIDTH
    c = jnp.arange(256)
    src = jnp.where(c < LANES, 2 * c, 2 * (c - LANES) + 1)
    perm = (jnp.arange(256)[:, None] == src[None, :]).astype(BF16)
    col = (jnp.arange(2 * D_FF) // 256) * 256 + src[jnp.arange(2 * D_FF) % 256]

    for i in range(depth):
        wi = w_in[i]
        w_cat = jnp.concatenate(
            [wi[:, :o2], wi[:, o3:], jnp.pad(wi[:, o2:o3], ((0, 0), (0, DT_PAD - SSD_HEADS)))], axis=1).astype(BF16)
        z, xbc, u, v, dtr = _inproj_call(hp, hs, _row(norm_mix_g[i]), w_cat, tm)

        mix_args = (conv_w[i], conv_b[i], dt_bias[i], a_log[i], d_skip[i], ssd_norm_g[i], v_norm_g[i], v_norm_b[i],
                    w_spatial[i], b_spatial[i], mlp_out_g[i])
        prm_p, _ = _mixer_params(*mix_args, seq_len=CHUNK)
        cat_p, s_p = _prompt_mixer_call(z, xbc, u, v, dtr, prm_p, bp, lp // CHUNK)
        ssm_p.append(s_p.reshape(bp, SSD_HEADS, HEAD_DIM, D_STATE).astype(state_ssm.dtype))
        conv_p.append(xbc[:tp].reshape(bp, lp, CONV_DIM)[:, lp - (CONV_W - 1):])

        prm_s, seg_ones = _mixer_params(*mix_args, seq_len=ls)
        xbc_s = xbc[tp:].reshape(bs, ls, CONV_DIM)
        xpad = jnp.concatenate([state_conv[i].astype(F32), xbc_s], axis=1)
        x_shift = [xpad[:, CONV_W - 1 - k:CONV_W - 1 - k + ls].reshape(ts, CONV_DIM) for k in range(CONV_W)]
        h0 = state_ssm[i].astype(F32).reshape(bs, SSD_WIDTH, D_STATE)
        cat_s, v_rows, s_s = _sample_mixer_call(z, x_shift, u, v, dtr, h0, prm_s, seg_ones, tp // CHUNK, ls)
        ssm_s.append(s_s.reshape(bs, SSD_HEADS, HEAD_DIM, D_STATE).astype(state_ssm.dtype))
        conv_s.append(xpad[:, ls:])
        v_s.append(v_rows.reshape(bs, ls, MLP_WIDTH))

        wr = jnp.pad(w_router[i].astype(F32), ((0, 0), (0, LANES - N_EXPERTS)))
        wr_hi = wr.astype(BF16)
        wr_lo = (wr - wr_hi.astype(F32)).astype(BF16)
        b_r = jnp.concatenate([b_router[i].astype(F32), jnp.full((LANES - N_EXPERTS,), -1e30, F32)]).reshape(1, LANES)
        h1, m, eid, gates = _out_router_call(cat_p, cat_s, hp, hs, w_out[i].astype(BF16), _row(norm_moe_g[i]),
                                             wr_hi, wr_lo, b_r, tm)

        be, nval, tok, pair = _route(eid[:, :TOP_K], tm_moe, nb_moe)
        b_up_g = b_up[i][:, col].reshape(N_EXPERTS, 1, 2 * D_FF)
        y4 = _moe_call(m, be, nval, tok, pair, w_up[i], b_up_g, w_down[i],
                       b_down[i].reshape(N_EXPERTS, 1, D_MODEL), perm, tm_moe, nb_moe)

        hp, hs = _ple_call(h1, y4.reshape(t_all, TOP_K * D_MODEL), gates,
                           p_prompt[i].reshape(tp, -1), p_sample[i].reshape(ts, -1), _row(norm_ple_g[i]),
                           w_ple_gate[i].astype(BF16), w_ple_proj[i].astype(BF16), _row(norm_final_g), tm)

    y_prompt = hp.reshape(bp, lp, d)
    y_sample = hs.reshape(bs, ls, d)
    return (y_prompt, y_sample, jnp.stack(ssm_p), jnp.stack(conv_p), jnp.stack(ssm_s), jnp.stack(conv_s),
            jnp.stack(v_s))
```

```python
import functools

import jax
import jax.numpy as jnp
from jax import lax
from jax.experimental import pallas as pl
from jax.experimental.pallas import tpu as pltpu

F32 = jnp.float32
BF16 = jnp.bfloat16
I32 = jnp.int32

EPS = 1e-6
D_MODEL = 1024
SSD_WIDTH = 512
SSD_HEADS = 8
HEAD_DIM = 64
SSD_GROUPS = 2
D_STATE = 128
CONV_W = 4
CONV_DIM = SSD_WIDTH + 2 * SSD_GROUPS * D_STATE
MLP_WIDTH = 512
MLP_HEADS = 8
N_EXPERTS = 32
TOP_K = 4
D_FF = 1024
SWIGLU_LIMIT = 7.0
SWIGLU_ALPHA = 1.702
TOPK_SHIFT = 2
assert 1 << TOPK_SHIFT == TOP_K
LANES = 128
CHUNK = 128
DT_PAD = LANES
IN_PAD = SSD_WIDTH + CONV_DIM + 2 * MLP_WIDTH + DT_PAD
VMEM_LIMIT = 56 * 1024 * 1024


def _cparams(sem):
    return pltpu.CompilerParams(dimension_semantics=sem, vmem_limit_bytes=VMEM_LIMIT)


def _const_spec(shape):
    return pl.BlockSpec(shape, lambda *_: (0,) * len(shape))


def _rms(x):
    return x * lax.rsqrt(jnp.mean(x * x, axis=-1, keepdims=True) + EPS)


def _dot(a, b):
    return jnp.dot(a, b, preferred_element_type=F32)


def _dot_nt(a, b):
    return lax.dot_general(a, b, (((1,), (1,)), ((), ())), preferred_element_type=F32)


def _split3(x):
    hi = x.astype(BF16)
    r = x - hi.astype(F32)
    mid = r.astype(BF16)
    lo = (r - mid.astype(F32)).astype(BF16)
    return hi, mid, lo


def _sel_right(x, m01):
    hi, mid, lo = _split3(x)
    return _dot(hi, m01) + _dot(mid, m01) + _dot(lo, m01)


def _sel_left(m01, x):
    hi, mid, lo = _split3(x)
    return _dot(m01, hi) + _dot(m01, mid) + _dot(m01, lo)


def _softplus(x):
    return jnp.maximum(x, 0.0) + jnp.log1p(jnp.exp(-jnp.abs(x)))


def _inproj_call(xp, xs, g, w, tm):
    tp, ts = xp.shape[0], xs.shape[0]
    n_p, n_s = tp // tm, ts // tm
    t_all = tp + ts
    segs = ((0, 512), (512, 1536), (1536, 2048), (2048, 2560), (2560, IN_PAD))

    def body(xp_ref, xs_ref, g_ref, w_ref, *outs):
        def run(x_ref):
            xn = (_rms(x_ref[...]) * g_ref[...]).astype(BF16)
            for (a, b), o in zip(segs, outs):
                o[...] = _dot(xn, w_ref[:, a:b])

        i = pl.program_id(0)

        @pl.when(i < n_p)
        def _():
            run(xp_ref)

        @pl.when(i >= n_p)
        def _():
            run(xs_ref)

    widths = [b - a for a, b in segs]
    return pl.pallas_call(
        body,
        out_shape=[jax.ShapeDtypeStruct((t_all, wd), F32) for wd in widths],
        grid=(n_p + n_s,),
        in_specs=[
            pl.BlockSpec((tm, D_MODEL), lambda i: (jnp.minimum(i, n_p - 1), 0)),
            pl.BlockSpec((tm, D_MODEL), lambda i: (jnp.maximum(i - n_p, 0), 0)),
            _const_spec((1, D_MODEL)),
            _const_spec((D_MODEL, IN_PAD)),
        ],
        out_specs=[pl.BlockSpec((tm, wd), lambda i: (i, 0)) for wd in widths],
        compiler_params=_cparams(("arbitrary",)),
        name="in_proj",
    )(xp, xs, g, w)


def _mixer_front(conv, dtr, dtb, alog, alog_x, rexp, tril, seg_ones):
    xact = conv * jax.nn.sigmoid(conv)
    xs = xact[:, :SSD_WIDTH]
    bm = xact[:, SSD_WIDTH:SSD_WIDTH + 256]
    cm = xact[:, SSD_WIDTH + 256:]
    dt = _softplus(dtr + dtb)
    a = dt * (-jnp.exp(alog))
    dt_x = _sel_right(dt, rexp)
    a_x = dt_x * (-jnp.exp(alog_x))
    acum = _sel_left(tril, a)
    acum_x = _sel_left(tril, a_x)
    if seg_ones is None:
        r = acum_x.shape[0]
        tot_x = jnp.broadcast_to(acum_x[r - 1:r, :], acum_x.shape)
    else:
        tot_x = _sel_left(seg_ones, a_x)
    return xs, bm, cm, dt_x, acum, acum_x, tot_x


def _ssd_intra(cmb, bmb, acum, xdt, mask):
    r = acum.shape[0]
    acum_t = acum.T
    lane = lax.broadcasted_iota(I32, (r, LANES), 1)
    low = lane < HEAD_DIM
    outs = []
    for g in range(SSD_GROUPS):
        sg = _dot_nt(cmb[:, LANES * g:LANES * (g + 1)], bmb[:, LANES * g:LANES * (g + 1)])
        for k in (2 * g, 2 * g + 1):
            parts = []
            for h in (2 * k, 2 * k + 1):
                seg = acum[:, h:h + 1] - acum_t[h:h + 1, :]
                parts.append((sg * jnp.exp(jnp.where(mask, seg, -jnp.inf))).astype(BF16))
            lhs = jnp.concatenate(parts, axis=1)
            xd = xdt[:, LANES * k:LANES * (k + 1)]
            rhs = jnp.concatenate([jnp.where(low, xd, 0.0), jnp.where(low, 0.0, xd)], axis=0).astype(BF16)
            outs.append(_dot(lhs, rhs))
    return jnp.concatenate(outs, axis=1)


def _mixer_back(y, z, u, v, sng, vng, vnb, wsp_ref, bsp, mog):
    r = y.shape[0]
    yg = y * (z * jax.nn.sigmoid(z))
    halves = []
    for g in range(SSD_GROUPS):
        t = yg[:, 256 * g:256 * (g + 1)]
        halves.append(_rms(t))
    yn = jnp.concatenate(halves, axis=1) * sng
    ug = jax.nn.gelu(u)
    vg = jax.nn.gelu(v)
    mu = jnp.mean(vg, axis=-1, keepdims=True)
    var = jnp.mean(jnp.square(vg - mu), axis=-1, keepdims=True)
    v_ln = (vg - mu) * lax.rsqrt(var + EPS) * vng + vnb
    lane = lax.broadcasted_iota(I32, (r, LANES), 1)
    low = lane < HEAD_DIM
    outs = []
    for k in range(MLP_HEADS // 2):
        vd = v_ln[:, LANES * k:LANES * (k + 1)]
        rhs = jnp.concatenate([jnp.where(low, vd, 0.0), jnp.where(low, 0.0, vd)], axis=0).astype(BF16)
        outs.append(_dot(wsp_ref[k], rhs))
    s = jnp.concatenate(outs, axis=1) + bsp
    m = _rms(ug * s) * mog
    return jnp.concatenate([yn, m], axis=1).astype(BF16), v_ln


_MIXER_PARAM_SHAPES = (
    (CONV_W, CONV_DIM), (1, CONV_DIM), (1, LANES), (1, LANES), (1, SSD_WIDTH), (LANES, SSD_WIDTH),
    (CHUNK, CHUNK), (1, SSD_WIDTH), (1, SSD_WIDTH), (1, MLP_WIDTH), (1, MLP_WIDTH),
    (MLP_HEADS // 2, CHUNK, 2 * CHUNK), (CHUNK, MLP_WIDTH), (1, MLP_WIDTH),
)


def _prompt_mixer_body(z_ref, xbc_ref, u_ref, v_ref, dt_ref,
                       cw_ref, cb_ref, dtb_ref, alog_ref, alogx_ref, rexp_ref, tril_ref, dskip_ref,
                       sng_ref, vng_ref, vnb_ref, wsp_ref, bsp_ref, mog_ref,
                       cat_ref, ssm_ref, ext_scr, s_scr):
    c = pl.program_id(1)
    r = CHUNK

    @pl.when(c == 0)
    def _():
        ext_scr[0:8, :] = jnp.zeros((8, CONV_DIM), F32)
        s_scr[...] = jnp.zeros_like(s_scr)

    x = xbc_ref[...]
    ext_scr[8:8 + r, :] = x
    cw = cw_ref[...]
    conv = (cb_ref[...] + cw[3:4] * x + cw[2:3] * ext_scr[7:7 + r, :]
            + cw[1:2] * ext_scr[6:6 + r, :] + cw[0:1] * ext_scr[5:5 + r, :])
    ext_scr[0:8, :] = x[r - 8:r, :]

    xs, bm, cm, dt_x, acum, acum_x, tot_x = _mixer_front(
        conv, dt_ref[...], dtb_ref[...], alog_ref[...], alogx_ref[...], rexp_ref[...], tril_ref[...], None)
    bmb, cmb = bm.astype(BF16), cm.astype(BF16)
    xdt = xs * dt_x
    row = lax.broadcasted_iota(I32, (r, r), 0)
    col = lax.broadcasted_iota(I32, (r, r), 1)
    y_diag = _ssd_intra(cmb, bmb, acum, xdt, row >= col)

    s_prev = s_scr[...]
    s_prev_b = s_prev.astype(BF16)
    y_off = jnp.concatenate(
        [_dot_nt(cmb[:, LANES * g:LANES * (g + 1)], s_prev_b[256 * g:256 * (g + 1), :]) for g in range(SSD_GROUPS)],
        axis=1)
    y = y_diag + y_off * jnp.exp(acum_x) + dskip_ref[...] * xs

    w_t = (xdt * jnp.exp(tot_x - acum_x)).T.astype(BF16)
    states = jnp.concatenate(
        [_dot(w_t[256 * g:256 * (g + 1), :], bmb[:, LANES * g:LANES * (g + 1)]) for g in range(SSD_GROUPS)], axis=0)
    s_new = s_prev * jnp.exp(tot_x).T + states
    s_scr[...] = s_new

    cat, _ = _mixer_back(y, z_ref[...], u_ref[...], v_ref[...], sng_ref[...], vng_ref[...], vnb_ref[...],
                         wsp_ref, bsp_ref[...], mog_ref[...])
    cat_ref[...] = cat

    @pl.when(c == pl.num_programs(1) - 1)
    def _():
        ssm_ref[0] = s_new


def _prompt_mixer_call(z, xbc, u, v, dtr, params, nb, nc):
    row = lambda b, c: (b * nc + c, 0)
    in_specs = [
        pl.BlockSpec((CHUNK, SSD_WIDTH), row), pl.BlockSpec((CHUNK, CONV_DIM), row),
        pl.BlockSpec((CHUNK, MLP_WIDTH), row), pl.BlockSpec((CHUNK, MLP_WIDTH), row),
        pl.BlockSpec((CHUNK, DT_PAD), row),
    ] + [_const_spec(s) for s in _MIXER_PARAM_SHAPES]
    return pl.pallas_call(
        _prompt_mixer_body,
        out_shape=[jax.ShapeDtypeStruct((nb * nc * CHUNK, D_MODEL), BF16),
                   jax.ShapeDtypeStruct((nb, SSD_WIDTH, D_STATE), F32)],
        grid=(nb, nc),
        in_specs=in_specs,
        out_specs=[pl.BlockSpec((CHUNK, D_MODEL), row),
                   pl.BlockSpec((1, SSD_WIDTH, D_STATE), lambda b, c: (b, 0, 0))],
        scratch_shapes=[pltpu.VMEM((CHUNK + 8, CONV_DIM), F32), pltpu.VMEM((SSD_WIDTH, D_STATE), F32)],
        compiler_params=_cparams(("arbitrary", "arbitrary")),
        name="prompt_mixer",
    )(z, xbc, u, v, dtr, *params)


def _sample_mixer_body(seq_len, z_ref, x0_ref, x1_ref, x2_ref, x3_ref, u_ref, v_ref, dt_ref, h_ref,
                       cw_ref, cb_ref, dtb_ref, alog_ref, alogx_ref, rexp_ref, tril_ref, dskip_ref,
                       sng_ref, vng_ref, vnb_ref, wsp_ref, bsp_ref, mog_ref, segones_ref,
                       cat_ref, vout_ref, hout_ref, cm_scr, bm_scr, wt_scr, dtt_scr, yoff_scr):
    r = CHUNK
    shift = seq_len.bit_length() - 1
    cw = cw_ref[...]
    conv = (cb_ref[...] + cw[3:4] * x0_ref[...] + cw[2:3] * x1_ref[...]
            + cw[1:2] * x2_ref[...] + cw[0:1] * x3_ref[...])
    xs, bm, cm, dt_x, acum, acum_x, tot_x = _mixer_front(
        conv, dt_ref[...], dtb_ref[...], alog_ref[...], alogx_ref[...], rexp_ref[...], tril_ref[...],
        segones_ref[...])
    bmb, cmb = bm.astype(BF16), cm.astype(BF16)
    xdt = xs * dt_x
    row = lax.broadcasted_iota(I32, (r, r), 0)
    col = lax.broadcasted_iota(I32, (r, r), 1)
    same = lax.shift_right_logical(row, shift) == lax.shift_right_logical(col, shift)
    y_diag = _ssd_intra(cmb, bmb, acum, xdt, same & (row >= col))

    cm_scr[...] = cm
    bm_scr[...] = bmb
    wt_scr[...] = (xdt * jnp.exp(tot_x - acum_x)).T
    dtt_scr[...] = jnp.exp(tot_x).T
    ones_b = jnp.ones((LANES, LANES), BF16)
    seqs_per_slab = 8 // seq_len

    def slab(j, carry):
        rows = pl.ds(pl.multiple_of(8 * j, 8), 8)
        cms = cm_scr[rows, :].astype(BF16)
        sub = lax.broadcasted_iota(I32, (8, 256), 0)
        lane = lax.broadcasted_iota(I32, (256, LANES), 1)
        for g in range(SSD_GROUPS):
            q_rows = slice(256 * g, 256 * (g + 1))
            acc = jnp.zeros((8, 256), F32)
            for q in range(seqs_per_slab):
                s = seqs_per_slab * j + q
                y_s = _dot_nt(cms[:, LANES * g:LANES * (g + 1)], h_ref[s, q_rows, :].astype(BF16))
                acc = jnp.where(lax.shift_right_logical(sub, shift) == q, y_s, acc)
            yoff_scr[rows, 256 * g:256 * (g + 1)] = acc
            for q in range(seqs_per_slab):
                s = seqs_per_slab * j + q
                w_sel = jnp.where(lax.shift_right_logical(lane, shift) == s, wt_scr[q_rows, :], 0.0).astype(BF16)
                st = _dot(w_sel, bm_scr[:, LANES * g:LANES * (g + 1)])
                d_sel = jnp.where(lane == s * seq_len, dtt_scr[q_rows, :], 0.0)
                hout_ref[s, q_rows, :] = h_ref[s, q_rows, :] * _sel_right(d_sel, ones_b) + st
        return carry

    lax.fori_loop(0, r // 8, slab, 0)

    y = y_diag + yoff_scr[...] * jnp.exp(acum_x) + dskip_ref[...] * xs
    cat, v_ln = _mixer_back(y, z_ref[...], u_ref[...], v_ref[...], sng_ref[...], vng_ref[...], vnb_ref[...],
                            wsp_ref, bsp_ref[...], mog_ref[...])
    cat_ref[...] = cat
    vout_ref[...] = v_ln


def _sample_mixer_call(z, x_shift, u, v, dtr, h0, params, seg_ones, row0, seq_len):
    ts = x_shift[0].shape[0]
    n = ts // CHUNK
    spt = CHUNK // seq_len
    off = lambda i: (row0 + i, 0)
    loc = lambda i: (i, 0)
    st3 = lambda i: (i, 0, 0)
    in_specs = (
        [pl.BlockSpec((CHUNK, SSD_WIDTH), off)]
        + [pl.BlockSpec((CHUNK, CONV_DIM), loc)] * 4
        + [pl.BlockSpec((CHUNK, MLP_WIDTH), off), pl.BlockSpec((CHUNK, MLP_WIDTH), off),
           pl.BlockSpec((CHUNK, DT_PAD), off), pl.BlockSpec((spt, SSD_WIDTH, D_STATE), st3)]
        + [_const_spec(s) for s in _MIXER_PARAM_SHAPES] + [_const_spec((CHUNK, CHUNK))])
    return pl.pallas_call(
        functools.partial(_sample_mixer_body, seq_len),
        out_shape=[jax.ShapeDtypeStruct((ts, D_MODEL), BF16), jax.ShapeDtypeStruct((ts, MLP_WIDTH), F32),
                   jax.ShapeDtypeStruct(h0.shape, F32)],
        grid=(n,),
        in_specs=in_specs,
        out_specs=[pl.BlockSpec((CHUNK, D_MODEL), loc), pl.BlockSpec((CHUNK, MLP_WIDTH), loc),
                   pl.BlockSpec((spt, SSD_WIDTH, D_STATE), st3)],
        scratch_shapes=[pltpu.VMEM((CHUNK, 256), F32), pltpu.VMEM((CHUNK, 256), BF16),
                        pltpu.VMEM((SSD_WIDTH, CHUNK), F32), pltpu.VMEM((SSD_WIDTH, CHUNK), F32),
                        pltpu.VMEM((CHUNK, SSD_WIDTH), F32)],
        compiler_params=_cparams(("arbitrary",)),
        name="sample_mixer",
    )(z, *x_shift, u, v, dtr, h0, *params, seg_ones)


def _out_router_call(cat_p, cat_s, xp, xs, w_out, g_moe, wr_hi, wr_lo, b_r, tm):
    tp, ts = xp.shape[0], xs.shape[0]
    n_p, n_s = tp // tm, ts // tm
    t_all = tp + ts

    def body(cp_ref, cs_ref, xp_ref, xs_ref, wo_ref, g_ref, wh_ref, wl_ref, br_ref,
             h1_ref, m_ref, eid_ref, gate_ref):
        def run(c_ref, x_ref):
            h1 = x_ref[...] + _dot(c_ref[...], wo_ref[...])
            h1_ref[...] = h1
            m = _rms(h1) * g_ref[...]
            m_ref[...] = m
            m_hi = m.astype(BF16)
            m_lo = (m - m_hi.astype(F32)).astype(BF16)
            logits = _dot(m_hi, wh_ref[...]) + _dot(m_lo, wh_ref[...]) + _dot(m_hi, wl_ref[...]) + br_ref[...]
            lane = lax.broadcasted_iota(I32, logits.shape, 1).astype(F32)
            work = logits
            vals, ids = [], []
            for _ in range(TOP_K):
                mx = jnp.max(work, axis=-1, keepdims=True)
                idx = jnp.min(jnp.where(work == mx, lane, float(LANES)), axis=-1, keepdims=True)
                vals.append(mx)
                ids.append(idx)
                work = jnp.where(lane == idx, -jnp.inf, work)
            ex = [jnp.exp(vv - vals[0]) for vv in vals]
            den = ex[0] + ex[1] + ex[2] + ex[3]
            eid = jnp.zeros(logits.shape, I32)
            gate = jnp.zeros(logits.shape, F32)
            for k in range(TOP_K):
                eid = jnp.where(lane == k, ids[k].astype(I32), eid)
                gate = jnp.where(lane == k, ex[k] / den, gate)
            eid_ref[...] = eid
            gate_ref[...] = gate

        i = pl.program_id(0)

        @pl.when(i < n_p)
        def _():
            run(cp_ref, xp_ref)

        @pl.when(i >= n_p)
        def _():
            run(cs_ref, xs_ref)

    pmap = lambda i: (jnp.minimum(i, n_p - 1), 0)
    smap = lambda i: (jnp.maximum(i - n_p, 0), 0)
    omap = lambda i: (i, 0)
    return pl.pallas_call(
        body,
        out_shape=[jax.ShapeDtypeStruct((t_all, D_MODEL), F32), jax.ShapeDtypeStruct((t_all, D_MODEL), F32),
                   jax.ShapeDtypeStruct((t_all, LANES), I32), jax.ShapeDtypeStruct((t_all, LANES), F32)],
        grid=(n_p + n_s,),
        in_specs=[pl.BlockSpec((tm, D_MODEL), pmap), pl.BlockSpec((tm, D_MODEL), smap),
                  pl.BlockSpec((tm, D_MODEL), pmap), pl.BlockSpec((tm, D_MODEL), smap),
                  _const_spec((D_MODEL, D_MODEL)), _const_spec((1, D_MODEL)),
                  _const_spec((D_MODEL, LANES)), _const_spec((D_MODEL, LANES)), _const_spec((1, LANES))],
        out_specs=[pl.BlockSpec((tm, D_MODEL), omap), pl.BlockSpec((tm, D_MODEL), omap),
                   pl.BlockSpec((tm, LANES), omap), pl.BlockSpec((tm, LANES), omap)],
        compiler_params=_cparams(("arbitrary",)),
        name="out_router",
    )(cat_p, cat_s, xp, xs, w_out, g_moe, wr_hi, wr_lo, b_r)


def _route(eid, tm, nb):
    t = eid.shape[0]
    tk = t * TOP_K
    flat = eid.reshape(tk)
    _, order = lax.sort((flat, jnp.arange(tk, dtype=I32)), num_keys=1, is_stable=True)
    counts = jnp.sum((flat[:, None] == jnp.arange(N_EXPERTS, dtype=I32)[None, :]).astype(I32), axis=0)
    nblk = (counts + tm - 1) // tm
    bend = jnp.cumsum(nblk)
    bstart = bend - nblk
    start = jnp.cumsum(counts) - counts
    nused = bend[-1]
    blk = jnp.arange(nb, dtype=I32)
    be = jnp.minimum(jnp.sum((blk[:, None] >= bend[None, :]).astype(I32), axis=1), N_EXPERTS - 1)
    be = jnp.where(blk < nused, be, be[jnp.maximum(nused - 1, 0)])
    used = blk < nused
    nval = jnp.where(used, jnp.clip(counts[be] - (blk - bstart[be]) * tm, 0, tm), 0).astype(I32)
    off = jnp.where(used, start[be] + (blk - bstart[be]) * tm, 0).astype(I32)
    rows = -(-(tk + tm) // LANES) + _id_rows(tm)
    ord2d = jnp.pad(order, (0, rows * LANES - tk)).reshape(rows, LANES)
    return be, nval, off, ord2d


def _id_rows(tm):
    return tm // LANES + 1


def _moe_body(tm, t_all, be_ref, nval_ref, off_ref, ord_hbm, m_hbm, wup_ref, bup_ref, wdn_ref, bdn_ref, perm_ref,
              y_hbm, ids, xbuf, ybuf, wup_b, wdn_b, isem, gsem, ssem):
    i = pl.program_id(0)
    nb = pl.num_programs(0)
    nv = nval_ref[i]
    slot = lax.rem(i, 2)
    nslot = 1 - slot
    nxt = jnp.minimum(i + 1, nb - 1)
    has_next = (i + 1 < nb) & (nval_ref[nxt] > 0)

    def ids_copy(b, s):
        row0 = lax.shift_right_logical(off_ref[b], 7)
        return pltpu.make_async_copy(ord_hbm.at[pl.ds(row0, _id_rows(tm))], ids.at[s], isem.at[s])

    def pair_id(base, s, r):
        p = base + r
        return ids[s, lax.shift_right_logical(p, 7), p & (LANES - 1)]

    def issue_gathers(b, s):
        base = off_ref[b] & (LANES - 1)

        def one(r, c):
            tok = lax.shift_right_logical(pair_id(base, s, r), TOPK_SHIFT)
            pltpu.make_async_copy(m_hbm.at[pl.ds(tok, 1)], xbuf.at[s, pl.ds(r, 1)], gsem.at[s]).start()
            return c

        lax.fori_loop(0, tm, one, 0, unroll=8)

    def scatter_row(base, s, r):
        pid = pair_id(base, s, r)
        dest = (pid & (TOP_K - 1)) * t_all + lax.shift_right_logical(pid, TOPK_SHIFT)
        return pltpu.make_async_copy(ybuf.at[s, pl.ds(r, 1)], y_hbm.at[pl.ds(dest, 1)], ssem.at[s])

    def issue_scatters(b, s, n):
        base = off_ref[b] & (LANES - 1)

        def one(r, c):
            scatter_row(base, s, r).start()
            return c

        @pl.when(n == tm)
        def _():
            lax.fori_loop(0, tm, one, 0, unroll=8)

        @pl.when(n < tm)
        def _():
            lax.fori_loop(0, n, one, 0)

    def wait_scatters(s, n):
        @pl.when(n == tm)
        def _():
            pltpu.make_async_copy(ybuf.at[s], y_hbm.at[pl.ds(0, tm)], ssem.at[s]).wait()

        def one(r, c):
            pltpu.make_async_copy(ybuf.at[s, pl.ds(0, 1)], y_hbm.at[pl.ds(0, 1)], ssem.at[s]).wait()
            return c

        @pl.when(n < tm)
        def _():
            lax.fori_loop(0, n, one, 0)

    @pl.when(nv > 0)
    def _():
        @pl.when(i == 0)
        def _():
            first = ids_copy(0, 0)
            first.start()
            first.wait()
            issue_gathers(0, 0)

        @pl.when(has_next)
        def _():
            ids_copy(nxt, nslot).start()

        prev = be_ref[jnp.maximum(i - 1, 0)]

        @pl.when((i == 0) | (be_ref[i] != prev))
        def _():
            for jb in range(2 * D_FF // 256):
                cols = slice(256 * jb, 256 * (jb + 1))
                wup_b[:, cols] = _dot(wup_ref[0, :, cols].astype(BF16), perm_ref[...]).astype(BF16)
            wdn_b[...] = wdn_ref[0].astype(BF16)

        pltpu.make_async_copy(m_hbm.at[pl.ds(0, tm)], xbuf.at[slot], gsem.at[slot]).wait()

        @pl.when(has_next)
        def _():
            ids_copy(nxt, nslot).wait()
            issue_gathers(nxt, nslot)

        x = xbuf[slot].astype(BF16)
        acts = []
        for jb in range(D_FF // LANES):
            h = _dot(x, wup_b[:, 256 * jb:256 * (jb + 1)]) + bup_ref[0, :, 256 * jb:256 * (jb + 1)]
            gate = jnp.minimum(h[:, :LANES], SWIGLU_LIMIT)
            lin = jnp.clip(h[:, LANES:], -SWIGLU_LIMIT, SWIGLU_LIMIT)
            acts.append((gate * jax.nn.sigmoid(SWIGLU_ALPHA * gate) * (lin + 1.0)).astype(BF16))
        act = jnp.concatenate(acts, axis=1)
        ybuf[slot] = _dot(act, wdn_b[...]) + bdn_ref[0]

        @pl.when(i > 0)
        def _():
            wait_scatters(nslot, nval_ref[jnp.maximum(i - 1, 0)])

        issue_scatters(i, slot, nv)

        @pl.when(jnp.logical_not(has_next))
        def _():
            wait_scatters(slot, nv)


def _moe_call(m, be, nval, off, ord2d, w_up, b_up_g, w_down, b_down, perm, tm, nb):
    t = m.shape[0]
    by_expert = lambda i, be, nv, off: (be[i], 0, 0)
    grid_spec = pltpu.PrefetchScalarGridSpec(
        num_scalar_prefetch=3,
        grid=(nb,),
        in_specs=[
            pl.BlockSpec(memory_space=pl.ANY),
            pl.BlockSpec(memory_space=pl.ANY),
            pl.BlockSpec((1, D_MODEL, 2 * D_FF), by_expert),
            pl.BlockSpec((1, 1, 2 * D_FF), by_expert),
            pl.BlockSpec((1, D_FF, D_MODEL), by_expert),
            pl.BlockSpec((1, 1, D_MODEL), by_expert),
            pl.BlockSpec((256, 256), lambda i, be, nv, off: (0, 0)),
        ],
        out_specs=pl.BlockSpec(memory_space=pl.ANY),
        scratch_shapes=[pltpu.SMEM((2, _id_rows(tm), LANES), I32),
                        pltpu.VMEM((2, tm, D_MODEL), F32), pltpu.VMEM((2, tm, D_MODEL), F32),
                        pltpu.VMEM((D_MODEL, 2 * D_FF), BF16), pltpu.VMEM((D_FF, D_MODEL), BF16),
                        pltpu.SemaphoreType.DMA((2,)), pltpu.SemaphoreType.DMA((2,)),
                        pltpu.SemaphoreType.DMA((2,))],
    )
    return pl.pallas_call(
        functools.partial(_moe_body, tm, t),
        out_shape=jax.ShapeDtypeStruct((TOP_K * t, D_MODEL), F32),
        grid_spec=grid_spec,
        compiler_params=_cparams(("arbitrary",)),
        name="moe_experts",
    )(be, nval, off, ord2d, m, w_up, b_up_g, w_down, b_down, perm)


def _ple_call(h1, y4, gates, pp, ps, g_ple, w_gate, w_proj, g_final, tm):
    tp, ts = pp.shape[0], ps.shape[0]
    n_p, n_s = tp // tm, ts // tm
    ple = pp.shape[1]

    def body(h1_ref, y0_ref, y1_ref, y2_ref, y3_ref, gt_ref, pp_ref, ps_ref, g_ref, wg_ref, wp_ref, gf_ref,
             yp_ref, ys_ref):
        def run(p_ref, o_ref):
            gt = gt_ref[...]
            moe = gt[:, 0:1] * y0_ref[...]
            for k, y_ref in enumerate((y1_ref, y2_ref, y3_ref), start=1):
                moe = moe + gt[:, k:k + 1] * y_ref[...]
            h2 = h1_ref[...] + moe
            a = (_rms(h2) * g_ref[...]).astype(BF16)
            gate = jax.nn.sigmoid(_dot(a, wg_ref[...]))
            pe = _dot(p_ref[...].astype(BF16), wp_ref[...])
            h3 = h2 + pe * gate
            o_ref[...] = _rms(h3) * gf_ref[...]

        i = pl.program_id(0)

        @pl.when(i < n_p)
        def _():
            run(pp_ref, yp_ref)

        @pl.when(i >= n_p)
        def _():
            run(ps_ref, ys_ref)

    pmap = lambda i: (jnp.minimum(i, n_p - 1), 0)
    smap = lambda i: (jnp.maximum(i - n_p, 0), 0)
    omap = lambda i: (i, 0)
    return pl.pallas_call(
        body,
        out_shape=[jax.ShapeDtypeStruct((tp, D_MODEL), F32), jax.ShapeDtypeStruct((ts, D_MODEL), F32)],
        grid=(n_p + n_s,),
        in_specs=[pl.BlockSpec((tm, D_MODEL), omap)]
                 + [pl.BlockSpec((tm, D_MODEL), functools.partial(lambda k, i: (k * (n_p + n_s) + i, 0), k))
                    for k in range(TOP_K)]
                 + [pl.BlockSpec((tm, LANES), omap), pl.BlockSpec((tm, ple), pmap), pl.BlockSpec((tm, ple), smap),
                  _const_spec((1, D_MODEL)), _const_spec((D_MODEL, D_MODEL)), _const_spec((ple, D_MODEL)),
                  _const_spec((1, D_MODEL))],
        out_specs=[pl.BlockSpec((tm, D_MODEL), pmap), pl.BlockSpec((tm, D_MODEL), smap)],
        compiler_params=_cparams(("arbitrary",)),
        name="ple_final",
    )(h1, y4, y4, y4, y4, gates, pp, ps, g_ple, w_gate, w_proj, g_final)


def _row(x, width=None):
    x = x.reshape(1, -1).astype(F32)
    if width is not None and x.shape[1] < width:
        x = jnp.pad(x, ((0, 0), (0, width - x.shape[1])))
    return x


def _mixer_params(conv_w, conv_b, dt_bias, a_log, d_skip, ssd_norm_g, v_norm_g, v_norm_b, w_spatial, b_spatial,
                  mlp_out_g, seq_len):
    n_seq = CHUNK // seq_len
    pos = jnp.arange(CHUNK) % seq_len
    same = (jnp.arange(CHUNK)[:, None] // seq_len) == (jnp.arange(CHUNK)[None, :] // seq_len)
    tril = (same & (jnp.arange(CHUNK)[:, None] >= jnp.arange(CHUNK)[None, :])).astype(BF16)
    rexp = (jnp.arange(LANES)[:, None] == (jnp.arange(SSD_WIDTH)[None, :] // HEAD_DIM)).astype(BF16)
    w_loc = jnp.tril(w_spatial[:, :seq_len, :seq_len])
    eye = jnp.eye(n_seq, dtype=F32)
    w_bd = jnp.einsum("st,hij->hsitj", eye, w_loc).reshape(MLP_HEADS, CHUNK, CHUNK)
    wsp = jnp.concatenate([w_bd[0::2], w_bd[1::2]], axis=2).astype(BF16)
    bsp = jnp.repeat(b_spatial[:, :seq_len].T[pos], MLP_WIDTH // MLP_HEADS, axis=1)
    params = (
        conv_w.astype(F32), _row(conv_b), _row(dt_bias, LANES), _row(a_log, LANES),
        _row(jnp.repeat(a_log, HEAD_DIM)), rexp, tril, _row(jnp.repeat(d_skip, HEAD_DIM)),
        _row(ssd_norm_g), _row(v_norm_g), _row(v_norm_b), wsp, bsp.astype(F32), _row(mlp_out_g),
    )
    return params, same.astype(BF16)


def _tile_rows(n):
    return 512 if n % 512 == 0 else CHUNK


def kernel(x_prompt, x_sample, state_ssm, state_conv, p_prompt, p_sample, norm_mix_g, w_in, conv_w, conv_b, dt_bias, a_log, d_skip, ssd_norm_g, v_norm_g, v_norm_b, w_spatial, b_spatial, mlp_out_g, w_out, norm_moe_g, w_router, b_router, w_up, b_up, w_down, b_down, norm_ple_g, w_ple_gate, w_ple_proj, norm_final_g):
    depth = norm_mix_g.shape[0]
    bp, lp, d = x_prompt.shape
    bs, ls, _ = x_sample.shape
    tp, ts = bp * lp, bs * ls
    assert depth == 1 and d == D_MODEL and lp % CHUNK == 0 and ts % CHUNK == 0 and 8 % ls == 0
    tm = _tile_rows(tp) if ts % _tile_rows(tp) == 0 else CHUNK
    t_all = tp + ts
    tm_moe = 256
    nb_moe = -(-t_all * TOP_K // tm_moe) + N_EXPERTS

    hp = x_prompt.reshape(tp, d)
    hs = x_sample.reshape(ts, d)
    ssm_p, conv_p, ssm_s, conv_s, v_s = [], [], [], [], []
    o1 = SSD_WIDTH
    o2 = o1 + CONV_DIM
    o3 = o2 + SSD_HEADS
    o4 = o3 + MLP_WIDTH
    c = jnp.arange(256)
    src = jnp.where(c < LANES, 2 * c, 2 * (c - LANES) + 1)
    perm = (jnp.arange(256)[:, None] == src[None, :]).astype(BF16)
    col = (jnp.arange(2 * D_FF) // 256) * 256 + src[jnp.arange(2 * D_FF) % 256]

    for i in range(depth):
        wi = w_in[i]
        w_cat = jnp.concatenate(
            [wi[:, :o2], wi[:, o3:], jnp.pad(wi[:, o2:o3], ((0, 0), (0, DT_PAD - SSD_HEADS)))], axis=1).astype(BF16)
        z, xbc, u, v, dtr = _inproj_call(hp, hs, _row(norm_mix_g[i]), w_cat, tm)

        mix_args = (conv_w[i], conv_b[i], dt_bias[i], a_log[i], d_skip[i], ssd_norm_g[i], v_norm_g[i], v_norm_b[i],
                    w_spatial[i], b_spatial[i], mlp_out_g[i])
        prm_p, _ = _mixer_params(*mix_args, seq_len=CHUNK)
        cat_p, s_p = _prompt_mixer_call(z, xbc, u, v, dtr, prm_p, bp, lp // CHUNK)
        ssm_p.append(s_p.reshape(bp, SSD_HEADS, HEAD_DIM, D_STATE).astype(state_ssm.dtype))
        conv_p.append(xbc[:tp].reshape(bp, lp, CONV_DIM)[:, lp - (CONV_W - 1):])

        prm_s, seg_ones = _mixer_params(*mix_args, seq_len=ls)
        xbc_s = xbc[tp:].reshape(bs, ls, CONV_DIM)
        xpad = jnp.concatenate([state_conv[i].astype(F32), xbc_s], axis=1)
        x_shift = [xpad[:, CONV_W - 1 - k:CONV_W - 1 - k + ls].reshape(ts, CONV_DIM) for k in range(CONV_W)]
        h0 = state_ssm[i].astype(F32).reshape(bs, SSD_WIDTH, D_STATE)
        cat_s, v_rows, s_s = _sample_mixer_call(z, x_shift, u, v, dtr, h0, prm_s, seg_ones, tp // CHUNK, ls)
        ssm_s.append(s_s.reshape(bs, SSD_HEADS, HEAD_DIM, D_STATE).astype(state_ssm.dtype))
        conv_s.append(xpad[:, ls:])
        v_s.append(v_rows.reshape(bs, ls, MLP_WIDTH))

        wr = jnp.pad(w_router[i].astype(F32), ((0, 0), (0, LANES - N_EXPERTS)))
        wr_hi = wr.astype(BF16)
        wr_lo = (wr - wr_hi.astype(F32)).astype(BF16)
        b_r = jnp.concatenate([b_router[i].astype(F32), jnp.full((LANES - N_EXPERTS,), -1e30, F32)]).reshape(1, LANES)
        h1, m, eid, gates = _out_router_call(cat_p, cat_s, hp, hs, w_out[i].astype(BF16), _row(norm_moe_g[i]),
                                             wr_hi, wr_lo, b_r, tm)

        be, nval, off, ord2d = _route(eid[:, :TOP_K], tm_moe, nb_moe)
        b_up_g = b_up[i][:, col].reshape(N_EXPERTS, 1, 2 * D_FF)
        y4 = _moe_call(m, be, nval, off, ord2d, w_up[i], b_up_g, w_down[i],
                       b_down[i].reshape(N_EXPERTS, 1, D_MODEL), perm, tm_moe, nb_moe)

        hp, hs = _ple_call(h1, y4, gates,
                           p_prompt[i].reshape(tp, -1), p_sample[i].reshape(ts, -1), _row(norm_ple_g[i]),
                           w_ple_gate[i].astype(BF16), w_ple_proj[i].astype(BF16), _row(norm_final_g), tm)

    y_prompt = hp.reshape(bp, lp, d)
    y_sample = hs.reshape(bs, ls, d)
    return (y_prompt, y_sample, jnp.stack(ssm_p), jnp.stack(conv_p), jnp.stack(ssm_s), jnp.stack(conv_s),
            jnp.stack(v_s))
```

```python
import functools

import jax
import jax.numpy as jnp
from jax import lax
from jax.experimental import pallas as pl
from jax.experimental.pallas import tpu as pltpu

F32 = jnp.float32
BF16 = jnp.bfloat16
I32 = jnp.int32

EPS = 1e-6
D_MODEL = 1024
SSD_WIDTH = 512
SSD_HEADS = 8
HEAD_DIM = 64
SSD_GROUPS = 2
D_STATE = 128
CONV_W = 4
CONV_DIM = SSD_WIDTH + 2 * SSD_GROUPS * D_STATE
MLP_WIDTH = 512
MLP_HEADS = 8
N_EXPERTS = 32
TOP_K = 4
D_FF = 1024
SWIGLU_LIMIT = 7.0
SWIGLU_ALPHA = 1.702
TOPK_SHIFT = 2
assert 1 << TOPK_SHIFT == TOP_K
LANES = 128
CHUNK = 128
DT_PAD = LANES
IN_PAD = SSD_WIDTH + CONV_DIM + 2 * MLP_WIDTH + DT_PAD
VMEM_LIMIT = 56 * 1024 * 1024


def _cparams(sem):
    return pltpu.CompilerParams(dimension_semantics=sem, vmem_limit_bytes=VMEM_LIMIT)


def _const_spec(shape):
    return pl.BlockSpec(shape, lambda *_: (0,) * len(shape))


def _rms(x):
    return x * lax.rsqrt(jnp.mean(x * x, axis=-1, keepdims=True) + EPS)


def _dot(a, b):
    return jnp.dot(a, b, preferred_element_type=F32)


def _dot_nt(a, b):
    return lax.dot_general(a, b, (((1,), (1,)), ((), ())), preferred_element_type=F32)


def _split3(x):
    hi = x.astype(BF16)
    r = x - hi.astype(F32)
    mid = r.astype(BF16)
    lo = (r - mid.astype(F32)).astype(BF16)
    return hi, mid, lo


def _sel_right(x, m01):
    hi, mid, lo = _split3(x)
    return _dot(hi, m01) + _dot(mid, m01) + _dot(lo, m01)


def _sel_left(m01, x):
    hi, mid, lo = _split3(x)
    return _dot(m01, hi) + _dot(m01, mid) + _dot(m01, lo)


def _softplus(x):
    return jnp.maximum(x, 0.0) + jnp.log1p(jnp.exp(-jnp.abs(x)))


def _inproj_call(xp, xs, g, w, tm):
    tp, ts = xp.shape[0], xs.shape[0]
    n_p, n_s = tp // tm, ts // tm
    t_all = tp + ts
    segs = ((0, 512), (512, 1536), (1536, 2048), (2048, 2560), (2560, IN_PAD))

    def body(xp_ref, xs_ref, g_ref, w_ref, *outs):
        def run(x_ref):
            xn = (_rms(x_ref[...]) * g_ref[...]).astype(BF16)
            for (a, b), o in zip(segs, outs):
                o[...] = _dot(xn, w_ref[:, a:b])

        i = pl.program_id(0)

        @pl.when(i < n_p)
        def _():
            run(xp_ref)

        @pl.when(i >= n_p)
        def _():
            run(xs_ref)

    widths = [b - a for a, b in segs]
    return pl.pallas_call(
        body,
        out_shape=[jax.ShapeDtypeStruct((t_all, wd), F32) for wd in widths],
        grid=(n_p + n_s,),
        in_specs=[
            pl.BlockSpec((tm, D_MODEL), lambda i: (jnp.minimum(i, n_p - 1), 0)),
            pl.BlockSpec((tm, D_MODEL), lambda i: (jnp.maximum(i - n_p, 0), 0)),
            _const_spec((1, D_MODEL)),
            _const_spec((D_MODEL, IN_PAD)),
        ],
        out_specs=[pl.BlockSpec((tm, wd), lambda i: (i, 0)) for wd in widths],
        compiler_params=_cparams(("arbitrary",)),
        name="in_proj",
    )(xp, xs, g, w)


def _mixer_front(conv, dtr, dtb, alog, alog_x, rexp, tril, seg_ones):
    xact = conv * jax.nn.sigmoid(conv)
    xs = xact[:, :SSD_WIDTH]
    bm = xact[:, SSD_WIDTH:SSD_WIDTH + 256]
    cm = xact[:, SSD_WIDTH + 256:]
    dt = _softplus(dtr + dtb)
    a = dt * (-jnp.exp(alog))
    dt_x = _sel_right(dt, rexp)
    a_x = dt_x * (-jnp.exp(alog_x))
    acum = _sel_left(tril, a)
    acum_x = _sel_left(tril, a_x)
    if seg_ones is None:
        r = acum_x.shape[0]
        tot_x = jnp.broadcast_to(acum_x[r - 1:r, :], acum_x.shape)
    else:
        tot_x = _sel_left(seg_ones, a_x)
    return xs, bm, cm, dt_x, acum, acum_x, tot_x


def _ssd_intra(cmb, bmb, acum, xdt, mask):
    r = acum.shape[0]
    acum_t = acum.T
    lane = lax.broadcasted_iota(I32, (r, LANES), 1)
    low = lane < HEAD_DIM
    outs = []
    for g in range(SSD_GROUPS):
        sg = _dot_nt(cmb[:, LANES * g:LANES * (g + 1)], bmb[:, LANES * g:LANES * (g + 1)])
        for k in (2 * g, 2 * g + 1):
            parts = []
            for h in (2 * k, 2 * k + 1):
                seg = acum[:, h:h + 1] - acum_t[h:h + 1, :]
                parts.append((sg * jnp.exp(jnp.where(mask, seg, -jnp.inf))).astype(BF16))
            lhs = jnp.concatenate(parts, axis=1)
            xd = xdt[:, LANES * k:LANES * (k + 1)]
            rhs = jnp.concatenate([jnp.where(low, xd, 0.0), jnp.where(low, 0.0, xd)], axis=0).astype(BF16)
            outs.append(_dot(lhs, rhs))
    return jnp.concatenate(outs, axis=1)


def _mixer_back(y, z, u, v, sng, vng, vnb, wsp_ref, bsp, mog):
    r = y.shape[0]
    yg = y * (z * jax.nn.sigmoid(z))
    halves = []
    for g in range(SSD_GROUPS):
        t = yg[:, 256 * g:256 * (g + 1)]
        halves.append(_rms(t))
    yn = jnp.concatenate(halves, axis=1) * sng
    ug = jax.nn.gelu(u)
    vg = jax.nn.gelu(v)
    mu = jnp.mean(vg, axis=-1, keepdims=True)
    var = jnp.mean(jnp.square(vg - mu), axis=-1, keepdims=True)
    v_ln = (vg - mu) * lax.rsqrt(var + EPS) * vng + vnb
    lane = lax.broadcasted_iota(I32, (r, LANES), 1)
    low = lane < HEAD_DIM
    outs = []
    for k in range(MLP_HEADS // 2):
        vd = v_ln[:, LANES * k:LANES * (k + 1)]
        rhs = jnp.concatenate([jnp.where(low, vd, 0.0), jnp.where(low, 0.0, vd)], axis=0).astype(BF16)
        outs.append(_dot(wsp_ref[k], rhs))
    s = jnp.concatenate(outs, axis=1) + bsp
    m = _rms(ug * s) * mog
    return jnp.concatenate([yn, m], axis=1).astype(BF16), v_ln


_MIXER_PARAM_SHAPES = (
    (CONV_W, CONV_DIM), (1, CONV_DIM), (1, LANES), (1, LANES), (1, SSD_WIDTH), (LANES, SSD_WIDTH),
    (CHUNK, CHUNK), (1, SSD_WIDTH), (1, SSD_WIDTH), (1, MLP_WIDTH), (1, MLP_WIDTH),
    (MLP_HEADS // 2, CHUNK, 2 * CHUNK), (CHUNK, MLP_WIDTH), (1, MLP_WIDTH),
)


def _prompt_mixer_body(z_ref, xbc_ref, u_ref, v_ref, dt_ref,
                       cw_ref, cb_ref, dtb_ref, alog_ref, alogx_ref, rexp_ref, tril_ref, dskip_ref,
                       sng_ref, vng_ref, vnb_ref, wsp_ref, bsp_ref, mog_ref,
                       cat_ref, ssm_ref, ext_scr, s_scr):
    c = pl.program_id(1)
    r = CHUNK

    @pl.when(c == 0)
    def _():
        ext_scr[0:8, :] = jnp.zeros((8, CONV_DIM), F32)
        s_scr[...] = jnp.zeros_like(s_scr)

    x = xbc_ref[...]
    ext_scr[8:8 + r, :] = x
    cw = cw_ref[...]
    conv = (cb_ref[...] + cw[3:4] * x + cw[2:3] * ext_scr[7:7 + r, :]
            + cw[1:2] * ext_scr[6:6 + r, :] + cw[0:1] * ext_scr[5:5 + r, :])
    ext_scr[0:8, :] = x[r - 8:r, :]

    xs, bm, cm, dt_x, acum, acum_x, tot_x = _mixer_front(
        conv, dt_ref[...], dtb_ref[...], alog_ref[...], alogx_ref[...], rexp_ref[...], tril_ref[...], None)
    bmb, cmb = bm.astype(BF16), cm.astype(BF16)
    xdt = xs * dt_x
    row = lax.broadcasted_iota(I32, (r, r), 0)
    col = lax.broadcasted_iota(I32, (r, r), 1)
    y_diag = _ssd_intra(cmb, bmb, acum, xdt, row >= col)

    s_prev = s_scr[...]
    s_prev_b = s_prev.astype(BF16)
    y_off = jnp.concatenate(
        [_dot_nt(cmb[:, LANES * g:LANES * (g + 1)], s_prev_b[256 * g:256 * (g + 1), :]) for g in range(SSD_GROUPS)],
        axis=1)
    y = y_diag + y_off * jnp.exp(acum_x) + dskip_ref[...] * xs

    w_t = (xdt * jnp.exp(tot_x - acum_x)).T.astype(BF16)
    states = jnp.concatenate(
        [_dot(w_t[256 * g:256 * (g + 1), :], bmb[:, LANES * g:LANES * (g + 1)]) for g in range(SSD_GROUPS)], axis=0)
    s_new = s_prev * jnp.exp(tot_x).T + states
    s_scr[...] = s_new

    cat, _ = _mixer_back(y, z_ref[...], u_ref[...], v_ref[...], sng_ref[...], vng_ref[...], vnb_ref[...],
                         wsp_ref, bsp_ref[...], mog_ref[...])
    cat_ref[...] = cat

    @pl.when(c == pl.num_programs(1) - 1)
    def _():
        ssm_ref[0] = s_new


def _prompt_mixer_call(z, xbc, u, v, dtr, params, nb, nc):
    row = lambda b, c: (b * nc + c, 0)
    in_specs = [
        pl.BlockSpec((CHUNK, SSD_WIDTH), row), pl.BlockSpec((CHUNK, CONV_DIM), row),
        pl.BlockSpec((CHUNK, MLP_WIDTH), row), pl.BlockSpec((CHUNK, MLP_WIDTH), row),
        pl.BlockSpec((CHUNK, DT_PAD), row),
    ] + [_const_spec(s) for s in _MIXER_PARAM_SHAPES]
    return pl.pallas_call(
        _prompt_mixer_body,
        out_shape=[jax.ShapeDtypeStruct((nb * nc * CHUNK, D_MODEL), BF16),
                   jax.ShapeDtypeStruct((nb, SSD_WIDTH, D_STATE), F32)],
        grid=(nb, nc),
        in_specs=in_specs,
        out_specs=[pl.BlockSpec((CHUNK, D_MODEL), row),
                   pl.BlockSpec((1, SSD_WIDTH, D_STATE), lambda b, c: (b, 0, 0))],
        scratch_shapes=[pltpu.VMEM((CHUNK + 8, CONV_DIM), F32), pltpu.VMEM((SSD_WIDTH, D_STATE), F32)],
        compiler_params=_cparams(("arbitrary", "arbitrary")),
        name="prompt_mixer",
    )(z, xbc, u, v, dtr, *params)


def _sample_mixer_body(seq_len, z_ref, x0_ref, x1_ref, x2_ref, x3_ref, u_ref, v_ref, dt_ref, h_ref,
                       cw_ref, cb_ref, dtb_ref, alog_ref, alogx_ref, rexp_ref, tril_ref, dskip_ref,
                       sng_ref, vng_ref, vnb_ref, wsp_ref, bsp_ref, mog_ref, segones_ref,
                       cat_ref, vout_ref, hout_ref, cm_scr, bm_scr, wt_scr, dtt_scr, yoff_scr):
    r = CHUNK
    shift = seq_len.bit_length() - 1
    cw = cw_ref[...]
    conv = (cb_ref[...] + cw[3:4] * x0_ref[...] + cw[2:3] * x1_ref[...]
            + cw[1:2] * x2_ref[...] + cw[0:1] * x3_ref[...])
    xs, bm, cm, dt_x, acum, acum_x, tot_x = _mixer_front(
        conv, dt_ref[...], dtb_ref[...], alog_ref[...], alogx_ref[...], rexp_ref[...], tril_ref[...],
        segones_ref[...])
    bmb, cmb = bm.astype(BF16), cm.astype(BF16)
    xdt = xs * dt_x
    row = lax.broadcasted_iota(I32, (r, r), 0)
    col = lax.broadcasted_iota(I32, (r, r), 1)
    same = lax.shift_right_logical(row, shift) == lax.shift_right_logical(col, shift)
    y_diag = _ssd_intra(cmb, bmb, acum, xdt, same & (row >= col))

    cm_scr[...] = cm
    bm_scr[...] = bmb
    wt_scr[...] = (xdt * jnp.exp(tot_x - acum_x)).T
    dtt_scr[...] = jnp.exp(tot_x).T
    ones_b = jnp.ones((LANES, LANES), BF16)
    seqs_per_slab = 8 // seq_len

    def slab(j, carry):
        rows = pl.ds(pl.multiple_of(8 * j, 8), 8)
        cms = cm_scr[rows, :].astype(BF16)
        sub = lax.broadcasted_iota(I32, (8, 256), 0)
        lane = lax.broadcasted_iota(I32, (256, LANES), 1)
        for g in range(SSD_GROUPS):
            q_rows = slice(256 * g, 256 * (g + 1))
            acc = jnp.zeros((8, 256), F32)
            for q in range(seqs_per_slab):
                s = seqs_per_slab * j + q
                y_s = _dot_nt(cms[:, LANES * g:LANES * (g + 1)], h_ref[s, q_rows, :].astype(BF16))
                acc = jnp.where(lax.shift_right_logical(sub, shift) == q, y_s, acc)
            yoff_scr[rows, 256 * g:256 * (g + 1)] = acc
            for q in range(seqs_per_slab):
                s = seqs_per_slab * j + q
                w_sel = jnp.where(lax.shift_right_logical(lane, shift) == s, wt_scr[q_rows, :], 0.0).astype(BF16)
                st = _dot(w_sel, bm_scr[:, LANES * g:LANES * (g + 1)])
                d_sel = jnp.where(lane == s * seq_len, dtt_scr[q_rows, :], 0.0)
                hout_ref[s, q_rows, :] = h_ref[s, q_rows, :] * _sel_right(d_sel, ones_b) + st
        return carry

    lax.fori_loop(0, r // 8, slab, 0)

    y = y_diag + yoff_scr[...] * jnp.exp(acum_x) + dskip_ref[...] * xs
    cat, v_ln = _mixer_back(y, z_ref[...], u_ref[...], v_ref[...], sng_ref[...], vng_ref[...], vnb_ref[...],
                            wsp_ref, bsp_ref[...], mog_ref[...])
    cat_ref[...] = cat
    vout_ref[...] = v_ln


def _sample_mixer_call(z, x_shift, u, v, dtr, h0, params, seg_ones, row0, seq_len):
    ts = x_shift[0].shape[0]
    n = ts // CHUNK
    spt = CHUNK // seq_len
    off = lambda i: (row0 + i, 0)
    loc = lambda i: (i, 0)
    st3 = lambda i: (i, 0, 0)
    in_specs = (
        [pl.BlockSpec((CHUNK, SSD_WIDTH), off)]
        + [pl.BlockSpec((CHUNK, CONV_DIM), loc)] * 4
        + [pl.BlockSpec((CHUNK, MLP_WIDTH), off), pl.BlockSpec((CHUNK, MLP_WIDTH), off),
           pl.BlockSpec((CHUNK, DT_PAD), off), pl.BlockSpec((spt, SSD_WIDTH, D_STATE), st3)]
        + [_const_spec(s) for s in _MIXER_PARAM_SHAPES] + [_const_spec((CHUNK, CHUNK))])
    return pl.pallas_call(
        functools.partial(_sample_mixer_body, seq_len),
        out_shape=[jax.ShapeDtypeStruct((ts, D_MODEL), BF16), jax.ShapeDtypeStruct((ts, MLP_WIDTH), F32),
                   jax.ShapeDtypeStruct(h0.shape, F32)],
        grid=(n,),
        in_specs=in_specs,
        out_specs=[pl.BlockSpec((CHUNK, D_MODEL), loc), pl.BlockSpec((CHUNK, MLP_WIDTH), loc),
                   pl.BlockSpec((spt, SSD_WIDTH, D_STATE), st3)],
        scratch_shapes=[pltpu.VMEM((CHUNK, 256), F32), pltpu.VMEM((CHUNK, 256), BF16),
                        pltpu.VMEM((SSD_WIDTH, CHUNK), F32), pltpu.VMEM((SSD_WIDTH, CHUNK), F32),
                        pltpu.VMEM((CHUNK, SSD_WIDTH), F32)],
        compiler_params=_cparams(("arbitrary",)),
        name="sample_mixer",
    )(z, *x_shift, u, v, dtr, h0, *params, seg_ones)


def _out_router_call(cat_p, cat_s, xp, xs, w_out, g_moe, wr_hi, wr_lo, b_r, tm):
    tp, ts = xp.shape[0], xs.shape[0]
    n_p, n_s = tp // tm, ts // tm
    t_all = tp + ts

    def body(cp_ref, cs_ref, xp_ref, xs_ref, wo_ref, g_ref, wh_ref, wl_ref, br_ref,
             h1_ref, m_ref, eid_ref, gate_ref):
        def run(c_ref, x_ref):
            h1 = x_ref[...] + _dot(c_ref[...], wo_ref[...])
            h1_ref[...] = h1
            m = _rms(h1) * g_ref[...]
            for j in range(D_MODEL // LANES):
                m_ref[:, j, :] = m[:, LANES * j:LANES * (j + 1)]
            m_hi = m.astype(BF16)
            m_lo = (m - m_hi.astype(F32)).astype(BF16)
            logits = _dot(m_hi, wh_ref[...]) + _dot(m_lo, wh_ref[...]) + _dot(m_hi, wl_ref[...]) + br_ref[...]
            lane = lax.broadcasted_iota(I32, logits.shape, 1).astype(F32)
            work = logits
            vals, ids = [], []
            for _ in range(TOP_K):
                mx = jnp.max(work, axis=-1, keepdims=True)
                idx = jnp.min(jnp.where(work == mx, lane, float(LANES)), axis=-1, keepdims=True)
                vals.append(mx)
                ids.append(idx)
                work = jnp.where(lane == idx, -jnp.inf, work)
            ex = [jnp.exp(vv - vals[0]) for vv in vals]
            den = ex[0] + ex[1] + ex[2] + ex[3]
            eid = jnp.zeros(logits.shape, I32)
            gate = jnp.zeros(logits.shape, F32)
            for k in range(TOP_K):
                eid = jnp.where(lane == k, ids[k].astype(I32), eid)
                gate = jnp.where(lane == k, ex[k] / den, gate)
            eid_ref[...] = eid
            gate_ref[...] = gate

        i = pl.program_id(0)

        @pl.when(i < n_p)
        def _():
            run(cp_ref, xp_ref)

        @pl.when(i >= n_p)
        def _():
            run(cs_ref, xs_ref)

    pmap = lambda i: (jnp.minimum(i, n_p - 1), 0)
    smap = lambda i: (jnp.maximum(i - n_p, 0), 0)
    omap = lambda i: (i, 0)
    return pl.pallas_call(
        body,
        out_shape=[jax.ShapeDtypeStruct((t_all, D_MODEL), F32),
                   jax.ShapeDtypeStruct((t_all, D_MODEL // LANES, LANES), F32),
                   jax.ShapeDtypeStruct((t_all, LANES), I32), jax.ShapeDtypeStruct((t_all, LANES), F32)],
        grid=(n_p + n_s,),
        in_specs=[pl.BlockSpec((tm, D_MODEL), pmap), pl.BlockSpec((tm, D_MODEL), smap),
                  pl.BlockSpec((tm, D_MODEL), pmap), pl.BlockSpec((tm, D_MODEL), smap),
                  _const_spec((D_MODEL, D_MODEL)), _const_spec((1, D_MODEL)),
                  _const_spec((D_MODEL, LANES)), _const_spec((D_MODEL, LANES)), _const_spec((1, LANES))],
        out_specs=[pl.BlockSpec((tm, D_MODEL), omap),
                   pl.BlockSpec((tm, D_MODEL // LANES, LANES), lambda i: (i, 0, 0)),
                   pl.BlockSpec((tm, LANES), omap), pl.BlockSpec((tm, LANES), omap)],
        compiler_params=_cparams(("arbitrary",)),
        name="out_router",
    )(cat_p, cat_s, xp, xs, w_out, g_moe, wr_hi, wr_lo, b_r)


def _route(eid, tm, nb):
    t = eid.shape[0]
    tk = t * TOP_K
    flat = eid.reshape(tk)
    _, order = lax.sort((flat, jnp.arange(tk, dtype=I32)), num_keys=1, is_stable=True)
    counts = jnp.sum((flat[:, None] == jnp.arange(N_EXPERTS, dtype=I32)[None, :]).astype(I32), axis=0)
    nblk = (counts + tm - 1) // tm
    bend = jnp.cumsum(nblk)
    bstart = bend - nblk
    start = jnp.cumsum(counts) - counts
    nused = bend[-1]
    blk = jnp.arange(nb, dtype=I32)
    be = jnp.minimum(jnp.sum((blk[:, None] >= bend[None, :]).astype(I32), axis=1), N_EXPERTS - 1)
    be = jnp.where(blk < nused, be, be[jnp.maximum(nused - 1, 0)])
    used = blk < nused
    nval = jnp.where(used, jnp.clip(counts[be] - (blk - bstart[be]) * tm, 0, tm), 0).astype(I32)
    off = jnp.where(used, start[be] + (blk - bstart[be]) * tm, 0).astype(I32)
    pad = (-(-(tk + tm) // LANES) + _id_rows(tm)) * LANES - tk
    tok = jnp.pad(lax.shift_right_logical(order, TOPK_SHIFT), (0, pad))
    dst = jnp.pad((order & (TOP_K - 1)) * t + lax.shift_right_logical(order, TOPK_SHIFT), (0, pad))
    return be, nval, off, tok, dst


def _id_rows(tm):
    return tm // LANES + 1


def _moe_body(tm, t_all, nb, be_ref, nval_ref, off_ref, tok_hbm, dst_hbm, m_hbm, wup_ref, bup_ref, wdn_ref, bdn_ref,
              perm_ref, y_hbm, gids, sids, xbuf, ybuf, wup_b, wdn_b, isem, gsem, ssem):
    i = pl.program_id(0)
    nv = nval_ref[i]
    slot = i & 1
    nslot = 1 - slot
    prv = jnp.maximum(i - 1, 0)
    nxt = jnp.minimum(i + 1, nb - 1)
    nx2 = jnp.minimum(i + 2, nb - 1)
    has_next = (i + 1 < nb) & (nval_ref[nxt] > 0)
    has_next2 = (i + 2 < nb) & (nval_ref[nx2] > 0)
    n_prev = jnp.where(i > 0, nval_ref[prv], 0)
    win = _id_rows(tm) * LANES
    spare = TOP_K * t_all

    def ids_copies(b):
        start = pl.multiple_of(lax.shift_right_logical(off_ref[b], 7) * LANES, LANES)
        ring = pl.ds(pl.multiple_of((b & 3) * win, LANES), win)
        return (pltpu.make_async_copy(tok_hbm.at[pl.ds(start, win)], gids.at[ring], isem.at[b & 3, 0]),
                pltpu.make_async_copy(dst_hbm.at[pl.ds(start, win)], sids.at[ring], isem.at[b & 3, 1]))

    def id_base(b):
        return (b & 3) * win + (off_ref[b] & (LANES - 1))

    def gather_row(base, s, r):
        return pltpu.make_async_copy(m_hbm.at[gids[base + r]], xbuf.at[s, r], gsem.at[s])

    def scatter_row(base, s, r, n):
        dest = jnp.where(r < n, sids[base + r], spare + r)
        return pltpu.make_async_copy(ybuf.at[s, pl.ds(r, 1)], y_hbm.at[pl.ds(dest, 1)], ssem.at[s])

    def wait_gathers(s):
        pltpu.make_async_copy(m_hbm.at[pl.ds(0, tm)], xbuf.at[s], gsem.at[s]).wait()

    def wait_scatters(s):
        pltpu.make_async_copy(ybuf.at[s], y_hbm.at[pl.ds(0, tm)], ssem.at[s]).wait()

    def for_rows(fn):
        def one(r, c):
            fn(r)
            return c

        lax.fori_loop(0, tm, one, 0)

    @pl.when(nv > 0)
    def _():
        @pl.when(i == 0)
        def _():
            ybuf[...] = jnp.zeros_like(ybuf)
            fill = pltpu.make_async_copy(ybuf.at[0], y_hbm.at[pl.ds(spare, tm)], ssem.at[0])
            fill.start()
            fill.wait()
            for b in range(4):
                for cp in ids_copies(b):
                    cp.start()
                    cp.wait()
            base0 = id_base(0)
            for_rows(lambda r: gather_row(base0, 0, r).start())

        @pl.when(has_next & (i >= 3))
        def _():
            for cp in ids_copies(nxt):
                cp.wait()

        @pl.when(has_next2 & (i >= 2))
        def _():
            for cp in ids_copies(nx2):
                cp.start()

        @pl.when((i == 0) | (be_ref[i] != be_ref[prv]))
        def _():
            for jb in range(2 * D_FF // 256):
                cols = slice(256 * jb, 256 * (jb + 1))
                wup_b[:, cols] = _dot(wup_ref[0, :, cols].astype(BF16), perm_ref[...]).astype(BF16)
            wdn_b[...] = wdn_ref[0].astype(BF16)

        wait_gathers(slot)

        g_base = id_base(nxt)
        s_base = id_base(prv)
        for r in range(tm):
            gather_row(g_base, nslot, r).start()
            scatter_row(s_base, nslot, r, n_prev).start()

        x = jnp.concatenate([xbuf[slot, :, j, :] for j in range(D_MODEL // LANES)], axis=1).astype(BF16)
        acts = []
        for jb in range(D_FF // LANES):
            h = _dot(x, wup_b[:, 256 * jb:256 * (jb + 1)]) + bup_ref[0, :, 256 * jb:256 * (jb + 1)]
            gate = jnp.minimum(h[:, :LANES], SWIGLU_LIMIT)
            lin = jnp.clip(h[:, LANES:], -SWIGLU_LIMIT, SWIGLU_LIMIT)
            acts.append((gate * jax.nn.sigmoid(SWIGLU_ALPHA * gate) * (lin + 1.0)).astype(BF16))
        act = jnp.concatenate(acts, axis=1)
        for c in range(D_MODEL // 256):
            ybuf[slot, :, 256 * c:256 * (c + 1)] = (
                _dot(act, wdn_b[:, 256 * c:256 * (c + 1)]) + bdn_ref[0, :, 256 * c:256 * (c + 1)])

        wait_scatters(nslot)

        @pl.when(jnp.logical_not(has_next))
        def _():
            wait_gathers(nslot)
            last_base = id_base(i)
            for_rows(lambda r: scatter_row(last_base, slot, r, nv).start())
            wait_scatters(slot)


def _moe_call(m, be, nval, off, tok, dst, w_up, b_up_g, w_down, b_down, perm, tm, nb):
    t = m.shape[0]
    assert nb >= 4
    by_expert = lambda i, be, nv, off: (be[i], 0, 0)
    grid_spec = pltpu.PrefetchScalarGridSpec(
        num_scalar_prefetch=3,
        grid=(nb,),
        in_specs=[
            pl.BlockSpec(memory_space=pl.ANY),
            pl.BlockSpec(memory_space=pl.ANY),
            pl.BlockSpec(memory_space=pl.ANY),
            pl.BlockSpec((1, D_MODEL, 2 * D_FF), by_expert),
            pl.BlockSpec((1, 1, 2 * D_FF), by_expert),
            pl.BlockSpec((1, D_FF, D_MODEL), by_expert),
            pl.BlockSpec((1, 1, D_MODEL), by_expert),
            pl.BlockSpec((256, 256), lambda i, be, nv, off: (0, 0)),
        ],
        out_specs=pl.BlockSpec(memory_space=pl.ANY),
        scratch_shapes=[pltpu.SMEM((4 * _id_rows(tm) * LANES,), I32), pltpu.SMEM((4 * _id_rows(tm) * LANES,), I32),
                        pltpu.VMEM((2, tm, 8, LANES), F32), pltpu.VMEM((2, tm, D_MODEL), F32),
                        pltpu.VMEM((D_MODEL, 2 * D_FF), BF16), pltpu.VMEM((D_FF, D_MODEL), BF16),
                        pltpu.SemaphoreType.DMA((4, 2)), pltpu.SemaphoreType.DMA((2,)),
                        pltpu.SemaphoreType.DMA((2,))],
    )
    return pl.pallas_call(
        functools.partial(_moe_body, tm, t, nb),
        out_shape=jax.ShapeDtypeStruct((TOP_K * t + tm, D_MODEL), F32),
        grid_spec=grid_spec,
        compiler_params=_cparams(("arbitrary",)),
        name="moe_experts",
    )(be, nval, off, tok, dst, m, w_up, b_up_g, w_down, b_down, perm)


def _ple_call(h1, y4, gates, pp, ps, g_ple, w_gate, w_proj, g_final, tm):
    tp, ts = pp.shape[0], ps.shape[0]
    n_p, n_s = tp // tm, ts // tm
    ple = pp.shape[1]

    def body(h1_ref, y0_ref, y1_ref, y2_ref, y3_ref, gt_ref, pp_ref, ps_ref, g_ref, wg_ref, wp_ref, gf_ref,
             yp_ref, ys_ref):
        def run(p_ref, o_ref):
            gt = gt_ref[...]
            moe = None
            for k, y_ref in enumerate((y0_ref, y1_ref, y2_ref, y3_ref)):
                moe = gt[:, k:k + 1] * y_ref[...] if moe is None else moe + gt[:, k:k + 1] * y_ref[...]
            h2 = h1_ref[...] + moe
            a = (_rms(h2) * g_ref[...]).astype(BF16)
            gate = jax.nn.sigmoid(_dot(a, wg_ref[...]))
            pe = _dot(p_ref[...].astype(BF16), wp_ref[...])
            h3 = h2 + pe * gate
            o_ref[...] = _rms(h3) * gf_ref[...]

        i = pl.program_id(0)

        @pl.when(i < n_p)
        def _():
            run(pp_ref, yp_ref)

        @pl.when(i >= n_p)
        def _():
            run(ps_ref, ys_ref)

    pmap = lambda i: (jnp.minimum(i, n_p - 1), 0)
    smap = lambda i: (jnp.maximum(i - n_p, 0), 0)
    omap = lambda i: (i, 0)
    return pl.pallas_call(
        body,
        out_shape=[jax.ShapeDtypeStruct((tp, D_MODEL), F32), jax.ShapeDtypeStruct((ts, D_MODEL), F32)],
        grid=(n_p + n_s,),
        in_specs=[pl.BlockSpec((tm, D_MODEL), omap)]
                 + [pl.BlockSpec((tm, D_MODEL), functools.partial(lambda k, i: (k * (n_p + n_s) + i, 0), k))
                    for k in range(TOP_K)]
                 + [pl.BlockSpec((tm, LANES), omap), pl.BlockSpec((tm, ple), pmap), pl.BlockSpec((tm, ple), smap),
                  _const_spec((1, D_MODEL)), _const_spec((D_MODEL, D_MODEL)), _const_spec((ple, D_MODEL)),
                  _const_spec((1, D_MODEL))],
        out_specs=[pl.BlockSpec((tm, D_MODEL), pmap), pl.BlockSpec((tm, D_MODEL), smap)],
        compiler_params=_cparams(("arbitrary",)),
        name="ple_final",
    )(h1, y4, y4, y4, y4, gates, pp, ps, g_ple, w_gate, w_proj, g_final)


def _row(x, width=None):
    x = x.reshape(1, -1).astype(F32)
    if width is not None and x.shape[1] < width:
        x = jnp.pad(x, ((0, 0), (0, width - x.shape[1])))
    return x


def _mixer_params(conv_w, conv_b, dt_bias, a_log, d_skip, ssd_norm_g, v_norm_g, v_norm_b, w_spatial, b_spatial,
                  mlp_out_g, seq_len):
    n_seq = CHUNK // seq_len
    pos = jnp.arange(CHUNK) % seq_len
    same = (jnp.arange(CHUNK)[:, None] // seq_len) == (jnp.arange(CHUNK)[None, :] // seq_len)
    tril = (same & (jnp.arange(CHUNK)[:, None] >= jnp.arange(CHUNK)[None, :])).astype(BF16)
    rexp = (jnp.arange(LANES)[:, None] == (jnp.arange(SSD_WIDTH)[None, :] // HEAD_DIM)).astype(BF16)
    w_loc = jnp.tril(w_spatial[:, :seq_len, :seq_len])
    eye = jnp.eye(n_seq, dtype=F32)
    w_bd = jnp.einsum("st,hij->hsitj", eye, w_loc).reshape(MLP_HEADS, CHUNK, CHUNK)
    wsp = jnp.concatenate([w_bd[0::2], w_bd[1::2]], axis=2).astype(BF16)
    bsp = jnp.repeat(b_spatial[:, :seq_len].T[pos], MLP_WIDTH // MLP_HEADS, axis=1)
    params = (
        conv_w.astype(F32), _row(conv_b), _row(dt_bias, LANES), _row(a_log, LANES),
        _row(jnp.repeat(a_log, HEAD_DIM)), rexp, tril, _row(jnp.repeat(d_skip, HEAD_DIM)),
        _row(ssd_norm_g), _row(v_norm_g), _row(v_norm_b), wsp, bsp.astype(F32), _row(mlp_out_g),
    )
    return params, same.astype(BF16)


def _tile_rows(n):
    return 512 if n % 512 == 0 else CHUNK


def kernel(x_prompt, x_sample, state_ssm, state_conv, p_prompt, p_sample, norm_mix_g, w_in, conv_w, conv_b, dt_bias, a_log, d_skip, ssd_norm_g, v_norm_g, v_norm_b, w_spatial, b_spatial, mlp_out_g, w_out, norm_moe_g, w_router, b_router, w_up, b_up, w_down, b_down, norm_ple_g, w_ple_gate, w_ple_proj, norm_final_g):
    depth = norm_mix_g.shape[0]
    bp, lp, d = x_prompt.shape
    bs, ls, _ = x_sample.shape
    tp, ts = bp * lp, bs * ls
    assert depth == 1 and d == D_MODEL and lp % CHUNK == 0 and ts % CHUNK == 0 and 8 % ls == 0
    tm = _tile_rows(tp) if ts % _tile_rows(tp) == 0 else CHUNK
    t_all = tp + ts
    tm_moe = 256
    nb_moe = -(-t_all * TOP_K // tm_moe) + N_EXPERTS

    hp = x_prompt.reshape(tp, d)
    hs = x_sample.reshape(ts, d)
    ssm_p, conv_p, ssm_s, conv_s, v_s = [], [], [], [], []
    o1 = SSD_WIDTH
    o2 = o1 + CONV_DIM
    o3 = o2 + SSD_HEADS
    o4 = o3 + MLP_WIDTH
    c = jnp.arange(256)
    src = jnp.where(c < LANES, 2 * c, 2 * (c - LANES) + 1)
    perm = (jnp.arange(256)[:, None] == src[None, :]).astype(BF16)
    col = (jnp.arange(2 * D_FF) // 256) * 256 + src[jnp.arange(2 * D_FF) % 256]

    for i in range(depth):
        wi = w_in[i]
        w_cat = jnp.concatenate(
            [wi[:, :o2], wi[:, o3:], jnp.pad(wi[:, o2:o3], ((0, 0), (0, DT_PAD - SSD_HEADS)))], axis=1).astype(BF16)
        z, xbc, u, v, dtr = _inproj_call(hp, hs, _row(norm_mix_g[i]), w_cat, tm)

        mix_args = (conv_w[i], conv_b[i], dt_bias[i], a_log[i], d_skip[i], ssd_norm_g[i], v_norm_g[i], v_norm_b[i],
                    w_spatial[i], b_spatial[i], mlp_out_g[i])
        prm_p, _ = _mixer_params(*mix_args, seq_len=CHUNK)
        cat_p, s_p = _prompt_mixer_call(z, xbc, u, v, dtr, prm_p, bp, lp // CHUNK)
        ssm_p.append(s_p.reshape(bp, SSD_HEADS, HEAD_DIM, D_STATE).astype(state_ssm.dtype))
        conv_p.append(xbc[:tp].reshape(bp, lp, CONV_DIM)[:, lp - (CONV_W - 1):])

        prm_s, seg_ones = _mixer_params(*mix_args, seq_len=ls)
        xbc_s = xbc[tp:].reshape(bs, ls, CONV_DIM)
        xpad = jnp.concatenate([state_conv[i].astype(F32), xbc_s], axis=1)
        x_shift = [xpad[:, CONV_W - 1 - k:CONV_W - 1 - k + ls].reshape(ts, CONV_DIM) for k in range(CONV_W)]
        h0 = state_ssm[i].astype(F32).reshape(bs, SSD_WIDTH, D_STATE)
        cat_s, v_rows, s_s = _sample_mixer_call(z, x_shift, u, v, dtr, h0, prm_s, seg_ones, tp // CHUNK, ls)
        ssm_s.append(s_s.reshape(bs, SSD_HEADS, HEAD_DIM, D_STATE).astype(state_ssm.dtype))
        conv_s.append(xpad[:, ls:])
        v_s.append(v_rows.reshape(bs, ls, MLP_WIDTH))

        wr = jnp.pad(w_router[i].astype(F32), ((0, 0), (0, LANES - N_EXPERTS)))
        wr_hi = wr.astype(BF16)
        wr_lo = (wr - wr_hi.astype(F32)).astype(BF16)
        b_r = jnp.concatenate([b_router[i].astype(F32), jnp.full((LANES - N_EXPERTS,), -1e30, F32)]).reshape(1, LANES)
        h1, m, eid, gates = _out_router_call(cat_p, cat_s, hp, hs, w_out[i].astype(BF16), _row(norm_moe_g[i]),
                                             wr_hi, wr_lo, b_r, tm)

        be, nval, off, tok, dst = _route(eid[:, :TOP_K], tm_moe, nb_moe)
        b_up_g = b_up[i][:, col].reshape(N_EXPERTS, 1, 2 * D_FF)
        y4 = _moe_call(m, be, nval, off, tok, dst, w_up[i], b_up_g, w_down[i],
                       b_down[i].reshape(N_EXPERTS, 1, D_MODEL), perm, tm_moe, nb_moe)

        hp, hs = _ple_call(h1, y4, gates,
                           p_prompt[i].reshape(tp, -1), p_sample[i].reshape(ts, -1), _row(norm_ple_g[i]),
                           w_ple_gate[i].astype(BF16), w_ple_proj[i].astype(BF16), _row(norm_final_g), tm)

    y_prompt = hp.reshape(bp, lp, d)
    y_sample = hs.reshape(bs, ls, d)
    return (y_prompt, y_sample, jnp.stack(ssm_p), jnp.stack(conv_p), jnp.stack(ssm_s), jnp.stack(conv_s),
            jnp.stack(v_s))
```

```python
import functools

import jax
import jax.numpy as jnp
from jax import lax
from jax.experimental import pallas as pl
from jax.experimental.pallas import tpu as pltpu

F32 = jnp.float32
BF16 = jnp.bfloat16
I32 = jnp.int32

EPS = 1e-6
D_MODEL = 1024
SSD_WIDTH = 512
SSD_HEADS = 8
HEAD_DIM = 64
SSD_GROUPS = 2
D_STATE = 128
CONV_W = 4
CONV_DIM = SSD_WIDTH + 2 * SSD_GROUPS * D_STATE
MLP_WIDTH = 512
MLP_HEADS = 8
N_EXPERTS = 32
TOP_K = 4
D_FF = 1024
SWIGLU_LIMIT = 7.0
SWIGLU_ALPHA = 1.702
TOPK_SHIFT = 2
assert 1 << TOPK_SHIFT == TOP_K
LANES = 128
CHUNK = 128
DT_PAD = LANES
IN_PAD = SSD_WIDTH + CONV_DIM + 2 * MLP_WIDTH + DT_PAD
VMEM_LIMIT = 56 * 1024 * 1024


def _cparams(sem):
    return pltpu.CompilerParams(dimension_semantics=sem, vmem_limit_bytes=VMEM_LIMIT)


def _const_spec(shape):
    return pl.BlockSpec(shape, lambda *_: (0,) * len(shape))


def _rms(x):
    return x * lax.rsqrt(jnp.mean(x * x, axis=-1, keepdims=True) + EPS)


def _dot(a, b):
    return jnp.dot(a, b, preferred_element_type=F32)


def _dot_nt(a, b):
    return lax.dot_general(a, b, (((1,), (1,)), ((), ())), preferred_element_type=F32)


def _split3(x):
    hi = x.astype(BF16)
    r = x - hi.astype(F32)
    mid = r.astype(BF16)
    lo = (r - mid.astype(F32)).astype(BF16)
    return hi, mid, lo


def _sel_right(x, m01):
    hi, mid, lo = _split3(x)
    return _dot(hi, m01) + _dot(mid, m01) + _dot(lo, m01)


def _sel_left(m01, x):
    hi, mid, lo = _split3(x)
    return _dot(m01, hi) + _dot(m01, mid) + _dot(m01, lo)


def _softplus(x):
    return jnp.maximum(x, 0.0) + jnp.log1p(jnp.exp(-jnp.abs(x)))


def _inproj_call(xp, xs, g, w, tm):
    tp, ts = xp.shape[0], xs.shape[0]
    n_p, n_s = tp // tm, ts // tm
    t_all = tp + ts
    segs = ((0, 512), (512, 1536), (1536, 2048), (2048, 2560), (2560, IN_PAD))

    def body(xp_ref, xs_ref, g_ref, w_ref, *outs):
        def run(x_ref):
            xn = (_rms(x_ref[...]) * g_ref[...]).astype(BF16)
            for (a, b), o in zip(segs, outs):
                o[...] = _dot(xn, w_ref[:, a:b])

        i = pl.program_id(0)

        @pl.when(i < n_p)
        def _():
            run(xp_ref)

        @pl.when(i >= n_p)
        def _():
            run(xs_ref)

    widths = [b - a for a, b in segs]
    return pl.pallas_call(
        body,
        out_shape=[jax.ShapeDtypeStruct((t_all, wd), F32) for wd in widths],
        grid=(n_p + n_s,),
        in_specs=[
            pl.BlockSpec((tm, D_MODEL), lambda i: (jnp.minimum(i, n_p - 1), 0)),
            pl.BlockSpec((tm, D_MODEL), lambda i: (jnp.maximum(i - n_p, 0), 0)),
            _const_spec((1, D_MODEL)),
            _const_spec((D_MODEL, IN_PAD)),
        ],
        out_specs=[pl.BlockSpec((tm, wd), lambda i: (i, 0)) for wd in widths],
        compiler_params=_cparams(("arbitrary",)),
        name="in_proj",
    )(xp, xs, g, w)


def _mixer_front(conv, dtr, dtb, alog, alog_x, rexp, tril, seg_ones):
    xact = conv * jax.nn.sigmoid(conv)
    xs = xact[:, :SSD_WIDTH]
    bm = xact[:, SSD_WIDTH:SSD_WIDTH + 256]
    cm = xact[:, SSD_WIDTH + 256:]
    dt = _softplus(dtr + dtb)
    a = dt * (-jnp.exp(alog))
    dt_x = _sel_right(dt, rexp)
    a_x = dt_x * (-jnp.exp(alog_x))
    acum = _sel_left(tril, a)
    acum_x = _sel_left(tril, a_x)
    if seg_ones is None:
        r = acum_x.shape[0]
        tot_x = jnp.broadcast_to(acum_x[r - 1:r, :], acum_x.shape)
    else:
        tot_x = _sel_left(seg_ones, a_x)
    return xs, bm, cm, dt_x, acum, acum_x, tot_x


def _ssd_intra(cmb, bmb, acum, xdt, mask):
    r = acum.shape[0]
    acum_t = acum.T
    lane = lax.broadcasted_iota(I32, (r, LANES), 1)
    low = lane < HEAD_DIM
    outs = []
    for g in range(SSD_GROUPS):
        sg = _dot_nt(cmb[:, LANES * g:LANES * (g + 1)], bmb[:, LANES * g:LANES * (g + 1)])
        for k in (2 * g, 2 * g + 1):
            parts = []
            for h in (2 * k, 2 * k + 1):
                seg = acum[:, h:h + 1] - acum_t[h:h + 1, :]
                parts.append((sg * jnp.exp(jnp.where(mask, seg, -jnp.inf))).astype(BF16))
            lhs = jnp.concatenate(parts, axis=1)
            xd = xdt[:, LANES * k:LANES * (k + 1)]
            rhs = jnp.concatenate([jnp.where(low, xd, 0.0), jnp.where(low, 0.0, xd)], axis=0).astype(BF16)
            outs.append(_dot(lhs, rhs))
    return jnp.concatenate(outs, axis=1)


def _mixer_back(y, z, u, v, sng, vng, vnb, wsp_ref, bsp, mog):
    r = y.shape[0]
    yg = y * (z * jax.nn.sigmoid(z))
    halves = []
    for g in range(SSD_GROUPS):
        t = yg[:, 256 * g:256 * (g + 1)]
        halves.append(_rms(t))
    yn = jnp.concatenate(halves, axis=1) * sng
    ug = jax.nn.gelu(u)
    vg = jax.nn.gelu(v)
    mu = jnp.mean(vg, axis=-1, keepdims=True)
    var = jnp.mean(jnp.square(vg - mu), axis=-1, keepdims=True)
    v_ln = (vg - mu) * lax.rsqrt(var + EPS) * vng + vnb
    lane = lax.broadcasted_iota(I32, (r, LANES), 1)
    low = lane < HEAD_DIM
    outs = []
    for k in range(MLP_HEADS // 2):
        vd = v_ln[:, LANES * k:LANES * (k + 1)]
        rhs = jnp.concatenate([jnp.where(low, vd, 0.0), jnp.where(low, 0.0, vd)], axis=0).astype(BF16)
        outs.append(_dot(wsp_ref[k], rhs))
    s = jnp.concatenate(outs, axis=1) + bsp
    m = _rms(ug * s) * mog
    return jnp.concatenate([yn, m], axis=1).astype(BF16), v_ln


_MIXER_PARAM_SHAPES = (
    (CONV_W, CONV_DIM), (1, CONV_DIM), (1, LANES), (1, LANES), (1, SSD_WIDTH), (LANES, SSD_WIDTH),
    (CHUNK, CHUNK), (1, SSD_WIDTH), (1, SSD_WIDTH), (1, MLP_WIDTH), (1, MLP_WIDTH),
    (MLP_HEADS // 2, CHUNK, 2 * CHUNK), (CHUNK, MLP_WIDTH), (1, MLP_WIDTH),
)


def _prompt_mixer_body(z_ref, xbc_ref, u_ref, v_ref, dt_ref,
                       cw_ref, cb_ref, dtb_ref, alog_ref, alogx_ref, rexp_ref, tril_ref, dskip_ref,
                       sng_ref, vng_ref, vnb_ref, wsp_ref, bsp_ref, mog_ref,
                       cat_ref, ssm_ref, ext_scr, s_scr):
    c = pl.program_id(1)
    r = CHUNK

    @pl.when(c == 0)
    def _():
        ext_scr[0:8, :] = jnp.zeros((8, CONV_DIM), F32)
        s_scr[...] = jnp.zeros_like(s_scr)

    x = xbc_ref[...]
    ext_scr[8:8 + r, :] = x
    cw = cw_ref[...]
    conv = (cb_ref[...] + cw[3:4] * x + cw[2:3] * ext_scr[7:7 + r, :]
            + cw[1:2] * ext_scr[6:6 + r, :] + cw[0:1] * ext_scr[5:5 + r, :])
    ext_scr[0:8, :] = x[r - 8:r, :]

    xs, bm, cm, dt_x, acum, acum_x, tot_x = _mixer_front(
        conv, dt_ref[...], dtb_ref[...], alog_ref[...], alogx_ref[...], rexp_ref[...], tril_ref[...], None)
    bmb, cmb = bm.astype(BF16), cm.astype(BF16)
    xdt = xs * dt_x
    row = lax.broadcasted_iota(I32, (r, r), 0)
    col = lax.broadcasted_iota(I32, (r, r), 1)
    y_diag = _ssd_intra(cmb, bmb, acum, xdt, row >= col)

    s_prev = s_scr[...]
    s_prev_b = s_prev.astype(BF16)
    y_off = jnp.concatenate(
        [_dot_nt(cmb[:, LANES * g:LANES * (g + 1)], s_prev_b[256 * g:256 * (g + 1), :]) for g in range(SSD_GROUPS)],
        axis=1)
    y = y_diag + y_off * jnp.exp(acum_x) + dskip_ref[...] * xs

    w_t = (xdt * jnp.exp(tot_x - acum_x)).T.astype(BF16)
    states = jnp.concatenate(
        [_dot(w_t[256 * g:256 * (g + 1), :], bmb[:, LANES * g:LANES * (g + 1)]) for g in range(SSD_GROUPS)], axis=0)
    s_new = s_prev * jnp.exp(tot_x).T + states
    s_scr[...] = s_new

    cat, _ = _mixer_back(y, z_ref[...], u_ref[...], v_ref[...], sng_ref[...], vng_ref[...], vnb_ref[...],
                         wsp_ref, bsp_ref[...], mog_ref[...])
    cat_ref[...] = cat

    @pl.when(c == pl.num_programs(1) - 1)
    def _():
        ssm_ref[0] = s_new


def _prompt_mixer_call(z, xbc, u, v, dtr, params, nb, nc):
    row = lambda b, c: (b * nc + c, 0)
    in_specs = [
        pl.BlockSpec((CHUNK, SSD_WIDTH), row), pl.BlockSpec((CHUNK, CONV_DIM), row),
        pl.BlockSpec((CHUNK, MLP_WIDTH), row), pl.BlockSpec((CHUNK, MLP_WIDTH), row),
        pl.BlockSpec((CHUNK, DT_PAD), row),
    ] + [_const_spec(s) for s in _MIXER_PARAM_SHAPES]
    return pl.pallas_call(
        _prompt_mixer_body,
        out_shape=[jax.ShapeDtypeStruct((nb * nc * CHUNK, D_MODEL), BF16),
                   jax.ShapeDtypeStruct((nb, SSD_WIDTH, D_STATE), F32)],
        grid=(nb, nc),
        in_specs=in_specs,
        out_specs=[pl.BlockSpec((CHUNK, D_MODEL), row),
                   pl.BlockSpec((1, SSD_WIDTH, D_STATE), lambda b, c: (b, 0, 0))],
        scratch_shapes=[pltpu.VMEM((CHUNK + 8, CONV_DIM), F32), pltpu.VMEM((SSD_WIDTH, D_STATE), F32)],
        compiler_params=_cparams(("arbitrary", "arbitrary")),
        name="prompt_mixer",
    )(z, xbc, u, v, dtr, *params)


def _sample_mixer_body(seq_len, z_ref, x0_ref, x1_ref, x2_ref, x3_ref, u_ref, v_ref, dt_ref, h_ref,
                       cw_ref, cb_ref, dtb_ref, alog_ref, alogx_ref, rexp_ref, tril_ref, dskip_ref,
                       sng_ref, vng_ref, vnb_ref, wsp_ref, bsp_ref, mog_ref, segones_ref,
                       cat_ref, vout_ref, hout_ref, cm_scr, bm_scr, wt_scr, dtt_scr, yoff_scr):
    r = CHUNK
    shift = seq_len.bit_length() - 1
    cw = cw_ref[...]
    conv = (cb_ref[...] + cw[3:4] * x0_ref[...] + cw[2:3] * x1_ref[...]
            + cw[1:2] * x2_ref[...] + cw[0:1] * x3_ref[...])
    xs, bm, cm, dt_x, acum, acum_x, tot_x = _mixer_front(
        conv, dt_ref[...], dtb_ref[...], alog_ref[...], alogx_ref[...], rexp_ref[...], tril_ref[...],
        segones_ref[...])
    bmb, cmb = bm.astype(BF16), cm.astype(BF16)
    xdt = xs * dt_x
    row = lax.broadcasted_iota(I32, (r, r), 0)
    col = lax.broadcasted_iota(I32, (r, r), 1)
    same = lax.shift_right_logical(row, shift) == lax.shift_right_logical(col, shift)
    y_diag = _ssd_intra(cmb, bmb, acum, xdt, same & (row >= col))

    cm_scr[...] = cm
    bm_scr[...] = bmb
    wt_scr[...] = (xdt * jnp.exp(tot_x - acum_x)).T
    dtt_scr[...] = jnp.exp(tot_x).T
    ones_b = jnp.ones((LANES, LANES), BF16)
    seqs_per_slab = 8 // seq_len

    def slab(j, carry):
        rows = pl.ds(pl.multiple_of(8 * j, 8), 8)
        cms = cm_scr[rows, :].astype(BF16)
        sub = lax.broadcasted_iota(I32, (8, 256), 0)
        lane = lax.broadcasted_iota(I32, (256, LANES), 1)
        for g in range(SSD_GROUPS):
            q_rows = slice(256 * g, 256 * (g + 1))
            acc = jnp.zeros((8, 256), F32)
            for q in range(seqs_per_slab):
                s = seqs_per_slab * j + q
                y_s = _dot_nt(cms[:, LANES * g:LANES * (g + 1)], h_ref[s, q_rows, :].astype(BF16))
                acc = jnp.where(lax.shift_right_logical(sub, shift) == q, y_s, acc)
            yoff_scr[rows, 256 * g:256 * (g + 1)] = acc
            for q in range(seqs_per_slab):
                s = seqs_per_slab * j + q
                w_sel = jnp.where(lax.shift_right_logical(lane, shift) == s, wt_scr[q_rows, :], 0.0).astype(BF16)
                st = _dot(w_sel, bm_scr[:, LANES * g:LANES * (g + 1)])
                d_sel = jnp.where(lane == s * seq_len, dtt_scr[q_rows, :], 0.0)
                hout_ref[s, q_rows, :] = h_ref[s, q_rows, :] * _sel_right(d_sel, ones_b) + st
        return carry

    lax.fori_loop(0, r // 8, slab, 0)

    y = y_diag + yoff_scr[...] * jnp.exp(acum_x) + dskip_ref[...] * xs
    cat, v_ln = _mixer_back(y, z_ref[...], u_ref[...], v_ref[...], sng_ref[...], vng_ref[...], vnb_ref[...],
                            wsp_ref, bsp_ref[...], mog_ref[...])
    cat_ref[...] = cat
    vout_ref[...] = v_ln


def _sample_mixer_call(z, x_shift, u, v, dtr, h0, params, seg_ones, row0, seq_len):
    ts = x_shift[0].shape[0]
    n = ts // CHUNK
    spt = CHUNK // seq_len
    off = lambda i: (row0 + i, 0)
    loc = lambda i: (i, 0)
    st3 = lambda i: (i, 0, 0)
    in_specs = (
        [pl.BlockSpec((CHUNK, SSD_WIDTH), off)]
        + [pl.BlockSpec((CHUNK, CONV_DIM), loc)] * 4
        + [pl.BlockSpec((CHUNK, MLP_WIDTH), off), pl.BlockSpec((CHUNK, MLP_WIDTH), off),
           pl.BlockSpec((CHUNK, DT_PAD), off), pl.BlockSpec((spt, SSD_WIDTH, D_STATE), st3)]
        + [_const_spec(s) for s in _MIXER_PARAM_SHAPES] + [_const_spec((CHUNK, CHUNK))])
    return pl.pallas_call(
        functools.partial(_sample_mixer_body, seq_len),
        out_shape=[jax.ShapeDtypeStruct((ts, D_MODEL), BF16), jax.ShapeDtypeStruct((ts, MLP_WIDTH), F32),
                   jax.ShapeDtypeStruct(h0.shape, F32)],
        grid=(n,),
        in_specs=in_specs,
        out_specs=[pl.BlockSpec((CHUNK, D_MODEL), loc), pl.BlockSpec((CHUNK, MLP_WIDTH), loc),
                   pl.BlockSpec((spt, SSD_WIDTH, D_STATE), st3)],
        scratch_shapes=[pltpu.VMEM((CHUNK, 256), F32), pltpu.VMEM((CHUNK, 256), BF16),
                        pltpu.VMEM((SSD_WIDTH, CHUNK), F32), pltpu.VMEM((SSD_WIDTH, CHUNK), F32),
                        pltpu.VMEM((CHUNK, SSD_WIDTH), F32)],
        compiler_params=_cparams(("arbitrary",)),
        name="sample_mixer",
    )(z, *x_shift, u, v, dtr, h0, *params, seg_ones)


def _out_router_call(cat_p, cat_s, xp, xs, w_out, g_moe, wr_hi, wr_lo, b_r, tm):
    tp, ts = xp.shape[0], xs.shape[0]
    n_p, n_s = tp // tm, ts // tm
    t_all = tp + ts

    def body(cp_ref, cs_ref, xp_ref, xs_ref, wo_ref, g_ref, wh_ref, wl_ref, br_ref,
             h1_ref, m_ref, eid_ref, gate_ref):
        def run(c_ref, x_ref):
            h1 = x_ref[...] + _dot(c_ref[...], wo_ref[...])
            h1_ref[...] = h1
            m = _rms(h1) * g_ref[...]
            for j in range(D_MODEL // LANES):
                m_ref[:, j, :] = m[:, LANES * j:LANES * (j + 1)]
            m_hi = m.astype(BF16)
            m_lo = (m - m_hi.astype(F32)).astype(BF16)
            logits = _dot(m_hi, wh_ref[...]) + _dot(m_lo, wh_ref[...]) + _dot(m_hi, wl_ref[...]) + br_ref[...]
            lane = lax.broadcasted_iota(I32, logits.shape, 1).astype(F32)
            work = logits
            vals, ids = [], []
            for _ in range(TOP_K):
                mx = jnp.max(work, axis=-1, keepdims=True)
                idx = jnp.min(jnp.where(work == mx, lane, float(LANES)), axis=-1, keepdims=True)
                vals.append(mx)
                ids.append(idx)
                work = jnp.where(lane == idx, -jnp.inf, work)
            ex = [jnp.exp(vv - vals[0]) for vv in vals]
            den = ex[0] + ex[1] + ex[2] + ex[3]
            eid = jnp.zeros(logits.shape, I32)
            gate = jnp.zeros(logits.shape, F32)
            for k in range(TOP_K):
                eid = jnp.where(lane == k, ids[k].astype(I32), eid)
                gate = jnp.where(lane == k, ex[k] / den, gate)
            eid_ref[...] = eid
            gate_ref[...] = gate

        i = pl.program_id(0)

        @pl.when(i < n_p)
        def _():
            run(cp_ref, xp_ref)

        @pl.when(i >= n_p)
        def _():
            run(cs_ref, xs_ref)

    pmap = lambda i: (jnp.minimum(i, n_p - 1), 0)
    smap = lambda i: (jnp.maximum(i - n_p, 0), 0)
    omap = lambda i: (i, 0)
    return pl.pallas_call(
        body,
        out_shape=[jax.ShapeDtypeStruct((t_all, D_MODEL), F32),
                   jax.ShapeDtypeStruct((t_all, D_MODEL // LANES, LANES), F32),
                   jax.ShapeDtypeStruct((t_all, LANES), I32), jax.ShapeDtypeStruct((t_all, LANES), F32)],
        grid=(n_p + n_s,),
        in_specs=[pl.BlockSpec((tm, D_MODEL), pmap), pl.BlockSpec((tm, D_MODEL), smap),
                  pl.BlockSpec((tm, D_MODEL), pmap), pl.BlockSpec((tm, D_MODEL), smap),
                  _const_spec((D_MODEL, D_MODEL)), _const_spec((1, D_MODEL)),
                  _const_spec((D_MODEL, LANES)), _const_spec((D_MODEL, LANES)), _const_spec((1, LANES))],
        out_specs=[pl.BlockSpec((tm, D_MODEL), omap),
                   pl.BlockSpec((tm, D_MODEL // LANES, LANES), lambda i: (i, 0, 0)),
                   pl.BlockSpec((tm, LANES), omap), pl.BlockSpec((tm, LANES), omap)],
        compiler_params=_cparams(("arbitrary",)),
        name="out_router",
    )(cat_p, cat_s, xp, xs, w_out, g_moe, wr_hi, wr_lo, b_r)


def _route(eid, tm, nb):
    t = eid.shape[0]
    tk = t * TOP_K
    flat = eid.reshape(tk)
    _, order = lax.sort((flat, jnp.arange(tk, dtype=I32)), num_keys=1, is_stable=True)
    counts = jnp.sum((flat[:, None] == jnp.arange(N_EXPERTS, dtype=I32)[None, :]).astype(I32), axis=0)
    nblk = (counts + tm - 1) // tm
    bend = jnp.cumsum(nblk)
    bstart = bend - nblk
    start = jnp.cumsum(counts) - counts
    nused = bend[-1]
    blk = jnp.arange(nb, dtype=I32)
    used = blk < nused
    be = jnp.minimum(jnp.sum((jnp.minimum(blk, nused - 1)[:, None] >= bend[None, :]).astype(I32), axis=1),
                     N_EXPERTS - 1)
    sel = (be[:, None] == jnp.arange(N_EXPERTS, dtype=I32)[None, :]).astype(I32)
    pick = lambda v: jnp.sum(sel * v[None, :], axis=1)
    done = (blk - pick(bstart)) * tm
    nval = jnp.where(used, jnp.clip(pick(counts) - done, 0, tm), 0).astype(I32)
    off = jnp.where(used, pick(start) + done, 0).astype(I32)
    pad = (-(-(tk + tm) // LANES) + _id_rows(tm)) * LANES - tk
    tok = jnp.pad(lax.shift_right_logical(order, TOPK_SHIFT), (0, pad))
    dst = jnp.pad((order & (TOP_K - 1)) * t + lax.shift_right_logical(order, TOPK_SHIFT), (0, pad))
    return be, nval, off, tok, dst


def _id_rows(tm):
    return tm // LANES + 1


def _moe_body(tm, t_all, nb, be_ref, nval_ref, off_ref, tok_hbm, dst_hbm, m_hbm, wup_ref, bup_ref, wdn_ref, bdn_ref,
              perm_ref, y_hbm, gids, sids, xbuf, ybuf, wup_b, wdn_b, isem, gsem, ssem):
    i = pl.program_id(0)
    nv = nval_ref[i]
    slot = i & 1
    nslot = 1 - slot
    prv = jnp.maximum(i - 1, 0)
    nxt = jnp.minimum(i + 1, nb - 1)
    nx2 = jnp.minimum(i + 2, nb - 1)
    has_next = (i + 1 < nb) & (nval_ref[nxt] > 0)
    has_next2 = (i + 2 < nb) & (nval_ref[nx2] > 0)
    n_prev = jnp.where(i > 0, nval_ref[prv], 0)
    win = _id_rows(tm) * LANES
    spare = TOP_K * t_all

    def ids_copies(b):
        start = pl.multiple_of(lax.shift_right_logical(off_ref[b], 7) * LANES, LANES)
        ring = pl.ds(pl.multiple_of((b & 3) * win, LANES), win)
        return (pltpu.make_async_copy(tok_hbm.at[pl.ds(start, win)], gids.at[ring], isem.at[b & 3, 0]),
                pltpu.make_async_copy(dst_hbm.at[pl.ds(start, win)], sids.at[ring], isem.at[b & 3, 1]))

    def id_base(b):
        return (b & 3) * win + (off_ref[b] & (LANES - 1))

    def gather_row(base, s, r):
        return pltpu.make_async_copy(m_hbm.at[gids[base + r]], xbuf.at[s, r], gsem.at[s])

    def scatter_row(base, s, r, n):
        dest = jnp.where(r < n, sids[base + r], spare + r)
        return pltpu.make_async_copy(ybuf.at[s, pl.ds(r, 1)], y_hbm.at[pl.ds(dest, 1)], ssem.at[s])

    def wait_gathers(s):
        pltpu.make_async_copy(m_hbm.at[pl.ds(0, tm)], xbuf.at[s], gsem.at[s]).wait()

    def wait_scatters(s):
        pltpu.make_async_copy(ybuf.at[s], y_hbm.at[pl.ds(0, tm)], ssem.at[s]).wait()

    def for_rows(fn):
        def one(r, c):
            fn(r)
            return c

        lax.fori_loop(0, tm, one, 0)

    @pl.when(nv > 0)
    def _():
        @pl.when(i == 0)
        def _():
            ybuf[...] = jnp.zeros_like(ybuf)
            fill = pltpu.make_async_copy(ybuf.at[0], y_hbm.at[pl.ds(spare, tm)], ssem.at[0])
            fill.start()
            fill.wait()
            for b in range(4):
                for cp in ids_copies(b):
                    cp.start()
                    cp.wait()
            base0 = id_base(0)
            for_rows(lambda r: gather_row(base0, 0, r).start())

        @pl.when(has_next & (i >= 3))
        def _():
            for cp in ids_copies(nxt):
                cp.wait()

        @pl.when(has_next2 & (i >= 2))
        def _():
            for cp in ids_copies(nx2):
                cp.start()

        @pl.when((i == 0) | (be_ref[i] != be_ref[prv]))
        def _():
            for jb in range(2 * D_FF // 256):
                cols = slice(256 * jb, 256 * (jb + 1))
                wup_b[:, cols] = _dot(wup_ref[0, :, cols].astype(BF16), perm_ref[...]).astype(BF16)
            wdn_b[...] = wdn_ref[0].astype(BF16)

        wait_gathers(slot)

        @pl.when(i >= 1)
        def _():
            wait_scatters(slot)

        g_base = id_base(nxt)
        s_base = id_base(prv)
        for r in range(tm):
            gather_row(g_base, nslot, r).start()
            scatter_row(s_base, nslot, r, n_prev).start()

        x = jnp.concatenate([xbuf[slot, :, j, :] for j in range(D_MODEL // LANES)], axis=1).astype(BF16)
        acts = []
        for jb in range(D_FF // LANES):
            h = _dot(x, wup_b[:, 256 * jb:256 * (jb + 1)]) + bup_ref[0, :, 256 * jb:256 * (jb + 1)]
            gate = jnp.minimum(h[:, :LANES], SWIGLU_LIMIT)
            lin = jnp.clip(h[:, LANES:], -SWIGLU_LIMIT, SWIGLU_LIMIT)
            acts.append((gate * jax.nn.sigmoid(SWIGLU_ALPHA * gate) * (lin + 1.0)).astype(BF16))
        act = jnp.concatenate(acts, axis=1)
        for c in range(D_MODEL // 256):
            ybuf[slot, :, 256 * c:256 * (c + 1)] = (
                _dot(act, wdn_b[:, 256 * c:256 * (c + 1)]) + bdn_ref[0, :, 256 * c:256 * (c + 1)])

        @pl.when(jnp.logical_not(has_next))
        def _():
            wait_gathers(nslot)
            wait_scatters(nslot)
            last_base = id_base(i)
            for_rows(lambda r: scatter_row(last_base, slot, r, nv).start())
            wait_scatters(slot)


def _moe_call(m, be, nval, off, tok, dst, w_up, b_up_g, w_down, b_down, perm, tm, nb):
    t = m.shape[0]
    assert nb >= 4
    by_expert = lambda i, be, nv, off: (be[i], 0, 0)
    grid_spec = pltpu.PrefetchScalarGridSpec(
        num_scalar_prefetch=3,
        grid=(nb,),
        in_specs=[
            pl.BlockSpec(memory_space=pl.ANY),
            pl.BlockSpec(memory_space=pl.ANY),
            pl.BlockSpec(memory_space=pl.ANY),
            pl.BlockSpec((1, D_MODEL, 2 * D_FF), by_expert),
            pl.BlockSpec((1, 1, 2 * D_FF), by_expert),
            pl.BlockSpec((1, D_FF, D_MODEL), by_expert),
            pl.BlockSpec((1, 1, D_MODEL), by_expert),
            pl.BlockSpec((256, 256), lambda i, be, nv, off: (0, 0)),
        ],
        out_specs=pl.BlockSpec(memory_space=pl.ANY),
        scratch_shapes=[pltpu.SMEM((4 * _id_rows(tm) * LANES,), I32), pltpu.SMEM((4 * _id_rows(tm) * LANES,), I32),
                        pltpu.VMEM((2, tm, 8, LANES), F32), pltpu.VMEM((2, tm, D_MODEL), F32),
                        pltpu.VMEM((D_MODEL, 2 * D_FF), BF16), pltpu.VMEM((D_FF, D_MODEL), BF16),
                        pltpu.SemaphoreType.DMA((4, 2)), pltpu.SemaphoreType.DMA((2,)),
                        pltpu.SemaphoreType.DMA((2,))],
    )
    return pl.pallas_call(
        functools.partial(_moe_body, tm, t, nb),
        out_shape=jax.ShapeDtypeStruct((TOP_K * t + tm, D_MODEL), F32),
        grid_spec=grid_spec,
        compiler_params=_cparams(("arbitrary",)),
        name="moe_experts",
    )(be, nval, off, tok, dst, m, w_up, b_up_g, w_down, b_down, perm)


def _ple_call(h1, y4, gates, pp, ps, g_ple, w_gate, w_proj, g_final, tm):
    tp, ts = pp.shape[0], ps.shape[0]
    n_p, n_s = tp // tm, ts // tm
    ple = pp.shape[1]

    def body(h1_ref, y0_ref, y1_ref, y2_ref, y3_ref, gt_ref, pp_ref, ps_ref, g_ref, wg_ref, wp_ref, gf_ref,
             yp_ref, ys_ref):
        def run(p_ref, o_ref):
            gt = gt_ref[...]
            moe = None
            for k, y_ref in enumerate((y0_ref, y1_ref, y2_ref, y3_ref)):
                moe = gt[:, k:k + 1] * y_ref[...] if moe is None else moe + gt[:, k:k + 1] * y_ref[...]
            h2 = h1_ref[...] + moe
            a = (_rms(h2) * g_ref[...]).astype(BF16)
            gate = jax.nn.sigmoid(_dot(a, wg_ref[...]))
            pe = _dot(p_ref[...].astype(BF16), wp_ref[...])
            h3 = h2 + pe * gate
            o_ref[...] = _rms(h3) * gf_ref[...]

        i = pl.program_id(0)

        @pl.when(i < n_p)
        def _():
            run(pp_ref, yp_ref)

        @pl.when(i >= n_p)
        def _():
            run(ps_ref, ys_ref)

    pmap = lambda i: (jnp.minimum(i, n_p - 1), 0)
    smap = lambda i: (jnp.maximum(i - n_p, 0), 0)
    omap = lambda i: (i, 0)
    return pl.pallas_call(
        body,
        out_shape=[jax.ShapeDtypeStruct((tp, D_MODEL), F32), jax.ShapeDtypeStruct((ts, D_MODEL), F32)],
        grid=(n_p + n_s,),
        in_specs=[pl.BlockSpec((tm, D_MODEL), omap)]
                 + [pl.BlockSpec((tm, D_MODEL), functools.partial(lambda k, i: (k * (n_p + n_s) + i, 0), k))
                    for k in range(TOP_K)]
                 + [pl.BlockSpec((tm, LANES), omap), pl.BlockSpec((tm, ple), pmap), pl.BlockSpec((tm, ple), smap),
                  _const_spec((1, D_MODEL)), _const_spec((D_MODEL, D_MODEL)), _const_spec((ple, D_MODEL)),
                  _const_spec((1, D_MODEL))],
        out_specs=[pl.BlockSpec((tm, D_MODEL), pmap), pl.BlockSpec((tm, D_MODEL), smap)],
        compiler_params=_cparams(("arbitrary",)),
        name="ple_final",
    )(h1, y4, y4, y4, y4, gates, pp, ps, g_ple, w_gate, w_proj, g_final)


def _row(x, width=None):
    x = x.reshape(1, -1).astype(F32)
    if width is not None and x.shape[1] < width:
        x = jnp.pad(x, ((0, 0), (0, width - x.shape[1])))
    return x


def _mixer_params(conv_w, conv_b, dt_bias, a_log, d_skip, ssd_norm_g, v_norm_g, v_norm_b, w_spatial, b_spatial,
                  mlp_out_g, seq_len):
    n_seq = CHUNK // seq_len
    pos = jnp.arange(CHUNK) % seq_len
    same = (jnp.arange(CHUNK)[:, None] // seq_len) == (jnp.arange(CHUNK)[None, :] // seq_len)
    tril = (same & (jnp.arange(CHUNK)[:, None] >= jnp.arange(CHUNK)[None, :])).astype(BF16)
    rexp = (jnp.arange(LANES)[:, None] == (jnp.arange(SSD_WIDTH)[None, :] // HEAD_DIM)).astype(BF16)
    w_loc = jnp.tril(w_spatial[:, :seq_len, :seq_len])
    eye = jnp.eye(n_seq, dtype=F32)
    w_bd = jnp.einsum("st,hij->hsitj", eye, w_loc).reshape(MLP_HEADS, CHUNK, CHUNK)
    wsp = (w_bd.reshape(MLP_HEADS // 2, 2, CHUNK, CHUNK).transpose(0, 2, 1, 3)
           .reshape(MLP_HEADS // 2, CHUNK, 2 * CHUNK).astype(BF16))
    bsp = jnp.repeat(b_spatial[:, :seq_len].T[pos], MLP_WIDTH // MLP_HEADS, axis=1)
    params = (
        conv_w.astype(F32), _row(conv_b), _row(dt_bias, LANES), _row(a_log, LANES),
        _row(jnp.repeat(a_log, HEAD_DIM)), rexp, tril, _row(jnp.repeat(d_skip, HEAD_DIM)),
        _row(ssd_norm_g), _row(v_norm_g), _row(v_norm_b), wsp, bsp.astype(F32), _row(mlp_out_g),
    )
    return params, same.astype(BF16)


def _tile_rows(n):
    return 512 if n % 512 == 0 else CHUNK


def kernel(x_prompt, x_sample, state_ssm, state_conv, p_prompt, p_sample, norm_mix_g, w_in, conv_w, conv_b, dt_bias, a_log, d_skip, ssd_norm_g, v_norm_g, v_norm_b, w_spatial, b_spatial, mlp_out_g, w_out, norm_moe_g, w_router, b_router, w_up, b_up, w_down, b_down, norm_ple_g, w_ple_gate, w_ple_proj, norm_final_g):
    depth = norm_mix_g.shape[0]
    bp, lp, d = x_prompt.shape
    bs, ls, _ = x_sample.shape
    tp, ts = bp * lp, bs * ls
    assert depth == 1 and d == D_MODEL and lp % CHUNK == 0 and ts % CHUNK == 0 and 8 % ls == 0
    tm = _tile_rows(tp) if ts % _tile_rows(tp) == 0 else CHUNK
    t_all = tp + ts
    tm_moe = 256
    nb_moe = -(-t_all * TOP_K // tm_moe) + N_EXPERTS

    hp = x_prompt.reshape(tp, d)
    hs = x_sample.reshape(ts, d)
    ssm_p, conv_p, ssm_s, conv_s, v_s = [], [], [], [], []
    o1 = SSD_WIDTH
    o2 = o1 + CONV_DIM
    o3 = o2 + SSD_HEADS
    o4 = o3 + MLP_WIDTH
    c = jnp.arange(256)
    src = jnp.where(c < LANES, 2 * c, 2 * (c - LANES) + 1)
    perm = (jnp.arange(256)[:, None] == src[None, :]).astype(BF16)

    for i in range(depth):
        wi = w_in[i]
        w_cat = jnp.concatenate(
            [wi[:, :o2], wi[:, o3:], jnp.pad(wi[:, o2:o3], ((0, 0), (0, DT_PAD - SSD_HEADS)))], axis=1).astype(BF16)
        z, xbc, u, v, dtr = _inproj_call(hp, hs, _row(norm_mix_g[i]), w_cat, tm)

        mix_args = (conv_w[i], conv_b[i], dt_bias[i], a_log[i], d_skip[i], ssd_norm_g[i], v_norm_g[i], v_norm_b[i],
                    w_spatial[i], b_spatial[i], mlp_out_g[i])
        prm_p, _ = _mixer_params(*mix_args, seq_len=CHUNK)
        cat_p, s_p = _prompt_mixer_call(z, xbc, u, v, dtr, prm_p, bp, lp // CHUNK)
        ssm_p.append(s_p.reshape(bp, SSD_HEADS, HEAD_DIM, D_STATE).astype(state_ssm.dtype))
        conv_p.append(jnp.stack([xbc[(b + 1) * lp - (CONV_W - 1):(b + 1) * lp] for b in range(bp)]))

        prm_s, seg_ones = _mixer_params(*mix_args, seq_len=ls)
        xbc_s = xbc[tp:].reshape(bs, ls, CONV_DIM)
        xpad = jnp.concatenate([state_conv[i].astype(F32), xbc_s], axis=1)
        x_shift = [xpad[:, CONV_W - 1 - k:CONV_W - 1 - k + ls].reshape(ts, CONV_DIM) for k in range(CONV_W)]
        h0 = state_ssm[i].astype(F32).reshape(bs, SSD_WIDTH, D_STATE)
        cat_s, v_rows, s_s = _sample_mixer_call(z, x_shift, u, v, dtr, h0, prm_s, seg_ones, tp // CHUNK, ls)
        ssm_s.append(s_s.reshape(bs, SSD_HEADS, HEAD_DIM, D_STATE).astype(state_ssm.dtype))
        conv_s.append(xpad[:, ls:])
        v_s.append(v_rows.reshape(bs, ls, MLP_WIDTH))

        wr = jnp.pad(w_router[i].astype(F32), ((0, 0), (0, LANES - N_EXPERTS)))
        wr_hi = wr.astype(BF16)
        wr_lo = (wr - wr_hi.astype(F32)).astype(BF16)
        b_r = jnp.concatenate([b_router[i].astype(F32), jnp.full((LANES - N_EXPERTS,), -1e30, F32)]).reshape(1, LANES)
        h1, m, eid, gates = _out_router_call(cat_p, cat_s, hp, hs, w_out[i].astype(BF16), _row(norm_moe_g[i]),
                                             wr_hi, wr_lo, b_r, tm)

        be, nval, off, tok, dst = _route(eid[:, :TOP_K], tm_moe, nb_moe)
        b_up_g = (b_up[i].astype(F32).reshape(N_EXPERTS, 2 * D_FF // 256, LANES, 2).transpose(0, 1, 3, 2)
                  .reshape(N_EXPERTS, 1, 2 * D_FF))
        y4 = _moe_call(m, be, nval, off, tok, dst, w_up[i], b_up_g, w_down[i],
                       b_down[i].reshape(N_EXPERTS, 1, D_MODEL), perm, tm_moe, nb_moe)

        hp, hs = _ple_call(h1, y4, gates,
                           p_prompt[i].reshape(tp, -1), p_sample[i].reshape(ts, -1), _row(norm_ple_g[i]),
                           w_ple_gate[i].astype(BF16), w_ple_proj[i].astype(BF16), _row(norm_final_g), tm)

    y_prompt = hp.reshape(bp, lp, d)
    y_sample = hs.reshape(bs, ls, d)
    return (y_prompt, y_sample, jnp.stack(ssm_p), jnp.stack(conv_p), jnp.stack(ssm_s), jnp.stack(conv_s),
            jnp.stack(v_s))
```

```python
import functools

import jax
import jax.numpy as jnp
from jax import lax
from jax.experimental import pallas as pl
from jax.experimental.pallas import tpu as pltpu
from jax.experimental.pallas import tpu_sc as plsc

F32 = jnp.float32
BF16 = jnp.bfloat16
I32 = jnp.int32

EPS = 1e-6
D_MODEL = 1024
SSD_WIDTH = 512
SSD_HEADS = 8
HEAD_DIM = 64
SSD_GROUPS = 2
D_STATE = 128
CONV_W = 4
CONV_DIM = SSD_WIDTH + 2 * SSD_GROUPS * D_STATE
MLP_WIDTH = 512
MLP_HEADS = 8
N_EXPERTS = 32
TOP_K = 4
D_FF = 1024
SWIGLU_LIMIT = 7.0
SWIGLU_ALPHA = 1.702
TOPK_SHIFT = 2
assert 1 << TOPK_SHIFT == TOP_K
LANES = 128
CHUNK = 128
DT_PAD = LANES
IN_PAD = SSD_WIDTH + CONV_DIM + 2 * MLP_WIDTH + DT_PAD
VMEM_LIMIT = 56 * 1024 * 1024


def _cparams(sem):
    return pltpu.CompilerParams(dimension_semantics=sem, vmem_limit_bytes=VMEM_LIMIT)


def _const_spec(shape):
    return pl.BlockSpec(shape, lambda *_: (0,) * len(shape))


def _rms(x):
    return x * lax.rsqrt(jnp.mean(x * x, axis=-1, keepdims=True) + EPS)


def _dot(a, b):
    return jnp.dot(a, b, preferred_element_type=F32)


def _dot_nt(a, b):
    return lax.dot_general(a, b, (((1,), (1,)), ((), ())), preferred_element_type=F32)


def _split3(x):
    hi = x.astype(BF16)
    r = x - hi.astype(F32)
    mid = r.astype(BF16)
    lo = (r - mid.astype(F32)).astype(BF16)
    return hi, mid, lo


def _sel_right(x, m01):
    hi, mid, lo = _split3(x)
    return _dot(hi, m01) + _dot(mid, m01) + _dot(lo, m01)


def _sel_left(m01, x):
    hi, mid, lo = _split3(x)
    return _dot(m01, hi) + _dot(m01, mid) + _dot(m01, lo)


def _softplus(x):
    return jnp.maximum(x, 0.0) + jnp.log1p(jnp.exp(-jnp.abs(x)))


def _inproj_call(xp, xs, g, w, tm):
    tp, ts = xp.shape[0], xs.shape[0]
    n_p, n_s = tp // tm, ts // tm
    t_all = tp + ts
    segs = ((0, 512), (512, 1536), (1536, 2048), (2048, 2560), (2560, IN_PAD))

    def body(xp_ref, xs_ref, g_ref, w_ref, *outs):
        def run(x_ref):
            xn = (_rms(x_ref[...]) * g_ref[...]).astype(BF16)
            for (a, b), o in zip(segs, outs):
                o[...] = _dot(xn, w_ref[:, a:b])

        i = pl.program_id(0)

        @pl.when(i < n_p)
        def _():
            run(xp_ref)

        @pl.when(i >= n_p)
        def _():
            run(xs_ref)

    widths = [b - a for a, b in segs]
    return pl.pallas_call(
        body,
        out_shape=[jax.ShapeDtypeStruct((t_all, wd), F32) for wd in widths],
        grid=(n_p + n_s,),
        in_specs=[
            pl.BlockSpec((tm, D_MODEL), lambda i: (jnp.minimum(i, n_p - 1), 0)),
            pl.BlockSpec((tm, D_MODEL), lambda i: (jnp.maximum(i - n_p, 0), 0)),
            _const_spec((1, D_MODEL)),
            _const_spec((D_MODEL, IN_PAD)),
        ],
        out_specs=[pl.BlockSpec((tm, wd), lambda i: (i, 0)) for wd in widths],
        compiler_params=_cparams(("arbitrary",)),
        name="in_proj",
    )(xp, xs, g, w)


def _mixer_front(conv, dtr, dtb, alog, alog_x, rexp, tril, seg_ones):
    xact = conv * jax.nn.sigmoid(conv)
    xs = xact[:, :SSD_WIDTH]
    bm = xact[:, SSD_WIDTH:SSD_WIDTH + 256]
    cm = xact[:, SSD_WIDTH + 256:]
    dt = _softplus(dtr + dtb)
    a = dt * (-jnp.exp(alog))
    dt_x = _sel_right(dt, rexp)
    a_x = dt_x * (-jnp.exp(alog_x))
    acum = _sel_left(tril, a)
    acum_x = _sel_left(tril, a_x)
    if seg_ones is None:
        r = acum_x.shape[0]
        tot_x = jnp.broadcast_to(acum_x[r - 1:r, :], acum_x.shape)
    else:
        tot_x = _sel_left(seg_ones, a_x)
    return xs, bm, cm, dt_x, acum, acum_x, tot_x


def _ssd_intra(cmb, bmb, acum, xdt, mask):
    r = acum.shape[0]
    acum_t = acum.T
    lane = lax.broadcasted_iota(I32, (r, LANES), 1)
    low = lane < HEAD_DIM
    outs = []
    for g in range(SSD_GROUPS):
        sg = _dot_nt(cmb[:, LANES * g:LANES * (g + 1)], bmb[:, LANES * g:LANES * (g + 1)])
        for k in (2 * g, 2 * g + 1):
            parts = []
            for h in (2 * k, 2 * k + 1):
                seg = acum[:, h:h + 1] - acum_t[h:h + 1, :]
                parts.append((sg * jnp.exp(jnp.where(mask, seg, -jnp.inf))).astype(BF16))
            lhs = jnp.concatenate(parts, axis=1)
            xd = xdt[:, LANES * k:LANES * (k + 1)]
            rhs = jnp.concatenate([jnp.where(low, xd, 0.0), jnp.where(low, 0.0, xd)], axis=0).astype(BF16)
            outs.append(_dot(lhs, rhs))
    return jnp.concatenate(outs, axis=1)


def _mixer_back(y, z, u, v, sng, vng, vnb, wsp_ref, bsp, mog):
    r = y.shape[0]
    yg = y * (z * jax.nn.sigmoid(z))
    halves = []
    for g in range(SSD_GROUPS):
        t = yg[:, 256 * g:256 * (g + 1)]
        halves.append(_rms(t))
    yn = jnp.concatenate(halves, axis=1) * sng
    ug = jax.nn.gelu(u)
    vg = jax.nn.gelu(v)
    mu = jnp.mean(vg, axis=-1, keepdims=True)
    var = jnp.mean(jnp.square(vg - mu), axis=-1, keepdims=True)
    v_ln = (vg - mu) * lax.rsqrt(var + EPS) * vng + vnb
    lane = lax.broadcasted_iota(I32, (r, LANES), 1)
    low = lane < HEAD_DIM
    outs = []
    for k in range(MLP_HEADS // 2):
        vd = v_ln[:, LANES * k:LANES * (k + 1)]
        rhs = jnp.concatenate([jnp.where(low, vd, 0.0), jnp.where(low, 0.0, vd)], axis=0).astype(BF16)
        outs.append(_dot(wsp_ref[k], rhs))
    s = jnp.concatenate(outs, axis=1) + bsp
    m = _rms(ug * s) * mog
    return jnp.concatenate([yn, m], axis=1).astype(BF16), v_ln


_MIXER_PARAM_SHAPES = (
    (CONV_W, CONV_DIM), (1, CONV_DIM), (1, LANES), (1, LANES), (1, SSD_WIDTH), (LANES, SSD_WIDTH),
    (CHUNK, CHUNK), (1, SSD_WIDTH), (1, SSD_WIDTH), (1, MLP_WIDTH), (1, MLP_WIDTH),
    (MLP_HEADS // 2, CHUNK, 2 * CHUNK), (CHUNK, MLP_WIDTH), (1, MLP_WIDTH),
)


def _prompt_mixer_body(z_ref, xbc_ref, u_ref, v_ref, dt_ref,
                       cw_ref, cb_ref, dtb_ref, alog_ref, alogx_ref, rexp_ref, tril_ref, dskip_ref,
                       sng_ref, vng_ref, vnb_ref, wsp_ref, bsp_ref, mog_ref,
                       cat_ref, ssm_ref, ext_scr, s_scr):
    c = pl.program_id(1)
    r = CHUNK

    @pl.when(c == 0)
    def _():
        ext_scr[0:8, :] = jnp.zeros((8, CONV_DIM), F32)
        s_scr[...] = jnp.zeros_like(s_scr)

    x = xbc_ref[...]
    ext_scr[8:8 + r, :] = x
    cw = cw_ref[...]
    conv = (cb_ref[...] + cw[3:4] * x + cw[2:3] * ext_scr[7:7 + r, :]
            + cw[1:2] * ext_scr[6:6 + r, :] + cw[0:1] * ext_scr[5:5 + r, :])
    ext_scr[0:8, :] = x[r - 8:r, :]

    xs, bm, cm, dt_x, acum, acum_x, tot_x = _mixer_front(
        conv, dt_ref[...], dtb_ref[...], alog_ref[...], alogx_ref[...], rexp_ref[...], tril_ref[...], None)
    bmb, cmb = bm.astype(BF16), cm.astype(BF16)
    xdt = xs * dt_x
    row = lax.broadcasted_iota(I32, (r, r), 0)
    col = lax.broadcasted_iota(I32, (r, r), 1)
    y_diag = _ssd_intra(cmb, bmb, acum, xdt, row >= col)

    s_prev = s_scr[...]
    s_prev_b = s_prev.astype(BF16)
    y_off = jnp.concatenate(
        [_dot_nt(cmb[:, LANES * g:LANES * (g + 1)], s_prev_b[256 * g:256 * (g + 1), :]) for g in range(SSD_GROUPS)],
        axis=1)
    y = y_diag + y_off * jnp.exp(acum_x) + dskip_ref[...] * xs

    w_t = (xdt * jnp.exp(tot_x - acum_x)).T.astype(BF16)
    states = jnp.concatenate(
        [_dot(w_t[256 * g:256 * (g + 1), :], bmb[:, LANES * g:LANES * (g + 1)]) for g in range(SSD_GROUPS)], axis=0)
    s_new = s_prev * jnp.exp(tot_x).T + states
    s_scr[...] = s_new

    cat, _ = _mixer_back(y, z_ref[...], u_ref[...], v_ref[...], sng_ref[...], vng_ref[...], vnb_ref[...],
                         wsp_ref, bsp_ref[...], mog_ref[...])
    cat_ref[...] = cat

    @pl.when(c == pl.num_programs(1) - 1)
    def _():
        ssm_ref[0] = s_new


def _prompt_mixer_call(z, xbc, u, v, dtr, params, nb, nc):
    row = lambda b, c: (b * nc + c, 0)
    in_specs = [
        pl.BlockSpec((CHUNK, SSD_WIDTH), row), pl.BlockSpec((CHUNK, CONV_DIM), row),
        pl.BlockSpec((CHUNK, MLP_WIDTH), row), pl.BlockSpec((CHUNK, MLP_WIDTH), row),
        pl.BlockSpec((CHUNK, DT_PAD), row),
    ] + [_const_spec(s) for s in _MIXER_PARAM_SHAPES]
    return pl.pallas_call(
        _prompt_mixer_body,
        out_shape=[jax.ShapeDtypeStruct((nb * nc * CHUNK, D_MODEL), BF16),
                   jax.ShapeDtypeStruct((nb, SSD_WIDTH, D_STATE), F32)],
        grid=(nb, nc),
        in_specs=in_specs,
        out_specs=[pl.BlockSpec((CHUNK, D_MODEL), row),
                   pl.BlockSpec((1, SSD_WIDTH, D_STATE), lambda b, c: (b, 0, 0))],
        scratch_shapes=[pltpu.VMEM((CHUNK + 8, CONV_DIM), F32), pltpu.VMEM((SSD_WIDTH, D_STATE), F32)],
        compiler_params=_cparams(("arbitrary", "arbitrary")),
        name="prompt_mixer",
    )(z, xbc, u, v, dtr, *params)


def _sample_mixer_body(seq_len, z_ref, x0_ref, x1_ref, x2_ref, x3_ref, u_ref, v_ref, dt_ref, h_ref,
                       cw_ref, cb_ref, dtb_ref, alog_ref, alogx_ref, rexp_ref, tril_ref, dskip_ref,
                       sng_ref, vng_ref, vnb_ref, wsp_ref, bsp_ref, mog_ref, segones_ref,
                       cat_ref, vout_ref, hout_ref, cm_scr, bm_scr, wt_scr, dtt_scr, yoff_scr):
    r = CHUNK
    shift = seq_len.bit_length() - 1
    cw = cw_ref[...]
    conv = (cb_ref[...] + cw[3:4] * x0_ref[...] + cw[2:3] * x1_ref[...]
            + cw[1:2] * x2_ref[...] + cw[0:1] * x3_ref[...])
    xs, bm, cm, dt_x, acum, acum_x, tot_x = _mixer_front(
        conv, dt_ref[...], dtb_ref[...], alog_ref[...], alogx_ref[...], rexp_ref[...], tril_ref[...],
        segones_ref[...])
    bmb, cmb = bm.astype(BF16), cm.astype(BF16)
    xdt = xs * dt_x
    row = lax.broadcasted_iota(I32, (r, r), 0)
    col = lax.broadcasted_iota(I32, (r, r), 1)
    same = lax.shift_right_logical(row, shift) == lax.shift_right_logical(col, shift)
    y_diag = _ssd_intra(cmb, bmb, acum, xdt, same & (row >= col))

    cm_scr[...] = cm
    bm_scr[...] = bmb
    wt_scr[...] = (xdt * jnp.exp(tot_x - acum_x)).T
    dtt_scr[...] = jnp.exp(tot_x).T
    ones_b = jnp.ones((LANES, LANES), BF16)
    seqs_per_slab = 8 // seq_len

    def slab(j, carry):
        rows = pl.ds(pl.multiple_of(8 * j, 8), 8)
        cms = cm_scr[rows, :].astype(BF16)
        sub = lax.broadcasted_iota(I32, (8, 256), 0)
        lane = lax.broadcasted_iota(I32, (256, LANES), 1)
        for g in range(SSD_GROUPS):
            q_rows = slice(256 * g, 256 * (g + 1))
            acc = jnp.zeros((8, 256), F32)
            for q in range(seqs_per_slab):
                s = seqs_per_slab * j + q
                y_s = _dot_nt(cms[:, LANES * g:LANES * (g + 1)], h_ref[s, q_rows, :].astype(BF16))
                acc = jnp.where(lax.shift_right_logical(sub, shift) == q, y_s, acc)
            yoff_scr[rows, 256 * g:256 * (g + 1)] = acc
            for q in range(seqs_per_slab):
                s = seqs_per_slab * j + q
                w_sel = jnp.where(lax.shift_right_logical(lane, shift) == s, wt_scr[q_rows, :], 0.0).astype(BF16)
                st = _dot(w_sel, bm_scr[:, LANES * g:LANES * (g + 1)])
                d_sel = jnp.where(lane == s * seq_len, dtt_scr[q_rows, :], 0.0)
                hout_ref[s, q_rows, :] = h_ref[s, q_rows, :] * _sel_right(d_sel, ones_b) + st
        return carry

    lax.fori_loop(0, r // 8, slab, 0)

    y = y_diag + yoff_scr[...] * jnp.exp(acum_x) + dskip_ref[...] * xs
    cat, v_ln = _mixer_back(y, z_ref[...], u_ref[...], v_ref[...], sng_ref[...], vng_ref[...], vnb_ref[...],
                            wsp_ref, bsp_ref[...], mog_ref[...])
    cat_ref[...] = cat
    vout_ref[...] = v_ln


def _sample_mixer_call(z, x_shift, u, v, dtr, h0, params, seg_ones, row0, seq_len):
    ts = x_shift[0].shape[0]
    n = ts // CHUNK
    spt = CHUNK // seq_len
    off = lambda i: (row0 + i, 0)
    loc = lambda i: (i, 0)
    st3 = lambda i: (i, 0, 0)
    in_specs = (
        [pl.BlockSpec((CHUNK, SSD_WIDTH), off)]
        + [pl.BlockSpec((CHUNK, CONV_DIM), loc)] * 4
        + [pl.BlockSpec((CHUNK, MLP_WIDTH), off), pl.BlockSpec((CHUNK, MLP_WIDTH), off),
           pl.BlockSpec((CHUNK, DT_PAD), off), pl.BlockSpec((spt, SSD_WIDTH, D_STATE), st3)]
        + [_const_spec(s) for s in _MIXER_PARAM_SHAPES] + [_const_spec((CHUNK, CHUNK))])
    return pl.pallas_call(
        functools.partial(_sample_mixer_body, seq_len),
        out_shape=[jax.ShapeDtypeStruct((ts, D_MODEL), BF16), jax.ShapeDtypeStruct((ts, MLP_WIDTH), F32),
                   jax.ShapeDtypeStruct(h0.shape, F32)],
        grid=(n,),
        in_specs=in_specs,
        out_specs=[pl.BlockSpec((CHUNK, D_MODEL), loc), pl.BlockSpec((CHUNK, MLP_WIDTH), loc),
                   pl.BlockSpec((spt, SSD_WIDTH, D_STATE), st3)],
        scratch_shapes=[pltpu.VMEM((CHUNK, 256), F32), pltpu.VMEM((CHUNK, 256), BF16),
                        pltpu.VMEM((SSD_WIDTH, CHUNK), F32), pltpu.VMEM((SSD_WIDTH, CHUNK), F32),
                        pltpu.VMEM((CHUNK, SSD_WIDTH), F32)],
        compiler_params=_cparams(("arbitrary",)),
        name="sample_mixer",
    )(z, *x_shift, u, v, dtr, h0, *params, seg_ones)


def _out_router_call(cat_p, cat_s, xp, xs, w_out, g_moe, wr_hi, wr_lo, b_r, tm):
    tp, ts = xp.shape[0], xs.shape[0]
    n_p, n_s = tp // tm, ts // tm
    t_all = tp + ts

    def body(cp_ref, cs_ref, xp_ref, xs_ref, wo_ref, g_ref, wh_ref, wl_ref, br_ref,
             h1_ref, m_ref, eid_ref, gate_ref):
        def run(c_ref, x_ref):
            h1 = x_ref[...] + _dot(c_ref[...], wo_ref[...])
            h1_ref[...] = h1
            m = _rms(h1) * g_ref[...]
            m_ref[...] = m
            m_hi = m.astype(BF16)
            m_lo = (m - m_hi.astype(F32)).astype(BF16)
            logits = _dot(m_hi, wh_ref[...]) + _dot(m_lo, wh_ref[...]) + _dot(m_hi, wl_ref[...]) + br_ref[...]
            lane = lax.broadcasted_iota(I32, logits.shape, 1).astype(F32)
            work = logits
            vals, ids = [], []
            for _ in range(TOP_K):
                mx = jnp.max(work, axis=-1, keepdims=True)
                idx = jnp.min(jnp.where(work == mx, lane, float(LANES)), axis=-1, keepdims=True)
                vals.append(mx)
                ids.append(idx)
                work = jnp.where(lane == idx, -jnp.inf, work)
            ex = [jnp.exp(vv - vals[0]) for vv in vals]
            den = ex[0] + ex[1] + ex[2] + ex[3]
            eid = jnp.zeros(logits.shape, I32)
            gate = jnp.zeros(logits.shape, F32)
            for k in range(TOP_K):
                eid = jnp.where(lane == k, ids[k].astype(I32), eid)
                gate = jnp.where(lane == k, ex[k] / den, gate)
            eid_ref[...] = eid
            gate_ref[...] = gate

        i = pl.program_id(0)

        @pl.when(i < n_p)
        def _():
            run(cp_ref, xp_ref)

        @pl.when(i >= n_p)
        def _():
            run(cs_ref, xs_ref)

    pmap = lambda i: (jnp.minimum(i, n_p - 1), 0)
    smap = lambda i: (jnp.maximum(i - n_p, 0), 0)
    omap = lambda i: (i, 0)
    return pl.pallas_call(
        body,
        out_shape=[jax.ShapeDtypeStruct((t_all, D_MODEL), F32), jax.ShapeDtypeStruct((t_all, D_MODEL), F32),
                   jax.ShapeDtypeStruct((t_all, LANES), I32), jax.ShapeDtypeStruct((t_all, LANES), F32)],
        grid=(n_p + n_s,),
        in_specs=[pl.BlockSpec((tm, D_MODEL), pmap), pl.BlockSpec((tm, D_MODEL), smap),
                  pl.BlockSpec((tm, D_MODEL), pmap), pl.BlockSpec((tm, D_MODEL), smap),
                  _const_spec((D_MODEL, D_MODEL)), _const_spec((1, D_MODEL)),
                  _const_spec((D_MODEL, LANES)), _const_spec((D_MODEL, LANES)), _const_spec((1, LANES))],
        out_specs=[pl.BlockSpec((tm, D_MODEL), omap), pl.BlockSpec((tm, D_MODEL), omap),
                   pl.BlockSpec((tm, LANES), omap), pl.BlockSpec((tm, LANES), omap)],
        compiler_params=_cparams(("arbitrary",)),
        name="out_router",
    )(cat_p, cat_s, xp, xs, w_out, g_moe, wr_hi, wr_lo, b_r)


def _route(eid, tm, nb):
    t = eid.shape[0]
    tk = t * TOP_K
    flat = eid.reshape(tk)
    experts = jnp.arange(N_EXPERTS, dtype=I32)
    real = jnp.sum((flat[:, None] == experts[None, :]).astype(I32), axis=0)
    need = (-real) % ROW_ALIGN
    fill = jnp.arange(ROW_ALIGN - 1, dtype=I32)
    fill_key = jnp.where(fill[None, :] < need[:, None], experts[:, None], N_EXPERTS).reshape(-1)
    n_fill = fill_key.shape[0]
    keys = jnp.concatenate([flat, fill_key])
    _, order = lax.sort((keys, jnp.arange(tk + n_fill, dtype=I32)), num_keys=1, is_stable=True)
    counts = real + need
    nblk = (counts + tm - 1) // tm
    bend = jnp.cumsum(nblk)
    bstart = bend - nblk
    start = jnp.cumsum(counts) - counts
    nused = bend[-1]
    blk = jnp.arange(nb, dtype=I32)
    used = blk < nused
    be = jnp.minimum(jnp.sum((jnp.minimum(blk, nused - 1)[:, None] >= bend[None, :]).astype(I32), axis=1),
                     N_EXPERTS - 1)
    sel = (be[:, None] == jnp.arange(N_EXPERTS, dtype=I32)[None, :]).astype(I32)
    pick = lambda v: jnp.sum(sel * v[None, :], axis=1)
    done = (blk - pick(bstart)) * tm
    nval = jnp.where(used, jnp.clip(pick(counts) - done, 0, tm), 0).astype(I32)
    off = jnp.where(used, pick(start) + done, 0).astype(I32)
    n_sorted = _sorted_rows(tk + n_fill, tm)
    pad = n_sorted - tk - n_fill
    is_real = order < tk
    tok = jnp.pad(jnp.where(is_real, lax.shift_right_logical(order, TOPK_SHIFT), 0), (0, pad))
    dst = jnp.concatenate([
        jnp.where(is_real, (order & (TOP_K - 1)) * t + lax.shift_right_logical(order, TOPK_SHIFT), order),
        tk + n_fill + jnp.arange(pad, dtype=I32)])
    return be, nval, off, tok, dst


ROW_ALIGN = 8
SC_WINDOW = 128
SC_COLS = 256
SC_TILES = 32


def _sorted_rows(tk, tm):
    unit = SC_WINDOW * SC_TILES
    return -(-(tk + tm) // unit) * unit


def _sc_mesh():
    return plsc.VectorSubcoreMesh(core_axis_name="core", subcore_axis_name="subcore")


def _sc_gather_rows(x, idx):
    n, d = idx.shape[0], x.shape[1]
    per_core = n // (2 * SC_WINDOW)

    @pl.kernel(out_type=jax.ShapeDtypeStruct((n, d), x.dtype), mesh=_sc_mesh(), name="sc_gather_rows")
    def run(x_hbm, i_hbm, o_hbm):
        for f in range(d // SC_COLS):
            x_cols = x_hbm.at[:, pl.ds(SC_COLS * f, SC_COLS)]

            def body(i_vmem, o_vmem, x_cols=x_cols):
                pltpu.sync_copy(x_cols.at[i_vmem.at[0]], o_vmem)

            pltpu.emit_pipeline(
                body, grid=(2, per_core),
                in_specs=[pl.BlockSpec((1, SC_WINDOW), lambda c, j: (c, j))],
                out_specs=[pl.BlockSpec((SC_WINDOW, SC_COLS), functools.partial(
                    lambda f, c, j: (c * per_core + j, f), f))],
                core_axis_name=("core", "subcore"),
                dimension_semantics=(pltpu.PARALLEL, pltpu.PARALLEL),
            )(i_hbm, o_hbm)

    return run(x, idx.reshape(2, n // 2))


def _sc_scatter_rows(y, idx, rows_out):
    n, d = y.shape
    per_core = n // (2 * SC_WINDOW)

    @pl.kernel(out_type=jax.ShapeDtypeStruct((rows_out, d), y.dtype), mesh=_sc_mesh(), name="sc_scatter_rows")
    def run(y_hbm, i_hbm, o_hbm):
        for f in range(d // SC_COLS):
            o_cols = o_hbm.at[:, pl.ds(SC_COLS * f, SC_COLS)]

            def body(y_vmem, i_vmem, o_cols=o_cols):
                pltpu.sync_copy(y_vmem, o_cols.at[i_vmem.at[0]])

            pltpu.emit_pipeline(
                body, grid=(2, per_core),
                in_specs=[pl.BlockSpec((SC_WINDOW, SC_COLS), functools.partial(
                              lambda f, c, j: (c * per_core + j, f), f)),
                          pl.BlockSpec((1, SC_WINDOW), lambda c, j: (c, j))],
                out_specs=[],
                core_axis_name=("core", "subcore"),
                dimension_semantics=(pltpu.PARALLEL, pltpu.PARALLEL),
            )(y_hbm, i_hbm)

    return run(y, idx.reshape(2, n // 2))


def _moe_body(tm, nb, be_ref, nval_ref, off_ref, x_hbm, wup_ref, bup_ref, wdn_ref, bdn_ref, perm_ref,
              y_hbm, xbuf, ybuf, wup_b, wdn_b, xsem, ysem):
    i = pl.program_id(0)
    nv = nval_ref[i]
    slot = i & 1
    prv = jnp.maximum(i - 1, 0)
    nxt = jnp.minimum(i + 1, nb - 1)
    has_next = (i + 1 < nb) & (nval_ref[nxt] > 0)

    def rows(b):
        return pl.ds(pl.multiple_of(off_ref[b], ROW_ALIGN), tm)

    def x_copy(b, s):
        return pltpu.make_async_copy(x_hbm.at[rows(b)], xbuf.at[s], xsem.at[s])

    def y_copy(b, s):
        return pltpu.make_async_copy(ybuf.at[s], y_hbm.at[rows(b)], ysem.at[s])

    @pl.when(nv > 0)
    def _():
        @pl.when(i == 0)
        def _():
            x_copy(0, 0).start()

        @pl.when(has_next)
        def _():
            x_copy(nxt, 1 - slot).start()

        @pl.when((i == 0) | (be_ref[i] != be_ref[prv]))
        def _():
            for jb in range(2 * D_FF // 256):
                cols = slice(256 * jb, 256 * (jb + 1))
                wup_b[:, cols] = _dot(wup_ref[0, :, cols].astype(BF16), perm_ref[...]).astype(BF16)
            wdn_b[...] = wdn_ref[0].astype(BF16)

        x_copy(i, slot).wait()

        x = xbuf[slot].astype(BF16)
        acts = []
        for jb in range(D_FF // LANES):
            h = _dot(x, wup_b[:, 256 * jb:256 * (jb + 1)]) + bup_ref[0, :, 256 * jb:256 * (jb + 1)]
            gate = jnp.minimum(h[:, :LANES], SWIGLU_LIMIT)
            lin = jnp.clip(h[:, LANES:], -SWIGLU_LIMIT, SWIGLU_LIMIT)
            acts.append((gate * jax.nn.sigmoid(SWIGLU_ALPHA * gate) * (lin + 1.0)).astype(BF16))
        act = jnp.concatenate(acts, axis=1)
        for c in range(D_MODEL // 256):
            ybuf[slot, :, 256 * c:256 * (c + 1)] = (
                _dot(act, wdn_b[:, 256 * c:256 * (c + 1)]) + bdn_ref[0, :, 256 * c:256 * (c + 1)])

        @pl.when(i >= 1)
        def _():
            y_copy(prv, 1 - slot).wait()
        y_copy(i, slot).start()

        @pl.when(jnp.logical_not(has_next))
        def _():
            y_copy(i, slot).wait()


def _moe_call(x_sorted, be, nval, off, w_up, b_up_g, w_down, b_down, perm, tm, nb):
    by_expert = lambda i, be, nv, off: (be[i], 0, 0)
    grid_spec = pltpu.PrefetchScalarGridSpec(
        num_scalar_prefetch=3,
        grid=(nb,),
        in_specs=[
            pl.BlockSpec(memory_space=pl.ANY),
            pl.BlockSpec((1, D_MODEL, 2 * D_FF), by_expert),
            pl.BlockSpec((1, 1, 2 * D_FF), by_expert),
            pl.BlockSpec((1, D_FF, D_MODEL), by_expert),
            pl.BlockSpec((1, 1, D_MODEL), by_expert),
            pl.BlockSpec((256, 256), lambda i, be, nv, off: (0, 0)),
        ],
        out_specs=pl.BlockSpec(memory_space=pl.ANY),
        scratch_shapes=[pltpu.VMEM((2, tm, D_MODEL), F32), pltpu.VMEM((2, tm, D_MODEL), F32),
                        pltpu.VMEM((D_MODEL, 2 * D_FF), BF16), pltpu.VMEM((D_FF, D_MODEL), BF16),
                        pltpu.SemaphoreType.DMA((2,)), pltpu.SemaphoreType.DMA((2,))],
    )
    return pl.pallas_call(
        functools.partial(_moe_body, tm, nb),
        out_shape=jax.ShapeDtypeStruct(x_sorted.shape, F32),
        grid_spec=grid_spec,
        compiler_params=_cparams(("arbitrary",)),
        name="moe_experts",
    )(be, nval, off, x_sorted, w_up, b_up_g, w_down, b_down, perm)


def _ple_call(h1, y4, gates, pp, ps, g_ple, w_gate, w_proj, g_final, tm):
    tp, ts = pp.shape[0], ps.shape[0]
    n_p, n_s = tp // tm, ts // tm
    ple = pp.shape[1]

    def body(h1_ref, y0_ref, y1_ref, y2_ref, y3_ref, gt_ref, pp_ref, ps_ref, g_ref, wg_ref, wp_ref, gf_ref,
             yp_ref, ys_ref):
        def run(p_ref, o_ref):
            gt = gt_ref[...]
            moe = None
            for k, y_ref in enumerate((y0_ref, y1_ref, y2_ref, y3_ref)):
                moe = gt[:, k:k + 1] * y_ref[...] if moe is None else moe + gt[:, k:k + 1] * y_ref[...]
            h2 = h1_ref[...] + moe
            a = (_rms(h2) * g_ref[...]).astype(BF16)
            gate = jax.nn.sigmoid(_dot(a, wg_ref[...]))
            pe = _dot(p_ref[...].astype(BF16), wp_ref[...])
            h3 = h2 + pe * gate
            o_ref[...] = _rms(h3) * gf_ref[...]

        i = pl.program_id(0)

        @pl.when(i < n_p)
        def _():
            run(pp_ref, yp_ref)

        @pl.when(i >= n_p)
        def _():
            run(ps_ref, ys_ref)

    pmap = lambda i: (jnp.minimum(i, n_p - 1), 0)
    smap = lambda i: (jnp.maximum(i - n_p, 0), 0)
    omap = lambda i: (i, 0)
    return pl.pallas_call(
        body,
        out_shape=[jax.ShapeDtypeStruct((tp, D_MODEL), F32), jax.ShapeDtypeStruct((ts, D_MODEL), F32)],
        grid=(n_p + n_s,),
        in_specs=[pl.BlockSpec((tm, D_MODEL), omap)]
                 + [pl.BlockSpec((tm, D_MODEL), functools.partial(lambda k, i: (k * (n_p + n_s) + i, 0), k))
                    for k in range(TOP_K)]
                 + [pl.BlockSpec((tm, LANES), omap), pl.BlockSpec((tm, ple), pmap), pl.BlockSpec((tm, ple), smap),
                  _const_spec((1, D_MODEL)), _const_spec((D_MODEL, D_MODEL)), _const_spec((ple, D_MODEL)),
                  _const_spec((1, D_MODEL))],
        out_specs=[pl.BlockSpec((tm, D_MODEL), pmap), pl.BlockSpec((tm, D_MODEL), smap)],
        compiler_params=_cparams(("arbitrary",)),
        name="ple_final",
    )(h1, y4, y4, y4, y4, gates, pp, ps, g_ple, w_gate, w_proj, g_final)


def _row(x, width=None):
    x = x.reshape(1, -1).astype(F32)
    if width is not None and x.shape[1] < width:
        x = jnp.pad(x, ((0, 0), (0, width - x.shape[1])))
    return x


def _mixer_params(conv_w, conv_b, dt_bias, a_log, d_skip, ssd_norm_g, v_norm_g, v_norm_b, w_spatial, b_spatial,
                  mlp_out_g, seq_len):
    n_seq = CHUNK // seq_len
    pos = jnp.arange(CHUNK) % seq_len
    same = (jnp.arange(CHUNK)[:, None] // seq_len) == (jnp.arange(CHUNK)[None, :] // seq_len)
    tril = (same & (jnp.arange(CHUNK)[:, None] >= jnp.arange(CHUNK)[None, :])).astype(BF16)
    rexp = (jnp.arange(LANES)[:, None] == (jnp.arange(SSD_WIDTH)[None, :] // HEAD_DIM)).astype(BF16)
    w_loc = jnp.tril(w_spatial[:, :seq_len, :seq_len])
    eye = jnp.eye(n_seq, dtype=F32)
    w_bd = jnp.einsum("st,hij->hsitj", eye, w_loc).reshape(MLP_HEADS, CHUNK, CHUNK)
    wsp = (w_bd.reshape(MLP_HEADS // 2, 2, CHUNK, CHUNK).transpose(0, 2, 1, 3)
           .reshape(MLP_HEADS // 2, CHUNK, 2 * CHUNK).astype(BF16))
    bsp = jnp.repeat(b_spatial[:, :seq_len].T[pos], MLP_WIDTH // MLP_HEADS, axis=1)
    params = (
        conv_w.astype(F32), _row(conv_b), _row(dt_bias, LANES), _row(a_log, LANES),
        _row(jnp.repeat(a_log, HEAD_DIM)), rexp, tril, _row(jnp.repeat(d_skip, HEAD_DIM)),
        _row(ssd_norm_g), _row(v_norm_g), _row(v_norm_b), wsp, bsp.astype(F32), _row(mlp_out_g),
    )
    return params, same.astype(BF16)


def _tile_rows(n):
    return 512 if n % 512 == 0 else CHUNK


def kernel(x_prompt, x_sample, state_ssm, state_conv, p_prompt, p_sample, norm_mix_g, w_in, conv_w, conv_b, dt_bias, a_log, d_skip, ssd_norm_g, v_norm_g, v_norm_b, w_spatial, b_spatial, mlp_out_g, w_out, norm_moe_g, w_router, b_router, w_up, b_up, w_down, b_down, norm_ple_g, w_ple_gate, w_ple_proj, norm_final_g):
    depth = norm_mix_g.shape[0]
    bp, lp, d = x_prompt.shape
    bs, ls, _ = x_sample.shape
    tp, ts = bp * lp, bs * ls
    assert depth == 1 and d == D_MODEL and lp % CHUNK == 0 and ts % CHUNK == 0 and 8 % ls == 0
    tm = _tile_rows(tp) if ts % _tile_rows(tp) == 0 else CHUNK
    t_all = tp + ts
    tm_moe = 256
    nb_moe = -(-t_all * TOP_K // tm_moe) + N_EXPERTS

    hp = x_prompt.reshape(tp, d)
    hs = x_sample.reshape(ts, d)
    ssm_p, conv_p, ssm_s, conv_s, v_s = [], [], [], [], []
    o1 = SSD_WIDTH
    o2 = o1 + CONV_DIM
    o3 = o2 + SSD_HEADS
    o4 = o3 + MLP_WIDTH
    c = jnp.arange(256)
    src = jnp.where(c < LANES, 2 * c, 2 * (c - LANES) + 1)
    perm = (jnp.arange(256)[:, None] == src[None, :]).astype(BF16)

    for i in range(depth):
        wi = w_in[i]
        w_cat = jnp.concatenate(
            [wi[:, :o2], wi[:, o3:], jnp.pad(wi[:, o2:o3], ((0, 0), (0, DT_PAD - SSD_HEADS)))], axis=1).astype(BF16)
        z, xbc, u, v, dtr = _inproj_call(hp, hs, _row(norm_mix_g[i]), w_cat, tm)

        mix_args = (conv_w[i], conv_b[i], dt_bias[i], a_log[i], d_skip[i], ssd_norm_g[i], v_norm_g[i], v_norm_b[i],
                    w_spatial[i], b_spatial[i], mlp_out_g[i])
        prm_p, _ = _mixer_params(*mix_args, seq_len=CHUNK)
        cat_p, s_p = _prompt_mixer_call(z, xbc, u, v, dtr, prm_p, bp, lp // CHUNK)
        ssm_p.append(s_p.reshape(bp, SSD_HEADS, HEAD_DIM, D_STATE).astype(state_ssm.dtype))
        conv_p.append(jnp.stack([xbc[(b + 1) * lp - (CONV_W - 1):(b + 1) * lp] for b in range(bp)]))

        prm_s, seg_ones = _mixer_params(*mix_args, seq_len=ls)
        xbc_s = xbc[tp:].reshape(bs, ls, CONV_DIM)
        xpad = jnp.concatenate([state_conv[i].astype(F32), xbc_s], axis=1)
        x_shift = [xpad[:, CONV_W - 1 - k:CONV_W - 1 - k + ls].reshape(ts, CONV_DIM) for k in range(CONV_W)]
        h0 = state_ssm[i].astype(F32).reshape(bs, SSD_WIDTH, D_STATE)
        cat_s, v_rows, s_s = _sample_mixer_call(z, x_shift, u, v, dtr, h0, prm_s, seg_ones, tp // CHUNK, ls)
        ssm_s.append(s_s.reshape(bs, SSD_HEADS, HEAD_DIM, D_STATE).astype(state_ssm.dtype))
        conv_s.append(xpad[:, ls:])
        v_s.append(v_rows.reshape(bs, ls, MLP_WIDTH))

        wr = jnp.pad(w_router[i].astype(F32), ((0, 0), (0, LANES - N_EXPERTS)))
        wr_hi = wr.astype(BF16)
        wr_lo = (wr - wr_hi.astype(F32)).astype(BF16)
        b_r = jnp.concatenate([b_router[i].astype(F32), jnp.full((LANES - N_EXPERTS,), -1e30, F32)]).reshape(1, LANES)
        h1, m, eid, gates = _out_router_call(cat_p, cat_s, hp, hs, w_out[i].astype(BF16), _row(norm_moe_g[i]),
                                             wr_hi, wr_lo, b_r, tm)

        be, nval, off, tok, dst = _route(eid[:, :TOP_K], tm_moe, nb_moe)
        b_up_g = (b_up[i].astype(F32).reshape(N_EXPERTS, 2 * D_FF // 256, LANES, 2).transpose(0, 1, 3, 2)
                  .reshape(N_EXPERTS, 1, 2 * D_FF))
        x_sorted = _sc_gather_rows(m, tok)
        y_sorted = _moe_call(x_sorted, be, nval, off, w_up[i], b_up_g, w_down[i],
                             b_down[i].reshape(N_EXPERTS, 1, D_MODEL), perm, tm_moe, nb_moe)
        y4 = _sc_scatter_rows(y_sorted, dst, dst.shape[0])

        hp, hs = _ple_call(h1, y4, gates,
                           p_prompt[i].reshape(tp, -1), p_sample[i].reshape(ts, -1), _row(norm_ple_g[i]),
                           w_ple_gate[i].astype(BF16), w_ple_proj[i].astype(BF16), _row(norm_final_g), tm)

    y_prompt = hp.reshape(bp, lp, d)
    y_sample = hs.reshape(bs, ls, d)
    return (y_prompt, y_sample, jnp.stack(ssm_p), jnp.stack(conv_p), jnp.stack(ssm_s), jnp.stack(conv_s),
            jnp.stack(v_s))
```

```python
import functools

import jax
import jax.numpy as jnp
from jax import lax
from jax.experimental import pallas as pl
from jax.experimental.pallas import tpu as pltpu
from jax.experimental.pallas import tpu_sc as plsc

F32 = jnp.float32
BF16 = jnp.bfloat16
I32 = jnp.int32

EPS = 1e-6
D_MODEL = 1024
SSD_WIDTH = 512
SSD_HEADS = 8
HEAD_DIM = 64
SSD_GROUPS = 2
D_STATE = 128
CONV_W = 4
CONV_DIM = SSD_WIDTH + 2 * SSD_GROUPS * D_STATE
MLP_WIDTH = 512
MLP_HEADS = 8
N_EXPERTS = 32
TOP_K = 4
D_FF = 1024
SWIGLU_LIMIT = 7.0
SWIGLU_ALPHA = 1.702
TOPK_SHIFT = 2
assert 1 << TOPK_SHIFT == TOP_K
LANES = 128
CHUNK = 128
DT_PAD = LANES
IN_PAD = SSD_WIDTH + CONV_DIM + 2 * MLP_WIDTH + DT_PAD
VMEM_LIMIT = 56 * 1024 * 1024


def _cparams(sem):
    return pltpu.CompilerParams(dimension_semantics=sem, vmem_limit_bytes=VMEM_LIMIT)


def _const_spec(shape):
    return pl.BlockSpec(shape, lambda *_: (0,) * len(shape))


def _rms(x):
    return x * lax.rsqrt(jnp.mean(x * x, axis=-1, keepdims=True) + EPS)


def _dot(a, b):
    return jnp.dot(a, b, preferred_element_type=F32)


def _dot_nt(a, b):
    return lax.dot_general(a, b, (((1,), (1,)), ((), ())), preferred_element_type=F32)


def _split3(x):
    hi = x.astype(BF16)
    r = x - hi.astype(F32)
    mid = r.astype(BF16)
    lo = (r - mid.astype(F32)).astype(BF16)
    return hi, mid, lo


def _sel_right(x, m01):
    hi, mid, lo = _split3(x)
    return _dot(hi, m01) + _dot(mid, m01) + _dot(lo, m01)


def _sel_left(m01, x):
    hi, mid, lo = _split3(x)
    return _dot(m01, hi) + _dot(m01, mid) + _dot(m01, lo)


def _softplus(x):
    return jnp.maximum(x, 0.0) + jnp.log1p(jnp.exp(-jnp.abs(x)))


def _inproj_call(xp, xs, g, w, tm):
    tp, ts = xp.shape[0], xs.shape[0]
    n_p, n_s = tp // tm, ts // tm
    t_all = tp + ts
    segs = ((0, 512), (512, 1536), (1536, 2048), (2048, 2560), (2560, IN_PAD))

    def body(xp_ref, xs_ref, g_ref, w_ref, *outs):
        def run(x_ref):
            xn = (_rms(x_ref[...]) * g_ref[...]).astype(BF16)
            for (a, b), o in zip(segs, outs):
                o[...] = _dot(xn, w_ref[:, a:b])

        i = pl.program_id(0)

        @pl.when(i < n_p)
        def _():
            run(xp_ref)

        @pl.when(i >= n_p)
        def _():
            run(xs_ref)

    widths = [b - a for a, b in segs]
    return pl.pallas_call(
        body,
        out_shape=[jax.ShapeDtypeStruct((t_all, wd), F32) for wd in widths],
        grid=(n_p + n_s,),
        in_specs=[
            pl.BlockSpec((tm, D_MODEL), lambda i: (jnp.minimum(i, n_p - 1), 0)),
            pl.BlockSpec((tm, D_MODEL), lambda i: (jnp.maximum(i - n_p, 0), 0)),
            _const_spec((1, D_MODEL)),
            _const_spec((D_MODEL, IN_PAD)),
        ],
        out_specs=[pl.BlockSpec((tm, wd), lambda i: (i, 0)) for wd in widths],
        compiler_params=_cparams(("arbitrary",)),
        name="in_proj",
    )(xp, xs, g, w)


def _mixer_front(conv, dtr, dtb, alog, alog_x, rexp, tril, seg_ones):
    xact = conv * jax.nn.sigmoid(conv)
    xs = xact[:, :SSD_WIDTH]
    bm = xact[:, SSD_WIDTH:SSD_WIDTH + 256]
    cm = xact[:, SSD_WIDTH + 256:]
    dt = _softplus(dtr + dtb)
    a = dt * (-jnp.exp(alog))
    dt_x = _sel_right(dt, rexp)
    a_x = dt_x * (-jnp.exp(alog_x))
    acum = _sel_left(tril, a)
    acum_x = _sel_left(tril, a_x)
    if seg_ones is None:
        r = acum_x.shape[0]
        tot_x = jnp.broadcast_to(acum_x[r - 1:r, :], acum_x.shape)
    else:
        tot_x = _sel_left(seg_ones, a_x)
    return xs, bm, cm, dt_x, acum, acum_x, tot_x


def _ssd_intra(cmb, bmb, acum, xdt, mask):
    r = acum.shape[0]
    acum_t = acum.T
    lane = lax.broadcasted_iota(I32, (r, LANES), 1)
    low = lane < HEAD_DIM
    outs = []
    for g in range(SSD_GROUPS):
        sg = _dot_nt(cmb[:, LANES * g:LANES * (g + 1)], bmb[:, LANES * g:LANES * (g + 1)])
        for k in (2 * g, 2 * g + 1):
            parts = []
            for h in (2 * k, 2 * k + 1):
                seg = acum[:, h:h + 1] - acum_t[h:h + 1, :]
                parts.append((sg * jnp.exp(jnp.where(mask, seg, -jnp.inf))).astype(BF16))
            lhs = jnp.concatenate(parts, axis=1)
            xd = xdt[:, LANES * k:LANES * (k + 1)]
            rhs = jnp.concatenate([jnp.where(low, xd, 0.0), jnp.where(low, 0.0, xd)], axis=0).astype(BF16)
            outs.append(_dot(lhs, rhs))
    return jnp.concatenate(outs, axis=1)


def _mixer_back(y, z, u, v, sng, vng, vnb, wsp_ref, bsp, mog):
    r = y.shape[0]
    yg = y * (z * jax.nn.sigmoid(z))
    halves = []
    for g in range(SSD_GROUPS):
        t = yg[:, 256 * g:256 * (g + 1)]
        halves.append(_rms(t))
    yn = jnp.concatenate(halves, axis=1) * sng
    ug = jax.nn.gelu(u)
    vg = jax.nn.gelu(v)
    mu = jnp.mean(vg, axis=-1, keepdims=True)
    var = jnp.mean(jnp.square(vg - mu), axis=-1, keepdims=True)
    v_ln = (vg - mu) * lax.rsqrt(var + EPS) * vng + vnb
    lane = lax.broadcasted_iota(I32, (r, LANES), 1)
    low = lane < HEAD_DIM
    outs = []
    for k in range(MLP_HEADS // 2):
        vd = v_ln[:, LANES * k:LANES * (k + 1)]
        rhs = jnp.concatenate([jnp.where(low, vd, 0.0), jnp.where(low, 0.0, vd)], axis=0).astype(BF16)
        outs.append(_dot(wsp_ref[k], rhs))
    s = jnp.concatenate(outs, axis=1) + bsp
    m = _rms(ug * s) * mog
    return jnp.concatenate([yn, m], axis=1).astype(BF16), v_ln


_MIXER_PARAM_SHAPES = (
    (CONV_W, CONV_DIM), (1, CONV_DIM), (1, LANES), (1, LANES), (1, SSD_WIDTH), (LANES, SSD_WIDTH),
    (CHUNK, CHUNK), (1, SSD_WIDTH), (1, SSD_WIDTH), (1, MLP_WIDTH), (1, MLP_WIDTH),
    (MLP_HEADS // 2, CHUNK, 2 * CHUNK), (CHUNK, MLP_WIDTH), (1, MLP_WIDTH),
)


def _prompt_mixer_body(z_ref, xbc_ref, u_ref, v_ref, dt_ref,
                       cw_ref, cb_ref, dtb_ref, alog_ref, alogx_ref, rexp_ref, tril_ref, dskip_ref,
                       sng_ref, vng_ref, vnb_ref, wsp_ref, bsp_ref, mog_ref,
                       cat_ref, ssm_ref, ext_scr, s_scr):
    c = pl.program_id(1)
    r = CHUNK

    @pl.when(c == 0)
    def _():
        ext_scr[0:8, :] = jnp.zeros((8, CONV_DIM), F32)
        s_scr[...] = jnp.zeros_like(s_scr)

    x = xbc_ref[...]
    ext_scr[8:8 + r, :] = x
    cw = cw_ref[...]
    conv = (cb_ref[...] + cw[3:4] * x + cw[2:3] * ext_scr[7:7 + r, :]
            + cw[1:2] * ext_scr[6:6 + r, :] + cw[0:1] * ext_scr[5:5 + r, :])
    ext_scr[0:8, :] = x[r - 8:r, :]

    xs, bm, cm, dt_x, acum, acum_x, tot_x = _mixer_front(
        conv, dt_ref[...], dtb_ref[...], alog_ref[...], alogx_ref[...], rexp_ref[...], tril_ref[...], None)
    bmb, cmb = bm.astype(BF16), cm.astype(BF16)
    xdt = xs * dt_x
    row = lax.broadcasted_iota(I32, (r, r), 0)
    col = lax.broadcasted_iota(I32, (r, r), 1)
    y_diag = _ssd_intra(cmb, bmb, acum, xdt, row >= col)

    s_prev = s_scr[...]
    s_prev_b = s_prev.astype(BF16)
    y_off = jnp.concatenate(
        [_dot_nt(cmb[:, LANES * g:LANES * (g + 1)], s_prev_b[256 * g:256 * (g + 1), :]) for g in range(SSD_GROUPS)],
        axis=1)
    y = y_diag + y_off * jnp.exp(acum_x) + dskip_ref[...] * xs

    w_t = (xdt * jnp.exp(tot_x - acum_x)).T.astype(BF16)
    states = jnp.concatenate(
        [_dot(w_t[256 * g:256 * (g + 1), :], bmb[:, LANES * g:LANES * (g + 1)]) for g in range(SSD_GROUPS)], axis=0)
    s_new = s_prev * jnp.exp(tot_x).T + states
    s_scr[...] = s_new

    cat, _ = _mixer_back(y, z_ref[...], u_ref[...], v_ref[...], sng_ref[...], vng_ref[...], vnb_ref[...],
                         wsp_ref, bsp_ref[...], mog_ref[...])
    cat_ref[...] = cat

    @pl.when(c == pl.num_programs(1) - 1)
    def _():
        ssm_ref[0] = s_new


def _prompt_mixer_call(z, xbc, u, v, dtr, params, nb, nc):
    row = lambda b, c: (b * nc + c, 0)
    in_specs = [
        pl.BlockSpec((CHUNK, SSD_WIDTH), row), pl.BlockSpec((CHUNK, CONV_DIM), row),
        pl.BlockSpec((CHUNK, MLP_WIDTH), row), pl.BlockSpec((CHUNK, MLP_WIDTH), row),
        pl.BlockSpec((CHUNK, DT_PAD), row),
    ] + [_const_spec(s) for s in _MIXER_PARAM_SHAPES]
    return pl.pallas_call(
        _prompt_mixer_body,
        out_shape=[jax.ShapeDtypeStruct((nb * nc * CHUNK, D_MODEL), BF16),
                   jax.ShapeDtypeStruct((nb, SSD_WIDTH, D_STATE), F32)],
        grid=(nb, nc),
        in_specs=in_specs,
        out_specs=[pl.BlockSpec((CHUNK, D_MODEL), row),
                   pl.BlockSpec((1, SSD_WIDTH, D_STATE), lambda b, c: (b, 0, 0))],
        scratch_shapes=[pltpu.VMEM((CHUNK + 8, CONV_DIM), F32), pltpu.VMEM((SSD_WIDTH, D_STATE), F32)],
        compiler_params=_cparams(("arbitrary", "arbitrary")),
        name="prompt_mixer",
    )(z, xbc, u, v, dtr, *params)


def _sample_mixer_body(seq_len, z_ref, x0_ref, x1_ref, x2_ref, x3_ref, u_ref, v_ref, dt_ref, h_ref,
                       cw_ref, cb_ref, dtb_ref, alog_ref, alogx_ref, rexp_ref, tril_ref, dskip_ref,
                       sng_ref, vng_ref, vnb_ref, wsp_ref, bsp_ref, mog_ref, segones_ref,
                       cat_ref, vout_ref, hout_ref, cm_scr, bm_scr, wt_scr, dtt_scr, yoff_scr):
    r = CHUNK
    shift = seq_len.bit_length() - 1
    cw = cw_ref[...]
    conv = (cb_ref[...] + cw[3:4] * x0_ref[...] + cw[2:3] * x1_ref[...]
            + cw[1:2] * x2_ref[...] + cw[0:1] * x3_ref[...])
    xs, bm, cm, dt_x, acum, acum_x, tot_x = _mixer_front(
        conv, dt_ref[...], dtb_ref[...], alog_ref[...], alogx_ref[...], rexp_ref[...], tril_ref[...],
        segones_ref[...])
    bmb, cmb = bm.astype(BF16), cm.astype(BF16)
    xdt = xs * dt_x
    row = lax.broadcasted_iota(I32, (r, r), 0)
    col = lax.broadcasted_iota(I32, (r, r), 1)
    same = lax.shift_right_logical(row, shift) == lax.shift_right_logical(col, shift)
    y_diag = _ssd_intra(cmb, bmb, acum, xdt, same & (row >= col))

    cm_scr[...] = cm
    bm_scr[...] = bmb
    wt_scr[...] = (xdt * jnp.exp(tot_x - acum_x)).T
    dtt_scr[...] = jnp.exp(tot_x).T
    ones_b = jnp.ones((LANES, LANES), BF16)
    seqs_per_slab = 8 // seq_len

    def slab(j, carry):
        rows = pl.ds(pl.multiple_of(8 * j, 8), 8)
        cms = cm_scr[rows, :].astype(BF16)
        sub = lax.broadcasted_iota(I32, (8, 256), 0)
        lane = lax.broadcasted_iota(I32, (256, LANES), 1)
        for g in range(SSD_GROUPS):
            q_rows = slice(256 * g, 256 * (g + 1))
            acc = jnp.zeros((8, 256), F32)
            for q in range(seqs_per_slab):
                s = seqs_per_slab * j + q
                y_s = _dot_nt(cms[:, LANES * g:LANES * (g + 1)], h_ref[s, q_rows, :].astype(BF16))
                acc = jnp.where(lax.shift_right_logical(sub, shift) == q, y_s, acc)
            yoff_scr[rows, 256 * g:256 * (g + 1)] = acc
            for q in range(seqs_per_slab):
                s = seqs_per_slab * j + q
                w_sel = jnp.where(lax.shift_right_logical(lane, shift) == s, wt_scr[q_rows, :], 0.0).astype(BF16)
                st = _dot(w_sel, bm_scr[:, LANES * g:LANES * (g + 1)])
                d_sel = jnp.where(lane == s * seq_len, dtt_scr[q_rows, :], 0.0)
                hout_ref[s, q_rows, :] = h_ref[s, q_rows, :] * _sel_right(d_sel, ones_b) + st
        return carry

    lax.fori_loop(0, r // 8, slab, 0)

    y = y_diag + yoff_scr[...] * jnp.exp(acum_x) + dskip_ref[...] * xs
    cat, v_ln = _mixer_back(y, z_ref[...], u_ref[...], v_ref[...], sng_ref[...], vng_ref[...], vnb_ref[...],
                            wsp_ref, bsp_ref[...], mog_ref[...])
    cat_ref[...] = cat
    vout_ref[...] = v_ln


def _sample_mixer_call(z, x_shift, u, v, dtr, h0, params, seg_ones, row0, seq_len):
    ts = x_shift[0].shape[0]
    n = ts // CHUNK
    spt = CHUNK // seq_len
    off = lambda i: (row0 + i, 0)
    loc = lambda i: (i, 0)
    st3 = lambda i: (i, 0, 0)
    in_specs = (
        [pl.BlockSpec((CHUNK, SSD_WIDTH), off)]
        + [pl.BlockSpec((CHUNK, CONV_DIM), loc)] * 4
        + [pl.BlockSpec((CHUNK, MLP_WIDTH), off), pl.BlockSpec((CHUNK, MLP_WIDTH), off),
           pl.BlockSpec((CHUNK, DT_PAD), off), pl.BlockSpec((spt, SSD_WIDTH, D_STATE), st3)]
        + [_const_spec(s) for s in _MIXER_PARAM_SHAPES] + [_const_spec((CHUNK, CHUNK))])
    return pl.pallas_call(
        functools.partial(_sample_mixer_body, seq_len),
        out_shape=[jax.ShapeDtypeStruct((ts, D_MODEL), BF16), jax.ShapeDtypeStruct((ts, MLP_WIDTH), F32),
                   jax.ShapeDtypeStruct(h0.shape, F32)],
        grid=(n,),
        in_specs=in_specs,
        out_specs=[pl.BlockSpec((CHUNK, D_MODEL), loc), pl.BlockSpec((CHUNK, MLP_WIDTH), loc),
                   pl.BlockSpec((spt, SSD_WIDTH, D_STATE), st3)],
        scratch_shapes=[pltpu.VMEM((CHUNK, 256), F32), pltpu.VMEM((CHUNK, 256), BF16),
                        pltpu.VMEM((SSD_WIDTH, CHUNK), F32), pltpu.VMEM((SSD_WIDTH, CHUNK), F32),
                        pltpu.VMEM((CHUNK, SSD_WIDTH), F32)],
        compiler_params=_cparams(("arbitrary",)),
        name="sample_mixer",
    )(z, *x_shift, u, v, dtr, h0, *params, seg_ones)


def _out_router_call(cat_p, cat_s, xp, xs, w_out, g_moe, wr_hi, wr_lo, b_r, tm):
    tp, ts = xp.shape[0], xs.shape[0]
    n_p, n_s = tp // tm, ts // tm
    t_all = tp + ts

    def body(cp_ref, cs_ref, xp_ref, xs_ref, wo_ref, g_ref, wh_ref, wl_ref, br_ref,
             h1_ref, m_ref, eid_ref, gate_ref):
        def run(c_ref, x_ref):
            h1 = x_ref[...] + _dot(c_ref[...], wo_ref[...])
            h1_ref[...] = h1
            m = _rms(h1) * g_ref[...]
            for j in range(D_MODEL // LANES):
                m_ref[:, j, :] = m[:, LANES * j:LANES * (j + 1)]
            m_hi = m.astype(BF16)
            m_lo = (m - m_hi.astype(F32)).astype(BF16)
            logits = _dot(m_hi, wh_ref[...]) + _dot(m_lo, wh_ref[...]) + _dot(m_hi, wl_ref[...]) + br_ref[...]
            lane = lax.broadcasted_iota(I32, logits.shape, 1).astype(F32)
            work = logits
            vals, ids = [], []
            for _ in range(TOP_K):
                mx = jnp.max(work, axis=-1, keepdims=True)
                idx = jnp.min(jnp.where(work == mx, lane, float(LANES)), axis=-1, keepdims=True)
                vals.append(mx)
                ids.append(idx)
                work = jnp.where(lane == idx, -jnp.inf, work)
            ex = [jnp.exp(vv - vals[0]) for vv in vals]
            den = ex[0] + ex[1] + ex[2] + ex[3]
            eid = jnp.zeros(logits.shape, I32)
            gate = jnp.zeros(logits.shape, F32)
            for k in range(TOP_K):
                eid = jnp.where(lane == k, ids[k].astype(I32), eid)
                gate = jnp.where(lane == k, ex[k] / den, gate)
            eid_ref[...] = eid
            gate_ref[...] = gate

        i = pl.program_id(0)

        @pl.when(i < n_p)
        def _():
            run(cp_ref, xp_ref)

        @pl.when(i >= n_p)
        def _():
            run(cs_ref, xs_ref)

    pmap = lambda i: (jnp.minimum(i, n_p - 1), 0)
    smap = lambda i: (jnp.maximum(i - n_p, 0), 0)
    omap = lambda i: (i, 0)
    return pl.pallas_call(
        body,
        out_shape=[jax.ShapeDtypeStruct((t_all, D_MODEL), F32),
                   jax.ShapeDtypeStruct((t_all, D_MODEL // LANES, LANES), F32),
                   jax.ShapeDtypeStruct((t_all, LANES), I32), jax.ShapeDtypeStruct((t_all, LANES), F32)],
        grid=(n_p + n_s,),
        in_specs=[pl.BlockSpec((tm, D_MODEL), pmap), pl.BlockSpec((tm, D_MODEL), smap),
                  pl.BlockSpec((tm, D_MODEL), pmap), pl.BlockSpec((tm, D_MODEL), smap),
                  _const_spec((D_MODEL, D_MODEL)), _const_spec((1, D_MODEL)),
                  _const_spec((D_MODEL, LANES)), _const_spec((D_MODEL, LANES)), _const_spec((1, LANES))],
        out_specs=[pl.BlockSpec((tm, D_MODEL), omap),
                   pl.BlockSpec((tm, D_MODEL // LANES, LANES), lambda i: (i, 0, 0)),
                   pl.BlockSpec((tm, LANES), omap), pl.BlockSpec((tm, LANES), omap)],
        compiler_params=_cparams(("arbitrary",)),
        name="out_router",
    )(cat_p, cat_s, xp, xs, w_out, g_moe, wr_hi, wr_lo, b_r)


def _route(eid, tm, nb):
    t = eid.shape[0]
    tk = t * TOP_K
    flat = eid.reshape(tk)
    experts = jnp.arange(N_EXPERTS, dtype=I32)
    real = jnp.sum((flat[:, None] == experts[None, :]).astype(I32), axis=0)
    need = (-real) % ROW_ALIGN
    fill = jnp.arange(ROW_ALIGN - 1, dtype=I32)
    fill_key = jnp.where(fill[None, :] < need[:, None], experts[:, None], N_EXPERTS).reshape(-1)
    n_fill = fill_key.shape[0]
    keys = jnp.concatenate([flat, fill_key])
    _, order = lax.sort((keys, jnp.arange(tk + n_fill, dtype=I32)), num_keys=1, is_stable=True)
    counts = real + need
    nblk = (counts + tm - 1) // tm
    bend = jnp.cumsum(nblk)
    bstart = bend - nblk
    start = jnp.cumsum(counts) - counts
    nused = bend[-1]
    blk = jnp.arange(nb, dtype=I32)
    used = blk < nused
    be = jnp.minimum(jnp.sum((jnp.minimum(blk, nused - 1)[:, None] >= bend[None, :]).astype(I32), axis=1),
                     N_EXPERTS - 1)
    sel = (be[:, None] == jnp.arange(N_EXPERTS, dtype=I32)[None, :]).astype(I32)
    pick = lambda v: jnp.sum(sel * v[None, :], axis=1)
    done = (blk - pick(bstart)) * tm
    nval = jnp.where(used, jnp.clip(pick(counts) - done, 0, tm), 0).astype(I32)
    off = jnp.where(used, pick(start) + done, 0).astype(I32)
    n_sorted = _sorted_rows(tk + n_fill, tm)
    pad = n_sorted - tk - n_fill
    is_real = order < tk
    tok = jnp.pad(jnp.where(is_real, lax.shift_right_logical(order, TOPK_SHIFT), 0), (0, pad))
    dst = jnp.concatenate([
        jnp.where(is_real, (order & (TOP_K - 1)) * t + lax.shift_right_logical(order, TOPK_SHIFT), order),
        tk + n_fill + jnp.arange(pad, dtype=I32)])
    return be, nval, off, tok, dst


ROW_ALIGN = 8
SC_WINDOW = 128
SC_COLS = 256
SC_GROUP = 32
SC_TILES = 32


def _sorted_rows(tk, tm):
    unit = SC_WINDOW * SC_TILES
    return -(-(tk + tm) // unit) * unit


def _sc_mesh():
    return plsc.VectorSubcoreMesh(core_axis_name="core", subcore_axis_name="subcore")


def _sc_gather_rows(x, idx):
    n = idx.shape[0]
    per_core = n // (2 * SC_GROUP)
    idx_rows = jnp.pad(idx.reshape(n // SC_GROUP, SC_GROUP), ((0, 0), (0, SC_WINDOW - SC_GROUP)))

    @pl.kernel(out_type=jax.ShapeDtypeStruct((n,) + x.shape[1:], x.dtype), mesh=_sc_mesh(), name="sc_gather_rows")
    def run(x_hbm, i_hbm, o_hbm):
        def body(i_vmem, o_vmem):
            pltpu.sync_copy(x_hbm.at[i_vmem.at[0, pl.ds(0, SC_GROUP)]], o_vmem)

        pltpu.emit_pipeline(
            body, grid=(2, per_core),
            in_specs=[pl.BlockSpec((1, SC_WINDOW), lambda c, j: (c * per_core + j, 0))],
            out_specs=[pl.BlockSpec((SC_GROUP,) + x.shape[1:], lambda c, j: (c * per_core + j, 0, 0))],
            core_axis_name=("core", "subcore"),
            dimension_semantics=(pltpu.PARALLEL, pltpu.PARALLEL),
        )(i_hbm, o_hbm)

    return run(x, idx_rows)


def _sc_scatter_rows(y, idx, rows_out):
    n, d = y.shape
    per_core = n // (2 * SC_WINDOW)

    @pl.kernel(out_type=jax.ShapeDtypeStruct((rows_out, d), y.dtype), mesh=_sc_mesh(), name="sc_scatter_rows")
    def run(y_hbm, i_hbm, o_hbm):
        for f in range(d // SC_COLS):
            o_cols = o_hbm.at[:, pl.ds(SC_COLS * f, SC_COLS)]

            def body(y_vmem, i_vmem, o_cols=o_cols):
                pltpu.sync_copy(y_vmem, o_cols.at[i_vmem.at[0]])

            pltpu.emit_pipeline(
                body, grid=(2, per_core),
                in_specs=[pl.BlockSpec((SC_WINDOW, SC_COLS), functools.partial(
                              lambda f, c, j: (c * per_core + j, f), f)),
                          pl.BlockSpec((1, SC_WINDOW), lambda c, j: (c, j))],
                out_specs=[],
                core_axis_name=("core", "subcore"),
                dimension_semantics=(pltpu.PARALLEL, pltpu.PARALLEL),
            )(y_hbm, i_hbm)

    return run(y, idx.reshape(2, n // 2))


def _moe_body(tm, nb, be_ref, nval_ref, off_ref, x_hbm, wup_ref, bup_ref, wdn_ref, bdn_ref, perm_ref,
              y_hbm, xbuf, ybuf, wup_b, wdn_b, xsem, ysem):
    i = pl.program_id(0)
    nv = nval_ref[i]
    slot = i & 1
    prv = jnp.maximum(i - 1, 0)
    nxt = jnp.minimum(i + 1, nb - 1)
    has_next = (i + 1 < nb) & (nval_ref[nxt] > 0)

    def rows(b):
        return pl.ds(pl.multiple_of(off_ref[b], ROW_ALIGN), tm)

    def x_copy(b, s):
        return pltpu.make_async_copy(x_hbm.at[rows(b)], xbuf.at[s], xsem.at[s])

    def y_copy(b, s):
        return pltpu.make_async_copy(ybuf.at[s], y_hbm.at[rows(b)], ysem.at[s])

    @pl.when(nv > 0)
    def _():
        @pl.when(i == 0)
        def _():
            x_copy(0, 0).start()

        @pl.when(has_next)
        def _():
            x_copy(nxt, 1 - slot).start()

        @pl.when((i == 0) | (be_ref[i] != be_ref[prv]))
        def _():
            for jb in range(2 * D_FF // 256):
                cols = slice(256 * jb, 256 * (jb + 1))
                wup_b[:, cols] = _dot(wup_ref[0, :, cols].astype(BF16), perm_ref[...]).astype(BF16)
            wdn_b[...] = wdn_ref[0].astype(BF16)

        x_copy(i, slot).wait()

        x = jnp.concatenate([xbuf[slot, :, j, :] for j in range(D_MODEL // LANES)], axis=1).astype(BF16)
        acts = []
        for jb in range(D_FF // LANES):
            h = _dot(x, wup_b[:, 256 * jb:256 * (jb + 1)]) + bup_ref[0, :, 256 * jb:256 * (jb + 1)]
            gate = jnp.minimum(h[:, :LANES], SWIGLU_LIMIT)
            lin = jnp.clip(h[:, LANES:], -SWIGLU_LIMIT, SWIGLU_LIMIT)
            acts.append((gate * jax.nn.sigmoid(SWIGLU_ALPHA * gate) * (lin + 1.0)).astype(BF16))
        act = jnp.concatenate(acts, axis=1)
        for c in range(D_MODEL // 256):
            ybuf[slot, :, 256 * c:256 * (c + 1)] = (
                _dot(act, wdn_b[:, 256 * c:256 * (c + 1)]) + bdn_ref[0, :, 256 * c:256 * (c + 1)])

        @pl.when(i >= 1)
        def _():
            y_copy(prv, 1 - slot).wait()
        y_copy(i, slot).start()

        @pl.when(jnp.logical_not(has_next))
        def _():
            y_copy(i, slot).wait()


def _moe_call(x_sorted, be, nval, off, w_up, b_up_g, w_down, b_down, perm, tm, nb):
    by_expert = lambda i, be, nv, off: (be[i], 0, 0)
    grid_spec = pltpu.PrefetchScalarGridSpec(
        num_scalar_prefetch=3,
        grid=(nb,),
        in_specs=[
            pl.BlockSpec(memory_space=pl.ANY),
            pl.BlockSpec((1, D_MODEL, 2 * D_FF), by_expert),
            pl.BlockSpec((1, 1, 2 * D_FF), by_expert),
            pl.BlockSpec((1, D_FF, D_MODEL), by_expert),
            pl.BlockSpec((1, 1, D_MODEL), by_expert),
            pl.BlockSpec((256, 256), lambda i, be, nv, off: (0, 0)),
        ],
        out_specs=pl.BlockSpec(memory_space=pl.ANY),
        scratch_shapes=[pltpu.VMEM((2, tm, D_MODEL // LANES, LANES), F32), pltpu.VMEM((2, tm, D_MODEL), F32),
                        pltpu.VMEM((D_MODEL, 2 * D_FF), BF16), pltpu.VMEM((D_FF, D_MODEL), BF16),
                        pltpu.SemaphoreType.DMA((2,)), pltpu.SemaphoreType.DMA((2,))],
    )
    return pl.pallas_call(
        functools.partial(_moe_body, tm, nb),
        out_shape=jax.ShapeDtypeStruct((x_sorted.shape[0], D_MODEL), F32),
        grid_spec=grid_spec,
        compiler_params=_cparams(("arbitrary",)),
        name="moe_experts",
    )(be, nval, off, x_sorted, w_up, b_up_g, w_down, b_down, perm)


def _ple_call(h1, y4, gates, pp, ps, g_ple, w_gate, w_proj, g_final, tm):
    tp, ts = pp.shape[0], ps.shape[0]
    n_p, n_s = tp // tm, ts // tm
    ple = pp.shape[1]

    def body(h1_ref, y0_ref, y1_ref, y2_ref, y3_ref, gt_ref, pp_ref, ps_ref, g_ref, wg_ref, wp_ref, gf_ref,
             yp_ref, ys_ref):
        def run(p_ref, o_ref):
            gt = gt_ref[...]
            moe = None
            for k, y_ref in enumerate((y0_ref, y1_ref, y2_ref, y3_ref)):
                moe = gt[:, k:k + 1] * y_ref[...] if moe is None else moe + gt[:, k:k + 1] * y_ref[...]
            h2 = h1_ref[...] + moe
            a = (_rms(h2) * g_ref[...]).astype(BF16)
            gate = jax.nn.sigmoid(_dot(a, wg_ref[...]))
            pe = _dot(p_ref[...].astype(BF16), wp_ref[...])
            h3 = h2 + pe * gate
            o_ref[...] = _rms(h3) * gf_ref[...]

        i = pl.program_id(0)

        @pl.when(i < n_p)
        def _():
            run(pp_ref, yp_ref)

        @pl.when(i >= n_p)
        def _():
            run(ps_ref, ys_ref)

    pmap = lambda i: (jnp.minimum(i, n_p - 1), 0)
    smap = lambda i: (jnp.maximum(i - n_p, 0), 0)
    omap = lambda i: (i, 0)
    return pl.pallas_call(
        body,
        out_shape=[jax.ShapeDtypeStruct((tp, D_MODEL), F32), jax.ShapeDtypeStruct((ts, D_MODEL), F32)],
        grid=(n_p + n_s,),
        in_specs=[pl.BlockSpec((tm, D_MODEL), omap)]
                 + [pl.BlockSpec((tm, D_MODEL), functools.partial(lambda k, i: (k * (n_p + n_s) + i, 0), k))
                    for k in range(TOP_K)]
                 + [pl.BlockSpec((tm, LANES), omap), pl.BlockSpec((tm, ple), pmap), pl.BlockSpec((tm, ple), smap),
                  _const_spec((1, D_MODEL)), _const_spec((D_MODEL, D_MODEL)), _const_spec((ple, D_MODEL)),
                  _const_spec((1, D_MODEL))],
        out_specs=[pl.BlockSpec((tm, D_MODEL), pmap), pl.BlockSpec((tm, D_MODEL), smap)],
        compiler_params=_cparams(("arbitrary",)),
        name="ple_final",
    )(h1, y4, y4, y4, y4, gates, pp, ps, g_ple, w_gate, w_proj, g_final)


def _row(x, width=None):
    x = x.reshape(1, -1).astype(F32)
    if width is not None and x.shape[1] < width:
        x = jnp.pad(x, ((0, 0), (0, width - x.shape[1])))
    return x


def _mixer_params(conv_w, conv_b, dt_bias, a_log, d_skip, ssd_norm_g, v_norm_g, v_norm_b, w_spatial, b_spatial,
                  mlp_out_g, seq_len):
    n_seq = CHUNK // seq_len
    pos = jnp.arange(CHUNK) % seq_len
    same = (jnp.arange(CHUNK)[:, None] // seq_len) == (jnp.arange(CHUNK)[None, :] // seq_len)
    tril = (same & (jnp.arange(CHUNK)[:, None] >= jnp.arange(CHUNK)[None, :])).astype(BF16)
    rexp = (jnp.arange(LANES)[:, None] == (jnp.arange(SSD_WIDTH)[None, :] // HEAD_DIM)).astype(BF16)
    w_loc = jnp.tril(w_spatial[:, :seq_len, :seq_len])
    eye = jnp.eye(n_seq, dtype=F32)
    w_bd = jnp.einsum("st,hij->hsitj", eye, w_loc).reshape(MLP_HEADS, CHUNK, CHUNK)
    wsp = (w_bd.reshape(MLP_HEADS // 2, 2, CHUNK, CHUNK).transpose(0, 2, 1, 3)
           .reshape(MLP_HEADS // 2, CHUNK, 2 * CHUNK).astype(BF16))
    bsp = jnp.repeat(b_spatial[:, :seq_len].T[pos], MLP_WIDTH // MLP_HEADS, axis=1)
    params = (
        conv_w.astype(F32), _row(conv_b), _row(dt_bias, LANES), _row(a_log, LANES),
        _row(jnp.repeat(a_log, HEAD_DIM)), rexp, tril, _row(jnp.repeat(d_skip, HEAD_DIM)),
        _row(ssd_norm_g), _row(v_norm_g), _row(v_norm_b), wsp, bsp.astype(F32), _row(mlp_out_g),
    )
    return params, same.astype(BF16)


def _tile_rows(n):
    return 512 if n % 512 == 0 else CHUNK


def kernel(x_prompt, x_sample, state_ssm, state_conv, p_prompt, p_sample, norm_mix_g, w_in, conv_w, conv_b, dt_bias, a_log, d_skip, ssd_norm_g, v_norm_g, v_norm_b, w_spatial, b_spatial, mlp_out_g, w_out, norm_moe_g, w_router, b_router, w_up, b_up, w_down, b_down, norm_ple_g, w_ple_gate, w_ple_proj, norm_final_g):
    depth = norm_mix_g.shape[0]
    bp, lp, d = x_prompt.shape
    bs, ls, _ = x_sample.shape
    tp, ts = bp * lp, bs * ls
    assert depth == 1 and d == D_MODEL and lp % CHUNK == 0 and ts % CHUNK == 0 and 8 % ls == 0
    tm = _tile_rows(tp) if ts % _tile_rows(tp) == 0 else CHUNK
    t_all = tp + ts
    tm_moe = 256
    nb_moe = -(-t_all * TOP_K // tm_moe) + N_EXPERTS

    hp = x_prompt.reshape(tp, d)
    hs = x_sample.reshape(ts, d)
    ssm_p, conv_p, ssm_s, conv_s, v_s = [], [], [], [], []
    o1 = SSD_WIDTH
    o2 = o1 + CONV_DIM
    o3 = o2 + SSD_HEADS
    o4 = o3 + MLP_WIDTH
    c = jnp.arange(256)
    src = jnp.where(c < LANES, 2 * c, 2 * (c - LANES) + 1)
    perm = (jnp.arange(256)[:, None] == src[None, :]).astype(BF16)

    for i in range(depth):
        wi = w_in[i]
        w_cat = jnp.concatenate(
            [wi[:, :o2], wi[:, o3:], jnp.pad(wi[:, o2:o3], ((0, 0), (0, DT_PAD - SSD_HEADS)))], axis=1).astype(BF16)
        z, xbc, u, v, dtr = _inproj_call(hp, hs, _row(norm_mix_g[i]), w_cat, tm)

        mix_args = (conv_w[i], conv_b[i], dt_bias[i], a_log[i], d_skip[i], ssd_norm_g[i], v_norm_g[i], v_norm_b[i],
                    w_spatial[i], b_spatial[i], mlp_out_g[i])
        prm_p, _ = _mixer_params(*mix_args, seq_len=CHUNK)
        cat_p, s_p = _prompt_mixer_call(z, xbc, u, v, dtr, prm_p, bp, lp // CHUNK)
        ssm_p.append(s_p.reshape(bp, SSD_HEADS, HEAD_DIM, D_STATE).astype(state_ssm.dtype))
        conv_p.append(jnp.stack([xbc[(b + 1) * lp - (CONV_W - 1):(b + 1) * lp] for b in range(bp)]))

        prm_s, seg_ones = _mixer_params(*mix_args, seq_len=ls)
        xbc_s = xbc[tp:].reshape(bs, ls, CONV_DIM)
        xpad = jnp.concatenate([state_conv[i].astype(F32), xbc_s], axis=1)
        x_shift = [xpad[:, CONV_W - 1 - k:CONV_W - 1 - k + ls].reshape(ts, CONV_DIM) for k in range(CONV_W)]
        h0 = state_ssm[i].astype(F32).reshape(bs, SSD_WIDTH, D_STATE)
        cat_s, v_rows, s_s = _sample_mixer_call(z, x_shift, u, v, dtr, h0, prm_s, seg_ones, tp // CHUNK, ls)
        ssm_s.append(s_s.reshape(bs, SSD_HEADS, HEAD_DIM, D_STATE).astype(state_ssm.dtype))
        conv_s.append(xpad[:, ls:])
        v_s.append(v_rows.reshape(bs, ls, MLP_WIDTH))

        wr = jnp.pad(w_router[i].astype(F32), ((0, 0), (0, LANES - N_EXPERTS)))
        wr_hi = wr.astype(BF16)
        wr_lo = (wr - wr_hi.astype(F32)).astype(BF16)
        b_r = jnp.concatenate([b_router[i].astype(F32), jnp.full((LANES - N_EXPERTS,), -1e30, F32)]).reshape(1, LANES)
        h1, m, eid, gates = _out_router_call(cat_p, cat_s, hp, hs, w_out[i].astype(BF16), _row(norm_moe_g[i]),
                                             wr_hi, wr_lo, b_r, tm)

        be, nval, off, tok, dst = _route(eid[:, :TOP_K], tm_moe, nb_moe)
        b_up_g = (b_up[i].astype(F32).reshape(N_EXPERTS, 2 * D_FF // 256, LANES, 2).transpose(0, 1, 3, 2)
                  .reshape(N_EXPERTS, 1, 2 * D_FF))
        x_sorted = _sc_gather_rows(m, tok)
        y_sorted = _moe_call(x_sorted, be, nval, off, w_up[i], b_up_g, w_down[i],
                             b_down[i].reshape(N_EXPERTS, 1, D_MODEL), perm, tm_moe, nb_moe)
        y4 = _sc_scatter_rows(y_sorted, dst, dst.shape[0])

        hp, hs = _ple_call(h1, y4, gates,
                           p_prompt[i].reshape(tp, -1), p_sample[i].reshape(ts, -1), _row(norm_ple_g[i]),
                           w_ple_gate[i].astype(BF16), w_ple_proj[i].astype(BF16), _row(norm_final_g), tm)

    y_prompt = hp.reshape(bp, lp, d)
    y_sample = hs.reshape(bs, ls, d)
    return (y_prompt, y_sample, jnp.stack(ssm_p), jnp.stack(conv_p), jnp.stack(ssm_s), jnp.stack(conv_s),
            jnp.stack(v_s))
```

```python
import functools

import jax
import jax.numpy as jnp
from jax import lax
from jax.experimental import pallas as pl
from jax.experimental.pallas import tpu as pltpu

F32 = jnp.float32
BF16 = jnp.bfloat16
I32 = jnp.int32

EPS = 1e-6
D_MODEL = 1024
SSD_WIDTH = 512
SSD_HEADS = 8
HEAD_DIM = 64
SSD_GROUPS = 2
D_STATE = 128
CONV_W = 4
CONV_DIM = SSD_WIDTH + 2 * SSD_GROUPS * D_STATE
MLP_WIDTH = 512
MLP_HEADS = 8
N_EXPERTS = 32
TOP_K = 4
D_FF = 1024
SWIGLU_LIMIT = 7.0
SWIGLU_ALPHA = 1.702
TOPK_SHIFT = 2
assert 1 << TOPK_SHIFT == TOP_K
LANES = 128
CHUNK = 128
DT_PAD = LANES
IN_PAD = SSD_WIDTH + CONV_DIM + 2 * MLP_WIDTH + DT_PAD
VMEM_LIMIT = 56 * 1024 * 1024


def _cparams(sem):
    return pltpu.CompilerParams(dimension_semantics=sem, vmem_limit_bytes=VMEM_LIMIT)


def _const_spec(shape):
    return pl.BlockSpec(shape, lambda *_: (0,) * len(shape))


def _rms(x):
    return x * lax.rsqrt(jnp.mean(x * x, axis=-1, keepdims=True) + EPS)


def _dot(a, b):
    return jnp.dot(a, b, preferred_element_type=F32)


def _dot_nt(a, b):
    return lax.dot_general(a, b, (((1,), (1,)), ((), ())), preferred_element_type=F32)


def _split3(x):
    hi = x.astype(BF16)
    r = x - hi.astype(F32)
    mid = r.astype(BF16)
    lo = (r - mid.astype(F32)).astype(BF16)
    return hi, mid, lo


def _sel_right(x, m01):
    hi, mid, lo = _split3(x)
    return _dot(hi, m01) + _dot(mid, m01) + _dot(lo, m01)


def _sel_left(m01, x):
    hi, mid, lo = _split3(x)
    return _dot(m01, hi) + _dot(m01, mid) + _dot(m01, lo)


def _softplus(x):
    return jnp.maximum(x, 0.0) + jnp.log1p(jnp.exp(-jnp.abs(x)))


def _inproj_call(xp, xs, g, w, tm):
    tp, ts = xp.shape[0], xs.shape[0]
    n_p, n_s = tp // tm, ts // tm
    t_all = tp + ts
    segs = ((0, 512), (512, 1536), (1536, 2048), (2048, 2560), (2560, IN_PAD))

    def body(xp_ref, xs_ref, g_ref, w_ref, *outs):
        def run(x_ref):
            xn = (_rms(x_ref[...]) * g_ref[...]).astype(BF16)
            for (a, b), o in zip(segs, outs):
                o[...] = _dot(xn, w_ref[:, a:b])

        i = pl.program_id(0)

        @pl.when(i < n_p)
        def _():
            run(xp_ref)

        @pl.when(i >= n_p)
        def _():
            run(xs_ref)

    widths = [b - a for a, b in segs]
    return pl.pallas_call(
        body,
        out_shape=[jax.ShapeDtypeStruct((t_all, wd), F32) for wd in widths],
        grid=(n_p + n_s,),
        in_specs=[
            pl.BlockSpec((tm, D_MODEL), lambda i: (jnp.minimum(i, n_p - 1), 0)),
            pl.BlockSpec((tm, D_MODEL), lambda i: (jnp.maximum(i - n_p, 0), 0)),
            _const_spec((1, D_MODEL)),
            _const_spec((D_MODEL, IN_PAD)),
        ],
        out_specs=[pl.BlockSpec((tm, wd), lambda i: (i, 0)) for wd in widths],
        compiler_params=_cparams(("arbitrary",)),
        name="in_proj",
    )(xp, xs, g, w)


def _mixer_front(conv, dtr, dtb, alog, alog_x, rexp, tril, seg_ones):
    xact = conv * jax.nn.sigmoid(conv)
    xs = xact[:, :SSD_WIDTH]
    bm = xact[:, SSD_WIDTH:SSD_WIDTH + 256]
    cm = xact[:, SSD_WIDTH + 256:]
    dt = _softplus(dtr + dtb)
    a = dt * (-jnp.exp(alog))
    dt_x = _sel_right(dt, rexp)
    a_x = dt_x * (-jnp.exp(alog_x))
    acum = _sel_left(tril, a)
    acum_x = _sel_left(tril, a_x)
    if seg_ones is None:
        r = acum_x.shape[0]
        tot_x = jnp.broadcast_to(acum_x[r - 1:r, :], acum_x.shape)
    else:
        tot_x = _sel_left(seg_ones, a_x)
    return xs, bm, cm, dt_x, acum, acum_x, tot_x


def _ssd_intra(cmb, bmb, acum, xdt, mask):
    r = acum.shape[0]
    acum_t = acum.T
    lane = lax.broadcasted_iota(I32, (r, LANES), 1)
    low = lane < HEAD_DIM
    outs = []
    for g in range(SSD_GROUPS):
        sg = _dot_nt(cmb[:, LANES * g:LANES * (g + 1)], bmb[:, LANES * g:LANES * (g + 1)])
        for k in (2 * g, 2 * g + 1):
            parts = []
            for h in (2 * k, 2 * k + 1):
                seg = acum[:, h:h + 1] - acum_t[h:h + 1, :]
                parts.append((sg * jnp.exp(jnp.where(mask, seg, -jnp.inf))).astype(BF16))
            lhs = jnp.concatenate(parts, axis=1)
            xd = xdt[:, LANES * k:LANES * (k + 1)]
            rhs = jnp.concatenate([jnp.where(low, xd, 0.0), jnp.where(low, 0.0, xd)], axis=0).astype(BF16)
            outs.append(_dot(lhs, rhs))
    return jnp.concatenate(outs, axis=1)


def _mixer_back(y, z, u, v, sng, vng, vnb, wsp_ref, bsp, mog):
    r = y.shape[0]
    yg = y * (z * jax.nn.sigmoid(z))
    halves = []
    for g in range(SSD_GROUPS):
        t = yg[:, 256 * g:256 * (g + 1)]
        halves.append(_rms(t))
    yn = jnp.concatenate(halves, axis=1) * sng
    ug = jax.nn.gelu(u)
    vg = jax.nn.gelu(v)
    mu = jnp.mean(vg, axis=-1, keepdims=True)
    var = jnp.mean(jnp.square(vg - mu), axis=-1, keepdims=True)
    v_ln = (vg - mu) * lax.rsqrt(var + EPS) * vng + vnb
    lane = lax.broadcasted_iota(I32, (r, LANES), 1)
    low = lane < HEAD_DIM
    outs = []
    for k in range(MLP_HEADS // 2):
        vd = v_ln[:, LANES * k:LANES * (k + 1)]
        rhs = jnp.concatenate([jnp.where(low, vd, 0.0), jnp.where(low, 0.0, vd)], axis=0).astype(BF16)
        outs.append(_dot(wsp_ref[k], rhs))
    s = jnp.concatenate(outs, axis=1) + bsp
    m = _rms(ug * s) * mog
    return jnp.concatenate([yn, m], axis=1).astype(BF16), v_ln


_MIXER_PARAM_SHAPES = (
    (CONV_W, CONV_DIM), (1, CONV_DIM), (1, LANES), (1, LANES), (1, SSD_WIDTH), (LANES, SSD_WIDTH),
    (CHUNK, CHUNK), (1, SSD_WIDTH), (1, SSD_WIDTH), (1, MLP_WIDTH), (1, MLP_WIDTH),
    (MLP_HEADS // 2, CHUNK, 2 * CHUNK), (CHUNK, MLP_WIDTH), (1, MLP_WIDTH),
)


def _prompt_mixer_body(z_ref, xbc_ref, u_ref, v_ref, dt_ref,
                       cw_ref, cb_ref, dtb_ref, alog_ref, alogx_ref, rexp_ref, tril_ref, dskip_ref,
                       sng_ref, vng_ref, vnb_ref, wsp_ref, bsp_ref, mog_ref,
                       cat_ref, ssm_ref, ext_scr, s_scr):
    c = pl.program_id(1)
    r = CHUNK

    @pl.when(c == 0)
    def _():
        ext_scr[0:8, :] = jnp.zeros((8, CONV_DIM), F32)
        s_scr[...] = jnp.zeros_like(s_scr)

    x = xbc_ref[...]
    ext_scr[8:8 + r, :] = x
    cw = cw_ref[...]
    conv = (cb_ref[...] + cw[3:4] * x + cw[2:3] * ext_scr[7:7 + r, :]
            + cw[1:2] * ext_scr[6:6 + r, :] + cw[0:1] * ext_scr[5:5 + r, :])
    ext_scr[0:8, :] = x[r - 8:r, :]

    xs, bm, cm, dt_x, acum, acum_x, tot_x = _mixer_front(
        conv, dt_ref[...], dtb_ref[...], alog_ref[...], alogx_ref[...], rexp_ref[...], tril_ref[...], None)
    bmb, cmb = bm.astype(BF16), cm.astype(BF16)
    xdt = xs * dt_x
    row = lax.broadcasted_iota(I32, (r, r), 0)
    col = lax.broadcasted_iota(I32, (r, r), 1)
    y_diag = _ssd_intra(cmb, bmb, acum, xdt, row >= col)

    s_prev = s_scr[...]
    s_prev_b = s_prev.astype(BF16)
    y_off = jnp.concatenate(
        [_dot_nt(cmb[:, LANES * g:LANES * (g + 1)], s_prev_b[256 * g:256 * (g + 1), :]) for g in range(SSD_GROUPS)],
        axis=1)
    y = y_diag + y_off * jnp.exp(acum_x) + dskip_ref[...] * xs

    w_t = (xdt * jnp.exp(tot_x - acum_x)).T.astype(BF16)
    states = jnp.concatenate(
        [_dot(w_t[256 * g:256 * (g + 1), :], bmb[:, LANES * g:LANES * (g + 1)]) for g in range(SSD_GROUPS)], axis=0)
    s_new = s_prev * jnp.exp(tot_x).T + states
    s_scr[...] = s_new

    cat, _ = _mixer_back(y, z_ref[...], u_ref[...], v_ref[...], sng_ref[...], vng_ref[...], vnb_ref[...],
                         wsp_ref, bsp_ref[...], mog_ref[...])
    cat_ref[...] = cat

    @pl.when(c == pl.num_programs(1) - 1)
    def _():
        ssm_ref[0] = s_new


def _prompt_mixer_call(z, xbc, u, v, dtr, params, nb, nc):
    row = lambda b, c: (b * nc + c, 0)
    in_specs = [
        pl.BlockSpec((CHUNK, SSD_WIDTH), row), pl.BlockSpec((CHUNK, CONV_DIM), row),
        pl.BlockSpec((CHUNK, MLP_WIDTH), row), pl.BlockSpec((CHUNK, MLP_WIDTH), row),
        pl.BlockSpec((CHUNK, DT_PAD), row),
    ] + [_const_spec(s) for s in _MIXER_PARAM_SHAPES]
    return pl.pallas_call(
        _prompt_mixer_body,
        out_shape=[jax.ShapeDtypeStruct((nb * nc * CHUNK, D_MODEL), BF16),
                   jax.ShapeDtypeStruct((nb, SSD_WIDTH, D_STATE), F32)],
        grid=(nb, nc),
        in_specs=in_specs,
        out_specs=[pl.BlockSpec((CHUNK, D_MODEL), row),
                   pl.BlockSpec((1, SSD_WIDTH, D_STATE), lambda b, c: (b, 0, 0))],
        scratch_shapes=[pltpu.VMEM((CHUNK + 8, CONV_DIM), F32), pltpu.VMEM((SSD_WIDTH, D_STATE), F32)],
        compiler_params=_cparams(("arbitrary", "arbitrary")),
        name="prompt_mixer",
    )(z, xbc, u, v, dtr, *params)


def _sample_mixer_body(seq_len, z_ref, x0_ref, x1_ref, x2_ref, x3_ref, u_ref, v_ref, dt_ref, h_ref,
                       cw_ref, cb_ref, dtb_ref, alog_ref, alogx_ref, rexp_ref, tril_ref, dskip_ref,
                       sng_ref, vng_ref, vnb_ref, wsp_ref, bsp_ref, mog_ref, segones_ref,
                       cat_ref, vout_ref, hout_ref, cm_scr, bm_scr, wt_scr, dtt_scr, yoff_scr):
    r = CHUNK
    shift = seq_len.bit_length() - 1
    cw = cw_ref[...]
    conv = (cb_ref[...] + cw[3:4] * x0_ref[...] + cw[2:3] * x1_ref[...]
            + cw[1:2] * x2_ref[...] + cw[0:1] * x3_ref[...])
    xs, bm, cm, dt_x, acum, acum_x, tot_x = _mixer_front(
        conv, dt_ref[...], dtb_ref[...], alog_ref[...], alogx_ref[...], rexp_ref[...], tril_ref[...],
        segones_ref[...])
    bmb, cmb = bm.astype(BF16), cm.astype(BF16)
    xdt = xs * dt_x
    row = lax.broadcasted_iota(I32, (r, r), 0)
    col = lax.broadcasted_iota(I32, (r, r), 1)
    same = lax.shift_right_logical(row, shift) == lax.shift_right_logical(col, shift)
    y_diag = _ssd_intra(cmb, bmb, acum, xdt, same & (row >= col))

    cm_scr[...] = cm
    bm_scr[...] = bmb
    wt_scr[...] = (xdt * jnp.exp(tot_x - acum_x)).T
    dtt_scr[...] = jnp.exp(tot_x).T
    ones_b = jnp.ones((LANES, LANES), BF16)
    seqs_per_slab = 8 // seq_len

    def slab(j, carry):
        rows = pl.ds(pl.multiple_of(8 * j, 8), 8)
        cms = cm_scr[rows, :].astype(BF16)
        sub = lax.broadcasted_iota(I32, (8, 256), 0)
        lane = lax.broadcasted_iota(I32, (256, LANES), 1)
        for g in range(SSD_GROUPS):
            q_rows = slice(256 * g, 256 * (g + 1))
            acc = jnp.zeros((8, 256), F32)
            for q in range(seqs_per_slab):
                s = seqs_per_slab * j + q
                y_s = _dot_nt(cms[:, LANES * g:LANES * (g + 1)], h_ref[s, q_rows, :].astype(BF16))
                acc = jnp.where(lax.shift_right_logical(sub, shift) == q, y_s, acc)
            yoff_scr[rows, 256 * g:256 * (g + 1)] = acc
            for q in range(seqs_per_slab):
                s = seqs_per_slab * j + q
                w_sel = jnp.where(lax.shift_right_logical(lane, shift) == s, wt_scr[q_rows, :], 0.0).astype(BF16)
                st = _dot(w_sel, bm_scr[:, LANES * g:LANES * (g + 1)])
                d_sel = jnp.where(lane == s * seq_len, dtt_scr[q_rows, :], 0.0)
                hout_ref[s, q_rows, :] = h_ref[s, q_rows, :] * _sel_right(d_sel, ones_b) + st
        return carry

    lax.fori_loop(0, r // 8, slab, 0)

    y = y_diag + yoff_scr[...] * jnp.exp(acum_x) + dskip_ref[...] * xs
    cat, v_ln = _mixer_back(y, z_ref[...], u_ref[...], v_ref[...], sng_ref[...], vng_ref[...], vnb_ref[...],
                            wsp_ref, bsp_ref[...], mog_ref[...])
    cat_ref[...] = cat
    vout_ref[...] = v_ln


def _sample_mixer_call(z, x_shift, u, v, dtr, h0, params, seg_ones, row0, seq_len):
    ts = x_shift[0].shape[0]
    n = ts // CHUNK
    spt = CHUNK // seq_len
    off = lambda i: (row0 + i, 0)
    loc = lambda i: (i, 0)
    st3 = lambda i: (i, 0, 0)
    in_specs = (
        [pl.BlockSpec((CHUNK, SSD_WIDTH), off)]
        + [pl.BlockSpec((CHUNK, CONV_DIM), loc)] * 4
        + [pl.BlockSpec((CHUNK, MLP_WIDTH), off), pl.BlockSpec((CHUNK, MLP_WIDTH), off),
           pl.BlockSpec((CHUNK, DT_PAD), off), pl.BlockSpec((spt, SSD_WIDTH, D_STATE), st3)]
        + [_const_spec(s) for s in _MIXER_PARAM_SHAPES] + [_const_spec((CHUNK, CHUNK))])
    return pl.pallas_call(
        functools.partial(_sample_mixer_body, seq_len),
        out_shape=[jax.ShapeDtypeStruct((ts, D_MODEL), BF16), jax.ShapeDtypeStruct((ts, MLP_WIDTH), F32),
                   jax.ShapeDtypeStruct(h0.shape, F32)],
        grid=(n,),
        in_specs=in_specs,
        out_specs=[pl.BlockSpec((CHUNK, D_MODEL), loc), pl.BlockSpec((CHUNK, MLP_WIDTH), loc),
                   pl.BlockSpec((spt, SSD_WIDTH, D_STATE), st3)],
        scratch_shapes=[pltpu.VMEM((CHUNK, 256), F32), pltpu.VMEM((CHUNK, 256), BF16),
                        pltpu.VMEM((SSD_WIDTH, CHUNK), F32), pltpu.VMEM((SSD_WIDTH, CHUNK), F32),
                        pltpu.VMEM((CHUNK, SSD_WIDTH), F32)],
        compiler_params=_cparams(("arbitrary",)),
        name="sample_mixer",
    )(z, *x_shift, u, v, dtr, h0, *params, seg_ones)


def _out_router_call(cat_p, cat_s, xp, xs, w_out, g_moe, wr_hi, wr_lo, b_r, tm):
    tp, ts = xp.shape[0], xs.shape[0]
    n_p, n_s = tp // tm, ts // tm
    t_all = tp + ts

    def body(cp_ref, cs_ref, xp_ref, xs_ref, wo_ref, g_ref, wh_ref, wl_ref, br_ref,
             h1_ref, m_ref, eid_ref, gate_ref):
        def run(c_ref, x_ref):
            h1 = x_ref[...] + _dot(c_ref[...], wo_ref[...])
            h1_ref[...] = h1
            m = _rms(h1) * g_ref[...]
            for j in range(D_MODEL // LANES):
                m_ref[:, j, :] = m[:, LANES * j:LANES * (j + 1)]
            m_hi = m.astype(BF16)
            m_lo = (m - m_hi.astype(F32)).astype(BF16)
            logits = _dot(m_hi, wh_ref[...]) + _dot(m_lo, wh_ref[...]) + _dot(m_hi, wl_ref[...]) + br_ref[...]
            lane = lax.broadcasted_iota(I32, logits.shape, 1).astype(F32)
            work = logits
            vals, ids = [], []
            for _ in range(TOP_K):
                mx = jnp.max(work, axis=-1, keepdims=True)
                idx = jnp.min(jnp.where(work == mx, lane, float(LANES)), axis=-1, keepdims=True)
                vals.append(mx)
                ids.append(idx)
                work = jnp.where(lane == idx, -jnp.inf, work)
            ex = [jnp.exp(vv - vals[0]) for vv in vals]
            den = ex[0] + ex[1] + ex[2] + ex[3]
            eid = jnp.zeros(logits.shape, I32)
            gate = jnp.zeros(logits.shape, F32)
            for k in range(TOP_K):
                eid = jnp.where(lane == k, ids[k].astype(I32), eid)
                gate = jnp.where(lane == k, ex[k] / den, gate)
            eid_ref[...] = eid
            gate_ref[...] = gate

        i = pl.program_id(0)

        @pl.when(i < n_p)
        def _():
            run(cp_ref, xp_ref)

        @pl.when(i >= n_p)
        def _():
            run(cs_ref, xs_ref)

    pmap = lambda i: (jnp.minimum(i, n_p - 1), 0)
    smap = lambda i: (jnp.maximum(i - n_p, 0), 0)
    omap = lambda i: (i, 0)
    return pl.pallas_call(
        body,
        out_shape=[jax.ShapeDtypeStruct((t_all, D_MODEL), F32),
                   jax.ShapeDtypeStruct((t_all, D_MODEL // LANES, LANES), F32),
                   jax.ShapeDtypeStruct((t_all, LANES), I32), jax.ShapeDtypeStruct((t_all, LANES), F32)],
        grid=(n_p + n_s,),
        in_specs=[pl.BlockSpec((tm, D_MODEL), pmap), pl.BlockSpec((tm, D_MODEL), smap),
                  pl.BlockSpec((tm, D_MODEL), pmap), pl.BlockSpec((tm, D_MODEL), smap),
                  _const_spec((D_MODEL, D_MODEL)), _const_spec((1, D_MODEL)),
                  _const_spec((D_MODEL, LANES)), _const_spec((D_MODEL, LANES)), _const_spec((1, LANES))],
        out_specs=[pl.BlockSpec((tm, D_MODEL), omap),
                   pl.BlockSpec((tm, D_MODEL // LANES, LANES), lambda i: (i, 0, 0)),
                   pl.BlockSpec((tm, LANES), omap), pl.BlockSpec((tm, LANES), omap)],
        compiler_params=_cparams(("arbitrary",)),
        name="out_router",
    )(cat_p, cat_s, xp, xs, w_out, g_moe, wr_hi, wr_lo, b_r)


def _route(eid, tm, nb):
    t = eid.shape[0]
    tk = t * TOP_K
    flat = eid.reshape(tk)
    _, order = lax.sort((flat, jnp.arange(tk, dtype=I32)), num_keys=1, is_stable=True)
    counts = jnp.sum((flat[:, None] == jnp.arange(N_EXPERTS, dtype=I32)[None, :]).astype(I32), axis=0)
    nblk = (counts + tm - 1) // tm
    bend = jnp.cumsum(nblk)
    bstart = bend - nblk
    start = jnp.cumsum(counts) - counts
    nused = bend[-1]
    blk = jnp.arange(nb, dtype=I32)
    used = blk < nused
    be = jnp.minimum(jnp.sum((jnp.minimum(blk, nused - 1)[:, None] >= bend[None, :]).astype(I32), axis=1),
                     N_EXPERTS - 1)
    sel = (be[:, None] == jnp.arange(N_EXPERTS, dtype=I32)[None, :]).astype(I32)
    pick = lambda v: jnp.sum(sel * v[None, :], axis=1)
    done = (blk - pick(bstart)) * tm
    nval = jnp.where(used, jnp.clip(pick(counts) - done, 0, tm), 0).astype(I32)
    off = jnp.where(used, pick(start) + done, 0).astype(I32)
    pad = (-(-(tk + tm) // LANES) + _id_rows(tm)) * LANES - tk
    tok = jnp.pad(lax.shift_right_logical(order, TOPK_SHIFT), (0, pad))
    dst = jnp.pad((order & (TOP_K - 1)) * t + lax.shift_right_logical(order, TOPK_SHIFT), (0, pad))
    return be, nval, off, tok, dst


def _id_rows(tm):
    return tm // LANES + 1


def _moe_body(tm, t_all, nb, be_ref, nval_ref, off_ref, tok_hbm, dst_hbm, m_hbm, wup_ref, bup_ref, wdn_ref, bdn_ref,
              perm_ref, y_hbm, gids, sids, xbuf, ybuf, wup_b, wdn_b, isem, gsem, ssem):
    i = pl.program_id(0)
    nv = nval_ref[i]
    slot = i & 1
    prv = jnp.maximum(i - 1, 0)
    nxt = jnp.minimum(i + 1, nb - 1)
    nx2 = jnp.minimum(i + 2, nb - 1)
    has_next = (i + 1 < nb) & (nval_ref[nxt] > 0)
    has_next2 = (i + 2 < nb) & (nval_ref[nx2] > 0)
    n_prev = jnp.where(i > 0, nval_ref[prv], 0)
    win = _id_rows(tm) * LANES
    spare = TOP_K * t_all

    def ids_copies(b):
        start = pl.multiple_of(lax.shift_right_logical(off_ref[b], 7) * LANES, LANES)
        ring = pl.ds(pl.multiple_of((b & 3) * win, LANES), win)
        return (pltpu.make_async_copy(tok_hbm.at[pl.ds(start, win)], gids.at[ring], isem.at[b & 3, 0]),
                pltpu.make_async_copy(dst_hbm.at[pl.ds(start, win)], sids.at[ring], isem.at[b & 3, 1]))

    def id_base(b):
        return (b & 3) * win + (off_ref[b] & (LANES - 1))

    def gather_row(base, s, r):
        return pltpu.make_async_copy(m_hbm.at[gids[base + r]], xbuf.at[s, r], gsem.at[s])

    def scatter_row(base, s, r, n):
        dest = jnp.where(r < n, sids[base + r], spare + r)
        return pltpu.make_async_copy(ybuf.at[s, pl.ds(r, 1)], y_hbm.at[pl.ds(dest, 1)], ssem.at[s])

    def wait_gathers(s):
        pltpu.make_async_copy(m_hbm.at[pl.ds(0, tm)], xbuf.at[s], gsem.at[s]).wait()

    def wait_scatters(s):
        pltpu.make_async_copy(ybuf.at[s], y_hbm.at[pl.ds(0, tm)], ssem.at[s]).wait()

    def for_rows(fn):
        def one(r, c):
            fn(r)
            return c

        lax.fori_loop(0, tm, one, 0)

    @pl.when(nv > 0)
    def _():
        @pl.when(i == 0)
        def _():
            ybuf[...] = jnp.zeros_like(ybuf)
            fill = pltpu.make_async_copy(ybuf.at[0], y_hbm.at[pl.ds(spare, tm)], ssem.at[0])
            fill.start()
            fill.wait()
            for b in range(4):
                for cp in ids_copies(b):
                    cp.start()
                    cp.wait()
            base0 = id_base(0)
            for_rows(lambda r: gather_row(base0, 0, r).start())

        @pl.when(has_next & (i >= 3))
        def _():
            for cp in ids_copies(nxt):
                cp.wait()

        @pl.when(has_next2 & (i >= 2))
        def _():
            for cp in ids_copies(nx2):
                cp.start()

        @pl.when((i == 0) | (be_ref[i] != be_ref[prv]))
        def _():
            for jb in range(2 * D_FF // 256):
                cols = slice(256 * jb, 256 * (jb + 1))
                wup_b[:, cols] = _dot(wup_ref[0, :, cols].astype(BF16), perm_ref[...]).astype(BF16)
            wdn_b[...] = wdn_ref[0].astype(BF16)

        def ffn_step(cur, oth):
            wait_gathers(cur)

            @pl.when(i >= 1)
            def _():
                wait_scatters(cur)

            g_base = id_base(nxt)
            s_base = id_base(prv)
            for r in range(tm):
                gather_row(g_base, oth, r).start()
                scatter_row(s_base, oth, r, n_prev).start()

            x = jnp.concatenate([xbuf[slot, :, j, :] for j in range(D_MODEL // LANES)], axis=1).astype(BF16)
            acts = []
            for jb in range(D_FF // LANES):
                h = _dot(x, wup_b[:, 256 * jb:256 * (jb + 1)]) + bup_ref[0, :, 256 * jb:256 * (jb + 1)]
                gate = jnp.minimum(h[:, :LANES], SWIGLU_LIMIT)
                lin = jnp.clip(h[:, LANES:], -SWIGLU_LIMIT, SWIGLU_LIMIT)
                acts.append((gate * jax.nn.sigmoid(SWIGLU_ALPHA * gate) * (lin + 1.0)).astype(BF16))
            act = jnp.concatenate(acts, axis=1)
            for c in range(D_MODEL // 256):
                ybuf[slot, :, 256 * c:256 * (c + 1)] = (
                    _dot(act, wdn_b[:, 256 * c:256 * (c + 1)]) + bdn_ref[0, :, 256 * c:256 * (c + 1)])

            @pl.when(jnp.logical_not(has_next))
            def _():
                wait_gathers(oth)
                wait_scatters(oth)
                last_base = id_base(i)
                for_rows(lambda r: scatter_row(last_base, cur, r, nv).start())
                wait_scatters(cur)

        for parity in range(2):
            pl.when(slot == parity)(functools.partial(ffn_step, parity, 1 - parity))


def _moe_call(m, be, nval, off, tok, dst, w_up, b_up_g, w_down, b_down, perm, tm, nb):
    t = m.shape[0]
    assert nb >= 4
    by_expert = lambda i, be, nv, off: (be[i], 0, 0)
    grid_spec = pltpu.PrefetchScalarGridSpec(
        num_scalar_prefetch=3,
        grid=(nb,),
        in_specs=[
            pl.BlockSpec(memory_space=pl.ANY),
            pl.BlockSpec(memory_space=pl.ANY),
            pl.BlockSpec(memory_space=pl.ANY),
            pl.BlockSpec((1, D_MODEL, 2 * D_FF), by_expert),
            pl.BlockSpec((1, 1, 2 * D_FF), by_expert),
            pl.BlockSpec((1, D_FF, D_MODEL), by_expert),
            pl.BlockSpec((1, 1, D_MODEL), by_expert),
            pl.BlockSpec((256, 256), lambda i, be, nv, off: (0, 0)),
        ],
        out_specs=pl.BlockSpec(memory_space=pl.ANY),
        scratch_shapes=[pltpu.SMEM((4 * _id_rows(tm) * LANES,), I32), pltpu.SMEM((4 * _id_rows(tm) * LANES,), I32),
                        pltpu.VMEM((2, tm, 8, LANES), F32), pltpu.VMEM((2, tm, D_MODEL), F32),
                        pltpu.VMEM((D_MODEL, 2 * D_FF), BF16), pltpu.VMEM((D_FF, D_MODEL), BF16),
                        pltpu.SemaphoreType.DMA((4, 2)), pltpu.SemaphoreType.DMA((2,)),
                        pltpu.SemaphoreType.DMA((2,))],
    )
    return pl.pallas_call(
        functools.partial(_moe_body, tm, t, nb),
        out_shape=jax.ShapeDtypeStruct((TOP_K * t + tm, D_MODEL), F32),
        grid_spec=grid_spec,
        compiler_params=_cparams(("arbitrary",)),
        name="moe_experts",
    )(be, nval, off, tok, dst, m, w_up, b_up_g, w_down, b_down, perm)


def _ple_call(h1, y4, gates, pp, ps, g_ple, w_gate, w_proj, g_final, tm):
    tp, ts = pp.shape[0], ps.shape[0]
    n_p, n_s = tp // tm, ts // tm
    ple = pp.shape[1]

    def body(h1_ref, y0_ref, y1_ref, y2_ref, y3_ref, gt_ref, pp_ref, ps_ref, g_ref, wg_ref, wp_ref, gf_ref,
             yp_ref, ys_ref):
        def run(p_ref, o_ref):
            gt = gt_ref[...]
            moe = None
            for k, y_ref in enumerate((y0_ref, y1_ref, y2_ref, y3_ref)):
                moe = gt[:, k:k + 1] * y_ref[...] if moe is None else moe + gt[:, k:k + 1] * y_ref[...]
            h2 = h1_ref[...] + moe
            a = (_rms(h2) * g_ref[...]).astype(BF16)
            gate = jax.nn.sigmoid(_dot(a, wg_ref[...]))
            pe = _dot(p_ref[...].astype(BF16), wp_ref[...])
            h3 = h2 + pe * gate
            o_ref[...] = _rms(h3) * gf_ref[...]

        i = pl.program_id(0)

        @pl.when(i < n_p)
        def _():
            run(pp_ref, yp_ref)

        @pl.when(i >= n_p)
        def _():
            run(ps_ref, ys_ref)

    pmap = lambda i: (jnp.minimum(i, n_p - 1), 0)
    smap = lambda i: (jnp.maximum(i - n_p, 0), 0)
    omap = lambda i: (i, 0)
    return pl.pallas_call(
        body,
        out_shape=[jax.ShapeDtypeStruct((tp, D_MODEL), F32), jax.ShapeDtypeStruct((ts, D_MODEL), F32)],
        grid=(n_p + n_s,),
        in_specs=[pl.BlockSpec((tm, D_MODEL), omap)]
                 + [pl.BlockSpec((tm, D_MODEL), functools.partial(lambda k, i: (k * (n_p + n_s) + i, 0), k))
                    for k in range(TOP_K)]
                 + [pl.BlockSpec((tm, LANES), omap), pl.BlockSpec((tm, ple), pmap), pl.BlockSpec((tm, ple), smap),
                  _const_spec((1, D_MODEL)), _const_spec((D_MODEL, D_MODEL)), _const_spec((ple, D_MODEL)),
                  _const_spec((1, D_MODEL))],
        out_specs=[pl.BlockSpec((tm, D_MODEL), pmap), pl.BlockSpec((tm, D_MODEL), smap)],
        compiler_params=_cparams(("arbitrary",)),
        name="ple_final",
    )(h1, y4, y4, y4, y4, gates, pp, ps, g_ple, w_gate, w_proj, g_final)


def _row(x, width=None):
    x = x.reshape(1, -1).astype(F32)
    if width is not None and x.shape[1] < width:
        x = jnp.pad(x, ((0, 0), (0, width - x.shape[1])))
    return x


def _mixer_params(conv_w, conv_b, dt_bias, a_log, d_skip, ssd_norm_g, v_norm_g, v_norm_b, w_spatial, b_spatial,
                  mlp_out_g, seq_len):
    n_seq = CHUNK // seq_len
    pos = jnp.arange(CHUNK) % seq_len
    same = (jnp.arange(CHUNK)[:, None] // seq_len) == (jnp.arange(CHUNK)[None, :] // seq_len)
    tril = (same & (jnp.arange(CHUNK)[:, None] >= jnp.arange(CHUNK)[None, :])).astype(BF16)
    rexp = (jnp.arange(LANES)[:, None] == (jnp.arange(SSD_WIDTH)[None, :] // HEAD_DIM)).astype(BF16)
    w_loc = jnp.tril(w_spatial[:, :seq_len, :seq_len])
    eye = jnp.eye(n_seq, dtype=F32)
    w_bd = jnp.einsum("st,hij->hsitj", eye, w_loc).reshape(MLP_HEADS, CHUNK, CHUNK)
    wsp = (w_bd.reshape(MLP_HEADS // 2, 2, CHUNK, CHUNK).transpose(0, 2, 1, 3)
           .reshape(MLP_HEADS // 2, CHUNK, 2 * CHUNK).astype(BF16))
    bsp = jnp.repeat(b_spatial[:, :seq_len].T[pos], MLP_WIDTH // MLP_HEADS, axis=1)
    params = (
        conv_w.astype(F32), _row(conv_b), _row(dt_bias, LANES), _row(a_log, LANES),
        _row(jnp.repeat(a_log, HEAD_DIM)), rexp, tril, _row(jnp.repeat(d_skip, HEAD_DIM)),
        _row(ssd_norm_g), _row(v_norm_g), _row(v_norm_b), wsp, bsp.astype(F32), _row(mlp_out_g),
    )
    return params, same.astype(BF16)


def _tile_rows(n):
    return 512 if n % 512 == 0 else CHUNK


def kernel(x_prompt, x_sample, state_ssm, state_conv, p_prompt, p_sample, norm_mix_g, w_in, conv_w, conv_b, dt_bias, a_log, d_skip, ssd_norm_g, v_norm_g, v_norm_b, w_spatial, b_spatial, mlp_out_g, w_out, norm_moe_g, w_router, b_router, w_up, b_up, w_down, b_down, norm_ple_g, w_ple_gate, w_ple_proj, norm_final_g):
    depth = norm_mix_g.shape[0]
    bp, lp, d = x_prompt.shape
    bs, ls, _ = x_sample.shape
    tp, ts = bp * lp, bs * ls
    assert depth == 1 and d == D_MODEL and lp % CHUNK == 0 and ts % CHUNK == 0 and 8 % ls == 0
    tm = _tile_rows(tp) if ts % _tile_rows(tp) == 0 else CHUNK
    t_all = tp + ts
    tm_moe = 256
    nb_moe = -(-t_all * TOP_K // tm_moe) + N_EXPERTS

    hp = x_prompt.reshape(tp, d)
    hs = x_sample.reshape(ts, d)
    ssm_p, conv_p, ssm_s, conv_s, v_s = [], [], [], [], []
    o1 = SSD_WIDTH
    o2 = o1 + CONV_DIM
    o3 = o2 + SSD_HEADS
    o4 = o3 + MLP_WIDTH
    c = jnp.arange(256)
    src = jnp.where(c < LANES, 2 * c, 2 * (c - LANES) + 1)
    perm = (jnp.arange(256)[:, None] == src[None, :]).astype(BF16)

    for i in range(depth):
        wi = w_in[i]
        w_cat = jnp.concatenate(
            [wi[:, :o2], wi[:, o3:], jnp.pad(wi[:, o2:o3], ((0, 0), (0, DT_PAD - SSD_HEADS)))], axis=1).astype(BF16)
        z, xbc, u, v, dtr = _inproj_call(hp, hs, _row(norm_mix_g[i]), w_cat, tm)

        mix_args = (conv_w[i], conv_b[i], dt_bias[i], a_log[i], d_skip[i], ssd_norm_g[i], v_norm_g[i], v_norm_b[i],
                    w_spatial[i], b_spatial[i], mlp_out_g[i])
        prm_p, _ = _mixer_params(*mix_args, seq_len=CHUNK)
        cat_p, s_p = _prompt_mixer_call(z, xbc, u, v, dtr, prm_p, bp, lp // CHUNK)
        ssm_p.append(s_p.reshape(bp, SSD_HEADS, HEAD_DIM, D_STATE).astype(state_ssm.dtype))
        conv_p.append(jnp.stack([xbc[(b + 1) * lp - (CONV_W - 1):(b + 1) * lp] for b in range(bp)]))

        prm_s, seg_ones = _mixer_params(*mix_args, seq_len=ls)
        xbc_s = xbc[tp:].reshape(bs, ls, CONV_DIM)
        xpad = jnp.concatenate([state_conv[i].astype(F32), xbc_s], axis=1)
        x_shift = [xpad[:, CONV_W - 1 - k:CONV_W - 1 - k + ls].reshape(ts, CONV_DIM) for k in range(CONV_W)]
        h0 = state_ssm[i].astype(F32).reshape(bs, SSD_WIDTH, D_STATE)
        cat_s, v_rows, s_s = _sample_mixer_call(z, x_shift, u, v, dtr, h0, prm_s, seg_ones, tp // CHUNK, ls)
        ssm_s.append(s_s.reshape(bs, SSD_HEADS, HEAD_DIM, D_STATE).astype(state_ssm.dtype))
        conv_s.append(xpad[:, ls:])
        v_s.append(v_rows.reshape(bs, ls, MLP_WIDTH))

        wr = jnp.pad(w_router[i].astype(F32), ((0, 0), (0, LANES - N_EXPERTS)))
        wr_hi = wr.astype(BF16)
        wr_lo = (wr - wr_hi.astype(F32)).astype(BF16)
        b_r = jnp.concatenate([b_router[i].astype(F32), jnp.full((LANES - N_EXPERTS,), -1e30, F32)]).reshape(1, LANES)
        h1, m, eid, gates = _out_router_call(cat_p, cat_s, hp, hs, w_out[i].astype(BF16), _row(norm_moe_g[i]),
                                             wr_hi, wr_lo, b_r, tm)

        be, nval, off, tok, dst = _route(eid[:, :TOP_K], tm_moe, nb_moe)
        b_up_g = (b_up[i].astype(F32).reshape(N_EXPERTS, 2 * D_FF // 256, LANES, 2).transpose(0, 1, 3, 2)
                  .reshape(N_EXPERTS, 1, 2 * D_FF))
        y4 = _moe_call(m, be, nval, off, tok, dst, w_up[i], b_up_g, w_down[i],
                       b_down[i].reshape(N_EXPERTS, 1, D_MODEL), perm, tm_moe, nb_moe)

        hp, hs = _ple_call(h1, y4, gates,
                           p_prompt[i].reshape(tp, -1), p_sample[i].reshape(ts, -1), _row(norm_ple_g[i]),
                           w_ple_gate[i].astype(BF16), w_ple_proj[i].astype(BF16), _row(norm_final_g), tm)

    y_prompt = hp.reshape(bp, lp, d)
    y_sample = hs.reshape(bs, ls, d)
    return (y_prompt, y_sample, jnp.stack(ssm_p), jnp.stack(conv_p), jnp.stack(ssm_s), jnp.stack(conv_s),
            jnp.stack(v_s))
```

```python
import functools

import jax
import jax.numpy as jnp
from jax import lax
from jax.experimental import pallas as pl
from jax.experimental.pallas import tpu as pltpu

F32 = jnp.float32
BF16 = jnp.bfloat16
I32 = jnp.int32

EPS = 1e-6
D_MODEL = 1024
SSD_WIDTH = 512
SSD_HEADS = 8
HEAD_DIM = 64
SSD_GROUPS = 2
D_STATE = 128
CONV_W = 4
CONV_DIM = SSD_WIDTH + 2 * SSD_GROUPS * D_STATE
MLP_WIDTH = 512
MLP_HEADS = 8
N_EXPERTS = 32
TOP_K = 4
D_FF = 1024
SWIGLU_LIMIT = 7.0
SWIGLU_ALPHA = 1.702
TOPK_SHIFT = 2
assert 1 << TOPK_SHIFT == TOP_K
LANES = 128
CHUNK = 128
DT_PAD = LANES
TOKEN_TILE_ROWS = D_MODEL // LANES
IN_PAD = SSD_WIDTH + CONV_DIM + 2 * MLP_WIDTH + DT_PAD
VMEM_LIMIT = 56 * 1024 * 1024


def _cparams(sem):
    return pltpu.CompilerParams(dimension_semantics=sem, vmem_limit_bytes=VMEM_LIMIT)


def _const_spec(shape):
    return pl.BlockSpec(shape, lambda *_: (0,) * len(shape))


def _rms(x):
    return x * lax.rsqrt(jnp.mean(x * x, axis=-1, keepdims=True) + EPS)


def _dot(a, b):
    return jnp.dot(a, b, preferred_element_type=F32)


def _dot_nt(a, b):
    return lax.dot_general(a, b, (((1,), (1,)), ((), ())), preferred_element_type=F32)


def _split3(x):
    hi = x.astype(BF16)
    r = x - hi.astype(F32)
    mid = r.astype(BF16)
    lo = (r - mid.astype(F32)).astype(BF16)
    return hi, mid, lo


def _sel_right(x, m01):
    hi, mid, lo = _split3(x)
    return _dot(hi, m01) + _dot(mid, m01) + _dot(lo, m01)


def _sel_left(m01, x):
    hi, mid, lo = _split3(x)
    return _dot(m01, hi) + _dot(m01, mid) + _dot(m01, lo)


def _softplus(x):
    return jnp.maximum(x, 0.0) + jnp.log1p(jnp.exp(-jnp.abs(x)))


def _inproj_call(xp, xs, g, w, tm):
    tp, ts = xp.shape[0], xs.shape[0]
    n_p, n_s = tp // tm, ts // tm
    t_all = tp + ts
    segs = ((0, 512), (512, 1536), (1536, 2048), (2048, 2560), (2560, IN_PAD))

    def body(xp_ref, xs_ref, g_ref, w_ref, *outs):
        def run(x_ref):
            xn = (_rms(x_ref[...]) * g_ref[...]).astype(BF16)
            for (a, b), o in zip(segs, outs):
                o[...] = _dot(xn, w_ref[:, a:b])

        i = pl.program_id(0)

        @pl.when(i < n_p)
        def _():
            run(xp_ref)

        @pl.when(i >= n_p)
        def _():
            run(xs_ref)

    widths = [b - a for a, b in segs]
    return pl.pallas_call(
        body,
        out_shape=[jax.ShapeDtypeStruct((t_all, wd), F32) for wd in widths],
        grid=(n_p + n_s,),
        in_specs=[
            pl.BlockSpec((tm, D_MODEL), lambda i: (jnp.minimum(i, n_p - 1), 0)),
            pl.BlockSpec((tm, D_MODEL), lambda i: (jnp.maximum(i - n_p, 0), 0)),
            _const_spec((1, D_MODEL)),
            _const_spec((D_MODEL, IN_PAD)),
        ],
        out_specs=[pl.BlockSpec((tm, wd), lambda i: (i, 0)) for wd in widths],
        compiler_params=_cparams(("arbitrary",)),
        name="in_proj",
    )(xp, xs, g, w)


def _mixer_front(conv, dtr, dtb, alog, alog_x, rexp, tril, seg_ones):
    xact = conv * jax.nn.sigmoid(conv)
    xs = xact[:, :SSD_WIDTH]
    bm = xact[:, SSD_WIDTH:SSD_WIDTH + 256]
    cm = xact[:, SSD_WIDTH + 256:]
    dt = _softplus(dtr + dtb)
    a = dt * (-jnp.exp(alog))
    dt_x = _sel_right(dt, rexp)
    a_x = dt_x * (-jnp.exp(alog_x))
    acum = _sel_left(tril, a)
    acum_x = _sel_left(tril, a_x)
    if seg_ones is None:
        r = acum_x.shape[0]
        tot_x = jnp.broadcast_to(acum_x[r - 1:r, :], acum_x.shape)
    else:
        tot_x = _sel_left(seg_ones, a_x)
    return xs, bm, cm, dt_x, acum, acum_x, tot_x


def _ssd_intra(cmb, bmb, acum, xdt, mask):
    r = acum.shape[0]
    acum_t = acum.T
    lane = lax.broadcasted_iota(I32, (r, LANES), 1)
    low = lane < HEAD_DIM
    outs = []
    for g in range(SSD_GROUPS):
        sg = _dot_nt(cmb[:, LANES * g:LANES * (g + 1)], bmb[:, LANES * g:LANES * (g + 1)])
        for k in (2 * g, 2 * g + 1):
            parts = []
            for h in (2 * k, 2 * k + 1):
                seg = acum[:, h:h + 1] - acum_t[h:h + 1, :]
                parts.append((sg * jnp.exp(jnp.where(mask, seg, -jnp.inf))).astype(BF16))
            lhs = jnp.concatenate(parts, axis=1)
            xd = xdt[:, LANES * k:LANES * (k + 1)]
            rhs = jnp.concatenate([jnp.where(low, xd, 0.0), jnp.where(low, 0.0, xd)], axis=0).astype(BF16)
            outs.append(_dot(lhs, rhs))
    return jnp.concatenate(outs, axis=1)


def _mixer_back(y, z, u, v, sng, vng, vnb, wsp_ref, bsp, mog):
    r = y.shape[0]
    yg = y * (z * jax.nn.sigmoid(z))
    halves = []
    for g in range(SSD_GROUPS):
        t = yg[:, 256 * g:256 * (g + 1)]
        halves.append(_rms(t))
    yn = jnp.concatenate(halves, axis=1) * sng
    ug = jax.nn.gelu(u)
    vg = jax.nn.gelu(v)
    mu = jnp.mean(vg, axis=-1, keepdims=True)
    var = jnp.mean(jnp.square(vg - mu), axis=-1, keepdims=True)
    v_ln = (vg - mu) * lax.rsqrt(var + EPS) * vng + vnb
    lane = lax.broadcasted_iota(I32, (r, LANES), 1)
    low = lane < HEAD_DIM
    outs = []
    for k in range(MLP_HEADS // 2):
        vd = v_ln[:, LANES * k:LANES * (k + 1)]
        rhs = jnp.concatenate([jnp.where(low, vd, 0.0), jnp.where(low, 0.0, vd)], axis=0).astype(BF16)
        outs.append(_dot(wsp_ref[k], rhs))
    s = jnp.concatenate(outs, axis=1) + bsp
    m = _rms(ug * s) * mog
    return jnp.concatenate([yn, m], axis=1).astype(BF16), v_ln


_MIXER_PARAM_SHAPES = (
    (CONV_W, CONV_DIM), (1, CONV_DIM), (1, LANES), (1, LANES), (1, SSD_WIDTH), (LANES, SSD_WIDTH),
    (CHUNK, CHUNK), (1, SSD_WIDTH), (1, SSD_WIDTH), (1, MLP_WIDTH), (1, MLP_WIDTH),
    (MLP_HEADS // 2, CHUNK, 2 * CHUNK), (CHUNK, MLP_WIDTH), (1, MLP_WIDTH),
)


def _prompt_mixer_body(z_ref, xbc_ref, u_ref, v_ref, dt_ref,
                       cw_ref, cb_ref, dtb_ref, alog_ref, alogx_ref, rexp_ref, tril_ref, dskip_ref,
                       sng_ref, vng_ref, vnb_ref, wsp_ref, bsp_ref, mog_ref,
                       cat_ref, ssm_ref, ext_scr, s_scr):
    c = pl.program_id(1)
    r = CHUNK

    @pl.when(c == 0)
    def _():
        ext_scr[0:8, :] = jnp.zeros((8, CONV_DIM), F32)
        s_scr[...] = jnp.zeros_like(s_scr)

    x = xbc_ref[...]
    ext_scr[8:8 + r, :] = x
    cw = cw_ref[...]
    conv = (cb_ref[...] + cw[3:4] * x + cw[2:3] * ext_scr[7:7 + r, :]
            + cw[1:2] * ext_scr[6:6 + r, :] + cw[0:1] * ext_scr[5:5 + r, :])
    ext_scr[0:8, :] = x[r - 8:r, :]

    xs, bm, cm, dt_x, acum, acum_x, tot_x = _mixer_front(
        conv, dt_ref[...], dtb_ref[...], alog_ref[...], alogx_ref[...], rexp_ref[...], tril_ref[...], None)
    bmb, cmb = bm.astype(BF16), cm.astype(BF16)
    xdt = xs * dt_x
    row = lax.broadcasted_iota(I32, (r, r), 0)
    col = lax.broadcasted_iota(I32, (r, r), 1)
    y_diag = _ssd_intra(cmb, bmb, acum, xdt, row >= col)

    s_prev = s_scr[...]
    s_prev_b = s_prev.astype(BF16)
    y_off = jnp.concatenate(
        [_dot_nt(cmb[:, LANES * g:LANES * (g + 1)], s_prev_b[256 * g:256 * (g + 1), :]) for g in range(SSD_GROUPS)],
        axis=1)
    y = y_diag + y_off * jnp.exp(acum_x) + dskip_ref[...] * xs

    w_t = (xdt * jnp.exp(tot_x - acum_x)).T.astype(BF16)
    states = jnp.concatenate(
        [_dot(w_t[256 * g:256 * (g + 1), :], bmb[:, LANES * g:LANES * (g + 1)]) for g in range(SSD_GROUPS)], axis=0)
    s_new = s_prev * jnp.exp(tot_x).T + states
    s_scr[...] = s_new

    cat, _ = _mixer_back(y, z_ref[...], u_ref[...], v_ref[...], sng_ref[...], vng_ref[...], vnb_ref[...],
                         wsp_ref, bsp_ref[...], mog_ref[...])
    cat_ref[...] = cat

    @pl.when(c == pl.num_programs(1) - 1)
    def _():
        ssm_ref[0] = s_new


def _prompt_mixer_call(z, xbc, u, v, dtr, params, nb, nc):
    row = lambda b, c: (b * nc + c, 0)
    in_specs = [
        pl.BlockSpec((CHUNK, SSD_WIDTH), row), pl.BlockSpec((CHUNK, CONV_DIM), row),
        pl.BlockSpec((CHUNK, MLP_WIDTH), row), pl.BlockSpec((CHUNK, MLP_WIDTH), row),
        pl.BlockSpec((CHUNK, DT_PAD), row),
    ] + [_const_spec(s) for s in _MIXER_PARAM_SHAPES]
    return pl.pallas_call(
        _prompt_mixer_body,
        out_shape=[jax.ShapeDtypeStruct((nb * nc * CHUNK, D_MODEL), BF16),
                   jax.ShapeDtypeStruct((nb, SSD_WIDTH, D_STATE), F32)],
        grid=(nb, nc),
        in_specs=in_specs,
        out_specs=[pl.BlockSpec((CHUNK, D_MODEL), row),
                   pl.BlockSpec((1, SSD_WIDTH, D_STATE), lambda b, c: (b, 0, 0))],
        scratch_shapes=[pltpu.VMEM((CHUNK + 8, CONV_DIM), F32), pltpu.VMEM((SSD_WIDTH, D_STATE), F32)],
        compiler_params=_cparams(("arbitrary", "arbitrary")),
        name="prompt_mixer",
    )(z, xbc, u, v, dtr, *params)


def _sample_mixer_body(seq_len, z_ref, x0_ref, x1_ref, x2_ref, x3_ref, u_ref, v_ref, dt_ref, h_ref,
                       cw_ref, cb_ref, dtb_ref, alog_ref, alogx_ref, rexp_ref, tril_ref, dskip_ref,
                       sng_ref, vng_ref, vnb_ref, wsp_ref, bsp_ref, mog_ref, segones_ref,
                       cat_ref, vout_ref, hout_ref, cm_scr, bm_scr, wt_scr, dtt_scr, yoff_scr):
    r = CHUNK
    shift = seq_len.bit_length() - 1
    cw = cw_ref[...]
    conv = (cb_ref[...] + cw[3:4] * x0_ref[...] + cw[2:3] * x1_ref[...]
            + cw[1:2] * x2_ref[...] + cw[0:1] * x3_ref[...])
    xs, bm, cm, dt_x, acum, acum_x, tot_x = _mixer_front(
        conv, dt_ref[...], dtb_ref[...], alog_ref[...], alogx_ref[...], rexp_ref[...], tril_ref[...],
        segones_ref[...])
    bmb, cmb = bm.astype(BF16), cm.astype(BF16)
    xdt = xs * dt_x
    row = lax.broadcasted_iota(I32, (r, r), 0)
    col = lax.broadcasted_iota(I32, (r, r), 1)
    same = lax.shift_right_logical(row, shift) == lax.shift_right_logical(col, shift)
    y_diag = _ssd_intra(cmb, bmb, acum, xdt, same & (row >= col))

    cm_scr[...] = cm
    bm_scr[...] = bmb
    wt_scr[...] = (xdt * jnp.exp(tot_x - acum_x)).T
    dtt_scr[...] = jnp.exp(tot_x).T
    ones_b = jnp.ones((LANES, LANES), BF16)
    seqs_per_slab = 8 // seq_len

    def slab(j, carry):
        rows = pl.ds(pl.multiple_of(8 * j, 8), 8)
        cms = cm_scr[rows, :].astype(BF16)
        sub = lax.broadcasted_iota(I32, (8, 256), 0)
        lane = lax.broadcasted_iota(I32, (256, LANES), 1)
        for g in range(SSD_GROUPS):
            q_rows = slice(256 * g, 256 * (g + 1))
            acc = jnp.zeros((8, 256), F32)
            for q in range(seqs_per_slab):
                s = seqs_per_slab * j + q
                y_s = _dot_nt(cms[:, LANES * g:LANES * (g + 1)], h_ref[s, q_rows, :].astype(BF16))
                acc = jnp.where(lax.shift_right_logical(sub, shift) == q, y_s, acc)
            yoff_scr[rows, 256 * g:256 * (g + 1)] = acc
            for q in range(seqs_per_slab):
                s = seqs_per_slab * j + q
                w_sel = jnp.where(lax.shift_right_logical(lane, shift) == s, wt_scr[q_rows, :], 0.0).astype(BF16)
                st = _dot(w_sel, bm_scr[:, LANES * g:LANES * (g + 1)])
                d_sel = jnp.where(lane == s * seq_len, dtt_scr[q_rows, :], 0.0)
                hout_ref[s, q_rows, :] = h_ref[s, q_rows, :] * _sel_right(d_sel, ones_b) + st
        return carry

    lax.fori_loop(0, r // 8, slab, 0)

    y = y_diag + yoff_scr[...] * jnp.exp(acum_x) + dskip_ref[...] * xs
    cat, v_ln = _mixer_back(y, z_ref[...], u_ref[...], v_ref[...], sng_ref[...], vng_ref[...], vnb_ref[...],
                            wsp_ref, bsp_ref[...], mog_ref[...])
    cat_ref[...] = cat
    vout_ref[...] = v_ln


def _sample_mixer_call(z, x_shift, u, v, dtr, h0, params, seg_ones, row0, seq_len):
    ts = x_shift[0].shape[0]
    n = ts // CHUNK
    spt = CHUNK // seq_len
    off = lambda i: (row0 + i, 0)
    loc = lambda i: (i, 0)
    st3 = lambda i: (i, 0, 0)
    in_specs = (
        [pl.BlockSpec((CHUNK, SSD_WIDTH), off)]
        + [pl.BlockSpec((CHUNK, CONV_DIM), loc)] * 4
        + [pl.BlockSpec((CHUNK, MLP_WIDTH), off), pl.BlockSpec((CHUNK, MLP_WIDTH), off),
           pl.BlockSpec((CHUNK, DT_PAD), off), pl.BlockSpec((spt, SSD_WIDTH, D_STATE), st3)]
        + [_const_spec(s) for s in _MIXER_PARAM_SHAPES] + [_const_spec((CHUNK, CHUNK))])
    return pl.pallas_call(
        functools.partial(_sample_mixer_body, seq_len),
        out_shape=[jax.ShapeDtypeStruct((ts, D_MODEL), BF16), jax.ShapeDtypeStruct((ts, MLP_WIDTH), F32),
                   jax.ShapeDtypeStruct(h0.shape, F32)],
        grid=(n,),
        in_specs=in_specs,
        out_specs=[pl.BlockSpec((CHUNK, D_MODEL), loc), pl.BlockSpec((CHUNK, MLP_WIDTH), loc),
                   pl.BlockSpec((spt, SSD_WIDTH, D_STATE), st3)],
        scratch_shapes=[pltpu.VMEM((CHUNK, 256), F32), pltpu.VMEM((CHUNK, 256), BF16),
                        pltpu.VMEM((SSD_WIDTH, CHUNK), F32), pltpu.VMEM((SSD_WIDTH, CHUNK), F32),
                        pltpu.VMEM((CHUNK, SSD_WIDTH), F32)],
        compiler_params=_cparams(("arbitrary",)),
        name="sample_mixer",
    )(z, *x_shift, u, v, dtr, h0, *params, seg_ones)


def _out_router_call(cat_p, cat_s, xp, xs, w_out, g_moe, wr_hi, wr_lo, b_r, tm):
    tp, ts = xp.shape[0], xs.shape[0]
    n_p, n_s = tp // tm, ts // tm
    t_all = tp + ts

    def body(cp_ref, cs_ref, xp_ref, xs_ref, wo_ref, g_ref, wh_ref, wl_ref, br_ref,
             h1_ref, m_ref, eid_ref, gate_ref):
        def run(c_ref, x_ref):
            h1 = x_ref[...] + _dot(c_ref[...], wo_ref[...])
            h1_ref[...] = h1
            m = _rms(h1) * g_ref[...]
            for j in range(TOKEN_TILE_ROWS):
                m_ref[pl.ds(j, tm, stride=TOKEN_TILE_ROWS), :] = m[:, LANES * j:LANES * (j + 1)]
            m_hi = m.astype(BF16)
            m_lo = (m - m_hi.astype(F32)).astype(BF16)
            logits = _dot(m_hi, wh_ref[...]) + _dot(m_lo, wh_ref[...]) + _dot(m_hi, wl_ref[...]) + br_ref[...]
            lane = lax.broadcasted_iota(I32, logits.shape, 1).astype(F32)
            work = logits
            vals, ids = [], []
            for _ in range(TOP_K):
                mx = jnp.max(work, axis=-1, keepdims=True)
                idx = jnp.min(jnp.where(work == mx, lane, float(LANES)), axis=-1, keepdims=True)
                vals.append(mx)
                ids.append(idx)
                work = jnp.where(lane == idx, -jnp.inf, work)
            ex = [jnp.exp(vv - vals[0]) for vv in vals]
            den = ex[0] + ex[1] + ex[2] + ex[3]
            eid = jnp.zeros(logits.shape, I32)
            gate = jnp.zeros(logits.shape, F32)
            for k in range(TOP_K):
                eid = jnp.where(lane == k, ids[k].astype(I32), eid)
                gate = jnp.where(lane == k, ex[k] / den, gate)
            eid_ref[...] = eid
            gate_ref[...] = gate

        i = pl.program_id(0)

        @pl.when(i < n_p)
        def _():
            run(cp_ref, xp_ref)

        @pl.when(i >= n_p)
        def _():
            run(cs_ref, xs_ref)

    pmap = lambda i: (jnp.minimum(i, n_p - 1), 0)
    smap = lambda i: (jnp.maximum(i - n_p, 0), 0)
    omap = lambda i: (i, 0)
    return pl.pallas_call(
        body,
        out_shape=[jax.ShapeDtypeStruct((t_all, D_MODEL), F32),
                   jax.ShapeDtypeStruct((t_all * TOKEN_TILE_ROWS, LANES), F32),
                   jax.ShapeDtypeStruct((t_all, LANES), I32), jax.ShapeDtypeStruct((t_all, LANES), F32)],
        grid=(n_p + n_s,),
        in_specs=[pl.BlockSpec((tm, D_MODEL), pmap), pl.BlockSpec((tm, D_MODEL), smap),
                  pl.BlockSpec((tm, D_MODEL), pmap), pl.BlockSpec((tm, D_MODEL), smap),
                  _const_spec((D_MODEL, D_MODEL)), _const_spec((1, D_MODEL)),
                  _const_spec((D_MODEL, LANES)), _const_spec((D_MODEL, LANES)), _const_spec((1, LANES))],
        out_specs=[pl.BlockSpec((tm, D_MODEL), omap), pl.BlockSpec((tm * TOKEN_TILE_ROWS, LANES), omap),
                   pl.BlockSpec((tm, LANES), omap), pl.BlockSpec((tm, LANES), omap)],
        compiler_params=_cparams(("arbitrary",)),
        name="out_router",
    )(cat_p, cat_s, xp, xs, w_out, g_moe, wr_hi, wr_lo, b_r)


def _route(eid, tm, nb):
    t = eid.shape[0]
    tk = t * TOP_K
    flat = eid.reshape(tk)
    _, order = lax.sort((flat, jnp.arange(tk, dtype=I32)), num_keys=1, is_stable=True)
    counts = jnp.sum((flat[:, None] == jnp.arange(N_EXPERTS, dtype=I32)[None, :]).astype(I32), axis=0)
    nblk = (counts + tm - 1) // tm
    bend = jnp.cumsum(nblk)
    bstart = bend - nblk
    start = jnp.cumsum(counts) - counts
    nused = bend[-1]
    blk = jnp.arange(nb, dtype=I32)
    used = blk < nused
    be = jnp.minimum(jnp.sum((jnp.minimum(blk, nused - 1)[:, None] >= bend[None, :]).astype(I32), axis=1),
                     N_EXPERTS - 1)
    sel = (be[:, None] == jnp.arange(N_EXPERTS, dtype=I32)[None, :]).astype(I32)
    pick = lambda v: jnp.sum(sel * v[None, :], axis=1)
    done = (blk - pick(bstart)) * tm
    nval = jnp.where(used, jnp.clip(pick(counts) - done, 0, tm), 0).astype(I32)
    off = jnp.where(used, pick(start) + done, 0).astype(I32)
    pad = (-(-(tk + tm) // LANES) + _id_rows(tm)) * LANES - tk
    tok = jnp.pad(lax.shift_right_logical(order, TOPK_SHIFT) * TOKEN_TILE_ROWS, (0, pad))
    dst = jnp.pad((order & (TOP_K - 1)) * t + lax.shift_right_logical(order, TOPK_SHIFT), (0, pad))
    return be, nval, off, tok, dst


def _id_rows(tm):
    return tm // LANES + 1


def _moe_body(tm, t_all, nb, be_ref, nval_ref, off_ref, tok_hbm, dst_hbm, m_hbm, wup_ref, bup_ref, wdn_ref, bdn_ref,
              perm_ref, y_hbm, gids, sids, xbuf, ybuf, wup_b, wdn_b, isem, gsem, ssem):
    i = pl.program_id(0)
    nv = nval_ref[i]
    slot = i & 1
    prv = jnp.maximum(i - 1, 0)
    nxt = jnp.minimum(i + 1, nb - 1)
    nx2 = jnp.minimum(i + 2, nb - 1)
    has_next = (i + 1 < nb) & (nval_ref[nxt] > 0)
    has_next2 = (i + 2 < nb) & (nval_ref[nx2] > 0)
    n_prev = jnp.where(i > 0, nval_ref[prv], 0)
    win = _id_rows(tm) * LANES
    spare = TOP_K * t_all

    def ids_copies(b):
        start = pl.multiple_of(lax.shift_right_logical(off_ref[b], 7) * LANES, LANES)
        ring = pl.ds(pl.multiple_of((b & 3) * win, LANES), win)
        return (pltpu.make_async_copy(tok_hbm.at[pl.ds(start, win)], gids.at[ring], isem.at[b & 3, 0]),
                pltpu.make_async_copy(dst_hbm.at[pl.ds(start, win)], sids.at[ring], isem.at[b & 3, 1]))

    def id_base(b):
        return (b & 3) * win + (off_ref[b] & (LANES - 1))

    def gather_row(base, s, r):
        src = pl.ds(pl.multiple_of(gids[base + r], TOKEN_TILE_ROWS), TOKEN_TILE_ROWS)
        return pltpu.make_async_copy(m_hbm.at[src], xbuf.at[s, pl.ds(TOKEN_TILE_ROWS * r, TOKEN_TILE_ROWS)],
                                     gsem.at[s])

    def scatter_row(base, s, r, n):
        dest = jnp.where(r < n, sids[base + r], spare + r)
        return pltpu.make_async_copy(ybuf.at[s, pl.ds(r, 1)], y_hbm.at[pl.ds(dest, 1)], ssem.at[s])

    def wait_gathers(s):
        pltpu.make_async_copy(m_hbm.at[pl.ds(0, tm * TOKEN_TILE_ROWS)], xbuf.at[s], gsem.at[s]).wait()

    def wait_scatters(s):
        pltpu.make_async_copy(ybuf.at[s], y_hbm.at[pl.ds(0, tm)], ssem.at[s]).wait()

    def for_rows(fn):
        def one(r, c):
            fn(r)
            return c

        lax.fori_loop(0, tm, one, 0)

    @pl.when(nv > 0)
    def _():
        @pl.when(i == 0)
        def _():
            ybuf[...] = jnp.zeros_like(ybuf)
            fill = pltpu.make_async_copy(ybuf.at[0], y_hbm.at[pl.ds(spare, tm)], ssem.at[0])
            fill.start()
            fill.wait()
            for b in range(4):
                for cp in ids_copies(b):
                    cp.start()
                    cp.wait()
            base0 = id_base(0)
            for_rows(lambda r: gather_row(base0, 0, r).start())

        @pl.when(has_next & (i >= 3))
        def _():
            for cp in ids_copies(nxt):
                cp.wait()

        @pl.when(has_next2 & (i >= 2))
        def _():
            for cp in ids_copies(nx2):
                cp.start()

        @pl.when((i == 0) | (be_ref[i] != be_ref[prv]))
        def _():
            for jb in range(2 * D_FF // 256):
                cols = slice(256 * jb, 256 * (jb + 1))
                wup_b[:, cols] = _dot(wup_ref[0, :, cols].astype(BF16), perm_ref[...]).astype(BF16)
            wdn_b[...] = wdn_ref[0].astype(BF16)

        def ffn_step(cur, oth):
            wait_gathers(cur)

            @pl.when(i >= 1)
            def _():
                wait_scatters(cur)

            g_base = id_base(nxt)
            s_base = id_base(prv)
            for r in range(tm):
                gather_row(g_base, oth, r).start()
                scatter_row(s_base, oth, r, n_prev).start()

            x = jnp.concatenate([xbuf[slot, pl.ds(j, tm, stride=TOKEN_TILE_ROWS), :]
                                 for j in range(TOKEN_TILE_ROWS)], axis=1).astype(BF16)
            acts = []
            for jb in range(D_FF // LANES):
                h = _dot(x, wup_b[:, 256 * jb:256 * (jb + 1)]) + bup_ref[0, :, 256 * jb:256 * (jb + 1)]
                gate = jnp.minimum(h[:, :LANES], SWIGLU_LIMIT)
                lin = jnp.clip(h[:, LANES:], -SWIGLU_LIMIT, SWIGLU_LIMIT)
                acts.append((gate * jax.nn.sigmoid(SWIGLU_ALPHA * gate) * (lin + 1.0)).astype(BF16))
            act = jnp.concatenate(acts, axis=1)
            for c in range(D_MODEL // 256):
                ybuf[slot, :, 256 * c:256 * (c + 1)] = (
                    _dot(act, wdn_b[:, 256 * c:256 * (c + 1)]) + bdn_ref[0, :, 256 * c:256 * (c + 1)])

            @pl.when(jnp.logical_not(has_next))
            def _():
                wait_gathers(oth)
                wait_scatters(oth)
                last_base = id_base(i)
                for_rows(lambda r: scatter_row(last_base, cur, r, nv).start())
                wait_scatters(cur)

        for parity in range(2):
            pl.when(slot == parity)(functools.partial(ffn_step, parity, 1 - parity))


def _moe_call(m, be, nval, off, tok, dst, w_up, b_up_g, w_down, b_down, perm, tm, nb):
    t = m.shape[0] // TOKEN_TILE_ROWS
    assert nb >= 4
    by_expert = lambda i, be, nv, off: (be[i], 0, 0)
    grid_spec = pltpu.PrefetchScalarGridSpec(
        num_scalar_prefetch=3,
        grid=(nb,),
        in_specs=[
            pl.BlockSpec(memory_space=pl.ANY),
            pl.BlockSpec(memory_space=pl.ANY),
            pl.BlockSpec(memory_space=pl.ANY),
            pl.BlockSpec((1, D_MODEL, 2 * D_FF), by_expert),
            pl.BlockSpec((1, 1, 2 * D_FF), by_expert),
            pl.BlockSpec((1, D_FF, D_MODEL), by_expert),
            pl.BlockSpec((1, 1, D_MODEL), by_expert),
            pl.BlockSpec((256, 256), lambda i, be, nv, off: (0, 0)),
        ],
        out_specs=pl.BlockSpec(memory_space=pl.ANY),
        scratch_shapes=[pltpu.SMEM((4 * _id_rows(tm) * LANES,), I32), pltpu.SMEM((4 * _id_rows(tm) * LANES,), I32),
                        pltpu.VMEM((2, tm * TOKEN_TILE_ROWS, LANES), F32), pltpu.VMEM((2, tm, D_MODEL), F32),
                        pltpu.VMEM((D_MODEL, 2 * D_FF), BF16), pltpu.VMEM((D_FF, D_MODEL), BF16),
                        pltpu.SemaphoreType.DMA((4, 2)), pltpu.SemaphoreType.DMA((2,)),
                        pltpu.SemaphoreType.DMA((2,))],
    )
    return pl.pallas_call(
        functools.partial(_moe_body, tm, t, nb),
        out_shape=jax.ShapeDtypeStruct((TOP_K * t + tm, D_MODEL), F32),
        grid_spec=grid_spec,
        compiler_params=_cparams(("arbitrary",)),
        name="moe_experts",
    )(be, nval, off, tok, dst, m, w_up, b_up_g, w_down, b_down, perm)


def _ple_call(h1, y4, gates, pp, ps, g_ple, w_gate, w_proj, g_final, tm):
    tp, ts = pp.shape[0], ps.shape[0]
    n_p, n_s = tp // tm, ts // tm
    ple = pp.shape[1]

    def body(h1_ref, y0_ref, y1_ref, y2_ref, y3_ref, gt_ref, pp_ref, ps_ref, g_ref, wg_ref, wp_ref, gf_ref,
             yp_ref, ys_ref):
        def run(p_ref, o_ref):
            gt = gt_ref[...]
            moe = None
            for k, y_ref in enumerate((y0_ref, y1_ref, y2_ref, y3_ref)):
                moe = gt[:, k:k + 1] * y_ref[...] if moe is None else moe + gt[:, k:k + 1] * y_ref[...]
            h2 = h1_ref[...] + moe
            a = (_rms(h2) * g_ref[...]).astype(BF16)
            gate = jax.nn.sigmoid(_dot(a, wg_ref[...]))
            pe = _dot(p_ref[...].astype(BF16), wp_ref[...])
            h3 = h2 + pe * gate
            o_ref[...] = _rms(h3) * gf_ref[...]

        i = pl.program_id(0)

        @pl.when(i < n_p)
        def _():
            run(pp_ref, yp_ref)

        @pl.when(i >= n_p)
        def _():
            run(ps_ref, ys_ref)

    pmap = lambda i: (jnp.minimum(i, n_p - 1), 0)
    smap = lambda i: (jnp.maximum(i - n_p, 0), 0)
    omap = lambda i: (i, 0)
    return pl.pallas_call(
        body,
        out_shape=[jax.ShapeDtypeStruct((tp, D_MODEL), F32), jax.ShapeDtypeStruct((ts, D_MODEL), F32)],
        grid=(n_p + n_s,),
        in_specs=[pl.BlockSpec((tm, D_MODEL), omap)]
                 + [pl.BlockSpec((tm, D_MODEL), functools.partial(lambda k, i: (k * (n_p + n_s) + i, 0), k))
                    for k in range(TOP_K)]
                 + [pl.BlockSpec((tm, LANES), omap), pl.BlockSpec((tm, ple), pmap), pl.BlockSpec((tm, ple), smap),
                  _const_spec((1, D_MODEL)), _const_spec((D_MODEL, D_MODEL)), _const_spec((ple, D_MODEL)),
                  _const_spec((1, D_MODEL))],
        out_specs=[pl.BlockSpec((tm, D_MODEL), pmap), pl.BlockSpec((tm, D_MODEL), smap)],
        compiler_params=_cparams(("arbitrary",)),
        name="ple_final",
    )(h1, y4, y4, y4, y4, gates, pp, ps, g_ple, w_gate, w_proj, g_final)


def _row(x, width=None):
    x = x.reshape(1, -1).astype(F32)
    if width is not None and x.shape[1] < width:
        x = jnp.pad(x, ((0, 0), (0, width - x.shape[1])))
    return x


def _mixer_params(conv_w, conv_b, dt_bias, a_log, d_skip, ssd_norm_g, v_norm_g, v_norm_b, w_spatial, b_spatial,
                  mlp_out_g, seq_len):
    n_seq = CHUNK // seq_len
    pos = jnp.arange(CHUNK) % seq_len
    same = (jnp.arange(CHUNK)[:, None] // seq_len) == (jnp.arange(CHUNK)[None, :] // seq_len)
    tril = (same & (jnp.arange(CHUNK)[:, None] >= jnp.arange(CHUNK)[None, :])).astype(BF16)
    rexp = (jnp.arange(LANES)[:, None] == (jnp.arange(SSD_WIDTH)[None, :] // HEAD_DIM)).astype(BF16)
    w_loc = jnp.tril(w_spatial[:, :seq_len, :seq_len])
    eye = jnp.eye(n_seq, dtype=F32)
    w_bd = jnp.einsum("st,hij->hsitj", eye, w_loc).reshape(MLP_HEADS, CHUNK, CHUNK)
    wsp = (w_bd.reshape(MLP_HEADS // 2, 2, CHUNK, CHUNK).transpose(0, 2, 1, 3)
           .reshape(MLP_HEADS // 2, CHUNK, 2 * CHUNK).astype(BF16))
    bsp = jnp.repeat(b_spatial[:, :seq_len].T[pos], MLP_WIDTH // MLP_HEADS, axis=1)
    params = (
        conv_w.astype(F32), _row(conv_b), _row(dt_bias, LANES), _row(a_log, LANES),
        _row(jnp.repeat(a_log, HEAD_DIM)), rexp, tril, _row(jnp.repeat(d_skip, HEAD_DIM)),
        _row(ssd_norm_g), _row(v_norm_g), _row(v_norm_b), wsp, bsp.astype(F32), _row(mlp_out_g),
    )
    return params, same.astype(BF16)


def _tile_rows(n):
    return 512 if n % 512 == 0 else CHUNK


def kernel(x_prompt, x_sample, state_ssm, state_conv, p_prompt, p_sample, norm_mix_g, w_in, conv_w, conv_b, dt_bias, a_log, d_skip, ssd_norm_g, v_norm_g, v_norm_b, w_spatial, b_spatial, mlp_out_g, w_out, norm_moe_g, w_router, b_router, w_up, b_up, w_down, b_down, norm_ple_g, w_ple_gate, w_ple_proj, norm_final_g):
    depth = norm_mix_g.shape[0]
    bp, lp, d = x_prompt.shape
    bs, ls, _ = x_sample.shape
    tp, ts = bp * lp, bs * ls
    assert depth == 1 and d == D_MODEL and lp % CHUNK == 0 and ts % CHUNK == 0 and 8 % ls == 0
    tm = _tile_rows(tp) if ts % _tile_rows(tp) == 0 else CHUNK
    t_all = tp + ts
    tm_moe = 256
    nb_moe = -(-t_all * TOP_K // tm_moe) + N_EXPERTS

    hp = x_prompt.reshape(tp, d)
    hs = x_sample.reshape(ts, d)
    ssm_p, conv_p, ssm_s, conv_s, v_s = [], [], [], [], []
    o1 = SSD_WIDTH
    o2 = o1 + CONV_DIM
    o3 = o2 + SSD_HEADS
    o4 = o3 + MLP_WIDTH
    c = jnp.arange(256)
    src = jnp.where(c < LANES, 2 * c, 2 * (c - LANES) + 1)
    perm = (jnp.arange(256)[:, None] == src[None, :]).astype(BF16)

    for i in range(depth):
        wi = w_in[i]
        w_cat = jnp.concatenate(
            [wi[:, :o2], wi[:, o3:], jnp.pad(wi[:, o2:o3], ((0, 0), (0, DT_PAD - SSD_HEADS)))], axis=1).astype(BF16)
        z, xbc, u, v, dtr = _inproj_call(hp, hs, _row(norm_mix_g[i]), w_cat, tm)

        mix_args = (conv_w[i], conv_b[i], dt_bias[i], a_log[i], d_skip[i], ssd_norm_g[i], v_norm_g[i], v_norm_b[i],
                    w_spatial[i], b_spatial[i], mlp_out_g[i])
        prm_p, _ = _mixer_params(*mix_args, seq_len=CHUNK)
        cat_p, s_p = _prompt_mixer_call(z, xbc, u, v, dtr, prm_p, bp, lp // CHUNK)
        ssm_p.append(s_p.reshape(bp, SSD_HEADS, HEAD_DIM, D_STATE).astype(state_ssm.dtype))
        conv_p.append(jnp.stack([xbc[(b + 1) * lp - (CONV_W - 1):(b + 1) * lp] for b in range(bp)]))

        prm_s, seg_ones = _mixer_params(*mix_args, seq_len=ls)
        xbc_s = xbc[tp:].reshape(bs, ls, CONV_DIM)
        xpad = jnp.concatenate([state_conv[i].astype(F32), xbc_s], axis=1)
        x_shift = [xpad[:, CONV_W - 1 - k:CONV_W - 1 - k + ls].reshape(ts, CONV_DIM) for k in range(CONV_W)]
        h0 = state_ssm[i].astype(F32).reshape(bs, SSD_WIDTH, D_STATE)
        cat_s, v_rows, s_s = _sample_mixer_call(z, x_shift, u, v, dtr, h0, prm_s, seg_ones, tp // CHUNK, ls)
        ssm_s.append(s_s.reshape(bs, SSD_HEADS, HEAD_DIM, D_STATE).astype(state_ssm.dtype))
        conv_s.append(xpad[:, ls:])
        v_s.append(v_rows.reshape(bs, ls, MLP_WIDTH))

        wr = jnp.pad(w_router[i].astype(F32), ((0, 0), (0, LANES - N_EXPERTS)))
        wr_hi = wr.astype(BF16)
        wr_lo = (wr - wr_hi.astype(F32)).astype(BF16)
        b_r = jnp.concatenate([b_router[i].astype(F32), jnp.full((LANES - N_EXPERTS,), -1e30, F32)]).reshape(1, LANES)
        h1, m, eid, gates = _out_router_call(cat_p, cat_s, hp, hs, w_out[i].astype(BF16), _row(norm_moe_g[i]),
                                             wr_hi, wr_lo, b_r, tm)

        be, nval, off, tok, dst = _route(eid[:, :TOP_K], tm_moe, nb_moe)
        b_up_g = (b_up[i].astype(F32).reshape(N_EXPERTS, 2 * D_FF // 256, LANES, 2).transpose(0, 1, 3, 2)
                  .reshape(N_EXPERTS, 1, 2 * D_FF))
        y4 = _moe_call(m, be, nval, off, tok, dst, w_up[i], b_up_g, w_down[i],
                       b_down[i].reshape(N_EXPERTS, 1, D_MODEL), perm, tm_moe, nb_moe)

        hp, hs = _ple_call(h1, y4, gates,
                           p_prompt[i].reshape(tp, -1), p_sample[i].reshape(ts, -1), _row(norm_ple_g[i]),
                           w_ple_gate[i].astype(BF16), w_ple_proj[i].astype(BF16), _row(norm_final_g), tm)

    y_prompt = hp.reshape(bp, lp, d)
    y_sample = hs.reshape(bs, ls, d)
    return (y_prompt, y_sample, jnp.stack(ssm_p), jnp.stack(conv_p), jnp.stack(ssm_s), jnp.stack(conv_s),
            jnp.stack(v_s))
```

```python
import functools

import jax
import jax.numpy as jnp
from jax import lax
from jax.experimental import pallas as pl
from jax.experimental.pallas import tpu as pltpu

F32 = jnp.float32
BF16 = jnp.bfloat16
I32 = jnp.int32

EPS = 1e-6
D_MODEL = 1024
SSD_WIDTH = 512
SSD_HEADS = 8
HEAD_DIM = 64
SSD_GROUPS = 2
D_STATE = 128
CONV_W = 4
CONV_DIM = SSD_WIDTH + 2 * SSD_GROUPS * D_STATE
MLP_WIDTH = 512
MLP_HEADS = 8
N_EXPERTS = 32
TOP_K = 4
D_FF = 1024
SWIGLU_LIMIT = 7.0
SWIGLU_ALPHA = 1.702
TOPK_SHIFT = 2
assert 1 << TOPK_SHIFT == TOP_K
LANES = 128
CHUNK = 128
DT_PAD = LANES
TOKEN_TILE_ROWS = D_MODEL // LANES
IN_PAD = SSD_WIDTH + CONV_DIM + 2 * MLP_WIDTH + DT_PAD
VMEM_LIMIT = 56 * 1024 * 1024


def _cparams(sem):
    return pltpu.CompilerParams(dimension_semantics=sem, vmem_limit_bytes=VMEM_LIMIT)


def _const_spec(shape):
    return pl.BlockSpec(shape, lambda *_: (0,) * len(shape))


def _rms(x):
    return x * lax.rsqrt(jnp.mean(x * x, axis=-1, keepdims=True) + EPS)


def _dot(a, b):
    return jnp.dot(a, b, preferred_element_type=F32)


def _dot_nt(a, b):
    return lax.dot_general(a, b, (((1,), (1,)), ((), ())), preferred_element_type=F32)


def _split3(x):
    hi = x.astype(BF16)
    r = x - hi.astype(F32)
    mid = r.astype(BF16)
    lo = (r - mid.astype(F32)).astype(BF16)
    return hi, mid, lo


def _sel_right(x, m01):
    hi, mid, lo = _split3(x)
    return _dot(hi, m01) + _dot(mid, m01) + _dot(lo, m01)


def _sel_left(m01, x):
    hi, mid, lo = _split3(x)
    return _dot(m01, hi) + _dot(m01, mid) + _dot(m01, lo)


def _softplus(x):
    return jnp.maximum(x, 0.0) + jnp.log1p(jnp.exp(-jnp.abs(x)))


def _inproj_call(xp, xs, g, w, tm):
    tp, ts = xp.shape[0], xs.shape[0]
    n_p, n_s = tp // tm, ts // tm
    t_all = tp + ts
    segs = ((0, 512), (512, 1536), (1536, 2048), (2048, 2560), (2560, IN_PAD))

    def body(xp_ref, xs_ref, g_ref, w_ref, *outs):
        def run(x_ref):
            xn = (_rms(x_ref[...]) * g_ref[...]).astype(BF16)
            for (a, b), o in zip(segs, outs):
                o[...] = _dot(xn, w_ref[:, a:b])

        i = pl.program_id(0)

        @pl.when(i < n_p)
        def _():
            run(xp_ref)

        @pl.when(i >= n_p)
        def _():
            run(xs_ref)

    widths = [b - a for a, b in segs]
    return pl.pallas_call(
        body,
        out_shape=[jax.ShapeDtypeStruct((t_all, wd), F32) for wd in widths],
        grid=(n_p + n_s,),
        in_specs=[
            pl.BlockSpec((tm, D_MODEL), lambda i: (jnp.minimum(i, n_p - 1), 0)),
            pl.BlockSpec((tm, D_MODEL), lambda i: (jnp.maximum(i - n_p, 0), 0)),
            _const_spec((1, D_MODEL)),
            _const_spec((D_MODEL, IN_PAD)),
        ],
        out_specs=[pl.BlockSpec((tm, wd), lambda i: (i, 0)) for wd in widths],
        compiler_params=_cparams(("arbitrary",)),
        name="in_proj",
    )(xp, xs, g, w)


def _mixer_front(conv, dtr, dtb, alog, alog_x, rexp, tril, seg_ones):
    xact = conv * jax.nn.sigmoid(conv)
    xs = xact[:, :SSD_WIDTH]
    bm = xact[:, SSD_WIDTH:SSD_WIDTH + 256]
    cm = xact[:, SSD_WIDTH + 256:]
    dt = _softplus(dtr + dtb)
    a = dt * (-jnp.exp(alog))
    dt_x = _sel_right(dt, rexp)
    a_x = dt_x * (-jnp.exp(alog_x))
    acum = _sel_left(tril, a)
    acum_x = _sel_left(tril, a_x)
    if seg_ones is None:
        r = acum_x.shape[0]
        tot_x = jnp.broadcast_to(acum_x[r - 1:r, :], acum_x.shape)
    else:
        tot_x = _sel_left(seg_ones, a_x)
    return xs, bm, cm, dt_x, acum, acum_x, tot_x


def _ssd_intra(cmb, bmb, acum, xdt, mask):
    r = acum.shape[0]
    acum_t = acum.T
    lane = lax.broadcasted_iota(I32, (r, LANES), 1)
    low = lane < HEAD_DIM
    outs = []
    for g in range(SSD_GROUPS):
        sg = _dot_nt(cmb[:, LANES * g:LANES * (g + 1)], bmb[:, LANES * g:LANES * (g + 1)])
        for k in (2 * g, 2 * g + 1):
            parts = []
            for h in (2 * k, 2 * k + 1):
                seg = acum[:, h:h + 1] - acum_t[h:h + 1, :]
                parts.append((sg * jnp.exp(jnp.where(mask, seg, -jnp.inf))).astype(BF16))
            lhs = jnp.concatenate(parts, axis=1)
            xd = xdt[:, LANES * k:LANES * (k + 1)]
            rhs = jnp.concatenate([jnp.where(low, xd, 0.0), jnp.where(low, 0.0, xd)], axis=0).astype(BF16)
            outs.append(_dot(lhs, rhs))
    return jnp.concatenate(outs, axis=1)


def _mixer_back(y, z, u, v, sng, vng, vnb, wsp_ref, bsp, mog):
    r = y.shape[0]
    yg = y * (z * jax.nn.sigmoid(z))
    halves = []
    for g in range(SSD_GROUPS):
        t = yg[:, 256 * g:256 * (g + 1)]
        halves.append(_rms(t))
    yn = jnp.concatenate(halves, axis=1) * sng
    ug = jax.nn.gelu(u)
    vg = jax.nn.gelu(v)
    mu = jnp.mean(vg, axis=-1, keepdims=True)
    var = jnp.mean(jnp.square(vg - mu), axis=-1, keepdims=True)
    v_ln = (vg - mu) * lax.rsqrt(var + EPS) * vng + vnb
    lane = lax.broadcasted_iota(I32, (r, LANES), 1)
    low = lane < HEAD_DIM
    outs = []
    for k in range(MLP_HEADS // 2):
        vd = v_ln[:, LANES * k:LANES * (k + 1)]
        rhs = jnp.concatenate([jnp.where(low, vd, 0.0), jnp.where(low, 0.0, vd)], axis=0).astype(BF16)
        outs.append(_dot(wsp_ref[k], rhs))
    s = jnp.concatenate(outs, axis=1) + bsp
    m = _rms(ug * s) * mog
    return jnp.concatenate([yn, m], axis=1).astype(BF16), v_ln


_MIXER_PARAM_SHAPES = (
    (CONV_W, CONV_DIM), (1, CONV_DIM), (1, LANES), (1, LANES), (1, SSD_WIDTH), (LANES, SSD_WIDTH),
    (CHUNK, CHUNK), (1, SSD_WIDTH), (1, SSD_WIDTH), (1, MLP_WIDTH), (1, MLP_WIDTH),
    (MLP_HEADS // 2, CHUNK, 2 * CHUNK), (CHUNK, MLP_WIDTH), (1, MLP_WIDTH),
)


def _prompt_mixer_body(z_ref, xbc_ref, u_ref, v_ref, dt_ref,
                       cw_ref, cb_ref, dtb_ref, alog_ref, alogx_ref, rexp_ref, tril_ref, dskip_ref,
                       sng_ref, vng_ref, vnb_ref, wsp_ref, bsp_ref, mog_ref,
                       cat_ref, ssm_ref, ext_scr, s_scr):
    c = pl.program_id(1)
    r = CHUNK

    @pl.when(c == 0)
    def _():
        ext_scr[0:8, :] = jnp.zeros((8, CONV_DIM), F32)
        s_scr[...] = jnp.zeros_like(s_scr)

    x = xbc_ref[...]
    ext_scr[8:8 + r, :] = x
    cw = cw_ref[...]
    conv = (cb_ref[...] + cw[3:4] * x + cw[2:3] * ext_scr[7:7 + r, :]
            + cw[1:2] * ext_scr[6:6 + r, :] + cw[0:1] * ext_scr[5:5 + r, :])
    ext_scr[0:8, :] = x[r - 8:r, :]

    xs, bm, cm, dt_x, acum, acum_x, tot_x = _mixer_front(
        conv, dt_ref[...], dtb_ref[...], alog_ref[...], alogx_ref[...], rexp_ref[...], tril_ref[...], None)
    bmb, cmb = bm.astype(BF16), cm.astype(BF16)
    xdt = xs * dt_x
    row = lax.broadcasted_iota(I32, (r, r), 0)
    col = lax.broadcasted_iota(I32, (r, r), 1)
    y_diag = _ssd_intra(cmb, bmb, acum, xdt, row >= col)

    s_prev = s_scr[...]
    s_prev_b = s_prev.astype(BF16)
    y_off = jnp.concatenate(
        [_dot_nt(cmb[:, LANES * g:LANES * (g + 1)], s_prev_b[256 * g:256 * (g + 1), :]) for g in range(SSD_GROUPS)],
        axis=1)
    y = y_diag + y_off * jnp.exp(acum_x) + dskip_ref[...] * xs

    w_t = (xdt * jnp.exp(tot_x - acum_x)).T.astype(BF16)
    states = jnp.concatenate(
        [_dot(w_t[256 * g:256 * (g + 1), :], bmb[:, LANES * g:LANES * (g + 1)]) for g in range(SSD_GROUPS)], axis=0)
    s_new = s_prev * jnp.exp(tot_x).T + states
    s_scr[...] = s_new

    cat, _ = _mixer_back(y, z_ref[...], u_ref[...], v_ref[...], sng_ref[...], vng_ref[...], vnb_ref[...],
                         wsp_ref, bsp_ref[...], mog_ref[...])
    cat_ref[...] = cat

    @pl.when(c == pl.num_programs(1) - 1)
    def _():
        ssm_ref[0] = s_new


def _prompt_mixer_call(z, xbc, u, v, dtr, params, nb, nc):
    row = lambda b, c: (b * nc + c, 0)
    in_specs = [
        pl.BlockSpec((CHUNK, SSD_WIDTH), row), pl.BlockSpec((CHUNK, CONV_DIM), row),
        pl.BlockSpec((CHUNK, MLP_WIDTH), row), pl.BlockSpec((CHUNK, MLP_WIDTH), row),
        pl.BlockSpec((CHUNK, DT_PAD), row),
    ] + [_const_spec(s) for s in _MIXER_PARAM_SHAPES]
    return pl.pallas_call(
        _prompt_mixer_body,
        out_shape=[jax.ShapeDtypeStruct((nb * nc * CHUNK, D_MODEL), BF16),
                   jax.ShapeDtypeStruct((nb, SSD_WIDTH, D_STATE), F32)],
        grid=(nb, nc),
        in_specs=in_specs,
        out_specs=[pl.BlockSpec((CHUNK, D_MODEL), row),
                   pl.BlockSpec((1, SSD_WIDTH, D_STATE), lambda b, c: (b, 0, 0))],
        scratch_shapes=[pltpu.VMEM((CHUNK + 8, CONV_DIM), F32), pltpu.VMEM((SSD_WIDTH, D_STATE), F32)],
        compiler_params=_cparams(("arbitrary", "arbitrary")),
        name="prompt_mixer",
    )(z, xbc, u, v, dtr, *params)


def _sample_mixer_body(seq_len, z_ref, x0_ref, x1_ref, x2_ref, x3_ref, u_ref, v_ref, dt_ref, h_ref,
                       cw_ref, cb_ref, dtb_ref, alog_ref, alogx_ref, rexp_ref, tril_ref, dskip_ref,
                       sng_ref, vng_ref, vnb_ref, wsp_ref, bsp_ref, mog_ref, segones_ref,
                       cat_ref, vout_ref, hout_ref, cm_scr, bm_scr, wt_scr, dtt_scr, yoff_scr):
    r = CHUNK
    shift = seq_len.bit_length() - 1
    cw = cw_ref[...]
    conv = (cb_ref[...] + cw[3:4] * x0_ref[...] + cw[2:3] * x1_ref[...]
            + cw[1:2] * x2_ref[...] + cw[0:1] * x3_ref[...])
    xs, bm, cm, dt_x, acum, acum_x, tot_x = _mixer_front(
        conv, dt_ref[...], dtb_ref[...], alog_ref[...], alogx_ref[...], rexp_ref[...], tril_ref[...],
        segones_ref[...])
    bmb, cmb = bm.astype(BF16), cm.astype(BF16)
    xdt = xs * dt_x
    row = lax.broadcasted_iota(I32, (r, r), 0)
    col = lax.broadcasted_iota(I32, (r, r), 1)
    same = lax.shift_right_logical(row, shift) == lax.shift_right_logical(col, shift)
    y_diag = _ssd_intra(cmb, bmb, acum, xdt, same & (row >= col))

    cm_scr[...] = cm
    bm_scr[...] = bmb
    wt_scr[...] = (xdt * jnp.exp(tot_x - acum_x)).T
    dtt_scr[...] = jnp.exp(tot_x).T
    ones_b = jnp.ones((LANES, LANES), BF16)
    seqs_per_slab = 8 // seq_len

    def slab(j, carry):
        rows = pl.ds(pl.multiple_of(8 * j, 8), 8)
        cms = cm_scr[rows, :].astype(BF16)
        sub = lax.broadcasted_iota(I32, (8, 256), 0)
        lane = lax.broadcasted_iota(I32, (256, LANES), 1)
        for g in range(SSD_GROUPS):
            q_rows = slice(256 * g, 256 * (g + 1))
            acc = jnp.zeros((8, 256), F32)
            for q in range(seqs_per_slab):
                s = seqs_per_slab * j + q
                y_s = _dot_nt(cms[:, LANES * g:LANES * (g + 1)], h_ref[s, q_rows, :].astype(BF16))
                acc = jnp.where(lax.shift_right_logical(sub, shift) == q, y_s, acc)
            yoff_scr[rows, 256 * g:256 * (g + 1)] = acc
            for q in range(seqs_per_slab):
                s = seqs_per_slab * j + q
                w_sel = jnp.where(lax.shift_right_logical(lane, shift) == s, wt_scr[q_rows, :], 0.0).astype(BF16)
                st = _dot(w_sel, bm_scr[:, LANES * g:LANES * (g + 1)])
                d_sel = jnp.where(lane == s * seq_len, dtt_scr[q_rows, :], 0.0)
                hout_ref[s, q_rows, :] = h_ref[s, q_rows, :] * _sel_right(d_sel, ones_b) + st
        return carry

    lax.fori_loop(0, r // 8, slab, 0)

    y = y_diag + yoff_scr[...] * jnp.exp(acum_x) + dskip_ref[...] * xs
    cat, v_ln = _mixer_back(y, z_ref[...], u_ref[...], v_ref[...], sng_ref[...], vng_ref[...], vnb_ref[...],
                            wsp_ref, bsp_ref[...], mog_ref[...])
    cat_ref[...] = cat
    vout_ref[...] = v_ln


def _sample_mixer_call(z, x_shift, u, v, dtr, h0, params, seg_ones, row0, seq_len):
    ts = x_shift[0].shape[0]
    n = ts // CHUNK
    spt = CHUNK // seq_len
    off = lambda i: (row0 + i, 0)
    loc = lambda i: (i, 0)
    st3 = lambda i: (i, 0, 0)
    in_specs = (
        [pl.BlockSpec((CHUNK, SSD_WIDTH), off)]
        + [pl.BlockSpec((CHUNK, CONV_DIM), loc)] * 4
        + [pl.BlockSpec((CHUNK, MLP_WIDTH), off), pl.BlockSpec((CHUNK, MLP_WIDTH), off),
           pl.BlockSpec((CHUNK, DT_PAD), off), pl.BlockSpec((spt, SSD_WIDTH, D_STATE), st3)]
        + [_const_spec(s) for s in _MIXER_PARAM_SHAPES] + [_const_spec((CHUNK, CHUNK))])
    return pl.pallas_call(
        functools.partial(_sample_mixer_body, seq_len),
        out_shape=[jax.ShapeDtypeStruct((ts, D_MODEL), BF16), jax.ShapeDtypeStruct((ts, MLP_WIDTH), F32),
                   jax.ShapeDtypeStruct(h0.shape, F32)],
        grid=(n,),
        in_specs=in_specs,
        out_specs=[pl.BlockSpec((CHUNK, D_MODEL), loc), pl.BlockSpec((CHUNK, MLP_WIDTH), loc),
                   pl.BlockSpec((spt, SSD_WIDTH, D_STATE), st3)],
        scratch_shapes=[pltpu.VMEM((CHUNK, 256), F32), pltpu.VMEM((CHUNK, 256), BF16),
                        pltpu.VMEM((SSD_WIDTH, CHUNK), F32), pltpu.VMEM((SSD_WIDTH, CHUNK), F32),
                        pltpu.VMEM((CHUNK, SSD_WIDTH), F32)],
        compiler_params=_cparams(("arbitrary",)),
        name="sample_mixer",
    )(z, *x_shift, u, v, dtr, h0, *params, seg_ones)


def _out_router_call(cat_p, cat_s, xp, xs, w_out, g_moe, wr_hi, wr_lo, b_r, tm):
    tp, ts = xp.shape[0], xs.shape[0]
    n_p, n_s = tp // tm, ts // tm
    t_all = tp + ts

    def body(cp_ref, cs_ref, xp_ref, xs_ref, wo_ref, g_ref, wh_ref, wl_ref, br_ref,
             h1_ref, m_ref, eid_ref, gate_ref):
        def run(c_ref, x_ref):
            h1 = x_ref[...] + _dot(c_ref[...], wo_ref[...])
            h1_ref[...] = h1
            m = _rms(h1) * g_ref[...]
            for j in range(TOKEN_TILE_ROWS):
                m_ref[pl.ds(j, tm, stride=TOKEN_TILE_ROWS), :] = m[:, LANES * j:LANES * (j + 1)]
            m_hi = m.astype(BF16)
            m_lo = (m - m_hi.astype(F32)).astype(BF16)
            logits = _dot(m_hi, wh_ref[...]) + _dot(m_lo, wh_ref[...]) + _dot(m_hi, wl_ref[...]) + br_ref[...]
            lane = lax.broadcasted_iota(I32, logits.shape, 1).astype(F32)
            work = logits
            vals, ids = [], []
            for _ in range(TOP_K):
                mx = jnp.max(work, axis=-1, keepdims=True)
                idx = jnp.min(jnp.where(work == mx, lane, float(LANES)), axis=-1, keepdims=True)
                vals.append(mx)
                ids.append(idx)
                work = jnp.where(lane == idx, -jnp.inf, work)
            ex = [jnp.exp(vv - vals[0]) for vv in vals]
            den = ex[0] + ex[1] + ex[2] + ex[3]
            eid = jnp.zeros(logits.shape, I32)
            gate = jnp.zeros(logits.shape, F32)
            for k in range(TOP_K):
                eid = jnp.where(lane == k, ids[k].astype(I32), eid)
                gate = jnp.where(lane == k, ex[k] / den, gate)
            eid_ref[...] = eid
            gate_ref[...] = gate

        i = pl.program_id(0)

        @pl.when(i < n_p)
        def _():
            run(cp_ref, xp_ref)

        @pl.when(i >= n_p)
        def _():
            run(cs_ref, xs_ref)

    pmap = lambda i: (jnp.minimum(i, n_p - 1), 0)
    smap = lambda i: (jnp.maximum(i - n_p, 0), 0)
    omap = lambda i: (i, 0)
    return pl.pallas_call(
        body,
        out_shape=[jax.ShapeDtypeStruct((t_all, D_MODEL), F32),
                   jax.ShapeDtypeStruct((t_all * TOKEN_TILE_ROWS, LANES), F32),
                   jax.ShapeDtypeStruct((t_all, LANES), I32), jax.ShapeDtypeStruct((t_all, LANES), F32)],
        grid=(n_p + n_s,),
        in_specs=[pl.BlockSpec((tm, D_MODEL), pmap), pl.BlockSpec((tm, D_MODEL), smap),
                  pl.BlockSpec((tm, D_MODEL), pmap), pl.BlockSpec((tm, D_MODEL), smap),
                  _const_spec((D_MODEL, D_MODEL)), _const_spec((1, D_MODEL)),
                  _const_spec((D_MODEL, LANES)), _const_spec((D_MODEL, LANES)), _const_spec((1, LANES))],
        out_specs=[pl.BlockSpec((tm, D_MODEL), omap), pl.BlockSpec((tm * TOKEN_TILE_ROWS, LANES), omap),
                   pl.BlockSpec((tm, LANES), omap), pl.BlockSpec((tm, LANES), omap)],
        compiler_params=_cparams(("arbitrary",)),
        name="out_router",
    )(cat_p, cat_s, xp, xs, w_out, g_moe, wr_hi, wr_lo, b_r)


def _route(eid, tm, nb):
    t = eid.shape[0]
    tk = t * TOP_K
    flat = eid.reshape(tk)
    _, order = lax.sort((flat, jnp.arange(tk, dtype=I32)), num_keys=1, is_stable=True)
    counts = jnp.sum((flat[:, None] == jnp.arange(N_EXPERTS, dtype=I32)[None, :]).astype(I32), axis=0)
    nblk = (counts + tm - 1) // tm
    bend = jnp.cumsum(nblk)
    bstart = bend - nblk
    start = jnp.cumsum(counts) - counts
    nused = bend[-1]
    blk = jnp.arange(nb, dtype=I32)
    used = blk < nused
    be = jnp.minimum(jnp.sum((jnp.minimum(blk, nused - 1)[:, None] >= bend[None, :]).astype(I32), axis=1),
                     N_EXPERTS - 1)
    sel = (be[:, None] == jnp.arange(N_EXPERTS, dtype=I32)[None, :]).astype(I32)
    pick = lambda v: jnp.sum(sel * v[None, :], axis=1)
    done = (blk - pick(bstart)) * tm
    nval = jnp.where(used, jnp.clip(pick(counts) - done, 0, tm), 0).astype(I32)
    off = jnp.where(used, pick(start) + done, 0).astype(I32)
    pad = (-(-(tk + tm) // LANES) + _id_rows(tm)) * LANES - tk
    tok = jnp.pad(lax.shift_right_logical(order, TOPK_SHIFT) * TOKEN_TILE_ROWS, (0, pad))
    dst = jnp.pad((order & (TOP_K - 1)) * t + lax.shift_right_logical(order, TOPK_SHIFT), (0, pad))
    return be, nval, off, tok, dst


def _id_rows(tm):
    return tm // LANES + 1


def _moe_body(tm, t_all, nb, be_ref, nval_ref, off_ref, tok_hbm, dst_hbm, m_hbm, wup_ref, bup_ref, wdn_ref, bdn_ref,
              perm_ref, y_hbm, gids, sids, xbuf, ybuf, wup_b, wdn_b, isem, gsem, ssem):
    i = pl.program_id(0)
    nv = nval_ref[i]
    slot = i & 1
    prv = jnp.maximum(i - 1, 0)
    nxt = jnp.minimum(i + 1, nb - 1)
    nx2 = jnp.minimum(i + 2, nb - 1)
    has_next = (i + 1 < nb) & (nval_ref[nxt] > 0)
    has_next2 = (i + 2 < nb) & (nval_ref[nx2] > 0)
    n_prev = jnp.where(i > 0, nval_ref[prv], 0)
    win = _id_rows(tm) * LANES
    spare = TOP_K * t_all

    def ids_copies(b):
        start = pl.multiple_of(lax.shift_right_logical(off_ref[b], 7) * LANES, LANES)
        ring = pl.ds(pl.multiple_of((b & 3) * win, LANES), win)
        return (pltpu.make_async_copy(tok_hbm.at[pl.ds(start, win)], gids.at[ring], isem.at[b & 3, 0]),
                pltpu.make_async_copy(dst_hbm.at[pl.ds(start, win)], sids.at[ring], isem.at[b & 3, 1]))

    def id_base(b):
        return (b & 3) * win + (off_ref[b] & (LANES - 1))

    def gather_row(base, s, r):
        src = pl.ds(pl.multiple_of(gids[base + r], TOKEN_TILE_ROWS), TOKEN_TILE_ROWS)
        return pltpu.make_async_copy(m_hbm.at[src], xbuf.at[s, pl.ds(TOKEN_TILE_ROWS * r, TOKEN_TILE_ROWS)],
                                     gsem.at[s])

    def scatter_row(base, s, r, n):
        dest = jnp.where(r < n, sids[base + r], spare + r)
        return pltpu.make_async_copy(ybuf.at[s, pl.ds(r, 1)], y_hbm.at[pl.ds(dest, 1)], ssem.at[s])

    def wait_gathers(s):
        pltpu.make_async_copy(m_hbm.at[pl.ds(0, tm * TOKEN_TILE_ROWS)], xbuf.at[s], gsem.at[s]).wait()

    def wait_scatters(s):
        pltpu.make_async_copy(ybuf.at[s], y_hbm.at[pl.ds(0, tm)], ssem.at[s]).wait()

    def for_rows(fn):
        def one(r, c):
            fn(r)
            return c

        lax.fori_loop(0, tm, one, 0)

    @pl.when(nv > 0)
    def _():
        @pl.when(i == 0)
        def _():
            ybuf[...] = jnp.zeros_like(ybuf)
            fill = pltpu.make_async_copy(ybuf.at[0], y_hbm.at[pl.ds(spare, tm)], ssem.at[0])
            fill.start()
            fill.wait()
            for b in range(4):
                for cp in ids_copies(b):
                    cp.start()
                    cp.wait()
            base0 = id_base(0)
            for_rows(lambda r: gather_row(base0, 0, r).start())

        @pl.when(has_next & (i >= 3))
        def _():
            for cp in ids_copies(nxt):
                cp.wait()

        @pl.when(has_next2 & (i >= 2))
        def _():
            for cp in ids_copies(nx2):
                cp.start()

        @pl.when((i == 0) | (be_ref[i] != be_ref[prv]))
        def _():
            for jb in range(2 * D_FF // 256):
                cols = slice(256 * jb, 256 * (jb + 1))
                wup_b[:, cols] = _dot(wup_ref[0, :, cols].astype(BF16), perm_ref[...]).astype(BF16)
            wdn_b[...] = wdn_ref[0].astype(BF16)

        def ffn_step(cur, oth):
            wait_gathers(cur)

            @pl.when(i >= 1)
            def _():
                wait_scatters(cur)

            g_base = id_base(nxt)
            s_base = id_base(prv)
            for r in range(tm):
                gather_row(g_base, oth, r).start(priority=r % 2)
                scatter_row(s_base, oth, r, n_prev).start(priority=r % 2)

            x = jnp.concatenate([xbuf[slot, pl.ds(j, tm, stride=TOKEN_TILE_ROWS), :]
                                 for j in range(TOKEN_TILE_ROWS)], axis=1).astype(BF16)
            acts = []
            for jb in range(D_FF // LANES):
                h = _dot(x, wup_b[:, 256 * jb:256 * (jb + 1)]) + bup_ref[0, :, 256 * jb:256 * (jb + 1)]
                gate = jnp.minimum(h[:, :LANES], SWIGLU_LIMIT)
                lin = jnp.clip(h[:, LANES:], -SWIGLU_LIMIT, SWIGLU_LIMIT)
                acts.append((gate * jax.nn.sigmoid(SWIGLU_ALPHA * gate) * (lin + 1.0)).astype(BF16))
            act = jnp.concatenate(acts, axis=1)
            for c in range(D_MODEL // 256):
                ybuf[slot, :, 256 * c:256 * (c + 1)] = (
                    _dot(act, wdn_b[:, 256 * c:256 * (c + 1)]) + bdn_ref[0, :, 256 * c:256 * (c + 1)])

            @pl.when(jnp.logical_not(has_next))
            def _():
                wait_gathers(oth)
                wait_scatters(oth)
                last_base = id_base(i)
                for_rows(lambda r: scatter_row(last_base, cur, r, nv).start())
                wait_scatters(cur)

        for parity in range(2):
            pl.when(slot == parity)(functools.partial(ffn_step, parity, 1 - parity))


def _moe_call(m, be, nval, off, tok, dst, w_up, b_up_g, w_down, b_down, perm, tm, nb):
    t = m.shape[0] // TOKEN_TILE_ROWS
    assert nb >= 4
    by_expert = lambda i, be, nv, off: (be[i], 0, 0)
    grid_spec = pltpu.PrefetchScalarGridSpec(
        num_scalar_prefetch=3,
        grid=(nb,),
        in_specs=[
            pl.BlockSpec(memory_space=pl.ANY),
            pl.BlockSpec(memory_space=pl.ANY),
            pl.BlockSpec(memory_space=pl.ANY),
            pl.BlockSpec((1, D_MODEL, 2 * D_FF), by_expert),
            pl.BlockSpec((1, 1, 2 * D_FF), by_expert),
            pl.BlockSpec((1, D_FF, D_MODEL), by_expert),
            pl.BlockSpec((1, 1, D_MODEL), by_expert),
            pl.BlockSpec((256, 256), lambda i, be, nv, off: (0, 0)),
        ],
        out_specs=pl.BlockSpec(memory_space=pl.ANY),
        scratch_shapes=[pltpu.SMEM((4 * _id_rows(tm) * LANES,), I32), pltpu.SMEM((4 * _id_rows(tm) * LANES,), I32),
                        pltpu.VMEM((2, tm * TOKEN_TILE_ROWS, LANES), F32), pltpu.VMEM((2, tm, D_MODEL), F32),
                        pltpu.VMEM((D_MODEL, 2 * D_FF), BF16), pltpu.VMEM((D_FF, D_MODEL), BF16),
                        pltpu.SemaphoreType.DMA((4, 2)), pltpu.SemaphoreType.DMA((2,)),
                        pltpu.SemaphoreType.DMA((2,))],
    )
    return pl.pallas_call(
        functools.partial(_moe_body, tm, t, nb),
        out_shape=jax.ShapeDtypeStruct((TOP_K * t + tm, D_MODEL), F32),
        grid_spec=grid_spec,
        compiler_params=_cparams(("arbitrary",)),
        name="moe_experts",
    )(be, nval, off, tok, dst, m, w_up, b_up_g, w_down, b_down, perm)


def _ple_call(h1, y4, gates, pp, ps, g_ple, w_gate, w_proj, g_final, tm):
    tp, ts = pp.shape[0], ps.shape[0]
    n_p, n_s = tp // tm, ts // tm
    ple = pp.shape[1]

    def body(h1_ref, y0_ref, y1_ref, y2_ref, y3_ref, gt_ref, pp_ref, ps_ref, g_ref, wg_ref, wp_ref, gf_ref,
             yp_ref, ys_ref):
        def run(p_ref, o_ref):
            gt = gt_ref[...]
            moe = None
            for k, y_ref in enumerate((y0_ref, y1_ref, y2_ref, y3_ref)):
                moe = gt[:, k:k + 1] * y_ref[...] if moe is None else moe + gt[:, k:k + 1] * y_ref[...]
            h2 = h1_ref[...] + moe
            a = (_rms(h2) * g_ref[...]).astype(BF16)
            gate = jax.nn.sigmoid(_dot(a, wg_ref[...]))
            pe = _dot(p_ref[...].astype(BF16), wp_ref[...])
            h3 = h2 + pe * gate
            o_ref[...] = _rms(h3) * gf_ref[...]

        i = pl.program_id(0)

        @pl.when(i < n_p)
        def _():
            run(pp_ref, yp_ref)

        @pl.when(i >= n_p)
        def _():
            run(ps_ref, ys_ref)

    pmap = lambda i: (jnp.minimum(i, n_p - 1), 0)
    smap = lambda i: (jnp.maximum(i - n_p, 0), 0)
    omap = lambda i: (i, 0)
    return pl.pallas_call(
        body,
        out_shape=[jax.ShapeDtypeStruct((tp, D_MODEL), F32), jax.ShapeDtypeStruct((ts, D_MODEL), F32)],
        grid=(n_p + n_s,),
        in_specs=[pl.BlockSpec((tm, D_MODEL), omap)]
                 + [pl.BlockSpec((tm, D_MODEL), functools.partial(lambda k, i: (k * (n_p + n_s) + i, 0), k))
                    for k in range(TOP_K)]
                 + [pl.BlockSpec((tm, LANES), omap), pl.BlockSpec((tm, ple), pmap), pl.BlockSpec((tm, ple), smap),
                  _const_spec((1, D_MODEL)), _const_spec((D_MODEL, D_MODEL)), _const_spec((ple, D_MODEL)),
                  _const_spec((1, D_MODEL))],
        out_specs=[pl.BlockSpec((tm, D_MODEL), pmap), pl.BlockSpec((tm, D_MODEL), smap)],
        compiler_params=_cparams(("arbitrary",)),
        name="ple_final",
    )(h1, y4, y4, y4, y4, gates, pp, ps, g_ple, w_gate, w_proj, g_final)


def _row(x, width=None):
    x = x.reshape(1, -1).astype(F32)
    if width is not None and x.shape[1] < width:
        x = jnp.pad(x, ((0, 0), (0, width - x.shape[1])))
    return x


def _mixer_params(conv_w, conv_b, dt_bias, a_log, d_skip, ssd_norm_g, v_norm_g, v_norm_b, w_spatial, b_spatial,
                  mlp_out_g, seq_len):
    n_seq = CHUNK // seq_len
    pos = jnp.arange(CHUNK) % seq_len
    same = (jnp.arange(CHUNK)[:, None] // seq_len) == (jnp.arange(CHUNK)[None, :] // seq_len)
    tril = (same & (jnp.arange(CHUNK)[:, None] >= jnp.arange(CHUNK)[None, :])).astype(BF16)
    rexp = (jnp.arange(LANES)[:, None] == (jnp.arange(SSD_WIDTH)[None, :] // HEAD_DIM)).astype(BF16)
    w_loc = jnp.tril(w_spatial[:, :seq_len, :seq_len])
    eye = jnp.eye(n_seq, dtype=F32)
    w_bd = jnp.einsum("st,hij->hsitj", eye, w_loc).reshape(MLP_HEADS, CHUNK, CHUNK)
    wsp = (w_bd.reshape(MLP_HEADS // 2, 2, CHUNK, CHUNK).transpose(0, 2, 1, 3)
           .reshape(MLP_HEADS // 2, CHUNK, 2 * CHUNK).astype(BF16))
    bsp = jnp.repeat(b_spatial[:, :seq_len].T[pos], MLP_WIDTH // MLP_HEADS, axis=1)
    params = (
        conv_w.astype(F32), _row(conv_b), _row(dt_bias, LANES), _row(a_log, LANES),
        _row(jnp.repeat(a_log, HEAD_DIM)), rexp, tril, _row(jnp.repeat(d_skip, HEAD_DIM)),
        _row(ssd_norm_g), _row(v_norm_g), _row(v_norm_b), wsp, bsp.astype(F32), _row(mlp_out_g),
    )
    return params, same.astype(BF16)


def _tile_rows(n):
    return 512 if n % 512 == 0 else CHUNK


def kernel(x_prompt, x_sample, state_ssm, state_conv, p_prompt, p_sample, norm_mix_g, w_in, conv_w, conv_b, dt_bias, a_log, d_skip, ssd_norm_g, v_norm_g, v_norm_b, w_spatial, b_spatial, mlp_out_g, w_out, norm_moe_g, w_router, b_router, w_up, b_up, w_down, b_down, norm_ple_g, w_ple_gate, w_ple_proj, norm_final_g):
    depth = norm_mix_g.shape[0]
    bp, lp, d = x_prompt.shape
    bs, ls, _ = x_sample.shape
    tp, ts = bp * lp, bs * ls
    assert depth == 1 and d == D_MODEL and lp % CHUNK == 0 and ts % CHUNK == 0 and 8 % ls == 0
    tm = _tile_rows(tp) if ts % _tile_rows(tp) == 0 else CHUNK
    t_all = tp + ts
    tm_moe = 256
    nb_moe = -(-t_all * TOP_K // tm_moe) + N_EXPERTS

    hp = x_prompt.reshape(tp, d)
    hs = x_sample.reshape(ts, d)
    ssm_p, conv_p, ssm_s, conv_s, v_s = [], [], [], [], []
    o1 = SSD_WIDTH
    o2 = o1 + CONV_DIM
    o3 = o2 + SSD_HEADS
    o4 = o3 + MLP_WIDTH
    c = jnp.arange(256)
    src = jnp.where(c < LANES, 2 * c, 2 * (c - LANES) + 1)
    perm = (jnp.arange(256)[:, None] == src[None, :]).astype(BF16)

    for i in range(depth):
        wi = w_in[i]
        w_cat = jnp.concatenate(
            [wi[:, :o2], wi[:, o3:], jnp.pad(wi[:, o2:o3], ((0, 0), (0, DT_PAD - SSD_HEADS)))], axis=1).astype(BF16)
        z, xbc, u, v, dtr = _inproj_call(hp, hs, _row(norm_mix_g[i]), w_cat, tm)

        mix_args = (conv_w[i], conv_b[i], dt_bias[i], a_log[i], d_skip[i], ssd_norm_g[i], v_norm_g[i], v_norm_b[i],
                    w_spatial[i], b_spatial[i], mlp_out_g[i])
        prm_p, _ = _mixer_params(*mix_args, seq_len=CHUNK)
        cat_p, s_p = _prompt_mixer_call(z, xbc, u, v, dtr, prm_p, bp, lp // CHUNK)
        ssm_p.append(s_p.reshape(bp, SSD_HEADS, HEAD_DIM, D_STATE).astype(state_ssm.dtype))
        conv_p.append(jnp.stack([xbc[(b + 1) * lp - (CONV_W - 1):(b + 1) * lp] for b in range(bp)]))

        prm_s, seg_ones = _mixer_params(*mix_args, seq_len=ls)
        xbc_s = xbc[tp:].reshape(bs, ls, CONV_DIM)
        xpad = jnp.concatenate([state_conv[i].astype(F32), xbc_s], axis=1)
        x_shift = [xpad[:, CONV_W - 1 - k:CONV_W - 1 - k + ls].reshape(ts, CONV_DIM) for k in range(CONV_W)]
        h0 = state_ssm[i].astype(F32).reshape(bs, SSD_WIDTH, D_STATE)
        cat_s, v_rows, s_s = _sample_mixer_call(z, x_shift, u, v, dtr, h0, prm_s, seg_ones, tp // CHUNK, ls)
        ssm_s.append(s_s.reshape(bs, SSD_HEADS, HEAD_DIM, D_STATE).astype(state_ssm.dtype))
        conv_s.append(xpad[:, ls:])
        v_s.append(v_rows.reshape(bs, ls, MLP_WIDTH))

        wr = jnp.pad(w_router[i].astype(F32), ((0, 0), (0, LANES - N_EXPERTS)))
        wr_hi = wr.astype(BF16)
        wr_lo = (wr - wr_hi.astype(F32)).astype(BF16)
        b_r = jnp.concatenate([b_router[i].astype(F32), jnp.full((LANES - N_EXPERTS,), -1e30, F32)]).reshape(1, LANES)
        h1, m, eid, gates = _out_router_call(cat_p, cat_s, hp, hs, w_out[i].astype(BF16), _row(norm_moe_g[i]),
                                             wr_hi, wr_lo, b_r, tm)

        be, nval, off, tok, dst = _route(eid[:, :TOP_K], tm_moe, nb_moe)
        b_up_g = (b_up[i].astype(F32).reshape(N_EXPERTS, 2 * D_FF // 256, LANES, 2).transpose(0, 1, 3, 2)
                  .reshape(N_EXPERTS, 1, 2 * D_FF))
        y4 = _moe_call(m, be, nval, off, tok, dst, w_up[i], b_up_g, w_down[i],
                       b_down[i].reshape(N_EXPERTS, 1, D_MODEL), perm, tm_moe, nb_moe)

        hp, hs = _ple_call(h1, y4, gates,
                           p_prompt[i].reshape(tp, -1), p_sample[i].reshape(ts, -1), _row(norm_ple_g[i]),
                           w_ple_gate[i].astype(BF16), w_ple_proj[i].astype(BF16), _row(norm_final_g), tm)

    y_prompt = hp.reshape(bp, lp, d)
    y_sample = hs.reshape(bs, ls, d)
    return (y_prompt, y_sample, jnp.stack(ssm_p), jnp.stack(conv_p), jnp.stack(ssm_s), jnp.stack(conv_s),
            jnp.stack(v_s))
```

```python
import functools

import jax
import jax.numpy as jnp
from jax import lax
from jax.experimental import pallas as pl
from jax.experimental.pallas import tpu as pltpu

F32 = jnp.float32
BF16 = jnp.bfloat16
I32 = jnp.int32

EPS = 1e-6
D_MODEL = 1024
SSD_WIDTH = 512
SSD_HEADS = 8
HEAD_DIM = 64
SSD_GROUPS = 2
D_STATE = 128
CONV_W = 4
CONV_DIM = SSD_WIDTH + 2 * SSD_GROUPS * D_STATE
MLP_WIDTH = 512
MLP_HEADS = 8
N_EXPERTS = 32
TOP_K = 4
D_FF = 1024
SWIGLU_LIMIT = 7.0
SWIGLU_ALPHA = 1.702
TOPK_SHIFT = 2
assert 1 << TOPK_SHIFT == TOP_K
LANES = 128
CHUNK = 128
DT_PAD = LANES
TOKEN_TILE_ROWS = D_MODEL // LANES
IN_PAD = SSD_WIDTH + CONV_DIM + 2 * MLP_WIDTH + DT_PAD
VMEM_LIMIT = 56 * 1024 * 1024


def _cparams(sem):
    return pltpu.CompilerParams(dimension_semantics=sem, vmem_limit_bytes=VMEM_LIMIT)


def _const_spec(shape):
    return pl.BlockSpec(shape, lambda *_: (0,) * len(shape))


def _rms(x):
    return x * lax.rsqrt(jnp.mean(x * x, axis=-1, keepdims=True) + EPS)


def _dot(a, b):
    return jnp.dot(a, b, preferred_element_type=F32)


def _dot_nt(a, b):
    return lax.dot_general(a, b, (((1,), (1,)), ((), ())), preferred_element_type=F32)


def _split3(x):
    hi = x.astype(BF16)
    r = x - hi.astype(F32)
    mid = r.astype(BF16)
    lo = (r - mid.astype(F32)).astype(BF16)
    return hi, mid, lo


def _sel_right(x, m01):
    hi, mid, lo = _split3(x)
    return _dot(hi, m01) + _dot(mid, m01) + _dot(lo, m01)


def _sel_left(m01, x):
    hi, mid, lo = _split3(x)
    return _dot(m01, hi) + _dot(m01, mid) + _dot(m01, lo)


def _softplus(x):
    return jnp.maximum(x, 0.0) + jnp.log1p(jnp.exp(-jnp.abs(x)))


def _inproj_call(xp, xs, g, w, tm):
    tp, ts = xp.shape[0], xs.shape[0]
    n_p, n_s = tp // tm, ts // tm
    t_all = tp + ts
    segs = ((0, 512), (512, 1536), (1536, 2048), (2048, 2560), (2560, IN_PAD))

    def body(xp_ref, xs_ref, g_ref, w_ref, *outs):
        def run(x_ref):
            xn = (_rms(x_ref[...]) * g_ref[...]).astype(BF16)
            for (a, b), o in zip(segs, outs):
                o[...] = _dot(xn, w_ref[:, a:b])

        i = pl.program_id(0)

        @pl.when(i < n_p)
        def _():
            run(xp_ref)

        @pl.when(i >= n_p)
        def _():
            run(xs_ref)

    widths = [b - a for a, b in segs]
    return pl.pallas_call(
        body,
        out_shape=[jax.ShapeDtypeStruct((t_all, wd), F32) for wd in widths],
        grid=(n_p + n_s,),
        in_specs=[
            pl.BlockSpec((tm, D_MODEL), lambda i: (jnp.minimum(i, n_p - 1), 0)),
            pl.BlockSpec((tm, D_MODEL), lambda i: (jnp.maximum(i - n_p, 0), 0)),
            _const_spec((1, D_MODEL)),
            _const_spec((D_MODEL, IN_PAD)),
        ],
        out_specs=[pl.BlockSpec((tm, wd), lambda i: (i, 0)) for wd in widths],
        compiler_params=_cparams(("arbitrary",)),
        name="in_proj",
    )(xp, xs, g, w)


def _mixer_front(conv, dtr, dtb, alog, alog_x, rexp, tril, seg_ones):
    xact = conv * jax.nn.sigmoid(conv)
    xs = xact[:, :SSD_WIDTH]
    bm = xact[:, SSD_WIDTH:SSD_WIDTH + 256]
    cm = xact[:, SSD_WIDTH + 256:]
    dt = _softplus(dtr + dtb)
    a = dt * (-jnp.exp(alog))
    dt_x = _sel_right(dt, rexp)
    a_x = dt_x * (-jnp.exp(alog_x))
    acum = _sel_left(tril, a)
    acum_x = _sel_left(tril, a_x)
    if seg_ones is None:
        r = acum_x.shape[0]
        tot_x = jnp.broadcast_to(acum_x[r - 1:r, :], acum_x.shape)
    else:
        tot_x = _sel_left(seg_ones, a_x)
    return xs, bm, cm, dt_x, acum, acum_x, tot_x


def _ssd_intra(cmb, bmb, acum, xdt, mask):
    r = acum.shape[0]
    acum_t = acum.T
    lane = lax.broadcasted_iota(I32, (r, LANES), 1)
    low = lane < HEAD_DIM
    outs = []
    for g in range(SSD_GROUPS):
        sg = _dot_nt(cmb[:, LANES * g:LANES * (g + 1)], bmb[:, LANES * g:LANES * (g + 1)])
        for k in (2 * g, 2 * g + 1):
            parts = []
            for h in (2 * k, 2 * k + 1):
                seg = acum[:, h:h + 1] - acum_t[h:h + 1, :]
                parts.append((sg * jnp.exp(jnp.where(mask, seg, -jnp.inf))).astype(BF16))
            lhs = jnp.concatenate(parts, axis=1)
            xd = xdt[:, LANES * k:LANES * (k + 1)]
            rhs = jnp.concatenate([jnp.where(low, xd, 0.0), jnp.where(low, 0.0, xd)], axis=0).astype(BF16)
            outs.append(_dot(lhs, rhs))
    return jnp.concatenate(outs, axis=1)


def _mixer_back(y, z, u, v, sng, vng, vnb, wsp_ref, bsp, mog):
    r = y.shape[0]
    yg = y * (z * jax.nn.sigmoid(z))
    halves = []
    for g in range(SSD_GROUPS):
        t = yg[:, 256 * g:256 * (g + 1)]
        halves.append(_rms(t))
    yn = jnp.concatenate(halves, axis=1) * sng
    ug = jax.nn.gelu(u)
    vg = jax.nn.gelu(v)
    mu = jnp.mean(vg, axis=-1, keepdims=True)
    var = jnp.mean(jnp.square(vg - mu), axis=-1, keepdims=True)
    v_ln = (vg - mu) * lax.rsqrt(var + EPS) * vng + vnb
    lane = lax.broadcasted_iota(I32, (r, LANES), 1)
    low = lane < HEAD_DIM
    outs = []
    for k in range(MLP_HEADS // 2):
        vd = v_ln[:, LANES * k:LANES * (k + 1)]
        rhs = jnp.concatenate([jnp.where(low, vd, 0.0), jnp.where(low, 0.0, vd)], axis=0).astype(BF16)
        outs.append(_dot(wsp_ref[k], rhs))
    s = jnp.concatenate(outs, axis=1) + bsp
    m = _rms(ug * s) * mog
    return jnp.concatenate([yn, m], axis=1).astype(BF16), v_ln


_MIXER_PARAM_SHAPES = (
    (CONV_W, CONV_DIM), (1, CONV_DIM), (1, LANES), (1, LANES), (1, SSD_WIDTH), (LANES, SSD_WIDTH),
    (CHUNK, CHUNK), (1, SSD_WIDTH), (1, SSD_WIDTH), (1, MLP_WIDTH), (1, MLP_WIDTH),
    (MLP_HEADS // 2, CHUNK, 2 * CHUNK), (CHUNK, MLP_WIDTH), (1, MLP_WIDTH),
)


def _prompt_mixer_body(z_ref, xbc_ref, u_ref, v_ref, dt_ref,
                       cw_ref, cb_ref, dtb_ref, alog_ref, alogx_ref, rexp_ref, tril_ref, dskip_ref,
                       sng_ref, vng_ref, vnb_ref, wsp_ref, bsp_ref, mog_ref,
                       cat_ref, ssm_ref, ext_scr, s_scr):
    c = pl.program_id(1)
    r = CHUNK

    @pl.when(c == 0)
    def _():
        ext_scr[0:8, :] = jnp.zeros((8, CONV_DIM), F32)
        s_scr[...] = jnp.zeros_like(s_scr)

    x = xbc_ref[...]
    ext_scr[8:8 + r, :] = x
    cw = cw_ref[...]
    conv = (cb_ref[...] + cw[3:4] * x + cw[2:3] * ext_scr[7:7 + r, :]
            + cw[1:2] * ext_scr[6:6 + r, :] + cw[0:1] * ext_scr[5:5 + r, :])
    ext_scr[0:8, :] = x[r - 8:r, :]

    xs, bm, cm, dt_x, acum, acum_x, tot_x = _mixer_front(
        conv, dt_ref[...], dtb_ref[...], alog_ref[...], alogx_ref[...], rexp_ref[...], tril_ref[...], None)
    bmb, cmb = bm.astype(BF16), cm.astype(BF16)
    xdt = xs * dt_x
    row = lax.broadcasted_iota(I32, (r, r), 0)
    col = lax.broadcasted_iota(I32, (r, r), 1)
    y_diag = _ssd_intra(cmb, bmb, acum, xdt, row >= col)

    s_prev = s_scr[...]
    s_prev_b = s_prev.astype(BF16)
    y_off = jnp.concatenate(
        [_dot_nt(cmb[:, LANES * g:LANES * (g + 1)], s_prev_b[256 * g:256 * (g + 1), :]) for g in range(SSD_GROUPS)],
        axis=1)
    y = y_diag + y_off * jnp.exp(acum_x) + dskip_ref[...] * xs

    w_t = (xdt * jnp.exp(tot_x - acum_x)).T.astype(BF16)
    states = jnp.concatenate(
        [_dot(w_t[256 * g:256 * (g + 1), :], bmb[:, LANES * g:LANES * (g + 1)]) for g in range(SSD_GROUPS)], axis=0)
    s_new = s_prev * jnp.exp(tot_x).T + states
    s_scr[...] = s_new

    cat, _ = _mixer_back(y, z_ref[...], u_ref[...], v_ref[...], sng_ref[...], vng_ref[...], vnb_ref[...],
                         wsp_ref, bsp_ref[...], mog_ref[...])
    cat_ref[...] = cat

    @pl.when(c == pl.num_programs(1) - 1)
    def _():
        ssm_ref[0] = s_new


def _prompt_mixer_call(z, xbc, u, v, dtr, params, nb, nc):
    row = lambda b, c: (b * nc + c, 0)
    in_specs = [
        pl.BlockSpec((CHUNK, SSD_WIDTH), row), pl.BlockSpec((CHUNK, CONV_DIM), row),
        pl.BlockSpec((CHUNK, MLP_WIDTH), row), pl.BlockSpec((CHUNK, MLP_WIDTH), row),
        pl.BlockSpec((CHUNK, DT_PAD), row),
    ] + [_const_spec(s) for s in _MIXER_PARAM_SHAPES]
    return pl.pallas_call(
        _prompt_mixer_body,
        out_shape=[jax.ShapeDtypeStruct((nb * nc * CHUNK, D_MODEL), BF16),
                   jax.ShapeDtypeStruct((nb, SSD_WIDTH, D_STATE), F32)],
        grid=(nb, nc),
        in_specs=in_specs,
        out_specs=[pl.BlockSpec((CHUNK, D_MODEL), row),
                   pl.BlockSpec((1, SSD_WIDTH, D_STATE), lambda b, c: (b, 0, 0))],
        scratch_shapes=[pltpu.VMEM((CHUNK + 8, CONV_DIM), F32), pltpu.VMEM((SSD_WIDTH, D_STATE), F32)],
        compiler_params=_cparams(("arbitrary", "arbitrary")),
        name="prompt_mixer",
    )(z, xbc, u, v, dtr, *params)


def _sample_mixer_body(seq_len, z_ref, x0_ref, x1_ref, x2_ref, x3_ref, u_ref, v_ref, dt_ref, h_ref,
                       cw_ref, cb_ref, dtb_ref, alog_ref, alogx_ref, rexp_ref, tril_ref, dskip_ref,
                       sng_ref, vng_ref, vnb_ref, wsp_ref, bsp_ref, mog_ref, segones_ref,
                       cat_ref, vout_ref, hout_ref, cm_scr, bm_scr, wt_scr, dtt_scr, yoff_scr):
    r = CHUNK
    shift = seq_len.bit_length() - 1
    cw = cw_ref[...]
    conv = (cb_ref[...] + cw[3:4] * x0_ref[...] + cw[2:3] * x1_ref[...]
            + cw[1:2] * x2_ref[...] + cw[0:1] * x3_ref[...])
    xs, bm, cm, dt_x, acum, acum_x, tot_x = _mixer_front(
        conv, dt_ref[...], dtb_ref[...], alog_ref[...], alogx_ref[...], rexp_ref[...], tril_ref[...],
        segones_ref[...])
    bmb, cmb = bm.astype(BF16), cm.astype(BF16)
    xdt = xs * dt_x
    row = lax.broadcasted_iota(I32, (r, r), 0)
    col = lax.broadcasted_iota(I32, (r, r), 1)
    same = lax.shift_right_logical(row, shift) == lax.shift_right_logical(col, shift)
    y_diag = _ssd_intra(cmb, bmb, acum, xdt, same & (row >= col))

    cm_scr[...] = cm
    bm_scr[...] = bmb
    wt_scr[...] = (xdt * jnp.exp(tot_x - acum_x)).T
    dtt_scr[...] = jnp.exp(tot_x).T
    ones_b = jnp.ones((LANES, LANES), BF16)
    seqs_per_slab = 8 // seq_len

    def slab(j, carry):
        rows = pl.ds(pl.multiple_of(8 * j, 8), 8)
        cms = cm_scr[rows, :].astype(BF16)
        sub = lax.broadcasted_iota(I32, (8, 256), 0)
        lane = lax.broadcasted_iota(I32, (256, LANES), 1)
        for g in range(SSD_GROUPS):
            q_rows = slice(256 * g, 256 * (g + 1))
            acc = jnp.zeros((8, 256), F32)
            for q in range(seqs_per_slab):
                s = seqs_per_slab * j + q
                y_s = _dot_nt(cms[:, LANES * g:LANES * (g + 1)], h_ref[s, q_rows, :].astype(BF16))
                acc = jnp.where(lax.shift_right_logical(sub, shift) == q, y_s, acc)
            yoff_scr[rows, 256 * g:256 * (g + 1)] = acc
            for q in range(seqs_per_slab):
                s = seqs_per_slab * j + q
                w_sel = jnp.where(lax.shift_right_logical(lane, shift) == s, wt_scr[q_rows, :], 0.0).astype(BF16)
                st = _dot(w_sel, bm_scr[:, LANES * g:LANES * (g + 1)])
                d_sel = jnp.where(lane == s * seq_len, dtt_scr[q_rows, :], 0.0)
                hout_ref[s, q_rows, :] = h_ref[s, q_rows, :] * _sel_right(d_sel, ones_b) + st
        return carry

    lax.fori_loop(0, r // 8, slab, 0)

    y = y_diag + yoff_scr[...] * jnp.exp(acum_x) + dskip_ref[...] * xs
    cat, v_ln = _mixer_back(y, z_ref[...], u_ref[...], v_ref[...], sng_ref[...], vng_ref[...], vnb_ref[...],
                            wsp_ref, bsp_ref[...], mog_ref[...])
    cat_ref[...] = cat
    vout_ref[...] = v_ln


def _sample_mixer_call(z, x_shift, u, v, dtr, h0, params, seg_ones, row0, seq_len):
    ts = x_shift[0].shape[0]
    n = ts // CHUNK
    spt = CHUNK // seq_len
    off = lambda i: (row0 + i, 0)
    loc = lambda i: (i, 0)
    st3 = lambda i: (i, 0, 0)
    in_specs = (
        [pl.BlockSpec((CHUNK, SSD_WIDTH), off)]
        + [pl.BlockSpec((CHUNK, CONV_DIM), loc)] * 4
        + [pl.BlockSpec((CHUNK, MLP_WIDTH), off), pl.BlockSpec((CHUNK, MLP_WIDTH), off),
           pl.BlockSpec((CHUNK, DT_PAD), off), pl.BlockSpec((spt, SSD_WIDTH, D_STATE), st3)]
        + [_const_spec(s) for s in _MIXER_PARAM_SHAPES] + [_const_spec((CHUNK, CHUNK))])
    return pl.pallas_call(
        functools.partial(_sample_mixer_body, seq_len),
        out_shape=[jax.ShapeDtypeStruct((ts, D_MODEL), BF16), jax.ShapeDtypeStruct((ts, MLP_WIDTH), F32),
                   jax.ShapeDtypeStruct(h0.shape, F32)],
        grid=(n,),
        in_specs=in_specs,
        out_specs=[pl.BlockSpec((CHUNK, D_MODEL), loc), pl.BlockSpec((CHUNK, MLP_WIDTH), loc),
                   pl.BlockSpec((spt, SSD_WIDTH, D_STATE), st3)],
        scratch_shapes=[pltpu.VMEM((CHUNK, 256), F32), pltpu.VMEM((CHUNK, 256), BF16),
                        pltpu.VMEM((SSD_WIDTH, CHUNK), F32), pltpu.VMEM((SSD_WIDTH, CHUNK), F32),
                        pltpu.VMEM((CHUNK, SSD_WIDTH), F32)],
        compiler_params=_cparams(("arbitrary",)),
        name="sample_mixer",
    )(z, *x_shift, u, v, dtr, h0, *params, seg_ones)


def _out_router_call(cat_p, cat_s, xp, xs, w_out, g_moe, wr_hi, wr_lo, b_r, tm):
    tp, ts = xp.shape[0], xs.shape[0]
    n_p, n_s = tp // tm, ts // tm
    t_all = tp + ts

    def body(cp_ref, cs_ref, xp_ref, xs_ref, wo_ref, g_ref, wh_ref, wl_ref, br_ref,
             h1_ref, m_ref, eid_ref, gate_ref):
        def run(c_ref, x_ref):
            h1 = x_ref[...] + _dot(c_ref[...], wo_ref[...])
            h1_ref[...] = h1
            m = _rms(h1) * g_ref[...]
            for j in range(TOKEN_TILE_ROWS):
                m_ref[pl.ds(j, tm, stride=TOKEN_TILE_ROWS), :] = m[:, LANES * j:LANES * (j + 1)]
            m_hi = m.astype(BF16)
            m_lo = (m - m_hi.astype(F32)).astype(BF16)
            logits = _dot(m_hi, wh_ref[...]) + _dot(m_lo, wh_ref[...]) + _dot(m_hi, wl_ref[...]) + br_ref[...]
            lane = lax.broadcasted_iota(I32, logits.shape, 1).astype(F32)
            work = logits
            vals, ids = [], []
            for _ in range(TOP_K):
                mx = jnp.max(work, axis=-1, keepdims=True)
                idx = jnp.min(jnp.where(work == mx, lane, float(LANES)), axis=-1, keepdims=True)
                vals.append(mx)
                ids.append(idx)
                work = jnp.where(lane == idx, -jnp.inf, work)
            ex = [jnp.exp(vv - vals[0]) for vv in vals]
            den = ex[0] + ex[1] + ex[2] + ex[3]
            eid = jnp.zeros(logits.shape, I32)
            gate = jnp.zeros(logits.shape, F32)
            for k in range(TOP_K):
                eid = jnp.where(lane == k, ids[k].astype(I32), eid)
                gate = jnp.where(lane == k, ex[k] / den, gate)
            eid_ref[...] = eid
            gate_ref[...] = gate

        i = pl.program_id(0)

        @pl.when(i < n_p)
        def _():
            run(cp_ref, xp_ref)

        @pl.when(i >= n_p)
        def _():
            run(cs_ref, xs_ref)

    pmap = lambda i: (jnp.minimum(i, n_p - 1), 0)
    smap = lambda i: (jnp.maximum(i - n_p, 0), 0)
    omap = lambda i: (i, 0)
    return pl.pallas_call(
        body,
        out_shape=[jax.ShapeDtypeStruct((t_all, D_MODEL), F32),
                   jax.ShapeDtypeStruct((t_all * TOKEN_TILE_ROWS, LANES), F32),
                   jax.ShapeDtypeStruct((t_all, LANES), I32), jax.ShapeDtypeStruct((t_all, LANES), F32)],
        grid=(n_p + n_s,),
        in_specs=[pl.BlockSpec((tm, D_MODEL), pmap), pl.BlockSpec((tm, D_MODEL), smap),
                  pl.BlockSpec((tm, D_MODEL), pmap), pl.BlockSpec((tm, D_MODEL), smap),
                  _const_spec((D_MODEL, D_MODEL)), _const_spec((1, D_MODEL)),
                  _const_spec((D_MODEL, LANES)), _const_spec((D_MODEL, LANES)), _const_spec((1, LANES))],
        out_specs=[pl.BlockSpec((tm, D_MODEL), omap), pl.BlockSpec((tm * TOKEN_TILE_ROWS, LANES), omap),
                   pl.BlockSpec((tm, LANES), omap), pl.BlockSpec((tm, LANES), omap)],
        compiler_params=_cparams(("arbitrary",)),
        name="out_router",
    )(cat_p, cat_s, xp, xs, w_out, g_moe, wr_hi, wr_lo, b_r)


def _route(eid, tm, nb):
    t = eid.shape[0]
    tk = t * TOP_K
    flat = eid.reshape(tk)
    _, order = lax.sort((flat, jnp.arange(tk, dtype=I32)), num_keys=1, is_stable=True)
    counts = jnp.sum((flat[:, None] == jnp.arange(N_EXPERTS, dtype=I32)[None, :]).astype(I32), axis=0)
    nblk = (counts + tm - 1) // tm
    bend = jnp.cumsum(nblk)
    bstart = bend - nblk
    start = jnp.cumsum(counts) - counts
    nused = bend[-1]
    blk = jnp.arange(nb, dtype=I32)
    used = blk < nused
    be = jnp.minimum(jnp.sum((jnp.minimum(blk, nused - 1)[:, None] >= bend[None, :]).astype(I32), axis=1),
                     N_EXPERTS - 1)
    sel = (be[:, None] == jnp.arange(N_EXPERTS, dtype=I32)[None, :]).astype(I32)
    pick = lambda v: jnp.sum(sel * v[None, :], axis=1)
    done = (blk - pick(bstart)) * tm
    nval = jnp.where(used, jnp.clip(pick(counts) - done, 0, tm), 0).astype(I32)
    off = jnp.where(used, pick(start) + done, 0).astype(I32)
    pad = (-(-(tk + tm) // LANES) + _id_rows(tm)) * LANES - tk
    tok = jnp.pad(lax.shift_right_logical(order, TOPK_SHIFT) * TOKEN_TILE_ROWS, (0, pad))
    dst = jnp.pad((order & (TOP_K - 1)) * t + lax.shift_right_logical(order, TOPK_SHIFT), (0, pad))
    return be, nval, off, tok, dst


def _id_rows(tm):
    return tm // LANES + 1


def _moe_body(tm, t_all, nb, be_ref, nval_ref, off_ref, tok_hbm, dst_hbm, m_hbm, wup_ref, bup_ref, wdn_ref, bdn_ref,
              perm_ref, y_hbm, gids, sids, xbuf, ybuf, wup_b, wdn_b, isem, gsem, ssem):
    i = pl.program_id(0)
    nv = nval_ref[i]
    slot = i & 1
    prv = jnp.maximum(i - 1, 0)
    nxt = jnp.minimum(i + 1, nb - 1)
    nx2 = jnp.minimum(i + 2, nb - 1)
    has_next = (i + 1 < nb) & (nval_ref[nxt] > 0)
    has_next2 = (i + 2 < nb) & (nval_ref[nx2] > 0)
    n_prev = jnp.where(i > 0, nval_ref[prv], 0)
    win = _id_rows(tm) * LANES
    spare = TOP_K * t_all

    def ids_copies(b):
        start = pl.multiple_of(lax.shift_right_logical(off_ref[b], 7) * LANES, LANES)
        ring = pl.ds(pl.multiple_of((b & 3) * win, LANES), win)
        return (pltpu.make_async_copy(tok_hbm.at[pl.ds(start, win)], gids.at[ring], isem.at[b & 3, 0]),
                pltpu.make_async_copy(dst_hbm.at[pl.ds(start, win)], sids.at[ring], isem.at[b & 3, 1]))

    def id_base(b):
        return (b & 3) * win + (off_ref[b] & (LANES - 1))

    def gather_row(base, s, r):
        src = pl.ds(pl.multiple_of(gids[base + r], TOKEN_TILE_ROWS), TOKEN_TILE_ROWS)
        return pltpu.make_async_copy(m_hbm.at[src], xbuf.at[s, pl.ds(TOKEN_TILE_ROWS * r, TOKEN_TILE_ROWS)],
                                     gsem.at[s])

    def scatter_row(base, s, r, n):
        dest = jnp.where(r < n, sids[base + r], spare + r)
        return pltpu.make_async_copy(ybuf.at[s, pl.ds(r, 1)], y_hbm.at[pl.ds(dest, 1)], ssem.at[s])

    def wait_gathers(s):
        pltpu.make_async_copy(m_hbm.at[pl.ds(0, tm * TOKEN_TILE_ROWS)], xbuf.at[s], gsem.at[s]).wait()

    def wait_scatters(s):
        pltpu.make_async_copy(ybuf.at[s], y_hbm.at[pl.ds(0, tm)], ssem.at[s]).wait()

    def for_rows(fn):
        def one(r, c):
            fn(r)
            return c

        lax.fori_loop(0, tm, one, 0)

    @pl.when(nv > 0)
    def _():
        @pl.when(i == 0)
        def _():
            ybuf[...] = jnp.zeros_like(ybuf)
            fill = pltpu.make_async_copy(ybuf.at[0], y_hbm.at[pl.ds(spare, tm)], ssem.at[0])
            fill.start()
            fill.wait()
            for b in range(4):
                for cp in ids_copies(b):
                    cp.start()
                    cp.wait()
            base0 = id_base(0)
            for_rows(lambda r: gather_row(base0, 0, r).start())

        @pl.when(has_next & (i >= 3))
        def _():
            for cp in ids_copies(nxt):
                cp.wait()

        @pl.when(has_next2 & (i >= 2))
        def _():
            for cp in ids_copies(nx2):
                cp.start()

        @pl.when((i == 0) | (be_ref[i] != be_ref[prv]))
        def _():
            for jb in range(2 * D_FF // 256):
                cols = slice(256 * jb, 256 * (jb + 1))
                wup_b[:, cols] = _dot(wup_ref[0, :, cols].astype(BF16), perm_ref[...]).astype(BF16)
            wdn_b[...] = wdn_ref[0].astype(BF16)

        y_cur = lax.rem(i, 3)
        y_prev = lax.rem(i + 2, 3)
        y_prev2 = lax.rem(i + 1, 3)

        def ffn_step(cur, oth):
            wait_gathers(cur)

            @pl.when(i >= 2)
            def _():
                wait_scatters(y_cur)

            g_base = id_base(nxt)
            s_base = id_base(prv)
            for r in range(tm):
                gather_row(g_base, oth, r).start()
                scatter_row(s_base, y_prev, r, n_prev).start()

            x = jnp.concatenate([xbuf[slot, pl.ds(j, tm, stride=TOKEN_TILE_ROWS), :]
                                 for j in range(TOKEN_TILE_ROWS)], axis=1).astype(BF16)
            acts = []
            for jb in range(D_FF // LANES):
                h = _dot(x, wup_b[:, 256 * jb:256 * (jb + 1)]) + bup_ref[0, :, 256 * jb:256 * (jb + 1)]
                gate = jnp.minimum(h[:, :LANES], SWIGLU_LIMIT)
                lin = jnp.clip(h[:, LANES:], -SWIGLU_LIMIT, SWIGLU_LIMIT)
                acts.append((gate * jax.nn.sigmoid(SWIGLU_ALPHA * gate) * (lin + 1.0)).astype(BF16))
            act = jnp.concatenate(acts, axis=1)
            for c in range(D_MODEL // 256):
                ybuf[y_cur, :, 256 * c:256 * (c + 1)] = (
                    _dot(act, wdn_b[:, 256 * c:256 * (c + 1)]) + bdn_ref[0, :, 256 * c:256 * (c + 1)])

            @pl.when(jnp.logical_not(has_next))
            def _():
                wait_gathers(oth)

                @pl.when(i >= 1)
                def _():
                    wait_scatters(y_prev2)
                wait_scatters(y_prev)
                last_base = id_base(i)
                for_rows(lambda r: scatter_row(last_base, y_cur, r, nv).start())
                wait_scatters(y_cur)

        for parity in range(2):
            pl.when(slot == parity)(functools.partial(ffn_step, parity, 1 - parity))


def _moe_call(m, be, nval, off, tok, dst, w_up, b_up_g, w_down, b_down, perm, tm, nb):
    t = m.shape[0] // TOKEN_TILE_ROWS
    assert nb >= 4
    by_expert = lambda i, be, nv, off: (be[i], 0, 0)
    grid_spec = pltpu.PrefetchScalarGridSpec(
        num_scalar_prefetch=3,
        grid=(nb,),
        in_specs=[
            pl.BlockSpec(memory_space=pl.ANY),
            pl.BlockSpec(memory_space=pl.ANY),
            pl.BlockSpec(memory_space=pl.ANY),
            pl.BlockSpec((1, D_MODEL, 2 * D_FF), by_expert),
            pl.BlockSpec((1, 1, 2 * D_FF), by_expert),
            pl.BlockSpec((1, D_FF, D_MODEL), by_expert),
            pl.BlockSpec((1, 1, D_MODEL), by_expert),
            pl.BlockSpec((256, 256), lambda i, be, nv, off: (0, 0)),
        ],
        out_specs=pl.BlockSpec(memory_space=pl.ANY),
        scratch_shapes=[pltpu.SMEM((4 * _id_rows(tm) * LANES,), I32), pltpu.SMEM((4 * _id_rows(tm) * LANES,), I32),
                        pltpu.VMEM((2, tm * TOKEN_TILE_ROWS, LANES), F32), pltpu.VMEM((3, tm, D_MODEL), F32),
                        pltpu.VMEM((D_MODEL, 2 * D_FF), BF16), pltpu.VMEM((D_FF, D_MODEL), BF16),
                        pltpu.SemaphoreType.DMA((4, 2)), pltpu.SemaphoreType.DMA((2,)),
                        pltpu.SemaphoreType.DMA((3,))],
    )
    return pl.pallas_call(
        functools.partial(_moe_body, tm, t, nb),
        out_shape=jax.ShapeDtypeStruct((TOP_K * t + tm, D_MODEL), F32),
        grid_spec=grid_spec,
        compiler_params=_cparams(("arbitrary",)),
        name="moe_experts",
    )(be, nval, off, tok, dst, m, w_up, b_up_g, w_down, b_down, perm)


def _ple_call(h1, y4, gates, pp, ps, g_ple, w_gate, w_proj, g_final, tm):
    tp, ts = pp.shape[0], ps.shape[0]
    n_p, n_s = tp // tm, ts // tm
    ple = pp.shape[1]

    def body(h1_ref, y0_ref, y1_ref, y2_ref, y3_ref, gt_ref, pp_ref, ps_ref, g_ref, wg_ref, wp_ref, gf_ref,
             yp_ref, ys_ref):
        def run(p_ref, o_ref):
            gt = gt_ref[...]
            moe = None
            for k, y_ref in enumerate((y0_ref, y1_ref, y2_ref, y3_ref)):
                moe = gt[:, k:k + 1] * y_ref[...] if moe is None else moe + gt[:, k:k + 1] * y_ref[...]
            h2 = h1_ref[...] + moe
            a = (_rms(h2) * g_ref[...]).astype(BF16)
            gate = jax.nn.sigmoid(_dot(a, wg_ref[...]))
            pe = _dot(p_ref[...].astype(BF16), wp_ref[...])
            h3 = h2 + pe * gate
            o_ref[...] = _rms(h3) * gf_ref[...]

        i = pl.program_id(0)

        @pl.when(i < n_p)
        def _():
            run(pp_ref, yp_ref)

        @pl.when(i >= n_p)
        def _():
            run(ps_ref, ys_ref)

    pmap = lambda i: (jnp.minimum(i, n_p - 1), 0)
    smap = lambda i: (jnp.maximum(i - n_p, 0), 0)
    omap = lambda i: (i, 0)
    return pl.pallas_call(
        body,
        out_shape=[jax.ShapeDtypeStruct((tp, D_MODEL), F32), jax.ShapeDtypeStruct((ts, D_MODEL), F32)],
        grid=(n_p + n_s,),
        in_specs=[pl.BlockSpec((tm, D_MODEL), omap)]
                 + [pl.BlockSpec((tm, D_MODEL), functools.partial(lambda k, i: (k * (n_p + n_s) + i, 0), k))
                    for k in range(TOP_K)]
                 + [pl.BlockSpec((tm, LANES), omap), pl.BlockSpec((tm, ple), pmap), pl.BlockSpec((tm, ple), smap),
                  _const_spec((1, D_MODEL)), _const_spec((D_MODEL, D_MODEL)), _const_spec((ple, D_MODEL)),
                  _const_spec((1, D_MODEL))],
        out_specs=[pl.BlockSpec((tm, D_MODEL), pmap), pl.BlockSpec((tm, D_MODEL), smap)],
        compiler_params=_cparams(("arbitrary",)),
        name="ple_final",
    )(h1, y4, y4, y4, y4, gates, pp, ps, g_ple, w_gate, w_proj, g_final)


def _row(x, width=None):
    x = x.reshape(1, -1).astype(F32)
    if width is not None and x.shape[1] < width:
        x = jnp.pad(x, ((0, 0), (0, width - x.shape[1])))
    return x


def _mixer_params(conv_w, conv_b, dt_bias, a_log, d_skip, ssd_norm_g, v_norm_g, v_norm_b, w_spatial, b_spatial,
                  mlp_out_g, seq_len):
    n_seq = CHUNK // seq_len
    pos = jnp.arange(CHUNK) % seq_len
    same = (jnp.arange(CHUNK)[:, None] // seq_len) == (jnp.arange(CHUNK)[None, :] // seq_len)
    tril = (same & (jnp.arange(CHUNK)[:, None] >= jnp.arange(CHUNK)[None, :])).astype(BF16)
    rexp = (jnp.arange(LANES)[:, None] == (jnp.arange(SSD_WIDTH)[None, :] // HEAD_DIM)).astype(BF16)
    w_loc = jnp.tril(w_spatial[:, :seq_len, :seq_len])
    eye = jnp.eye(n_seq, dtype=F32)
    w_bd = jnp.einsum("st,hij->hsitj", eye, w_loc).reshape(MLP_HEADS, CHUNK, CHUNK)
    wsp = (w_bd.reshape(MLP_HEADS // 2, 2, CHUNK, CHUNK).transpose(0, 2, 1, 3)
           .reshape(MLP_HEADS // 2, CHUNK, 2 * CHUNK).astype(BF16))
    bsp = jnp.repeat(b_spatial[:, :seq_len].T[pos], MLP_WIDTH // MLP_HEADS, axis=1)
    params = (
        conv_w.astype(F32), _row(conv_b), _row(dt_bias, LANES), _row(a_log, LANES),
        _row(jnp.repeat(a_log, HEAD_DIM)), rexp, tril, _row(jnp.repeat(d_skip, HEAD_DIM)),
        _row(ssd_norm_g), _row(v_norm_g), _row(v_norm_b), wsp, bsp.astype(F32), _row(mlp_out_g),
    )
    return params, same.astype(BF16)


def _tile_rows(n):
    return 512 if n % 512 == 0 else CHUNK


def kernel(x_prompt, x_sample, state_ssm, state_conv, p_prompt, p_sample, norm_mix_g, w_in, conv_w, conv_b, dt_bias, a_log, d_skip, ssd_norm_g, v_norm_g, v_norm_b, w_spatial, b_spatial, mlp_out_g, w_out, norm_moe_g, w_router, b_router, w_up, b_up, w_down, b_down, norm_ple_g, w_ple_gate, w_ple_proj, norm_final_g):
    depth = norm_mix_g.shape[0]
    bp, lp, d = x_prompt.shape
    bs, ls, _ = x_sample.shape
    tp, ts = bp * lp, bs * ls
    assert depth == 1 and d == D_MODEL and lp % CHUNK == 0 and ts % CHUNK == 0 and 8 % ls == 0
    tm = _tile_rows(tp) if ts % _tile_rows(tp) == 0 else CHUNK
    t_all = tp + ts
    tm_moe = 256
    nb_moe = -(-t_all * TOP_K // tm_moe) + N_EXPERTS

    hp = x_prompt.reshape(tp, d)
    hs = x_sample.reshape(ts, d)
    ssm_p, conv_p, ssm_s, conv_s, v_s = [], [], [], [], []
    o1 = SSD_WIDTH
    o2 = o1 + CONV_DIM
    o3 = o2 + SSD_HEADS
    o4 = o3 + MLP_WIDTH
    c = jnp.arange(256)
    src = jnp.where(c < LANES, 2 * c, 2 * (c - LANES) + 1)
    perm = (jnp.arange(256)[:, None] == src[None, :]).astype(BF16)

    for i in range(depth):
        wi = w_in[i]
        w_cat = jnp.concatenate(
            [wi[:, :o2], wi[:, o3:], jnp.pad(wi[:, o2:o3], ((0, 0), (0, DT_PAD - SSD_HEADS)))], axis=1).astype(BF16)
        z, xbc, u, v, dtr = _inproj_call(hp, hs, _row(norm_mix_g[i]), w_cat, tm)

        mix_args = (conv_w[i], conv_b[i], dt_bias[i], a_log[i], d_skip[i], ssd_norm_g[i], v_norm_g[i], v_norm_b[i],
                    w_spatial[i], b_spatial[i], mlp_out_g[i])
        prm_p, _ = _mixer_params(*mix_args, seq_len=CHUNK)
        cat_p, s_p = _prompt_mixer_call(z, xbc, u, v, dtr, prm_p, bp, lp // CHUNK)
        ssm_p.append(s_p.reshape(bp, SSD_HEADS, HEAD_DIM, D_STATE).astype(state_ssm.dtype))
        conv_p.append(jnp.stack([xbc[(b + 1) * lp - (CONV_W - 1):(b + 1) * lp] for b in range(bp)]))

        prm_s, seg_ones = _mixer_params(*mix_args, seq_len=ls)
        xbc_s = xbc[tp:].reshape(bs, ls, CONV_DIM)
        xpad = jnp.concatenate([state_conv[i].astype(F32), xbc_s], axis=1)
        x_shift = [xpad[:, CONV_W - 1 - k:CONV_W - 1 - k + ls].reshape(ts, CONV_DIM) for k in range(CONV_W)]
        h0 = state_ssm[i].astype(F32).reshape(bs, SSD_WIDTH, D_STATE)
        cat_s, v_rows, s_s = _sample_mixer_call(z, x_shift, u, v, dtr, h0, prm_s, seg_ones, tp // CHUNK, ls)
        ssm_s.append(s_s.reshape(bs, SSD_HEADS, HEAD_DIM, D_STATE).astype(state_ssm.dtype))
        conv_s.append(xpad[:, ls:])
        v_s.append(v_rows.reshape(bs, ls, MLP_WIDTH))

        wr = jnp.pad(w_router[i].astype(F32), ((0, 0), (0, LANES - N_EXPERTS)))
        wr_hi = wr.astype(BF16)
        wr_lo = (wr - wr_hi.astype(F32)).astype(BF16)
        b_r = jnp.concatenate([b_router[i].astype(F32), jnp.full((LANES - N_EXPERTS,), -1e30, F32)]).reshape(1, LANES)
        h1, m, eid, gates = _out_router_call(cat_p, cat_s, hp, hs, w_out[i].astype(BF16), _row(norm_moe_g[i]),
                                             wr_hi, wr_lo, b_r, tm)

        be, nval, off, tok, dst = _route(eid[:, :TOP_K], tm_moe, nb_moe)
        b_up_g = (b_up[i].astype(F32).reshape(N_EXPERTS, 2 * D_FF // 256, LANES, 2).transpose(0, 1, 3, 2)
                  .reshape(N_EXPERTS, 1, 2 * D_FF))
        y4 = _moe_call(m, be, nval, off, tok, dst, w_up[i], b_up_g, w_down[i],
                       b_down[i].reshape(N_EXPERTS, 1, D_MODEL), perm, tm_moe, nb_moe)

        hp, hs = _ple_call(h1, y4, gates,
                           p_prompt[i].reshape(tp, -1), p_sample[i].reshape(ts, -1), _row(norm_ple_g[i]),
                           w_ple_gate[i].astype(BF16), w_ple_proj[i].astype(BF16), _row(norm_final_g), tm)

    y_prompt = hp.reshape(bp, lp, d)
    y_sample = hs.reshape(bs, ls, d)
    return (y_prompt, y_sample, jnp.stack(ssm_p), jnp.stack(conv_p), jnp.stack(ssm_s), jnp.stack(conv_s),
            jnp.stack(v_s))
```

```python
import functools

import jax
import jax.numpy as jnp
from jax import lax
from jax.experimental import pallas as pl
from jax.experimental.pallas import tpu as pltpu

F32 = jnp.float32
BF16 = jnp.bfloat16
I32 = jnp.int32

EPS = 1e-6
D_MODEL = 1024
SSD_WIDTH = 512
SSD_HEADS = 8
HEAD_DIM = 64
SSD_GROUPS = 2
D_STATE = 128
CONV_W = 4
CONV_DIM = SSD_WIDTH + 2 * SSD_GROUPS * D_STATE
MLP_WIDTH = 512
MLP_HEADS = 8
N_EXPERTS = 32
TOP_K = 4
D_FF = 1024
SWIGLU_LIMIT = 7.0
SWIGLU_ALPHA = 1.702
TOPK_SHIFT = 2
assert 1 << TOPK_SHIFT == TOP_K
LANES = 128
CHUNK = 128
DT_PAD = LANES
TOKEN_TILE_ROWS = D_MODEL // LANES
IN_PAD = SSD_WIDTH + CONV_DIM + 2 * MLP_WIDTH + DT_PAD
VMEM_LIMIT = 56 * 1024 * 1024


def _cparams(sem):
    return pltpu.CompilerParams(dimension_semantics=sem, vmem_limit_bytes=VMEM_LIMIT)


def _const_spec(shape):
    return pl.BlockSpec(shape, lambda *_: (0,) * len(shape))


def _rms(x):
    return x * lax.rsqrt(jnp.mean(x * x, axis=-1, keepdims=True) + EPS)


def _dot(a, b):
    return jnp.dot(a, b, preferred_element_type=F32)


def _dot_nt(a, b):
    return lax.dot_general(a, b, (((1,), (1,)), ((), ())), preferred_element_type=F32)


def _split3(x):
    hi = x.astype(BF16)
    r = x - hi.astype(F32)
    mid = r.astype(BF16)
    lo = (r - mid.astype(F32)).astype(BF16)
    return hi, mid, lo


def _sel_right(x, m01):
    hi, mid, lo = _split3(x)
    return _dot(hi, m01) + _dot(mid, m01) + _dot(lo, m01)


def _sel_left(m01, x):
    hi, mid, lo = _split3(x)
    return _dot(m01, hi) + _dot(m01, mid) + _dot(m01, lo)


def _softplus(x):
    return jnp.maximum(x, 0.0) + jnp.log1p(jnp.exp(-jnp.abs(x)))


def _inproj_call(xp, xs, g, w, tm):
    tp, ts = xp.shape[0], xs.shape[0]
    n_p, n_s = tp // tm, ts // tm
    t_all = tp + ts
    segs = ((0, 512), (512, 1536), (1536, 2048), (2048, 2560), (2560, IN_PAD))

    def body(xp_ref, xs_ref, g_ref, w_ref, *outs):
        def run(x_ref):
            xn = (_rms(x_ref[...]) * g_ref[...]).astype(BF16)
            for (a, b), o in zip(segs, outs):
                o[...] = _dot(xn, w_ref[:, a:b])

        i = pl.program_id(0)

        @pl.when(i < n_p)
        def _():
            run(xp_ref)

        @pl.when(i >= n_p)
        def _():
            run(xs_ref)

    widths = [b - a for a, b in segs]
    return pl.pallas_call(
        body,
        out_shape=[jax.ShapeDtypeStruct((t_all, wd), F32) for wd in widths],
        grid=(n_p + n_s,),
        in_specs=[
            pl.BlockSpec((tm, D_MODEL), lambda i: (jnp.minimum(i, n_p - 1), 0)),
            pl.BlockSpec((tm, D_MODEL), lambda i: (jnp.maximum(i - n_p, 0), 0)),
            _const_spec((1, D_MODEL)),
            _const_spec((D_MODEL, IN_PAD)),
        ],
        out_specs=[pl.BlockSpec((tm, wd), lambda i: (i, 0)) for wd in widths],
        compiler_params=_cparams(("arbitrary",)),
        name="in_proj",
    )(xp, xs, g, w)


def _mixer_front(conv, dtr, dtb, alog, alog_x, rexp, tril, seg_ones):
    xact = conv * jax.nn.sigmoid(conv)
    xs = xact[:, :SSD_WIDTH]
    bm = xact[:, SSD_WIDTH:SSD_WIDTH + 256]
    cm = xact[:, SSD_WIDTH + 256:]
    dt = _softplus(dtr + dtb)
    a = dt * (-jnp.exp(alog))
    dt_x = _sel_right(dt, rexp)
    a_x = dt_x * (-jnp.exp(alog_x))
    acum = _sel_left(tril, a)
    acum_x = _sel_left(tril, a_x)
    if seg_ones is None:
        r = acum_x.shape[0]
        tot_x = jnp.broadcast_to(acum_x[r - 1:r, :], acum_x.shape)
    else:
        tot_x = _sel_left(seg_ones, a_x)
    return xs, bm, cm, dt_x, acum, acum_x, tot_x


def _ssd_intra(cmb, bmb, acum, xdt, mask):
    r = acum.shape[0]
    acum_t = acum.T
    lane = lax.broadcasted_iota(I32, (r, LANES), 1)
    low = lane < HEAD_DIM
    outs = []
    for g in range(SSD_GROUPS):
        sg = _dot_nt(cmb[:, LANES * g:LANES * (g + 1)], bmb[:, LANES * g:LANES * (g + 1)])
        for k in (2 * g, 2 * g + 1):
            parts = []
            for h in (2 * k, 2 * k + 1):
                seg = acum[:, h:h + 1] - acum_t[h:h + 1, :]
                parts.append((sg * jnp.exp(jnp.where(mask, seg, -jnp.inf))).astype(BF16))
            lhs = jnp.concatenate(parts, axis=1)
            xd = xdt[:, LANES * k:LANES * (k + 1)]
            rhs = jnp.concatenate([jnp.where(low, xd, 0.0), jnp.where(low, 0.0, xd)], axis=0).astype(BF16)
            outs.append(_dot(lhs, rhs))
    return jnp.concatenate(outs, axis=1)


def _mixer_back(y, z, u, v, sng, vng, vnb, wsp_ref, bsp, mog):
    r = y.shape[0]
    yg = y * (z * jax.nn.sigmoid(z))
    halves = []
    for g in range(SSD_GROUPS):
        t = yg[:, 256 * g:256 * (g + 1)]
        halves.append(_rms(t))
    yn = jnp.concatenate(halves, axis=1) * sng
    ug = jax.nn.gelu(u)
    vg = jax.nn.gelu(v)
    mu = jnp.mean(vg, axis=-1, keepdims=True)
    var = jnp.mean(jnp.square(vg - mu), axis=-1, keepdims=True)
    v_ln = (vg - mu) * lax.rsqrt(var + EPS) * vng + vnb
    lane = lax.broadcasted_iota(I32, (r, LANES), 1)
    low = lane < HEAD_DIM
    outs = []
    for k in range(MLP_HEADS // 2):
        vd = v_ln[:, LANES * k:LANES * (k + 1)]
        rhs = jnp.concatenate([jnp.where(low, vd, 0.0), jnp.where(low, 0.0, vd)], axis=0).astype(BF16)
        outs.append(_dot(wsp_ref[k], rhs))
    s = jnp.concatenate(outs, axis=1) + bsp
    m = _rms(ug * s) * mog
    return jnp.concatenate([yn, m], axis=1).astype(BF16), v_ln


_MIXER_PARAM_SHAPES = (
    (CONV_W, CONV_DIM), (1, CONV_DIM), (1, LANES), (1, LANES), (1, SSD_WIDTH), (LANES, SSD_WIDTH),
    (CHUNK, CHUNK), (1, SSD_WIDTH), (1, SSD_WIDTH), (1, MLP_WIDTH), (1, MLP_WIDTH),
    (MLP_HEADS // 2, CHUNK, 2 * CHUNK), (CHUNK, MLP_WIDTH), (1, MLP_WIDTH),
)


def _prompt_mixer_body(z_ref, xbc_ref, u_ref, v_ref, dt_ref,
                       cw_ref, cb_ref, dtb_ref, alog_ref, alogx_ref, rexp_ref, tril_ref, dskip_ref,
                       sng_ref, vng_ref, vnb_ref, wsp_ref, bsp_ref, mog_ref,
                       cat_ref, ssm_ref, ext_scr, s_scr):
    c = pl.program_id(1)
    r = CHUNK

    @pl.when(c == 0)
    def _():
        ext_scr[0:8, :] = jnp.zeros((8, CONV_DIM), F32)
        s_scr[...] = jnp.zeros_like(s_scr)

    x = xbc_ref[...]
    ext_scr[8:8 + r, :] = x
    cw = cw_ref[...]
    conv = (cb_ref[...] + cw[3:4] * x + cw[2:3] * ext_scr[7:7 + r, :]
            + cw[1:2] * ext_scr[6:6 + r, :] + cw[0:1] * ext_scr[5:5 + r, :])
    ext_scr[0:8, :] = x[r - 8:r, :]

    xs, bm, cm, dt_x, acum, acum_x, tot_x = _mixer_front(
        conv, dt_ref[...], dtb_ref[...], alog_ref[...], alogx_ref[...], rexp_ref[...], tril_ref[...], None)
    bmb, cmb = bm.astype(BF16), cm.astype(BF16)
    xdt = xs * dt_x
    row = lax.broadcasted_iota(I32, (r, r), 0)
    col = lax.broadcasted_iota(I32, (r, r), 1)
    y_diag = _ssd_intra(cmb, bmb, acum, xdt, row >= col)

    s_prev = s_scr[...]
    s_prev_b = s_prev.astype(BF16)
    y_off = jnp.concatenate(
        [_dot_nt(cmb[:, LANES * g:LANES * (g + 1)], s_prev_b[256 * g:256 * (g + 1), :]) for g in range(SSD_GROUPS)],
        axis=1)
    y = y_diag + y_off * jnp.exp(acum_x) + dskip_ref[...] * xs

    w_t = (xdt * jnp.exp(tot_x - acum_x)).T.astype(BF16)
    states = jnp.concatenate(
        [_dot(w_t[256 * g:256 * (g + 1), :], bmb[:, LANES * g:LANES * (g + 1)]) for g in range(SSD_GROUPS)], axis=0)
    s_new = s_prev * jnp.exp(tot_x).T + states
    s_scr[...] = s_new

    cat, _ = _mixer_back(y, z_ref[...], u_ref[...], v_ref[...], sng_ref[...], vng_ref[...], vnb_ref[...],
                         wsp_ref, bsp_ref[...], mog_ref[...])
    cat_ref[...] = cat

    @pl.when(c == pl.num_programs(1) - 1)
    def _():
        ssm_ref[0] = s_new


def _prompt_mixer_call(z, xbc, u, v, dtr, params, nb, nc):
    row = lambda b, c: (b * nc + c, 0)
    in_specs = [
        pl.BlockSpec((CHUNK, SSD_WIDTH), row), pl.BlockSpec((CHUNK, CONV_DIM), row),
        pl.BlockSpec((CHUNK, MLP_WIDTH), row), pl.BlockSpec((CHUNK, MLP_WIDTH), row),
        pl.BlockSpec((CHUNK, DT_PAD), row),
    ] + [_const_spec(s) for s in _MIXER_PARAM_SHAPES]
    return pl.pallas_call(
        _prompt_mixer_body,
        out_shape=[jax.ShapeDtypeStruct((nb * nc * CHUNK, D_MODEL), BF16),
                   jax.ShapeDtypeStruct((nb, SSD_WIDTH, D_STATE), F32)],
        grid=(nb, nc),
        in_specs=in_specs,
        out_specs=[pl.BlockSpec((CHUNK, D_MODEL), row),
                   pl.BlockSpec((1, SSD_WIDTH, D_STATE), lambda b, c: (b, 0, 0))],
        scratch_shapes=[pltpu.VMEM((CHUNK + 8, CONV_DIM), F32), pltpu.VMEM((SSD_WIDTH, D_STATE), F32)],
        compiler_params=_cparams(("arbitrary", "arbitrary")),
        name="prompt_mixer",
    )(z, xbc, u, v, dtr, *params)


def _sample_mixer_body(seq_len, z_ref, x0_ref, x1_ref, x2_ref, x3_ref, u_ref, v_ref, dt_ref, h_ref,
                       cw_ref, cb_ref, dtb_ref, alog_ref, alogx_ref, rexp_ref, tril_ref, dskip_ref,
                       sng_ref, vng_ref, vnb_ref, wsp_ref, bsp_ref, mog_ref, segones_ref,
                       cat_ref, vout_ref, hout_ref, cm_scr, bm_scr, wt_scr, dtt_scr, yoff_scr):
    r = CHUNK
    shift = seq_len.bit_length() - 1
    cw = cw_ref[...]
    conv = (cb_ref[...] + cw[3:4] * x0_ref[...] + cw[2:3] * x1_ref[...]
            + cw[1:2] * x2_ref[...] + cw[0:1] * x3_ref[...])
    xs, bm, cm, dt_x, acum, acum_x, tot_x = _mixer_front(
        conv, dt_ref[...], dtb_ref[...], alog_ref[...], alogx_ref[...], rexp_ref[...], tril_ref[...],
        segones_ref[...])
    bmb, cmb = bm.astype(BF16), cm.astype(BF16)
    xdt = xs * dt_x
    row = lax.broadcasted_iota(I32, (r, r), 0)
    col = lax.broadcasted_iota(I32, (r, r), 1)
    same = lax.shift_right_logical(row, shift) == lax.shift_right_logical(col, shift)
    y_diag = _ssd_intra(cmb, bmb, acum, xdt, same & (row >= col))

    cm_scr[...] = cm
    bm_scr[...] = bmb
    wt_scr[...] = (xdt * jnp.exp(tot_x - acum_x)).T
    dtt_scr[...] = jnp.exp(tot_x).T
    ones_b = jnp.ones((LANES, LANES), BF16)
    seqs_per_slab = 8 // seq_len

    def slab(j, carry):
        rows = pl.ds(pl.multiple_of(8 * j, 8), 8)
        cms = cm_scr[rows, :].astype(BF16)
        sub = lax.broadcasted_iota(I32, (8, 256), 0)
        lane = lax.broadcasted_iota(I32, (256, LANES), 1)
        for g in range(SSD_GROUPS):
            q_rows = slice(256 * g, 256 * (g + 1))
            acc = jnp.zeros((8, 256), F32)
            for q in range(seqs_per_slab):
                s = seqs_per_slab * j + q
                y_s = _dot_nt(cms[:, LANES * g:LANES * (g + 1)], h_ref[s, q_rows, :].astype(BF16))
                acc = jnp.where(lax.shift_right_logical(sub, shift) == q, y_s, acc)
            yoff_scr[rows, 256 * g:256 * (g + 1)] = acc
            for q in range(seqs_per_slab):
                s = seqs_per_slab * j + q
                w_sel = jnp.where(lax.shift_right_logical(lane, shift) == s, wt_scr[q_rows, :], 0.0).astype(BF16)
                st = _dot(w_sel, bm_scr[:, LANES * g:LANES * (g + 1)])
                d_sel = jnp.where(lane == s * seq_len, dtt_scr[q_rows, :], 0.0)
                hout_ref[s, q_rows, :] = h_ref[s, q_rows, :] * _sel_right(d_sel, ones_b) + st
        return carry

    lax.fori_loop(0, r // 8, slab, 0)

    y = y_diag + yoff_scr[...] * jnp.exp(acum_x) + dskip_ref[...] * xs
    cat, v_ln = _mixer_back(y, z_ref[...], u_ref[...], v_ref[...], sng_ref[...], vng_ref[...], vnb_ref[...],
                            wsp_ref, bsp_ref[...], mog_ref[...])
    cat_ref[...] = cat
    vout_ref[...] = v_ln


def _sample_mixer_call(z, x_shift, u, v, dtr, h0, params, seg_ones, row0, seq_len):
    ts = x_shift[0].shape[0]
    n = ts // CHUNK
    spt = CHUNK // seq_len
    off = lambda i: (row0 + i, 0)
    loc = lambda i: (i, 0)
    st3 = lambda i: (i, 0, 0)
    in_specs = (
        [pl.BlockSpec((CHUNK, SSD_WIDTH), off)]
        + [pl.BlockSpec((CHUNK, CONV_DIM), loc)] * 4
        + [pl.BlockSpec((CHUNK, MLP_WIDTH), off), pl.BlockSpec((CHUNK, MLP_WIDTH), off),
           pl.BlockSpec((CHUNK, DT_PAD), off), pl.BlockSpec((spt, SSD_WIDTH, D_STATE), st3)]
        + [_const_spec(s) for s in _MIXER_PARAM_SHAPES] + [_const_spec((CHUNK, CHUNK))])
    return pl.pallas_call(
        functools.partial(_sample_mixer_body, seq_len),
        out_shape=[jax.ShapeDtypeStruct((ts, D_MODEL), BF16), jax.ShapeDtypeStruct((ts, MLP_WIDTH), F32),
                   jax.ShapeDtypeStruct(h0.shape, F32)],
        grid=(n,),
        in_specs=in_specs,
        out_specs=[pl.BlockSpec((CHUNK, D_MODEL), loc), pl.BlockSpec((CHUNK, MLP_WIDTH), loc),
                   pl.BlockSpec((spt, SSD_WIDTH, D_STATE), st3)],
        scratch_shapes=[pltpu.VMEM((CHUNK, 256), F32), pltpu.VMEM((CHUNK, 256), BF16),
                        pltpu.VMEM((SSD_WIDTH, CHUNK), F32), pltpu.VMEM((SSD_WIDTH, CHUNK), F32),
                        pltpu.VMEM((CHUNK, SSD_WIDTH), F32)],
        compiler_params=_cparams(("arbitrary",)),
        name="sample_mixer",
    )(z, *x_shift, u, v, dtr, h0, *params, seg_ones)


def _out_router_call(cat_p, cat_s, xp, xs, w_out, g_moe, wr_hi, wr_lo, b_r, tm):
    tp, ts = xp.shape[0], xs.shape[0]
    n_p, n_s = tp // tm, ts // tm
    t_all = tp + ts

    def body(cp_ref, cs_ref, xp_ref, xs_ref, wo_ref, g_ref, wh_ref, wl_ref, br_ref,
             h1_ref, m_ref, eid_ref, gate_ref):
        def run(c_ref, x_ref):
            h1 = x_ref[...] + _dot(c_ref[...], wo_ref[...])
            h1_ref[...] = h1
            m = _rms(h1) * g_ref[...]
            for j in range(TOKEN_TILE_ROWS):
                m_ref[pl.ds(j, tm, stride=TOKEN_TILE_ROWS), :] = m[:, LANES * j:LANES * (j + 1)]
            m_hi = m.astype(BF16)
            m_lo = (m - m_hi.astype(F32)).astype(BF16)
            logits = _dot(m_hi, wh_ref[...]) + _dot(m_lo, wh_ref[...]) + _dot(m_hi, wl_ref[...]) + br_ref[...]
            lane = lax.broadcasted_iota(I32, logits.shape, 1).astype(F32)
            work = logits
            vals, ids = [], []
            for _ in range(TOP_K):
                mx = jnp.max(work, axis=-1, keepdims=True)
                idx = jnp.min(jnp.where(work == mx, lane, float(LANES)), axis=-1, keepdims=True)
                vals.append(mx)
                ids.append(idx)
                work = jnp.where(lane == idx, -jnp.inf, work)
            ex = [jnp.exp(vv - vals[0]) for vv in vals]
            den = ex[0] + ex[1] + ex[2] + ex[3]
            eid = jnp.zeros(logits.shape, I32)
            gate = jnp.zeros(logits.shape, F32)
            for k in range(TOP_K):
                eid = jnp.where(lane == k, ids[k].astype(I32), eid)
                gate = jnp.where(lane == k, ex[k] / den, gate)
            eid_ref[...] = eid
            gate_ref[...] = gate

        i = pl.program_id(0)

        @pl.when(i < n_p)
        def _():
            run(cp_ref, xp_ref)

        @pl.when(i >= n_p)
        def _():
            run(cs_ref, xs_ref)

    pmap = lambda i: (jnp.minimum(i, n_p - 1), 0)
    smap = lambda i: (jnp.maximum(i - n_p, 0), 0)
    omap = lambda i: (i, 0)
    return pl.pallas_call(
        body,
        out_shape=[jax.ShapeDtypeStruct((t_all, D_MODEL), F32),
                   jax.ShapeDtypeStruct((t_all * TOKEN_TILE_ROWS, LANES), F32),
                   jax.ShapeDtypeStruct((t_all, LANES), I32), jax.ShapeDtypeStruct((t_all, LANES), F32)],
        grid=(n_p + n_s,),
        in_specs=[pl.BlockSpec((tm, D_MODEL), pmap), pl.BlockSpec((tm, D_MODEL), smap),
                  pl.BlockSpec((tm, D_MODEL), pmap), pl.BlockSpec((tm, D_MODEL), smap),
                  _const_spec((D_MODEL, D_MODEL)), _const_spec((1, D_MODEL)),
                  _const_spec((D_MODEL, LANES)), _const_spec((D_MODEL, LANES)), _const_spec((1, LANES))],
        out_specs=[pl.BlockSpec((tm, D_MODEL), omap), pl.BlockSpec((tm * TOKEN_TILE_ROWS, LANES), omap),
                   pl.BlockSpec((tm, LANES), omap), pl.BlockSpec((tm, LANES), omap)],
        compiler_params=_cparams(("arbitrary",)),
        name="out_router",
    )(cat_p, cat_s, xp, xs, w_out, g_moe, wr_hi, wr_lo, b_r)


def _route(eid, tm, nb):
    t = eid.shape[0]
    tk = t * TOP_K
    flat = eid.reshape(tk)
    _, order = lax.sort((flat, jnp.arange(tk, dtype=I32)), num_keys=1, is_stable=True)
    counts = jnp.sum((flat[:, None] == jnp.arange(N_EXPERTS, dtype=I32)[None, :]).astype(I32), axis=0)
    nblk = (counts + tm - 1) // tm
    bend = jnp.cumsum(nblk)
    bstart = bend - nblk
    start = jnp.cumsum(counts) - counts
    nused = bend[-1]
    blk = jnp.arange(nb, dtype=I32)
    used = blk < nused
    be = jnp.minimum(jnp.sum((jnp.minimum(blk, nused - 1)[:, None] >= bend[None, :]).astype(I32), axis=1),
                     N_EXPERTS - 1)
    sel = (be[:, None] == jnp.arange(N_EXPERTS, dtype=I32)[None, :]).astype(I32)
    pick = lambda v: jnp.sum(sel * v[None, :], axis=1)
    done = (blk - pick(bstart)) * tm
    nval = jnp.where(used, jnp.clip(pick(counts) - done, 0, tm), 0).astype(I32)
    off = jnp.where(used, pick(start) + done, 0).astype(I32)
    pad = (-(-(tk + tm) // LANES) + _id_rows(tm)) * LANES - tk
    tok = jnp.pad(lax.shift_right_logical(order, TOPK_SHIFT) * TOKEN_TILE_ROWS, (0, pad))
    dst = jnp.pad((order & (TOP_K - 1)) * t + lax.shift_right_logical(order, TOPK_SHIFT), (0, pad))
    return be, nval, off, tok, dst


def _id_rows(tm):
    return tm // LANES + 1


def _moe_body(tm, t_all, nb, be_ref, nval_ref, off_ref, tok_hbm, dst_hbm, m_hbm, wup_ref, bup_ref, wdn_ref, bdn_ref,
              perm_ref, y_hbm, gids, sids, xbuf, ybuf, wup_b, wdn_b, isem, gsem, ssem):
    i = pl.program_id(0)
    nv = nval_ref[i]
    slot = i & 1
    prv = jnp.maximum(i - 1, 0)
    nxt = jnp.minimum(i + 1, nb - 1)
    nx2 = jnp.minimum(i + 2, nb - 1)
    has_next = (i + 1 < nb) & (nval_ref[nxt] > 0)
    has_next2 = (i + 2 < nb) & (nval_ref[nx2] > 0)
    n_prev = jnp.where(i > 0, nval_ref[prv], 0)
    win = _id_rows(tm) * LANES
    spare = TOP_K * t_all

    def ids_copies(b):
        start = pl.multiple_of(lax.shift_right_logical(off_ref[b], 7) * LANES, LANES)
        ring = pl.ds(pl.multiple_of((b & 3) * win, LANES), win)
        return (pltpu.make_async_copy(tok_hbm.at[pl.ds(start, win)], gids.at[ring], isem.at[b & 3, 0]),
                pltpu.make_async_copy(dst_hbm.at[pl.ds(start, win)], sids.at[ring], isem.at[b & 3, 1]))

    def id_base(b):
        return (b & 3) * win + (off_ref[b] & (LANES - 1))

    def gather_row(base, s, r):
        src = pl.ds(pl.multiple_of(gids[base + r], TOKEN_TILE_ROWS), TOKEN_TILE_ROWS)
        return pltpu.make_async_copy(m_hbm.at[src], xbuf.at[s, pl.ds(TOKEN_TILE_ROWS * r, TOKEN_TILE_ROWS)],
                                     gsem.at[s])

    def scatter_row(base, s, r, n):
        dest = jnp.where(r < n, sids[base + r], spare + r)
        return pltpu.make_async_copy(ybuf.at[s, pl.ds(r, 1)], y_hbm.at[pl.ds(dest, 1)], ssem.at[s])

    def wait_gathers(s):
        pltpu.make_async_copy(m_hbm.at[pl.ds(0, tm * TOKEN_TILE_ROWS)], xbuf.at[s], gsem.at[s]).wait()

    def wait_scatters(s):
        pltpu.make_async_copy(ybuf.at[s], y_hbm.at[pl.ds(0, tm)], ssem.at[s]).wait()

    def for_rows(fn):
        def one(r, c):
            fn(r)
            return c

        lax.fori_loop(0, tm, one, 0)

    @pl.when(nv > 0)
    def _():
        @pl.when(i == 0)
        def _():
            ybuf[...] = jnp.zeros_like(ybuf)
            fill = pltpu.make_async_copy(ybuf.at[0], y_hbm.at[pl.ds(spare, tm)], ssem.at[0])
            fill.start()
            fill.wait()
            for b in range(4):
                for cp in ids_copies(b):
                    cp.start()
                    cp.wait()
            base0 = id_base(0)
            for_rows(lambda r: gather_row(base0, 0, r).start())

        @pl.when(has_next & (i >= 3))
        def _():
            for cp in ids_copies(nxt):
                cp.wait()

        @pl.when(has_next2 & (i >= 2))
        def _():
            for cp in ids_copies(nx2):
                cp.start()

        @pl.when((i == 0) | (be_ref[i] != be_ref[prv]))
        def _():
            for jb in range(2 * D_FF // 256):
                cols = slice(256 * jb, 256 * (jb + 1))
                wup_b[:, cols] = _dot(wup_ref[0, :, cols].astype(BF16), perm_ref[...]).astype(BF16)
            wdn_b[...] = wdn_ref[0].astype(BF16)

        y_cur = lax.rem(i, 3)
        y_prev = lax.rem(i + 2, 3)
        y_prev2 = lax.rem(i + 1, 3)

        def ffn_step(cur, oth):
            wait_gathers(cur)

            @pl.when(i >= 2)
            def _():
                wait_scatters(y_cur)

            g_base = id_base(nxt)
            s_base = id_base(prv)
            for r in range(tm):
                gather_row(g_base, oth, r).start()
                scatter_row(s_base, y_prev, r, n_prev).start()

            x = jnp.concatenate([xbuf[slot, pl.ds(j, tm, stride=TOKEN_TILE_ROWS), :]
                                 for j in range(TOKEN_TILE_ROWS)], axis=1).astype(BF16)
            acts = []
            for jb in range(D_FF // LANES):
                h = _dot(x, wup_b[:, 256 * jb:256 * (jb + 1)]) + bup_ref[0, :, 256 * jb:256 * (jb + 1)]
                gate = jnp.minimum(h[:, :LANES], SWIGLU_LIMIT)
                lin = jnp.clip(h[:, LANES:], -SWIGLU_LIMIT, SWIGLU_LIMIT)
                acts.append((gate * jax.nn.sigmoid(SWIGLU_ALPHA * gate) * (lin + 1.0)).astype(BF16))
            act = jnp.concatenate(acts, axis=1)
            for c in range(D_MODEL // 256):
                ybuf[y_cur, :, 256 * c:256 * (c + 1)] = (
                    _dot(act, wdn_b[:, 256 * c:256 * (c + 1)]) + bdn_ref[0, :, 256 * c:256 * (c + 1)])

            @pl.when(jnp.logical_not(has_next))
            def _():
                wait_gathers(oth)

                @pl.when(i >= 1)
                def _():
                    wait_scatters(y_prev2)
                wait_scatters(y_prev)
                last_base = id_base(i)
                for_rows(lambda r: scatter_row(last_base, y_cur, r, nv).start())
                wait_scatters(y_cur)

        for parity in range(2):
            pl.when(slot == parity)(functools.partial(ffn_step, parity, 1 - parity))


def _moe_call(m, be, nval, off, tok, dst, w_up, b_up_g, w_down, b_down, perm, tm, nb):
    t = m.shape[0] // TOKEN_TILE_ROWS
    assert nb >= 4
    by_expert = lambda i, be, nv, off: (be[i], 0, 0)
    grid_spec = pltpu.PrefetchScalarGridSpec(
        num_scalar_prefetch=3,
        grid=(nb,),
        in_specs=[
            pl.BlockSpec(memory_space=pl.ANY),
            pl.BlockSpec(memory_space=pl.ANY),
            pl.BlockSpec(memory_space=pl.ANY),
            pl.BlockSpec((1, D_MODEL, 2 * D_FF), by_expert),
            pl.BlockSpec((1, 1, 2 * D_FF), by_expert),
            pl.BlockSpec((1, D_FF, D_MODEL), by_expert),
            pl.BlockSpec((1, 1, D_MODEL), by_expert),
            pl.BlockSpec((256, 256), lambda i, be, nv, off: (0, 0)),
        ],
        out_specs=pl.BlockSpec(memory_space=pl.ANY),
        scratch_shapes=[pltpu.SMEM((4 * _id_rows(tm) * LANES,), I32), pltpu.SMEM((4 * _id_rows(tm) * LANES,), I32),
                        pltpu.VMEM((2, tm * TOKEN_TILE_ROWS, LANES), F32), pltpu.VMEM((3, tm, D_MODEL), F32),
                        pltpu.VMEM((D_MODEL, 2 * D_FF), BF16), pltpu.VMEM((D_FF, D_MODEL), BF16),
                        pltpu.SemaphoreType.DMA((4, 2)), pltpu.SemaphoreType.DMA((2,)),
                        pltpu.SemaphoreType.DMA((3,))],
    )
    return pl.pallas_call(
        functools.partial(_moe_body, tm, t, nb),
        out_shape=jax.ShapeDtypeStruct((TOP_K * t + tm, D_MODEL), F32),
        grid_spec=grid_spec,
        compiler_params=_cparams(("arbitrary",)),
        name="moe_experts",
    )(be, nval, off, tok, dst, m, w_up, b_up_g, w_down, b_down, perm)


def _ple_call(h1, y4, gates, pp, ps, g_ple, w_gate, w_proj, g_final, tm):
    tp, ts = pp.shape[0], ps.shape[0]
    n_p, n_s = tp // tm, ts // tm
    ple = pp.shape[1]

    def body(h1_ref, y0_ref, y1_ref, y2_ref, y3_ref, gt_ref, pp_ref, ps_ref, g_ref, wg_ref, wp_ref, gf_ref,
             yp_ref, ys_ref):
        def run(p_ref, o_ref):
            gt = gt_ref[...]
            moe = None
            for k, y_ref in enumerate((y0_ref, y1_ref, y2_ref, y3_ref)):
                moe = gt[:, k:k + 1] * y_ref[...] if moe is None else moe + gt[:, k:k + 1] * y_ref[...]
            h2 = h1_ref[...] + moe
            a = (_rms(h2) * g_ref[...]).astype(BF16)
            gate = jax.nn.sigmoid(_dot(a, wg_ref[...]))
            pe = _dot(p_ref[...].astype(BF16), wp_ref[...])
            h3 = h2 + pe * gate
            o_ref[...] = _rms(h3) * gf_ref[...]

        i = pl.program_id(0)

        @pl.when(i < n_p)
        def _():
            run(pp_ref, yp_ref)

        @pl.when(i >= n_p)
        def _():
            run(ps_ref, ys_ref)

    pmap = lambda i: (jnp.minimum(i, n_p - 1), 0)
    smap = lambda i: (jnp.maximum(i - n_p, 0), 0)
    omap = lambda i: (i, 0)
    return pl.pallas_call(
        body,
        out_shape=[jax.ShapeDtypeStruct((tp, D_MODEL), F32), jax.ShapeDtypeStruct((ts, D_MODEL), F32)],
        grid=(n_p + n_s,),
        in_specs=[pl.BlockSpec((tm, D_MODEL), omap)]
                 + [pl.BlockSpec((tm, D_MODEL), functools.partial(lambda k, i: (k * (n_p + n_s) + i, 0), k))
                    for k in range(TOP_K)]
                 + [pl.BlockSpec((tm, LANES), omap), pl.BlockSpec((tm, ple), pmap), pl.BlockSpec((tm, ple), smap),
                  _const_spec((1, D_MODEL)), _const_spec((D_MODEL, D_MODEL)), _const_spec((ple, D_MODEL)),
                  _const_spec((1, D_MODEL))],
        out_specs=[pl.BlockSpec((tm, D_MODEL), pmap), pl.BlockSpec((tm, D_MODEL), smap)],
        compiler_params=_cparams(("arbitrary",)),
        name="ple_final",
    )(h1, y4, y4, y4, y4, gates, pp, ps, g_ple, w_gate, w_proj, g_final)


def _row(x, width=None):
    x = x.reshape(1, -1).astype(F32)
    if width is not None and x.shape[1] < width:
        x = jnp.pad(x, ((0, 0), (0, width - x.shape[1])))
    return x


def _mixer_params(conv_w, conv_b, dt_bias, a_log, d_skip, ssd_norm_g, v_norm_g, v_norm_b, w_spatial, b_spatial,
                  mlp_out_g, seq_len):
    pos = jnp.arange(CHUNK) % seq_len
    same = (jnp.arange(CHUNK)[:, None] // seq_len) == (jnp.arange(CHUNK)[None, :] // seq_len)
    tril = (same & (jnp.arange(CHUNK)[:, None] >= jnp.arange(CHUNK)[None, :])).astype(BF16)
    rexp = (jnp.arange(LANES)[:, None] == (jnp.arange(SSD_WIDTH)[None, :] // HEAD_DIM)).astype(BF16)
    w_loc = jnp.tril(w_spatial[:, :seq_len, :seq_len])
    onehot = (pos[:, None] == jnp.arange(seq_len)[None, :]).astype(F32)
    tiled = jnp.einsum("iq,hqr,jr->hij", onehot, w_loc.astype(F32), onehot, precision=lax.Precision.HIGHEST)
    w_bd = jnp.where(same[None], tiled, 0.0)
    wsp = (w_bd.reshape(MLP_HEADS // 2, 2, CHUNK, CHUNK).transpose(0, 2, 1, 3)
           .reshape(MLP_HEADS // 2, CHUNK, 2 * CHUNK).astype(BF16))
    bsp = jnp.repeat(b_spatial[:, :seq_len].T[pos], MLP_WIDTH // MLP_HEADS, axis=1)
    params = (
        conv_w.astype(F32), _row(conv_b), _row(dt_bias, LANES), _row(a_log, LANES),
        _row(jnp.repeat(a_log, HEAD_DIM)), rexp, tril, _row(jnp.repeat(d_skip, HEAD_DIM)),
        _row(ssd_norm_g), _row(v_norm_g), _row(v_norm_b), wsp, bsp.astype(F32), _row(mlp_out_g),
    )
    return params, same.astype(BF16)


def _tile_rows(n):
    return 512 if n % 512 == 0 else CHUNK


def kernel(x_prompt, x_sample, state_ssm, state_conv, p_prompt, p_sample, norm_mix_g, w_in, conv_w, conv_b, dt_bias, a_log, d_skip, ssd_norm_g, v_norm_g, v_norm_b, w_spatial, b_spatial, mlp_out_g, w_out, norm_moe_g, w_router, b_router, w_up, b_up, w_down, b_down, norm_ple_g, w_ple_gate, w_ple_proj, norm_final_g):
    depth = norm_mix_g.shape[0]
    bp, lp, d = x_prompt.shape
    bs, ls, _ = x_sample.shape
    tp, ts = bp * lp, bs * ls
    assert depth == 1 and d == D_MODEL and lp % CHUNK == 0 and ts % CHUNK == 0 and 8 % ls == 0
    tm = _tile_rows(tp) if ts % _tile_rows(tp) == 0 else CHUNK
    t_all = tp + ts
    tm_moe = 256
    nb_moe = -(-t_all * TOP_K // tm_moe) + N_EXPERTS

    hp = x_prompt.reshape(tp, d)
    hs = x_sample.reshape(ts, d)
    ssm_p, conv_p, ssm_s, conv_s, v_s = [], [], [], [], []
    o1 = SSD_WIDTH
    o2 = o1 + CONV_DIM
    o3 = o2 + SSD_HEADS
    o4 = o3 + MLP_WIDTH
    c = jnp.arange(256)
    src = jnp.where(c < LANES, 2 * c, 2 * (c - LANES) + 1)
    perm = (jnp.arange(256)[:, None] == src[None, :]).astype(BF16)

    for i in range(depth):
        wi = w_in[i]
        w_cat = jnp.concatenate(
            [wi[:, :o2], wi[:, o3:], jnp.pad(wi[:, o2:o3], ((0, 0), (0, DT_PAD - SSD_HEADS)))], axis=1).astype(BF16)
        z, xbc, u, v, dtr = _inproj_call(hp, hs, _row(norm_mix_g[i]), w_cat, tm)

        mix_args = (conv_w[i], conv_b[i], dt_bias[i], a_log[i], d_skip[i], ssd_norm_g[i], v_norm_g[i], v_norm_b[i],
                    w_spatial[i], b_spatial[i], mlp_out_g[i])
        prm_p, _ = _mixer_params(*mix_args, seq_len=CHUNK)
        cat_p, s_p = _prompt_mixer_call(z, xbc, u, v, dtr, prm_p, bp, lp // CHUNK)
        ssm_p.append(s_p.reshape(bp, SSD_HEADS, HEAD_DIM, D_STATE).astype(state_ssm.dtype))
        conv_p.append(jnp.stack([xbc[(b + 1) * lp - (CONV_W - 1):(b + 1) * lp] for b in range(bp)]))

        prm_s, seg_ones = _mixer_params(*mix_args, seq_len=ls)
        xbc_s = xbc[tp:].reshape(bs, ls, CONV_DIM)
        xpad = jnp.concatenate([state_conv[i].astype(F32), xbc_s], axis=1)
        x_shift = [xpad[:, CONV_W - 1 - k:CONV_W - 1 - k + ls].reshape(ts, CONV_DIM) for k in range(CONV_W)]
        h0 = state_ssm[i].astype(F32).reshape(bs, SSD_WIDTH, D_STATE)
        cat_s, v_rows, s_s = _sample_mixer_call(z, x_shift, u, v, dtr, h0, prm_s, seg_ones, tp // CHUNK, ls)
        ssm_s.append(s_s.reshape(bs, SSD_HEADS, HEAD_DIM, D_STATE).astype(state_ssm.dtype))
        conv_s.append(xpad[:, ls:])
        v_s.append(v_rows.reshape(bs, ls, MLP_WIDTH))

        wr = jnp.pad(w_router[i].astype(F32), ((0, 0), (0, LANES - N_EXPERTS)))
        wr_hi = wr.astype(BF16)
        wr_lo = (wr - wr_hi.astype(F32)).astype(BF16)
        b_r = jnp.concatenate([b_router[i].astype(F32), jnp.full((LANES - N_EXPERTS,), -1e30, F32)]).reshape(1, LANES)
        h1, m, eid, gates = _out_router_call(cat_p, cat_s, hp, hs, w_out[i].astype(BF16), _row(norm_moe_g[i]),
                                             wr_hi, wr_lo, b_r, tm)

        be, nval, off, tok, dst = _route(eid[:, :TOP_K], tm_moe, nb_moe)
        b_up_g = (b_up[i].astype(F32).reshape(N_EXPERTS, 2 * D_FF // 256, LANES, 2).transpose(0, 1, 3, 2)
                  .reshape(N_EXPERTS, 1, 2 * D_FF))
        y4 = _moe_call(m, be, nval, off, tok, dst, w_up[i], b_up_g, w_down[i],
                       b_down[i].reshape(N_EXPERTS, 1, D_MODEL), perm, tm_moe, nb_moe)

        hp, hs = _ple_call(h1, y4, gates,
                           p_prompt[i].reshape(tp, -1), p_sample[i].reshape(ts, -1), _row(norm_ple_g[i]),
                           w_ple_gate[i].astype(BF16), w_ple_proj[i].astype(BF16), _row(norm_final_g), tm)

    y_prompt = hp.reshape(bp, lp, d)
    y_sample = hs.reshape(bs, ls, d)
    return (y_prompt, y_sample, jnp.stack(ssm_p), jnp.stack(conv_p), jnp.stack(ssm_s), jnp.stack(conv_s),
            jnp.stack(v_s))
```

```python
import functools

import jax
import jax.numpy as jnp
from jax import lax
from jax.experimental import pallas as pl
from jax.experimental.pallas import tpu as pltpu

F32 = jnp.float32
BF16 = jnp.bfloat16
I32 = jnp.int32

EPS = 1e-6
D_MODEL = 1024
SSD_WIDTH = 512
SSD_HEADS = 8
HEAD_DIM = 64
SSD_GROUPS = 2
D_STATE = 128
CONV_W = 4
CONV_DIM = SSD_WIDTH + 2 * SSD_GROUPS * D_STATE
MLP_WIDTH = 512
MLP_HEADS = 8
N_EXPERTS = 32
TOP_K = 4
D_FF = 1024
SWIGLU_LIMIT = 7.0
SWIGLU_ALPHA = 1.702
TOPK_SHIFT = 2
assert 1 << TOPK_SHIFT == TOP_K
LANES = 128
CHUNK = 128
DT_PAD = LANES
TOKEN_TILE_ROWS = D_MODEL // LANES
IN_PAD = SSD_WIDTH + CONV_DIM + 2 * MLP_WIDTH + DT_PAD
VMEM_LIMIT = 56 * 1024 * 1024


def _cparams(sem):
    return pltpu.CompilerParams(dimension_semantics=sem, vmem_limit_bytes=VMEM_LIMIT)


def _const_spec(shape):
    return pl.BlockSpec(shape, lambda *_: (0,) * len(shape))


def _rms(x):
    return x * lax.rsqrt(jnp.mean(x * x, axis=-1, keepdims=True) + EPS)


def _dot(a, b):
    return jnp.dot(a, b, preferred_element_type=F32)


def _dot_nt(a, b):
    return lax.dot_general(a, b, (((1,), (1,)), ((), ())), preferred_element_type=F32)


def _split3(x):
    hi = x.astype(BF16)
    r = x - hi.astype(F32)
    mid = r.astype(BF16)
    lo = (r - mid.astype(F32)).astype(BF16)
    return hi, mid, lo


def _sel_right(x, m01):
    hi, mid, lo = _split3(x)
    return _dot(hi, m01) + _dot(mid, m01) + _dot(lo, m01)


def _sel_left(m01, x):
    hi, mid, lo = _split3(x)
    return _dot(m01, hi) + _dot(m01, mid) + _dot(m01, lo)


def _softplus(x):
    return jnp.maximum(x, 0.0) + jnp.log1p(jnp.exp(-jnp.abs(x)))


def _inproj_call(xp, xs, g, w, tm):
    tp, ts = xp.shape[0], xs.shape[0]
    n_p, n_s = tp // tm, ts // tm
    t_all = tp + ts
    segs = ((0, 512), (512, 1536), (1536, 2048), (2048, 2560), (2560, IN_PAD))

    def body(xp_ref, xs_ref, g_ref, w_ref, *outs):
        def run(x_ref):
            xn = (_rms(x_ref[...]) * g_ref[...]).astype(BF16)
            for (a, b), o in zip(segs, outs):
                o[...] = _dot(xn, w_ref[:, a:b])

        i = pl.program_id(0)

        @pl.when(i < n_p)
        def _():
            run(xp_ref)

        @pl.when(i >= n_p)
        def _():
            run(xs_ref)

    widths = [b - a for a, b in segs]
    return pl.pallas_call(
        body,
        out_shape=[jax.ShapeDtypeStruct((t_all, wd), F32) for wd in widths],
        grid=(n_p + n_s,),
        in_specs=[
            pl.BlockSpec((tm, D_MODEL), lambda i: (jnp.minimum(i, n_p - 1), 0)),
            pl.BlockSpec((tm, D_MODEL), lambda i: (jnp.maximum(i - n_p, 0), 0)),
            _const_spec((1, D_MODEL)),
            _const_spec((D_MODEL, IN_PAD)),
        ],
        out_specs=[pl.BlockSpec((tm, wd), lambda i: (i, 0)) for wd in widths],
        compiler_params=_cparams(("arbitrary",)),
        name="in_proj",
    )(xp, xs, g, w)


def _mixer_front(conv, dtr, dtb, alog, alog_x, rexp, tril, seg_ones):
    xact = conv * jax.nn.sigmoid(conv)
    xs = xact[:, :SSD_WIDTH]
    bm = xact[:, SSD_WIDTH:SSD_WIDTH + 256]
    cm = xact[:, SSD_WIDTH + 256:]
    dt = _softplus(dtr + dtb)
    a = dt * (-jnp.exp(alog))
    dt_x = _sel_right(dt, rexp)
    a_x = dt_x * (-jnp.exp(alog_x))
    acum = _sel_left(tril, a)
    acum_x = _sel_left(tril, a_x)
    if seg_ones is None:
        r = acum_x.shape[0]
        tot_x = jnp.broadcast_to(acum_x[r - 1:r, :], acum_x.shape)
    else:
        tot_x = _sel_left(seg_ones, a_x)
    return xs, bm, cm, dt_x, acum, acum_x, tot_x


def _ssd_intra(cmb, bmb, acum, xdt, mask):
    r = acum.shape[0]
    acum_t = acum.T
    lane = lax.broadcasted_iota(I32, (r, LANES), 1)
    low = lane < HEAD_DIM
    outs = []
    for g in range(SSD_GROUPS):
        sg = _dot_nt(cmb[:, LANES * g:LANES * (g + 1)], bmb[:, LANES * g:LANES * (g + 1)])
        for k in (2 * g, 2 * g + 1):
            parts = []
            for h in (2 * k, 2 * k + 1):
                seg = acum[:, h:h + 1] - acum_t[h:h + 1, :]
                parts.append((sg * jnp.exp(jnp.where(mask, seg, -jnp.inf))).astype(BF16))
            lhs = jnp.concatenate(parts, axis=1)
            xd = xdt[:, LANES * k:LANES * (k + 1)]
            rhs = jnp.concatenate([jnp.where(low, xd, 0.0), jnp.where(low, 0.0, xd)], axis=0).astype(BF16)
            outs.append(_dot(lhs, rhs))
    return jnp.concatenate(outs, axis=1)


def _mixer_back(y, z, u, v, sng, vng, vnb, wsp_ref, bsp, mog):
    r = y.shape[0]
    yg = y * (z * jax.nn.sigmoid(z))
    halves = []
    for g in range(SSD_GROUPS):
        t = yg[:, 256 * g:256 * (g + 1)]
        halves.append(_rms(t))
    yn = jnp.concatenate(halves, axis=1) * sng
    ug = jax.nn.gelu(u)
    vg = jax.nn.gelu(v)
    mu = jnp.mean(vg, axis=-1, keepdims=True)
    var = jnp.mean(jnp.square(vg - mu), axis=-1, keepdims=True)
    v_ln = (vg - mu) * lax.rsqrt(var + EPS) * vng + vnb
    lane = lax.broadcasted_iota(I32, (r, LANES), 1)
    low = lane < HEAD_DIM
    outs = []
    for k in range(MLP_HEADS // 2):
        vd = v_ln[:, LANES * k:LANES * (k + 1)]
        rhs = jnp.concatenate([jnp.where(low, vd, 0.0), jnp.where(low, 0.0, vd)], axis=0).astype(BF16)
        outs.append(_dot(wsp_ref[k], rhs))
    s = jnp.concatenate(outs, axis=1) + bsp
    m = _rms(ug * s) * mog
    return jnp.concatenate([yn, m], axis=1).astype(BF16), v_ln


_MIXER_PARAM_SHAPES = (
    (CONV_W, CONV_DIM), (1, CONV_DIM), (1, LANES), (1, LANES), (1, SSD_WIDTH), (LANES, SSD_WIDTH),
    (CHUNK, CHUNK), (1, SSD_WIDTH), (1, SSD_WIDTH), (1, MLP_WIDTH), (1, MLP_WIDTH),
    (MLP_HEADS // 2, CHUNK, 2 * CHUNK), (CHUNK, MLP_WIDTH), (1, MLP_WIDTH),
)


def _prompt_mixer_body(z_ref, xbc_ref, u_ref, v_ref, dt_ref,
                       cw_ref, cb_ref, dtb_ref, alog_ref, alogx_ref, rexp_ref, tril_ref, dskip_ref,
                       sng_ref, vng_ref, vnb_ref, wsp_ref, bsp_ref, mog_ref,
                       cat_ref, ssm_ref, ext_scr, s_scr):
    c = pl.program_id(1)
    r = CHUNK

    @pl.when(c == 0)
    def _():
        ext_scr[0:8, :] = jnp.zeros((8, CONV_DIM), F32)
        s_scr[...] = jnp.zeros_like(s_scr)

    x = xbc_ref[...]
    ext_scr[8:8 + r, :] = x
    cw = cw_ref[...]
    conv = (cb_ref[...] + cw[3:4] * x + cw[2:3] * ext_scr[7:7 + r, :]
            + cw[1:2] * ext_scr[6:6 + r, :] + cw[0:1] * ext_scr[5:5 + r, :])
    ext_scr[0:8, :] = x[r - 8:r, :]

    xs, bm, cm, dt_x, acum, acum_x, tot_x = _mixer_front(
        conv, dt_ref[...], dtb_ref[...], alog_ref[...], alogx_ref[...], rexp_ref[...], tril_ref[...], None)
    bmb, cmb = bm.astype(BF16), cm.astype(BF16)
    xdt = xs * dt_x
    row = lax.broadcasted_iota(I32, (r, r), 0)
    col = lax.broadcasted_iota(I32, (r, r), 1)
    y_diag = _ssd_intra(cmb, bmb, acum, xdt, row >= col)

    s_prev = s_scr[...]
    s_prev_b = s_prev.astype(BF16)
    y_off = jnp.concatenate(
        [_dot_nt(cmb[:, LANES * g:LANES * (g + 1)], s_prev_b[256 * g:256 * (g + 1), :]) for g in range(SSD_GROUPS)],
        axis=1)
    y = y_diag + y_off * jnp.exp(acum_x) + dskip_ref[...] * xs

    w_t = (xdt * jnp.exp(tot_x - acum_x)).T.astype(BF16)
    states = jnp.concatenate(
        [_dot(w_t[256 * g:256 * (g + 1), :], bmb[:, LANES * g:LANES * (g + 1)]) for g in range(SSD_GROUPS)], axis=0)
    s_new = s_prev * jnp.exp(tot_x).T + states
    s_scr[...] = s_new

    cat, _ = _mixer_back(y, z_ref[...], u_ref[...], v_ref[...], sng_ref[...], vng_ref[...], vnb_ref[...],
                         wsp_ref, bsp_ref[...], mog_ref[...])
    cat_ref[...] = cat

    @pl.when(c == pl.num_programs(1) - 1)
    def _():
        ssm_ref[0] = s_new


def _prompt_mixer_call(z, xbc, u, v, dtr, params, nb, nc):
    row = lambda b, c: (b * nc + c, 0)
    in_specs = [
        pl.BlockSpec((CHUNK, SSD_WIDTH), row), pl.BlockSpec((CHUNK, CONV_DIM), row),
        pl.BlockSpec((CHUNK, MLP_WIDTH), row), pl.BlockSpec((CHUNK, MLP_WIDTH), row),
        pl.BlockSpec((CHUNK, DT_PAD), row),
    ] + [_const_spec(s) for s in _MIXER_PARAM_SHAPES]
    return pl.pallas_call(
        _prompt_mixer_body,
        out_shape=[jax.ShapeDtypeStruct((nb * nc * CHUNK, D_MODEL), BF16),
                   jax.ShapeDtypeStruct((nb, SSD_WIDTH, D_STATE), F32)],
        grid=(nb, nc),
        in_specs=in_specs,
        out_specs=[pl.BlockSpec((CHUNK, D_MODEL), row),
                   pl.BlockSpec((1, SSD_WIDTH, D_STATE), lambda b, c: (b, 0, 0))],
        scratch_shapes=[pltpu.VMEM((CHUNK + 8, CONV_DIM), F32), pltpu.VMEM((SSD_WIDTH, D_STATE), F32)],
        compiler_params=_cparams(("arbitrary", "arbitrary")),
        name="prompt_mixer",
    )(z, xbc, u, v, dtr, *params)


def _sample_mixer_body(seq_len, z_ref, x0_ref, x1_ref, x2_ref, x3_ref, u_ref, v_ref, dt_ref, h_ref,
                       cw_ref, cb_ref, dtb_ref, alog_ref, alogx_ref, rexp_ref, tril_ref, dskip_ref,
                       sng_ref, vng_ref, vnb_ref, wsp_ref, bsp_ref, mog_ref, segones_ref,
                       cat_ref, vout_ref, hout_ref, cm_scr, bm_scr, wt_scr, dtt_scr, yoff_scr):
    r = CHUNK
    shift = seq_len.bit_length() - 1
    cw = cw_ref[...]
    conv = (cb_ref[...] + cw[3:4] * x0_ref[...] + cw[2:3] * x1_ref[...]
            + cw[1:2] * x2_ref[...] + cw[0:1] * x3_ref[...])
    xs, bm, cm, dt_x, acum, acum_x, tot_x = _mixer_front(
        conv, dt_ref[...], dtb_ref[...], alog_ref[...], alogx_ref[...], rexp_ref[...], tril_ref[...],
        segones_ref[...])
    bmb, cmb = bm.astype(BF16), cm.astype(BF16)
    xdt = xs * dt_x
    row = lax.broadcasted_iota(I32, (r, r), 0)
    col = lax.broadcasted_iota(I32, (r, r), 1)
    same = lax.shift_right_logical(row, shift) == lax.shift_right_logical(col, shift)
    y_diag = _ssd_intra(cmb, bmb, acum, xdt, same & (row >= col))

    cm_scr[...] = cm
    bm_scr[...] = bmb
    wt_scr[...] = (xdt * jnp.exp(tot_x - acum_x)).T
    dtt_scr[...] = jnp.exp(tot_x).T
    ones_b = jnp.ones((LANES, LANES), BF16)
    seqs_per_slab = 8 // seq_len

    def slab(j, carry):
        rows = pl.ds(pl.multiple_of(8 * j, 8), 8)
        cms = cm_scr[rows, :].astype(BF16)
        sub = lax.broadcasted_iota(I32, (8, 256), 0)
        lane = lax.broadcasted_iota(I32, (256, LANES), 1)
        for g in range(SSD_GROUPS):
            q_rows = slice(256 * g, 256 * (g + 1))
            acc = jnp.zeros((8, 256), F32)
            for q in range(seqs_per_slab):
                s = seqs_per_slab * j + q
                y_s = _dot_nt(cms[:, LANES * g:LANES * (g + 1)], h_ref[s, q_rows, :].astype(BF16))
                acc = jnp.where(lax.shift_right_logical(sub, shift) == q, y_s, acc)
            yoff_scr[rows, 256 * g:256 * (g + 1)] = acc
            for q in range(seqs_per_slab):
                s = seqs_per_slab * j + q
                w_sel = jnp.where(lax.shift_right_logical(lane, shift) == s, wt_scr[q_rows, :], 0.0).astype(BF16)
                st = _dot(w_sel, bm_scr[:, LANES * g:LANES * (g + 1)])
                d_sel = jnp.where(lane == s * seq_len, dtt_scr[q_rows, :], 0.0)
                hout_ref[s, q_rows, :] = h_ref[s, q_rows, :] * _sel_right(d_sel, ones_b) + st
        return carry

    lax.fori_loop(0, r // 8, slab, 0)

    y = y_diag + yoff_scr[...] * jnp.exp(acum_x) + dskip_ref[...] * xs
    cat, v_ln = _mixer_back(y, z_ref[...], u_ref[...], v_ref[...], sng_ref[...], vng_ref[...], vnb_ref[...],
                            wsp_ref, bsp_ref[...], mog_ref[...])
    cat_ref[...] = cat
    vout_ref[...] = v_ln


def _sample_mixer_call(z, x_shift, u, v, dtr, h0, params, seg_ones, row0, seq_len):
    ts = x_shift[0].shape[0]
    n = ts // CHUNK
    spt = CHUNK // seq_len
    off = lambda i: (row0 + i, 0)
    loc = lambda i: (i, 0)
    st3 = lambda i: (i, 0, 0)
    in_specs = (
        [pl.BlockSpec((CHUNK, SSD_WIDTH), off)]
        + [pl.BlockSpec((CHUNK, CONV_DIM), loc)] * 4
        + [pl.BlockSpec((CHUNK, MLP_WIDTH), off), pl.BlockSpec((CHUNK, MLP_WIDTH), off),
           pl.BlockSpec((CHUNK, DT_PAD), off), pl.BlockSpec((spt, SSD_WIDTH, D_STATE), st3)]
        + [_const_spec(s) for s in _MIXER_PARAM_SHAPES] + [_const_spec((CHUNK, CHUNK))])
    return pl.pallas_call(
        functools.partial(_sample_mixer_body, seq_len),
        out_shape=[jax.ShapeDtypeStruct((ts, D_MODEL), BF16), jax.ShapeDtypeStruct((ts, MLP_WIDTH), F32),
                   jax.ShapeDtypeStruct(h0.shape, F32)],
        grid=(n,),
        in_specs=in_specs,
        out_specs=[pl.BlockSpec((CHUNK, D_MODEL), loc), pl.BlockSpec((CHUNK, MLP_WIDTH), loc),
                   pl.BlockSpec((spt, SSD_WIDTH, D_STATE), st3)],
        scratch_shapes=[pltpu.VMEM((CHUNK, 256), F32), pltpu.VMEM((CHUNK, 256), BF16),
                        pltpu.VMEM((SSD_WIDTH, CHUNK), F32), pltpu.VMEM((SSD_WIDTH, CHUNK), F32),
                        pltpu.VMEM((CHUNK, SSD_WIDTH), F32)],
        compiler_params=_cparams(("arbitrary",)),
        name="sample_mixer",
    )(z, *x_shift, u, v, dtr, h0, *params, seg_ones)


def _out_router_call(cat_p, cat_s, xp, xs, w_out, g_moe, wr_hi, wr_lo, b_r, tm):
    tp, ts = xp.shape[0], xs.shape[0]
    n_p, n_s = tp // tm, ts // tm
    t_all = tp + ts

    def body(cp_ref, cs_ref, xp_ref, xs_ref, wo_ref, g_ref, wh_ref, wl_ref, br_ref,
             h1_ref, m_ref, eid_ref, gate_ref):
        def run(c_ref, x_ref):
            h1 = x_ref[...] + _dot(c_ref[...], wo_ref[...])
            h1_ref[...] = h1
            m = _rms(h1) * g_ref[...]
            for j in range(TOKEN_TILE_ROWS):
                m_ref[pl.ds(j, tm, stride=TOKEN_TILE_ROWS), :] = m[:, LANES * j:LANES * (j + 1)]
            m_hi = m.astype(BF16)
            m_lo = (m - m_hi.astype(F32)).astype(BF16)
            logits = _dot(m_hi, wh_ref[...]) + _dot(m_lo, wh_ref[...]) + _dot(m_hi, wl_ref[...]) + br_ref[...]
            lane = lax.broadcasted_iota(I32, logits.shape, 1).astype(F32)
            work = logits
            vals, ids = [], []
            for _ in range(TOP_K):
                mx = jnp.max(work, axis=-1, keepdims=True)
                idx = jnp.min(jnp.where(work == mx, lane, float(LANES)), axis=-1, keepdims=True)
                vals.append(mx)
                ids.append(idx)
                work = jnp.where(lane == idx, -jnp.inf, work)
            ex = [jnp.exp(vv - vals[0]) for vv in vals]
            den = ex[0] + ex[1] + ex[2] + ex[3]
            eid = jnp.zeros(logits.shape, I32)
            gate = jnp.zeros(logits.shape, F32)
            for k in range(TOP_K):
                eid = jnp.where(lane == k, ids[k].astype(I32), eid)
                gate = jnp.where(lane == k, ex[k] / den, gate)
            eid_ref[...] = eid
            gate_ref[...] = gate

        i = pl.program_id(0)

        @pl.when(i < n_p)
        def _():
            run(cp_ref, xp_ref)

        @pl.when(i >= n_p)
        def _():
            run(cs_ref, xs_ref)

    pmap = lambda i: (jnp.minimum(i, n_p - 1), 0)
    smap = lambda i: (jnp.maximum(i - n_p, 0), 0)
    omap = lambda i: (i, 0)
    return pl.pallas_call(
        body,
        out_shape=[jax.ShapeDtypeStruct((t_all, D_MODEL), F32),
                   jax.ShapeDtypeStruct((t_all * TOKEN_TILE_ROWS, LANES), F32),
                   jax.ShapeDtypeStruct((t_all, LANES), I32), jax.ShapeDtypeStruct((t_all, LANES), F32)],
        grid=(n_p + n_s,),
        in_specs=[pl.BlockSpec((tm, D_MODEL), pmap), pl.BlockSpec((tm, D_MODEL), smap),
                  pl.BlockSpec((tm, D_MODEL), pmap), pl.BlockSpec((tm, D_MODEL), smap),
                  _const_spec((D_MODEL, D_MODEL)), _const_spec((1, D_MODEL)),
                  _const_spec((D_MODEL, LANES)), _const_spec((D_MODEL, LANES)), _const_spec((1, LANES))],
        out_specs=[pl.BlockSpec((tm, D_MODEL), omap), pl.BlockSpec((tm * TOKEN_TILE_ROWS, LANES), omap),
                   pl.BlockSpec((tm, LANES), omap), pl.BlockSpec((tm, LANES), omap)],
        compiler_params=_cparams(("arbitrary",)),
        name="out_router",
    )(cat_p, cat_s, xp, xs, w_out, g_moe, wr_hi, wr_lo, b_r)


def _route(eid, tm, nb):
    t = eid.shape[0]
    tk = t * TOP_K
    flat = eid.reshape(tk)
    id_bits = max(tk - 1, 1).bit_length()
    assert id_bits + (N_EXPERTS - 1).bit_length() <= 31
    order = lax.sort(lax.shift_left(flat, id_bits) | jnp.arange(tk, dtype=I32)) & ((1 << id_bits) - 1)
    counts = jnp.sum((flat[:, None] == jnp.arange(N_EXPERTS, dtype=I32)[None, :]).astype(I32), axis=0)
    nblk = (counts + tm - 1) // tm
    bend = jnp.cumsum(nblk)
    bstart = bend - nblk
    start = jnp.cumsum(counts) - counts
    nused = bend[-1]
    blk = jnp.arange(nb, dtype=I32)
    used = blk < nused
    be = jnp.minimum(jnp.sum((jnp.minimum(blk, nused - 1)[:, None] >= bend[None, :]).astype(I32), axis=1),
                     N_EXPERTS - 1)
    sel = (be[:, None] == jnp.arange(N_EXPERTS, dtype=I32)[None, :]).astype(I32)
    pick = lambda v: jnp.sum(sel * v[None, :], axis=1)
    done = (blk - pick(bstart)) * tm
    nval = jnp.where(used, jnp.clip(pick(counts) - done, 0, tm), 0).astype(I32)
    off = jnp.where(used, pick(start) + done, 0).astype(I32)
    pad = (-(-(tk + tm) // LANES) + _id_rows(tm)) * LANES - tk
    tok = jnp.pad(lax.shift_right_logical(order, TOPK_SHIFT) * TOKEN_TILE_ROWS, (0, pad))
    dst = jnp.pad((order & (TOP_K - 1)) * t + lax.shift_right_logical(order, TOPK_SHIFT), (0, pad))
    return be, nval, off, tok, dst


def _id_rows(tm):
    return tm // LANES + 1


def _moe_body(tm, t_all, nb, be_ref, nval_ref, off_ref, tok_hbm, dst_hbm, m_hbm, wup_ref, bup_ref, wdn_ref, bdn_ref,
              perm_ref, y_hbm, gids, sids, xbuf, ybuf, wup_b, wdn_b, isem, gsem, ssem):
    i = pl.program_id(0)
    nv = nval_ref[i]
    slot = i & 1
    prv = jnp.maximum(i - 1, 0)
    nxt = jnp.minimum(i + 1, nb - 1)
    nx2 = jnp.minimum(i + 2, nb - 1)
    has_next = (i + 1 < nb) & (nval_ref[nxt] > 0)
    has_next2 = (i + 2 < nb) & (nval_ref[nx2] > 0)
    n_prev = jnp.where(i > 0, nval_ref[prv], 0)
    win = _id_rows(tm) * LANES
    spare = TOP_K * t_all

    def ids_copies(b):
        start = pl.multiple_of(lax.shift_right_logical(off_ref[b], 7) * LANES, LANES)
        ring = pl.ds(pl.multiple_of((b & 3) * win, LANES), win)
        return (pltpu.make_async_copy(tok_hbm.at[pl.ds(start, win)], gids.at[ring], isem.at[b & 3, 0]),
                pltpu.make_async_copy(dst_hbm.at[pl.ds(start, win)], sids.at[ring], isem.at[b & 3, 1]))

    def id_base(b):
        return (b & 3) * win + (off_ref[b] & (LANES - 1))

    def gather_row(base, s, r):
        src = pl.ds(pl.multiple_of(gids[base + r], TOKEN_TILE_ROWS), TOKEN_TILE_ROWS)
        return pltpu.make_async_copy(m_hbm.at[src], xbuf.at[s, pl.ds(TOKEN_TILE_ROWS * r, TOKEN_TILE_ROWS)],
                                     gsem.at[s])

    def scatter_row(base, s, r, n):
        dest = jnp.where(r < n, sids[base + r], spare + r)
        return pltpu.make_async_copy(ybuf.at[s, pl.ds(r, 1)], y_hbm.at[pl.ds(dest, 1)], ssem.at[s])

    def wait_gathers(s):
        pltpu.make_async_copy(m_hbm.at[pl.ds(0, tm * TOKEN_TILE_ROWS)], xbuf.at[s], gsem.at[s]).wait()

    def wait_scatters(s):
        pltpu.make_async_copy(ybuf.at[s], y_hbm.at[pl.ds(0, tm)], ssem.at[s]).wait()

    def for_rows(fn):
        def one(r, c):
            fn(r)
            return c

        lax.fori_loop(0, tm, one, 0)

    @pl.when(nv > 0)
    def _():
        @pl.when(i == 0)
        def _():
            ybuf[...] = jnp.zeros_like(ybuf)
            fill = pltpu.make_async_copy(ybuf.at[0], y_hbm.at[pl.ds(spare, tm)], ssem.at[0])
            fill.start()
            fill.wait()
            for b in range(4):
                for cp in ids_copies(b):
                    cp.start()
                    cp.wait()
            base0 = id_base(0)
            for_rows(lambda r: gather_row(base0, 0, r).start())

        @pl.when(has_next & (i >= 3))
        def _():
            for cp in ids_copies(nxt):
                cp.wait()

        @pl.when(has_next2 & (i >= 2))
        def _():
            for cp in ids_copies(nx2):
                cp.start()

        @pl.when((i == 0) | (be_ref[i] != be_ref[prv]))
        def _():
            for jb in range(2 * D_FF // 256):
                cols = slice(256 * jb, 256 * (jb + 1))
                wup_b[:, cols] = _dot(wup_ref[0, :, cols].astype(BF16), perm_ref[...]).astype(BF16)
            wdn_b[...] = wdn_ref[0].astype(BF16)

        y_cur = lax.rem(i, 3)
        y_prev = lax.rem(i + 2, 3)
        y_prev2 = lax.rem(i + 1, 3)

        def ffn_step(cur, oth):
            wait_gathers(cur)

            @pl.when(i >= 2)
            def _():
                wait_scatters(y_cur)

            g_base = id_base(nxt)
            s_base = id_base(prv)
            for r in range(tm):
                gather_row(g_base, oth, r).start()
                scatter_row(s_base, y_prev, r, n_prev).start()

            x = jnp.concatenate([xbuf[slot, pl.ds(j, tm, stride=TOKEN_TILE_ROWS), :]
                                 for j in range(TOKEN_TILE_ROWS)], axis=1).astype(BF16)
            acts = []
            for jb in range(D_FF // LANES):
                h = _dot(x, wup_b[:, 256 * jb:256 * (jb + 1)]) + bup_ref[0, :, 256 * jb:256 * (jb + 1)]
                gate = jnp.minimum(h[:, :LANES], SWIGLU_LIMIT)
                lin = jnp.clip(h[:, LANES:], -SWIGLU_LIMIT, SWIGLU_LIMIT)
                acts.append((gate * jax.nn.sigmoid(SWIGLU_ALPHA * gate) * (lin + 1.0)).astype(BF16))
            act = jnp.concatenate(acts, axis=1)
            for c in range(D_MODEL // 256):
                ybuf[y_cur, :, 256 * c:256 * (c + 1)] = (
                    _dot(act, wdn_b[:, 256 * c:256 * (c + 1)]) + bdn_ref[0, :, 256 * c:256 * (c + 1)])

            @pl.when(jnp.logical_not(has_next))
            def _():
                wait_gathers(oth)

                @pl.when(i >= 1)
                def _():
                    wait_scatters(y_prev2)
                wait_scatters(y_prev)
                last_base = id_base(i)
                for_rows(lambda r: scatter_row(last_base, y_cur, r, nv).start())
                wait_scatters(y_cur)

        for parity in range(2):
            pl.when(slot == parity)(functools.partial(ffn_step, parity, 1 - parity))


def _moe_call(m, be, nval, off, tok, dst, w_up, b_up_g, w_down, b_down, perm, tm, nb):
    t = m.shape[0] // TOKEN_TILE_ROWS
    assert nb >= 4
    by_expert = lambda i, be, nv, off: (be[i], 0, 0)
    grid_spec = pltpu.PrefetchScalarGridSpec(
        num_scalar_prefetch=3,
        grid=(nb,),
        in_specs=[
            pl.BlockSpec(memory_space=pl.ANY),
            pl.BlockSpec(memory_space=pl.ANY),
            pl.BlockSpec(memory_space=pl.ANY),
            pl.BlockSpec((1, D_MODEL, 2 * D_FF), by_expert),
            pl.BlockSpec((1, 1, 2 * D_FF), by_expert),
            pl.BlockSpec((1, D_FF, D_MODEL), by_expert),
            pl.BlockSpec((1, 1, D_MODEL), by_expert),
            pl.BlockSpec((256, 256), lambda i, be, nv, off: (0, 0)),
        ],
        out_specs=pl.BlockSpec(memory_space=pl.ANY),
        scratch_shapes=[pltpu.SMEM((4 * _id_rows(tm) * LANES,), I32), pltpu.SMEM((4 * _id_rows(tm) * LANES,), I32),
                        pltpu.VMEM((2, tm * TOKEN_TILE_ROWS, LANES), F32), pltpu.VMEM((3, tm, D_MODEL), F32),
                        pltpu.VMEM((D_MODEL, 2 * D_FF), BF16), pltpu.VMEM((D_FF, D_MODEL), BF16),
                        pltpu.SemaphoreType.DMA((4, 2)), pltpu.SemaphoreType.DMA((2,)),
                        pltpu.SemaphoreType.DMA((3,))],
    )
    return pl.pallas_call(
        functools.partial(_moe_body, tm, t, nb),
        out_shape=jax.ShapeDtypeStruct((TOP_K * t + tm, D_MODEL), F32),
        grid_spec=grid_spec,
        compiler_params=_cparams(("arbitrary",)),
        name="moe_experts",
    )(be, nval, off, tok, dst, m, w_up, b_up_g, w_down, b_down, perm)


def _ple_call(h1, y4, gates, pp, ps, g_ple, w_gate, w_proj, g_final, tm):
    tp, ts = pp.shape[0], ps.shape[0]
    n_p, n_s = tp // tm, ts // tm
    ple = pp.shape[1]

    def body(h1_ref, y0_ref, y1_ref, y2_ref, y3_ref, gt_ref, pp_ref, ps_ref, g_ref, wg_ref, wp_ref, gf_ref,
             yp_ref, ys_ref):
        def run(p_ref, o_ref):
            gt = gt_ref[...]
            moe = None
            for k, y_ref in enumerate((y0_ref, y1_ref, y2_ref, y3_ref)):
                moe = gt[:, k:k + 1] * y_ref[...] if moe is None else moe + gt[:, k:k + 1] * y_ref[...]
            h2 = h1_ref[...] + moe
            a = (_rms(h2) * g_ref[...]).astype(BF16)
            gate = jax.nn.sigmoid(_dot(a, wg_ref[...]))
            pe = _dot(p_ref[...].astype(BF16), wp_ref[...])
            h3 = h2 + pe * gate
            o_ref[...] = _rms(h3) * gf_ref[...]

        i = pl.program_id(0)

        @pl.when(i < n_p)
        def _():
            run(pp_ref, yp_ref)

        @pl.when(i >= n_p)
        def _():
            run(ps_ref, ys_ref)

    pmap = lambda i: (jnp.minimum(i, n_p - 1), 0)
    smap = lambda i: (jnp.maximum(i - n_p, 0), 0)
    omap = lambda i: (i, 0)
    return pl.pallas_call(
        body,
        out_shape=[jax.ShapeDtypeStruct((tp, D_MODEL), F32), jax.ShapeDtypeStruct((ts, D_MODEL), F32)],
        grid=(n_p + n_s,),
        in_specs=[pl.BlockSpec((tm, D_MODEL), omap)]
                 + [pl.BlockSpec((tm, D_MODEL), functools.partial(lambda k, i: (k * (n_p + n_s) + i, 0), k))
                    for k in range(TOP_K)]
                 + [pl.BlockSpec((tm, LANES), omap), pl.BlockSpec((tm, ple), pmap), pl.BlockSpec((tm, ple), smap),
                  _const_spec((1, D_MODEL)), _const_spec((D_MODEL, D_MODEL)), _const_spec((ple, D_MODEL)),
                  _const_spec((1, D_MODEL))],
        out_specs=[pl.BlockSpec((tm, D_MODEL), pmap), pl.BlockSpec((tm, D_MODEL), smap)],
        compiler_params=_cparams(("arbitrary",)),
        name="ple_final",
    )(h1, y4, y4, y4, y4, gates, pp, ps, g_ple, w_gate, w_proj, g_final)


def _row(x, width=None):
    x = x.reshape(1, -1).astype(F32)
    if width is not None and x.shape[1] < width:
        x = jnp.pad(x, ((0, 0), (0, width - x.shape[1])))
    return x


def _mixer_params(conv_w, conv_b, dt_bias, a_log, d_skip, ssd_norm_g, v_norm_g, v_norm_b, w_spatial, b_spatial,
                  mlp_out_g, seq_len):
    pos = jnp.arange(CHUNK) % seq_len
    same = (jnp.arange(CHUNK)[:, None] // seq_len) == (jnp.arange(CHUNK)[None, :] // seq_len)
    tril = (same & (jnp.arange(CHUNK)[:, None] >= jnp.arange(CHUNK)[None, :])).astype(BF16)
    rexp = (jnp.arange(LANES)[:, None] == (jnp.arange(SSD_WIDTH)[None, :] // HEAD_DIM)).astype(BF16)
    w_loc = jnp.tril(w_spatial[:, :seq_len, :seq_len])
    onehot = (pos[:, None] == jnp.arange(seq_len)[None, :]).astype(F32)
    tiled = jnp.einsum("iq,hqr,jr->hij", onehot, w_loc.astype(F32), onehot, precision=lax.Precision.HIGHEST)
    w_bd = jnp.where(same[None], tiled, 0.0)
    wsp = (w_bd.reshape(MLP_HEADS // 2, 2, CHUNK, CHUNK).transpose(0, 2, 1, 3)
           .reshape(MLP_HEADS // 2, CHUNK, 2 * CHUNK).astype(BF16))
    bsp = jnp.repeat(b_spatial[:, :seq_len].T[pos], MLP_WIDTH // MLP_HEADS, axis=1)
    params = (
        conv_w.astype(F32), _row(conv_b), _row(dt_bias, LANES), _row(a_log, LANES),
        _row(jnp.repeat(a_log, HEAD_DIM)), rexp, tril, _row(jnp.repeat(d_skip, HEAD_DIM)),
        _row(ssd_norm_g), _row(v_norm_g), _row(v_norm_b), wsp, bsp.astype(F32), _row(mlp_out_g),
    )
    return params, same.astype(BF16)


def _tile_rows(n):
    return 512 if n % 512 == 0 else CHUNK


def kernel(x_prompt, x_sample, state_ssm, state_conv, p_prompt, p_sample, norm_mix_g, w_in, conv_w, conv_b, dt_bias, a_log, d_skip, ssd_norm_g, v_norm_g, v_norm_b, w_spatial, b_spatial, mlp_out_g, w_out, norm_moe_g, w_router, b_router, w_up, b_up, w_down, b_down, norm_ple_g, w_ple_gate, w_ple_proj, norm_final_g):
    depth = norm_mix_g.shape[0]
    bp, lp, d = x_prompt.shape
    bs, ls, _ = x_sample.shape
    tp, ts = bp * lp, bs * ls
    assert depth == 1 and d == D_MODEL and lp % CHUNK == 0 and ts % CHUNK == 0 and 8 % ls == 0
    tm = _tile_rows(tp) if ts % _tile_rows(tp) == 0 else CHUNK
    t_all = tp + ts
    tm_moe = 256
    nb_moe = -(-t_all * TOP_K // tm_moe) + N_EXPERTS

    hp = x_prompt.reshape(tp, d)
    hs = x_sample.reshape(ts, d)
    ssm_p, conv_p, ssm_s, conv_s, v_s = [], [], [], [], []
    o1 = SSD_WIDTH
    o2 = o1 + CONV_DIM
    o3 = o2 + SSD_HEADS
    o4 = o3 + MLP_WIDTH
    c = jnp.arange(256)
    src = jnp.where(c < LANES, 2 * c, 2 * (c - LANES) + 1)
    perm = (jnp.arange(256)[:, None] == src[None, :]).astype(BF16)

    for i in range(depth):
        wi = w_in[i]
        w_cat = jnp.concatenate(
            [wi[:, :o2], wi[:, o3:], jnp.pad(wi[:, o2:o3], ((0, 0), (0, DT_PAD - SSD_HEADS)))], axis=1).astype(BF16)
        z, xbc, u, v, dtr = _inproj_call(hp, hs, _row(norm_mix_g[i]), w_cat, tm)

        mix_args = (conv_w[i], conv_b[i], dt_bias[i], a_log[i], d_skip[i], ssd_norm_g[i], v_norm_g[i], v_norm_b[i],
                    w_spatial[i], b_spatial[i], mlp_out_g[i])
        prm_p, _ = _mixer_params(*mix_args, seq_len=CHUNK)
        cat_p, s_p = _prompt_mixer_call(z, xbc, u, v, dtr, prm_p, bp, lp // CHUNK)
        ssm_p.append(s_p.reshape(bp, SSD_HEADS, HEAD_DIM, D_STATE).astype(state_ssm.dtype))
        conv_p.append(jnp.stack([xbc[(b + 1) * lp - (CONV_W - 1):(b + 1) * lp] for b in range(bp)]))

        prm_s, seg_ones = _mixer_params(*mix_args, seq_len=ls)
        xbc_s = xbc[tp:].reshape(bs, ls, CONV_DIM)
        xpad = jnp.concatenate([state_conv[i].astype(F32), xbc_s], axis=1)
        x_shift = [xpad[:, CONV_W - 1 - k:CONV_W - 1 - k + ls].reshape(ts, CONV_DIM) for k in range(CONV_W)]
        h0 = state_ssm[i].astype(F32).reshape(bs, SSD_WIDTH, D_STATE)
        cat_s, v_rows, s_s = _sample_mixer_call(z, x_shift, u, v, dtr, h0, prm_s, seg_ones, tp // CHUNK, ls)
        ssm_s.append(s_s.reshape(bs, SSD_HEADS, HEAD_DIM, D_STATE).astype(state_ssm.dtype))
        conv_s.append(xpad[:, ls:])
        v_s.append(v_rows.reshape(bs, ls, MLP_WIDTH))

        wr = jnp.pad(w_router[i].astype(F32), ((0, 0), (0, LANES - N_EXPERTS)))
        wr_hi = wr.astype(BF16)
        wr_lo = (wr - wr_hi.astype(F32)).astype(BF16)
        b_r = jnp.concatenate([b_router[i].astype(F32), jnp.full((LANES - N_EXPERTS,), -1e30, F32)]).reshape(1, LANES)
        h1, m, eid, gates = _out_router_call(cat_p, cat_s, hp, hs, w_out[i].astype(BF16), _row(norm_moe_g[i]),
                                             wr_hi, wr_lo, b_r, tm)

        be, nval, off, tok, dst = _route(eid[:, :TOP_K], tm_moe, nb_moe)
        b_up_g = (b_up[i].astype(F32).reshape(N_EXPERTS, 2 * D_FF // 256, LANES, 2).transpose(0, 1, 3, 2)
                  .reshape(N_EXPERTS, 1, 2 * D_FF))
        y4 = _moe_call(m, be, nval, off, tok, dst, w_up[i], b_up_g, w_down[i],
                       b_down[i].reshape(N_EXPERTS, 1, D_MODEL), perm, tm_moe, nb_moe)

        hp, hs = _ple_call(h1, y4, gates,
                           p_prompt[i].reshape(tp, -1), p_sample[i].reshape(ts, -1), _row(norm_ple_g[i]),
                           w_ple_gate[i].astype(BF16), w_ple_proj[i].astype(BF16), _row(norm_final_g), tm)

    y_prompt = hp.reshape(bp, lp, d)
    y_sample = hs.reshape(bs, ls, d)
    return (y_prompt, y_sample, jnp.stack(ssm_p), jnp.stack(conv_p), jnp.stack(ssm_s), jnp.stack(conv_s),
            jnp.stack(v_s))
```

```python
import functools

import jax
import jax.numpy as jnp
from jax import lax
from jax.experimental import pallas as pl
from jax.experimental.pallas import tpu as pltpu

F32 = jnp.float32
BF16 = jnp.bfloat16
I32 = jnp.int32

EPS = 1e-6
D_MODEL = 1024
SSD_WIDTH = 512
SSD_HEADS = 8
HEAD_DIM = 64
SSD_GROUPS = 2
D_STATE = 128
CONV_W = 4
CONV_DIM = SSD_WIDTH + 2 * SSD_GROUPS * D_STATE
MLP_WIDTH = 512
MLP_HEADS = 8
N_EXPERTS = 32
TOP_K = 4
D_FF = 1024
SWIGLU_LIMIT = 7.0
SWIGLU_ALPHA = 1.702
TOPK_SHIFT = 2
assert 1 << TOPK_SHIFT == TOP_K
LANES = 128
CHUNK = 128
DT_PAD = LANES
TOKEN_TILE_ROWS = D_MODEL // LANES
IN_PAD = SSD_WIDTH + CONV_DIM + 2 * MLP_WIDTH + DT_PAD
VMEM_LIMIT = 56 * 1024 * 1024


def _cparams(sem):
    return pltpu.CompilerParams(dimension_semantics=sem, vmem_limit_bytes=VMEM_LIMIT)


def _const_spec(shape):
    return pl.BlockSpec(shape, lambda *_: (0,) * len(shape))


def _rms(x):
    return x * lax.rsqrt(jnp.mean(x * x, axis=-1, keepdims=True) + EPS)


def _dot(a, b):
    return jnp.dot(a, b, preferred_element_type=F32)


def _dot_nt(a, b):
    return lax.dot_general(a, b, (((1,), (1,)), ((), ())), preferred_element_type=F32)


def _split3(x):
    hi = x.astype(BF16)
    r = x - hi.astype(F32)
    mid = r.astype(BF16)
    lo = (r - mid.astype(F32)).astype(BF16)
    return hi, mid, lo


def _sel_right(x, m01):
    hi, mid, lo = _split3(x)
    return _dot(hi, m01) + _dot(mid, m01) + _dot(lo, m01)


def _sel_left(m01, x):
    hi, mid, lo = _split3(x)
    return _dot(m01, hi) + _dot(m01, mid) + _dot(m01, lo)


def _softplus(x):
    return jnp.maximum(x, 0.0) + jnp.log1p(jnp.exp(-jnp.abs(x)))


def _inproj_call(xp, xs, g, w, tm):
    tp, ts = xp.shape[0], xs.shape[0]
    n_p, n_s = tp // tm, ts // tm
    t_all = tp + ts
    segs = ((0, 512), (512, 1536), (1536, 2048), (2048, 2560), (2560, IN_PAD))

    def body(xp_ref, xs_ref, g_ref, w_ref, *outs):
        def run(x_ref):
            xn = (_rms(x_ref[...]) * g_ref[...]).astype(BF16)
            for (a, b), o in zip(segs, outs):
                o[...] = _dot(xn, w_ref[:, a:b])

        i = pl.program_id(0)

        @pl.when(i < n_p)
        def _():
            run(xp_ref)

        @pl.when(i >= n_p)
        def _():
            run(xs_ref)

    widths = [b - a for a, b in segs]
    return pl.pallas_call(
        body,
        out_shape=[jax.ShapeDtypeStruct((t_all, wd), F32) for wd in widths],
        grid=(n_p + n_s,),
        in_specs=[
            pl.BlockSpec((tm, D_MODEL), lambda i: (jnp.minimum(i, n_p - 1), 0)),
            pl.BlockSpec((tm, D_MODEL), lambda i: (jnp.maximum(i - n_p, 0), 0)),
            _const_spec((1, D_MODEL)),
            _const_spec((D_MODEL, IN_PAD)),
        ],
        out_specs=[pl.BlockSpec((tm, wd), lambda i: (i, 0)) for wd in widths],
        compiler_params=_cparams(("arbitrary",)),
        name="in_proj",
    )(xp, xs, g, w)


def _mixer_front(conv, dtr, dtb, alog, alog_x, rexp, tril, seg_ones):
    xact = conv * jax.nn.sigmoid(conv)
    xs = xact[:, :SSD_WIDTH]
    bm = xact[:, SSD_WIDTH:SSD_WIDTH + 256]
    cm = xact[:, SSD_WIDTH + 256:]
    dt = _softplus(dtr + dtb)
    a = dt * (-jnp.exp(alog))
    dt_x = _sel_right(dt, rexp)
    a_x = dt_x * (-jnp.exp(alog_x))
    acum = _sel_left(tril, a)
    acum_x = _sel_left(tril, a_x)
    if seg_ones is None:
        r = acum_x.shape[0]
        tot_x = jnp.broadcast_to(acum_x[r - 1:r, :], acum_x.shape)
    else:
        tot_x = _sel_left(seg_ones, a_x)
    return xs, bm, cm, dt_x, acum, acum_x, tot_x


def _ssd_intra(cmb, bmb, acum, xdt, mask):
    r = acum.shape[0]
    acum_t = acum.T
    lane = lax.broadcasted_iota(I32, (r, LANES), 1)
    low = lane < HEAD_DIM
    outs = []
    for g in range(SSD_GROUPS):
        sg = _dot_nt(cmb[:, LANES * g:LANES * (g + 1)], bmb[:, LANES * g:LANES * (g + 1)])
        for k in (2 * g, 2 * g + 1):
            parts = []
            for h in (2 * k, 2 * k + 1):
                seg = acum[:, h:h + 1] - acum_t[h:h + 1, :]
                parts.append((sg * jnp.exp(jnp.where(mask, seg, -jnp.inf))).astype(BF16))
            lhs = jnp.concatenate(parts, axis=1)
            xd = xdt[:, LANES * k:LANES * (k + 1)]
            rhs = jnp.concatenate([jnp.where(low, xd, 0.0), jnp.where(low, 0.0, xd)], axis=0).astype(BF16)
            outs.append(_dot(lhs, rhs))
    return jnp.concatenate(outs, axis=1)


def _mixer_back(y, z, u, v, sng, vng, vnb, wsp_ref, bsp, mog):
    r = y.shape[0]
    yg = y * (z * jax.nn.sigmoid(z))
    halves = []
    for g in range(SSD_GROUPS):
        t = yg[:, 256 * g:256 * (g + 1)]
        halves.append(_rms(t))
    yn = jnp.concatenate(halves, axis=1) * sng
    ug = jax.nn.gelu(u)
    vg = jax.nn.gelu(v)
    mu = jnp.mean(vg, axis=-1, keepdims=True)
    var = jnp.mean(jnp.square(vg - mu), axis=-1, keepdims=True)
    v_ln = (vg - mu) * lax.rsqrt(var + EPS) * vng + vnb
    lane = lax.broadcasted_iota(I32, (r, LANES), 1)
    low = lane < HEAD_DIM
    outs = []
    for k in range(MLP_HEADS // 2):
        vd = v_ln[:, LANES * k:LANES * (k + 1)]
        rhs = jnp.concatenate([jnp.where(low, vd, 0.0), jnp.where(low, 0.0, vd)], axis=0).astype(BF16)
        outs.append(_dot(wsp_ref[k], rhs))
    s = jnp.concatenate(outs, axis=1) + bsp
    m = _rms(ug * s) * mog
    return jnp.concatenate([yn, m], axis=1).astype(BF16), v_ln


_MIXER_PARAM_SHAPES = (
    (CONV_W, CONV_DIM), (1, CONV_DIM), (1, LANES), (1, LANES), (1, SSD_WIDTH), (LANES, SSD_WIDTH),
    (CHUNK, CHUNK), (1, SSD_WIDTH), (1, SSD_WIDTH), (1, MLP_WIDTH), (1, MLP_WIDTH),
    (MLP_HEADS // 2, CHUNK, 2 * CHUNK), (CHUNK, MLP_WIDTH), (1, MLP_WIDTH),
)


def _prompt_mixer_body(z_ref, xbc_ref, u_ref, v_ref, dt_ref,
                       cw_ref, cb_ref, dtb_ref, alog_ref, alogx_ref, rexp_ref, tril_ref, dskip_ref,
                       sng_ref, vng_ref, vnb_ref, wsp_ref, bsp_ref, mog_ref,
                       cat_ref, ssm_ref, ext_scr, s_scr):
    c = pl.program_id(1)
    r = CHUNK

    @pl.when(c == 0)
    def _():
        ext_scr[0:8, :] = jnp.zeros((8, CONV_DIM), F32)
        s_scr[...] = jnp.zeros_like(s_scr)

    x = xbc_ref[...]
    ext_scr[8:8 + r, :] = x
    cw = cw_ref[...]
    conv = (cb_ref[...] + cw[3:4] * x + cw[2:3] * ext_scr[7:7 + r, :]
            + cw[1:2] * ext_scr[6:6 + r, :] + cw[0:1] * ext_scr[5:5 + r, :])
    ext_scr[0:8, :] = x[r - 8:r, :]

    xs, bm, cm, dt_x, acum, acum_x, tot_x = _mixer_front(
        conv, dt_ref[...], dtb_ref[...], alog_ref[...], alogx_ref[...], rexp_ref[...], tril_ref[...], None)
    bmb, cmb = bm.astype(BF16), cm.astype(BF16)
    xdt = xs * dt_x
    row = lax.broadcasted_iota(I32, (r, r), 0)
    col = lax.broadcasted_iota(I32, (r, r), 1)
    y_diag = _ssd_intra(cmb, bmb, acum, xdt, row >= col)

    s_prev = s_scr[...]
    s_prev_b = s_prev.astype(BF16)
    y_off = jnp.concatenate(
        [_dot_nt(cmb[:, LANES * g:LANES * (g + 1)], s_prev_b[256 * g:256 * (g + 1), :]) for g in range(SSD_GROUPS)],
        axis=1)
    y = y_diag + y_off * jnp.exp(acum_x) + dskip_ref[...] * xs

    w_t = (xdt * jnp.exp(tot_x - acum_x)).T.astype(BF16)
    states = jnp.concatenate(
        [_dot(w_t[256 * g:256 * (g + 1), :], bmb[:, LANES * g:LANES * (g + 1)]) for g in range(SSD_GROUPS)], axis=0)
    s_new = s_prev * jnp.exp(tot_x).T + states
    s_scr[...] = s_new

    cat, _ = _mixer_back(y, z_ref[...], u_ref[...], v_ref[...], sng_ref[...], vng_ref[...], vnb_ref[...],
                         wsp_ref, bsp_ref[...], mog_ref[...])
    cat_ref[...] = cat

    @pl.when(c == pl.num_programs(1) - 1)
    def _():
        ssm_ref[0] = s_new


def _prompt_mixer_call(z, xbc, u, v, dtr, params, nb, nc):
    row = lambda b, c: (b * nc + c, 0)
    in_specs = [
        pl.BlockSpec((CHUNK, SSD_WIDTH), row), pl.BlockSpec((CHUNK, CONV_DIM), row),
        pl.BlockSpec((CHUNK, MLP_WIDTH), row), pl.BlockSpec((CHUNK, MLP_WIDTH), row),
        pl.BlockSpec((CHUNK, DT_PAD), row),
    ] + [_const_spec(s) for s in _MIXER_PARAM_SHAPES]
    return pl.pallas_call(
        _prompt_mixer_body,
        out_shape=[jax.ShapeDtypeStruct((nb * nc * CHUNK, D_MODEL), BF16),
                   jax.ShapeDtypeStruct((nb, SSD_WIDTH, D_STATE), F32)],
        grid=(nb, nc),
        in_specs=in_specs,
        out_specs=[pl.BlockSpec((CHUNK, D_MODEL), row),
                   pl.BlockSpec((1, SSD_WIDTH, D_STATE), lambda b, c: (b, 0, 0))],
        scratch_shapes=[pltpu.VMEM((CHUNK + 8, CONV_DIM), F32), pltpu.VMEM((SSD_WIDTH, D_STATE), F32)],
        compiler_params=_cparams(("arbitrary", "arbitrary")),
        name="prompt_mixer",
    )(z, xbc, u, v, dtr, *params)


def _sample_mixer_body(seq_len, z_ref, x0_ref, x1_ref, x2_ref, x3_ref, u_ref, v_ref, dt_ref, h_ref,
                       cw_ref, cb_ref, dtb_ref, alog_ref, alogx_ref, rexp_ref, tril_ref, dskip_ref,
                       sng_ref, vng_ref, vnb_ref, wsp_ref, bsp_ref, mog_ref, segones_ref,
                       cat_ref, vout_ref, hout_ref, cm_scr, bm_scr, wt_scr, dtt_scr, yoff_scr):
    r = CHUNK
    shift = seq_len.bit_length() - 1
    cw = cw_ref[...]
    conv = (cb_ref[...] + cw[3:4] * x0_ref[...] + cw[2:3] * x1_ref[...]
            + cw[1:2] * x2_ref[...] + cw[0:1] * x3_ref[...])
    xs, bm, cm, dt_x, acum, acum_x, tot_x = _mixer_front(
        conv, dt_ref[...], dtb_ref[...], alog_ref[...], alogx_ref[...], rexp_ref[...], tril_ref[...],
        segones_ref[...])
    bmb, cmb = bm.astype(BF16), cm.astype(BF16)
    xdt = xs * dt_x
    row = lax.broadcasted_iota(I32, (r, r), 0)
    col = lax.broadcasted_iota(I32, (r, r), 1)
    same = lax.shift_right_logical(row, shift) == lax.shift_right_logical(col, shift)
    y_diag = _ssd_intra(cmb, bmb, acum, xdt, same & (row >= col))

    cm_scr[...] = cm
    bm_scr[...] = bmb
    wt_scr[...] = (xdt * jnp.exp(tot_x - acum_x)).T
    dtt_scr[...] = jnp.exp(tot_x).T
    ones_b = jnp.ones((LANES, LANES), BF16)
    seqs_per_slab = 8 // seq_len

    def slab(j, carry):
        rows = pl.ds(pl.multiple_of(8 * j, 8), 8)
        cms = cm_scr[rows, :].astype(BF16)
        sub = lax.broadcasted_iota(I32, (8, 256), 0)
        lane = lax.broadcasted_iota(I32, (256, LANES), 1)
        for g in range(SSD_GROUPS):
            q_rows = slice(256 * g, 256 * (g + 1))
            acc = jnp.zeros((8, 256), F32)
            for q in range(seqs_per_slab):
                s = seqs_per_slab * j + q
                y_s = _dot_nt(cms[:, LANES * g:LANES * (g + 1)], h_ref[s, q_rows, :].astype(BF16))
                acc = jnp.where(lax.shift_right_logical(sub, shift) == q, y_s, acc)
            yoff_scr[rows, 256 * g:256 * (g + 1)] = acc
            for q in range(seqs_per_slab):
                s = seqs_per_slab * j + q
                w_sel = jnp.where(lax.shift_right_logical(lane, shift) == s, wt_scr[q_rows, :], 0.0).astype(BF16)
                st = _dot(w_sel, bm_scr[:, LANES * g:LANES * (g + 1)])
                d_sel = jnp.where(lane == s * seq_len, dtt_scr[q_rows, :], 0.0)
                hout_ref[s, q_rows, :] = h_ref[s, q_rows, :] * _sel_right(d_sel, ones_b) + st
        return carry

    lax.fori_loop(0, r // 8, slab, 0)

    y = y_diag + yoff_scr[...] * jnp.exp(acum_x) + dskip_ref[...] * xs
    cat, v_ln = _mixer_back(y, z_ref[...], u_ref[...], v_ref[...], sng_ref[...], vng_ref[...], vnb_ref[...],
                            wsp_ref, bsp_ref[...], mog_ref[...])
    cat_ref[...] = cat
    vout_ref[...] = v_ln


def _sample_mixer_call(z, x_shift, u, v, dtr, h0, params, seg_ones, row0, seq_len):
    ts = x_shift[0].shape[0]
    n = ts // CHUNK
    spt = CHUNK // seq_len
    off = lambda i: (row0 + i, 0)
    loc = lambda i: (i, 0)
    st3 = lambda i: (i, 0, 0)
    in_specs = (
        [pl.BlockSpec((CHUNK, SSD_WIDTH), off)]
        + [pl.BlockSpec((CHUNK, CONV_DIM), loc)] * 4
        + [pl.BlockSpec((CHUNK, MLP_WIDTH), off), pl.BlockSpec((CHUNK, MLP_WIDTH), off),
           pl.BlockSpec((CHUNK, DT_PAD), off), pl.BlockSpec((spt, SSD_WIDTH, D_STATE), st3)]
        + [_const_spec(s) for s in _MIXER_PARAM_SHAPES] + [_const_spec((CHUNK, CHUNK))])
    return pl.pallas_call(
        functools.partial(_sample_mixer_body, seq_len),
        out_shape=[jax.ShapeDtypeStruct((ts, D_MODEL), BF16), jax.ShapeDtypeStruct((ts, MLP_WIDTH), F32),
                   jax.ShapeDtypeStruct(h0.shape, F32)],
        grid=(n,),
        in_specs=in_specs,
        out_specs=[pl.BlockSpec((CHUNK, D_MODEL), loc), pl.BlockSpec((CHUNK, MLP_WIDTH), loc),
                   pl.BlockSpec((spt, SSD_WIDTH, D_STATE), st3)],
        scratch_shapes=[pltpu.VMEM((CHUNK, 256), F32), pltpu.VMEM((CHUNK, 256), BF16),
                        pltpu.VMEM((SSD_WIDTH, CHUNK), F32), pltpu.VMEM((SSD_WIDTH, CHUNK), F32),
                        pltpu.VMEM((CHUNK, SSD_WIDTH), F32)],
        compiler_params=_cparams(("arbitrary",)),
        name="sample_mixer",
    )(z, *x_shift, u, v, dtr, h0, *params, seg_ones)


def _out_router_call(cat_p, cat_s, xp, xs, w_out, g_moe, wr_hi, wr_lo, b_r, tm):
    tp, ts = xp.shape[0], xs.shape[0]
    n_p, n_s = tp // tm, ts // tm
    t_all = tp + ts

    def body(cp_ref, cs_ref, xp_ref, xs_ref, wo_ref, g_ref, wh_ref, wl_ref, br_ref,
             h1_ref, m_ref, eid_ref, gate_ref):
        def run(c_ref, x_ref):
            h1 = x_ref[...] + _dot(c_ref[...], wo_ref[...])
            h1_ref[...] = h1
            m = _rms(h1) * g_ref[...]
            for j in range(TOKEN_TILE_ROWS):
                m_ref[pl.ds(j, tm, stride=TOKEN_TILE_ROWS), :] = m[:, LANES * j:LANES * (j + 1)]
            m_hi = m.astype(BF16)
            m_lo = (m - m_hi.astype(F32)).astype(BF16)
            logits = _dot(m_hi, wh_ref[...]) + _dot(m_lo, wh_ref[...]) + _dot(m_hi, wl_ref[...]) + br_ref[...]
            lane = lax.broadcasted_iota(I32, logits.shape, 1).astype(F32)
            work = logits
            vals, ids = [], []
            for _ in range(TOP_K):
                mx = jnp.max(work, axis=-1, keepdims=True)
                idx = jnp.min(jnp.where(work == mx, lane, float(LANES)), axis=-1, keepdims=True)
                vals.append(mx)
                ids.append(idx)
                work = jnp.where(lane == idx, -jnp.inf, work)
            ex = [jnp.exp(vv - vals[0]) for vv in vals]
            den = ex[0] + ex[1] + ex[2] + ex[3]
            eid = jnp.zeros(logits.shape, I32)
            gate = jnp.zeros(logits.shape, F32)
            for k in range(TOP_K):
                eid = jnp.where(lane == k, ids[k].astype(I32), eid)
                gate = jnp.where(lane == k, ex[k] / den, gate)
            eid_ref[...] = eid
            gate_ref[...] = gate

        i = pl.program_id(0)

        @pl.when(i < n_p)
        def _():
            run(cp_ref, xp_ref)

        @pl.when(i >= n_p)
        def _():
            run(cs_ref, xs_ref)

    pmap = lambda i: (jnp.minimum(i, n_p - 1), 0)
    smap = lambda i: (jnp.maximum(i - n_p, 0), 0)
    omap = lambda i: (i, 0)
    return pl.pallas_call(
        body,
        out_shape=[jax.ShapeDtypeStruct((t_all, D_MODEL), F32),
                   jax.ShapeDtypeStruct((t_all * TOKEN_TILE_ROWS, LANES), F32),
                   jax.ShapeDtypeStruct((t_all, LANES), I32), jax.ShapeDtypeStruct((t_all, LANES), F32)],
        grid=(n_p + n_s,),
        in_specs=[pl.BlockSpec((tm, D_MODEL), pmap), pl.BlockSpec((tm, D_MODEL), smap),
                  pl.BlockSpec((tm, D_MODEL), pmap), pl.BlockSpec((tm, D_MODEL), smap),
                  _const_spec((D_MODEL, D_MODEL)), _const_spec((1, D_MODEL)),
                  _const_spec((D_MODEL, LANES)), _const_spec((D_MODEL, LANES)), _const_spec((1, LANES))],
        out_specs=[pl.BlockSpec((tm, D_MODEL), omap), pl.BlockSpec((tm * TOKEN_TILE_ROWS, LANES), omap),
                   pl.BlockSpec((tm, LANES), omap), pl.BlockSpec((tm, LANES), omap)],
        compiler_params=_cparams(("arbitrary",)),
        name="out_router",
    )(cat_p, cat_s, xp, xs, w_out, g_moe, wr_hi, wr_lo, b_r)


def _route(eid, tm, nb):
    t = eid.shape[0]
    tk = t * TOP_K
    flat = eid.reshape(tk)
    _, order = lax.sort((flat, jnp.arange(tk, dtype=I32)), num_keys=1, is_stable=True)
    counts = jnp.sum((flat[:, None] == jnp.arange(N_EXPERTS, dtype=I32)[None, :]).astype(I32), axis=0)
    nblk = (counts + tm - 1) // tm
    bend = jnp.cumsum(nblk)
    bstart = bend - nblk
    start = jnp.cumsum(counts) - counts
    nused = bend[-1]
    blk = jnp.arange(nb, dtype=I32)
    used = blk < nused
    be = jnp.minimum(jnp.sum((jnp.minimum(blk, nused - 1)[:, None] >= bend[None, :]).astype(I32), axis=1),
                     N_EXPERTS - 1)
    sel = (be[:, None] == jnp.arange(N_EXPERTS, dtype=I32)[None, :]).astype(I32)
    pick = lambda v: jnp.sum(sel * v[None, :], axis=1)
    done = (blk - pick(bstart)) * tm
    nval = jnp.where(used, jnp.clip(pick(counts) - done, 0, tm), 0).astype(I32)
    off = jnp.where(used, pick(start) + done, 0).astype(I32)
    pad = (-(-(tk + tm) // LANES) + _id_rows(tm)) * LANES - tk
    tok = jnp.pad(lax.shift_right_logical(order, TOPK_SHIFT) * TOKEN_TILE_ROWS, (0, pad))
    dst = jnp.pad(((order & (TOP_K - 1)) * t + lax.shift_right_logical(order, TOPK_SHIFT)) * TOKEN_TILE_ROWS,
                  (0, pad))
    return be, nval, off, tok, dst


def _id_rows(tm):
    return tm // LANES + 1


def _moe_body(tm, t_all, nb, be_ref, nval_ref, off_ref, tok_hbm, dst_hbm, m_hbm, wup_ref, bup_ref, wdn_ref, bdn_ref,
              perm_ref, y_hbm, gids, sids, xbuf, ybuf, wup_b, wdn_b, isem, gsem, ssem):
    i = pl.program_id(0)
    nv = nval_ref[i]
    slot = i & 1
    prv = jnp.maximum(i - 1, 0)
    nxt = jnp.minimum(i + 1, nb - 1)
    nx2 = jnp.minimum(i + 2, nb - 1)
    has_next = (i + 1 < nb) & (nval_ref[nxt] > 0)
    has_next2 = (i + 2 < nb) & (nval_ref[nx2] > 0)
    n_prev = jnp.where(i > 0, nval_ref[prv], 0)
    win = _id_rows(tm) * LANES
    spare = TOP_K * t_all * TOKEN_TILE_ROWS

    def ids_copies(b):
        start = pl.multiple_of(lax.shift_right_logical(off_ref[b], 7) * LANES, LANES)
        ring = pl.ds(pl.multiple_of((b & 3) * win, LANES), win)
        return (pltpu.make_async_copy(tok_hbm.at[pl.ds(start, win)], gids.at[ring], isem.at[b & 3, 0]),
                pltpu.make_async_copy(dst_hbm.at[pl.ds(start, win)], sids.at[ring], isem.at[b & 3, 1]))

    def id_base(b):
        return (b & 3) * win + (off_ref[b] & (LANES - 1))

    def gather_row(base, s, r):
        src = pl.ds(pl.multiple_of(gids[base + r], TOKEN_TILE_ROWS), TOKEN_TILE_ROWS)
        return pltpu.make_async_copy(m_hbm.at[src], xbuf.at[s, pl.ds(TOKEN_TILE_ROWS * r, TOKEN_TILE_ROWS)],
                                     gsem.at[s])

    def scatter_row(base, s, r, n):
        dest = jnp.where(r < n, sids[base + r], spare + TOKEN_TILE_ROWS * r)
        rows = pl.ds(pl.multiple_of(dest, TOKEN_TILE_ROWS), TOKEN_TILE_ROWS)
        return pltpu.make_async_copy(ybuf.at[s, pl.ds(TOKEN_TILE_ROWS * r, TOKEN_TILE_ROWS)], y_hbm.at[rows],
                                     ssem.at[s])

    def wait_gathers(s):
        pltpu.make_async_copy(m_hbm.at[pl.ds(0, tm * TOKEN_TILE_ROWS)], xbuf.at[s], gsem.at[s]).wait()

    def wait_scatters(s):
        pltpu.make_async_copy(ybuf.at[s], y_hbm.at[pl.ds(0, tm * TOKEN_TILE_ROWS)], ssem.at[s]).wait()

    def for_rows(fn):
        def one(r, c):
            fn(r)
            return c

        lax.fori_loop(0, tm, one, 0)

    @pl.when(nv > 0)
    def _():
        @pl.when(i == 0)
        def _():
            ybuf[...] = jnp.zeros_like(ybuf)
            fill = pltpu.make_async_copy(ybuf.at[0], y_hbm.at[pl.ds(spare, tm * TOKEN_TILE_ROWS)], ssem.at[0])
            fill.start()
            fill.wait()
            for b in range(4):
                for cp in ids_copies(b):
                    cp.start()
                    cp.wait()
            base0 = id_base(0)
            for_rows(lambda r: gather_row(base0, 0, r).start())

        @pl.when(has_next & (i >= 3))
        def _():
            for cp in ids_copies(nxt):
                cp.wait()

        @pl.when(has_next2 & (i >= 2))
        def _():
            for cp in ids_copies(nx2):
                cp.start()

        @pl.when((i == 0) | (be_ref[i] != be_ref[prv]))
        def _():
            for jb in range(2 * D_FF // 256):
                cols = slice(256 * jb, 256 * (jb + 1))
                wup_b[:, cols] = _dot(wup_ref[0, :, cols].astype(BF16), perm_ref[...]).astype(BF16)
            wdn_b[...] = wdn_ref[0].astype(BF16)

        y_cur = lax.rem(i, 3)
        y_prev = lax.rem(i + 2, 3)
        y_prev2 = lax.rem(i + 1, 3)

        def ffn_step(cur, oth):
            wait_gathers(cur)

            @pl.when(i >= 2)
            def _():
                wait_scatters(y_cur)

            g_base = id_base(nxt)
            s_base = id_base(prv)
            for r in range(tm):
                gather_row(g_base, oth, r).start()
                scatter_row(s_base, y_prev, r, n_prev).start()

            x = jnp.concatenate([xbuf[slot, pl.ds(j, tm, stride=TOKEN_TILE_ROWS), :]
                                 for j in range(TOKEN_TILE_ROWS)], axis=1).astype(BF16)
            acts = []
            for jb in range(D_FF // LANES):
                h = _dot(x, wup_b[:, 256 * jb:256 * (jb + 1)]) + bup_ref[0, :, 256 * jb:256 * (jb + 1)]
                gate = jnp.minimum(h[:, :LANES], SWIGLU_LIMIT)
                lin = jnp.clip(h[:, LANES:], -SWIGLU_LIMIT, SWIGLU_LIMIT)
                acts.append((gate * jax.nn.sigmoid(SWIGLU_ALPHA * gate) * (lin + 1.0)).astype(BF16))
            act = jnp.concatenate(acts, axis=1)
            for c in range(D_MODEL // 256):
                yc = _dot(act, wdn_b[:, 256 * c:256 * (c + 1)]) + bdn_ref[0, :, 256 * c:256 * (c + 1)]
                for half in range(2):
                    ybuf[y_cur, pl.ds(2 * c + half, tm, stride=TOKEN_TILE_ROWS), :] = (
                        yc[:, LANES * half:LANES * (half + 1)])

            @pl.when(jnp.logical_not(has_next))
            def _():
                wait_gathers(oth)

                @pl.when(i >= 1)
                def _():
                    wait_scatters(y_prev2)
                wait_scatters(y_prev)
                last_base = id_base(i)
                for_rows(lambda r: scatter_row(last_base, y_cur, r, nv).start())
                wait_scatters(y_cur)

        for parity in range(2):
            pl.when(slot == parity)(functools.partial(ffn_step, parity, 1 - parity))


def _moe_call(m, be, nval, off, tok, dst, w_up, b_up_g, w_down, b_down, perm, tm, nb):
    t = m.shape[0] // TOKEN_TILE_ROWS
    assert nb >= 4
    by_expert = lambda i, be, nv, off: (be[i], 0, 0)
    grid_spec = pltpu.PrefetchScalarGridSpec(
        num_scalar_prefetch=3,
        grid=(nb,),
        in_specs=[
            pl.BlockSpec(memory_space=pl.ANY),
            pl.BlockSpec(memory_space=pl.ANY),
            pl.BlockSpec(memory_space=pl.ANY),
            pl.BlockSpec((1, D_MODEL, 2 * D_FF), by_expert),
            pl.BlockSpec((1, 1, 2 * D_FF), by_expert),
            pl.BlockSpec((1, D_FF, D_MODEL), by_expert),
            pl.BlockSpec((1, 1, D_MODEL), by_expert),
            pl.BlockSpec((256, 256), lambda i, be, nv, off: (0, 0)),
        ],
        out_specs=pl.BlockSpec(memory_space=pl.ANY),
        scratch_shapes=[pltpu.SMEM((4 * _id_rows(tm) * LANES,), I32), pltpu.SMEM((4 * _id_rows(tm) * LANES,), I32),
                        pltpu.VMEM((2, tm * TOKEN_TILE_ROWS, LANES), F32),
                        pltpu.VMEM((3, tm * TOKEN_TILE_ROWS, LANES), F32),
                        pltpu.VMEM((D_MODEL, 2 * D_FF), BF16), pltpu.VMEM((D_FF, D_MODEL), BF16),
                        pltpu.SemaphoreType.DMA((4, 2)), pltpu.SemaphoreType.DMA((2,)),
                        pltpu.SemaphoreType.DMA((3,))],
    )
    return pl.pallas_call(
        functools.partial(_moe_body, tm, t, nb),
        out_shape=jax.ShapeDtypeStruct(((TOP_K * t + tm) * TOKEN_TILE_ROWS, LANES), F32),
        grid_spec=grid_spec,
        compiler_params=_cparams(("arbitrary",)),
        name="moe_experts",
    )(be, nval, off, tok, dst, m, w_up, b_up_g, w_down, b_down, perm)


def _ple_call(h1, y4, gates, pp, ps, g_ple, w_gate, w_proj, g_final, tm):
    tp, ts = pp.shape[0], ps.shape[0]
    n_p, n_s = tp // tm, ts // tm
    ple = pp.shape[1]

    def body(h1_ref, y0_ref, y1_ref, y2_ref, y3_ref, gt_ref, pp_ref, ps_ref, g_ref, wg_ref, wp_ref, gf_ref,
             yp_ref, ys_ref):
        def run(p_ref, o_ref):
            gt = gt_ref[...]
            moe = None
            for k, y_ref in enumerate((y0_ref, y1_ref, y2_ref, y3_ref)):
                y_k = jnp.concatenate([y_ref[pl.ds(j, tm, stride=TOKEN_TILE_ROWS), :]
                                       for j in range(TOKEN_TILE_ROWS)], axis=1)
                moe = gt[:, k:k + 1] * y_k if moe is None else moe + gt[:, k:k + 1] * y_k
            h2 = h1_ref[...] + moe
            a = (_rms(h2) * g_ref[...]).astype(BF16)
            gate = jax.nn.sigmoid(_dot(a, wg_ref[...]))
            pe = _dot(p_ref[...].astype(BF16), wp_ref[...])
            h3 = h2 + pe * gate
            o_ref[...] = _rms(h3) * gf_ref[...]

        i = pl.program_id(0)

        @pl.when(i < n_p)
        def _():
            run(pp_ref, yp_ref)

        @pl.when(i >= n_p)
        def _():
            run(ps_ref, ys_ref)

    pmap = lambda i: (jnp.minimum(i, n_p - 1), 0)
    smap = lambda i: (jnp.maximum(i - n_p, 0), 0)
    omap = lambda i: (i, 0)
    return pl.pallas_call(
        body,
        out_shape=[jax.ShapeDtypeStruct((tp, D_MODEL), F32), jax.ShapeDtypeStruct((ts, D_MODEL), F32)],
        grid=(n_p + n_s,),
        in_specs=[pl.BlockSpec((tm, D_MODEL), omap)]
                 + [pl.BlockSpec((tm * TOKEN_TILE_ROWS, LANES),
                                 functools.partial(lambda k, i: (k * (n_p + n_s) + i, 0), k)) for k in range(TOP_K)]
                 + [pl.BlockSpec((tm, LANES), omap), pl.BlockSpec((tm, ple), pmap), pl.BlockSpec((tm, ple), smap),
                  _const_spec((1, D_MODEL)), _const_spec((D_MODEL, D_MODEL)), _const_spec((ple, D_MODEL)),
                  _const_spec((1, D_MODEL))],
        out_specs=[pl.BlockSpec((tm, D_MODEL), pmap), pl.BlockSpec((tm, D_MODEL), smap)],
        compiler_params=_cparams(("arbitrary",)),
        name="ple_final",
    )(h1, y4, y4, y4, y4, gates, pp, ps, g_ple, w_gate, w_proj, g_final)


def _row(x, width=None):
    x = x.reshape(1, -1).astype(F32)
    if width is not None and x.shape[1] < width:
        x = jnp.pad(x, ((0, 0), (0, width - x.shape[1])))
    return x


def _mixer_params(conv_w, conv_b, dt_bias, a_log, d_skip, ssd_norm_g, v_norm_g, v_norm_b, w_spatial, b_spatial,
                  mlp_out_g, seq_len):
    pos = jnp.arange(CHUNK) % seq_len
    same = (jnp.arange(CHUNK)[:, None] // seq_len) == (jnp.arange(CHUNK)[None, :] // seq_len)
    tril = (same & (jnp.arange(CHUNK)[:, None] >= jnp.arange(CHUNK)[None, :])).astype(BF16)
    rexp = (jnp.arange(LANES)[:, None] == (jnp.arange(SSD_WIDTH)[None, :] // HEAD_DIM)).astype(BF16)
    w_loc = jnp.tril(w_spatial[:, :seq_len, :seq_len])
    onehot = (pos[:, None] == jnp.arange(seq_len)[None, :]).astype(F32)
    tiled = jnp.einsum("iq,hqr,jr->hij", onehot, w_loc.astype(F32), onehot, precision=lax.Precision.HIGHEST)
    w_bd = jnp.where(same[None], tiled, 0.0)
    wsp = (w_bd.reshape(MLP_HEADS // 2, 2, CHUNK, CHUNK).transpose(0, 2, 1, 3)
           .reshape(MLP_HEADS // 2, CHUNK, 2 * CHUNK).astype(BF16))
    bsp = jnp.repeat(b_spatial[:, :seq_len].T[pos], MLP_WIDTH // MLP_HEADS, axis=1)
    params = (
        conv_w.astype(F32), _row(conv_b), _row(dt_bias, LANES), _row(a_log, LANES),
        _row(jnp.repeat(a_log, HEAD_DIM)), rexp, tril, _row(jnp.repeat(d_skip, HEAD_DIM)),
        _row(ssd_norm_g), _row(v_norm_g), _row(v_norm_b), wsp, bsp.astype(F32), _row(mlp_out_g),
    )
    return params, same.astype(BF16)


def _tile_rows(n):
    return 512 if n % 512 == 0 else CHUNK


def kernel(x_prompt, x_sample, state_ssm, state_conv, p_prompt, p_sample, norm_mix_g, w_in, conv_w, conv_b, dt_bias, a_log, d_skip, ssd_norm_g, v_norm_g, v_norm_b, w_spatial, b_spatial, mlp_out_g, w_out, norm_moe_g, w_router, b_router, w_up, b_up, w_down, b_down, norm_ple_g, w_ple_gate, w_ple_proj, norm_final_g):
    depth = norm_mix_g.shape[0]
    bp, lp, d = x_prompt.shape
    bs, ls, _ = x_sample.shape
    tp, ts = bp * lp, bs * ls
    assert depth == 1 and d == D_MODEL and lp % CHUNK == 0 and ts % CHUNK == 0 and 8 % ls == 0
    tm = _tile_rows(tp) if ts % _tile_rows(tp) == 0 else CHUNK
    t_all = tp + ts
    tm_moe = 256
    nb_moe = -(-t_all * TOP_K // tm_moe) + N_EXPERTS

    hp = x_prompt.reshape(tp, d)
    hs = x_sample.reshape(ts, d)
    ssm_p, conv_p, ssm_s, conv_s, v_s = [], [], [], [], []
    o1 = SSD_WIDTH
    o2 = o1 + CONV_DIM
    o3 = o2 + SSD_HEADS
    o4 = o3 + MLP_WIDTH
    c = jnp.arange(256)
    src = jnp.where(c < LANES, 2 * c, 2 * (c - LANES) + 1)
    perm = (jnp.arange(256)[:, None] == src[None, :]).astype(BF16)

    for i in range(depth):
        wi = w_in[i]
        w_cat = jnp.concatenate(
            [wi[:, :o2], wi[:, o3:], jnp.pad(wi[:, o2:o3], ((0, 0), (0, DT_PAD - SSD_HEADS)))], axis=1).astype(BF16)
        z, xbc, u, v, dtr = _inproj_call(hp, hs, _row(norm_mix_g[i]), w_cat, tm)

        mix_args = (conv_w[i], conv_b[i], dt_bias[i], a_log[i], d_skip[i], ssd_norm_g[i], v_norm_g[i], v_norm_b[i],
                    w_spatial[i], b_spatial[i], mlp_out_g[i])
        prm_p, _ = _mixer_params(*mix_args, seq_len=CHUNK)
        cat_p, s_p = _prompt_mixer_call(z, xbc, u, v, dtr, prm_p, bp, lp // CHUNK)
        ssm_p.append(s_p.reshape(bp, SSD_HEADS, HEAD_DIM, D_STATE).astype(state_ssm.dtype))
        conv_p.append(jnp.stack([xbc[(b + 1) * lp - (CONV_W - 1):(b + 1) * lp] for b in range(bp)]))

        prm_s, seg_ones = _mixer_params(*mix_args, seq_len=ls)
        xbc_s = xbc[tp:].reshape(bs, ls, CONV_DIM)
        xpad = jnp.concatenate([state_conv[i].astype(F32), xbc_s], axis=1)
        x_shift = [xpad[:, CONV_W - 1 - k:CONV_W - 1 - k + ls].reshape(ts, CONV_DIM) for k in range(CONV_W)]
        h0 = state_ssm[i].astype(F32).reshape(bs, SSD_WIDTH, D_STATE)
        cat_s, v_rows, s_s = _sample_mixer_call(z, x_shift, u, v, dtr, h0, prm_s, seg_ones, tp // CHUNK, ls)
        ssm_s.append(s_s.reshape(bs, SSD_HEADS, HEAD_DIM, D_STATE).astype(state_ssm.dtype))
        conv_s.append(xpad[:, ls:])
        v_s.append(v_rows.reshape(bs, ls, MLP_WIDTH))

        wr = jnp.pad(w_router[i].astype(F32), ((0, 0), (0, LANES - N_EXPERTS)))
        wr_hi = wr.astype(BF16)
        wr_lo = (wr - wr_hi.astype(F32)).astype(BF16)
        b_r = jnp.concatenate([b_router[i].astype(F32), jnp.full((LANES - N_EXPERTS,), -1e30, F32)]).reshape(1, LANES)
        h1, m, eid, gates = _out_router_call(cat_p, cat_s, hp, hs, w_out[i].astype(BF16), _row(norm_moe_g[i]),
                                             wr_hi, wr_lo, b_r, tm)

        be, nval, off, tok, dst = _route(eid[:, :TOP_K], tm_moe, nb_moe)
        b_up_g = (b_up[i].astype(F32).reshape(N_EXPERTS, 2 * D_FF // 256, LANES, 2).transpose(0, 1, 3, 2)
                  .reshape(N_EXPERTS, 1, 2 * D_FF))
        y4 = _moe_call(m, be, nval, off, tok, dst, w_up[i], b_up_g, w_down[i],
                       b_down[i].reshape(N_EXPERTS, 1, D_MODEL), perm, tm_moe, nb_moe)

        hp, hs = _ple_call(h1, y4, gates,
                           p_prompt[i].reshape(tp, -1), p_sample[i].reshape(ts, -1), _row(norm_ple_g[i]),
                           w_ple_gate[i].astype(BF16), w_ple_proj[i].astype(BF16), _row(norm_final_g), tm)

    y_prompt = hp.reshape(bp, lp, d)
    y_sample = hs.reshape(bs, ls, d)
    return (y_prompt, y_sample, jnp.stack(ssm_p), jnp.stack(conv_p), jnp.stack(ssm_s), jnp.stack(conv_s),
            jnp.stack(v_s))
```

```python
import functools

import jax
import jax.numpy as jnp
from jax import lax
from jax.experimental import pallas as pl
from jax.experimental.pallas import tpu as pltpu

F32 = jnp.float32
BF16 = jnp.bfloat16
I32 = jnp.int32

EPS = 1e-6
D_MODEL = 1024
SSD_WIDTH = 512
SSD_HEADS = 8
HEAD_DIM = 64
SSD_GROUPS = 2
D_STATE = 128
CONV_W = 4
CONV_DIM = SSD_WIDTH + 2 * SSD_GROUPS * D_STATE
MLP_WIDTH = 512
MLP_HEADS = 8
N_EXPERTS = 32
TOP_K = 4
D_FF = 1024
SWIGLU_LIMIT = 7.0
SWIGLU_ALPHA = 1.702
TOPK_SHIFT = 2
assert 1 << TOPK_SHIFT == TOP_K
LANES = 128
CHUNK = 128
DT_PAD = LANES
TOKEN_TILE_ROWS = D_MODEL // LANES
MOE_BLOCK_ROWS = 256
IN_PAD = SSD_WIDTH + CONV_DIM + 2 * MLP_WIDTH + DT_PAD
VMEM_LIMIT = 56 * 1024 * 1024


def _cparams(sem):
    return pltpu.CompilerParams(dimension_semantics=sem, vmem_limit_bytes=VMEM_LIMIT)


def _const_spec(shape):
    return pl.BlockSpec(shape, lambda *_: (0,) * len(shape))


def _rms(x):
    return x * lax.rsqrt(jnp.mean(x * x, axis=-1, keepdims=True) + EPS)


def _dot(a, b):
    return jnp.dot(a, b, preferred_element_type=F32)


def _dot_nt(a, b):
    return lax.dot_general(a, b, (((1,), (1,)), ((), ())), preferred_element_type=F32)


def _split3(x):
    hi = x.astype(BF16)
    r = x - hi.astype(F32)
    mid = r.astype(BF16)
    lo = (r - mid.astype(F32)).astype(BF16)
    return hi, mid, lo


def _sel_right(x, m01):
    hi, mid, lo = _split3(x)
    return _dot(hi, m01) + _dot(mid, m01) + _dot(lo, m01)


def _sel_left(m01, x):
    hi, mid, lo = _split3(x)
    return _dot(m01, hi) + _dot(m01, mid) + _dot(m01, lo)


def _softplus(x):
    return jnp.maximum(x, 0.0) + jnp.log1p(jnp.exp(-jnp.abs(x)))


def _inproj_call(xp, xs, g, w, tm):
    tp, ts = xp.shape[0], xs.shape[0]
    n_p, n_s = tp // tm, ts // tm
    t_all = tp + ts
    segs = ((0, 512), (512, 1536), (1536, 2048), (2048, 2560), (2560, IN_PAD))

    def body(xp_ref, xs_ref, g_ref, w_ref, *outs):
        def run(x_ref):
            xn = (_rms(x_ref[...]) * g_ref[...]).astype(BF16)
            for (a, b), o in zip(segs, outs):
                o[...] = _dot(xn, w_ref[:, a:b])

        i = pl.program_id(0)

        @pl.when(i < n_p)
        def _():
            run(xp_ref)

        @pl.when(i >= n_p)
        def _():
            run(xs_ref)

    widths = [b - a for a, b in segs]
    return pl.pallas_call(
        body,
        out_shape=[jax.ShapeDtypeStruct((t_all, wd), F32) for wd in widths],
        grid=(n_p + n_s,),
        in_specs=[
            pl.BlockSpec((tm, D_MODEL), lambda i: (jnp.minimum(i, n_p - 1), 0)),
            pl.BlockSpec((tm, D_MODEL), lambda i: (jnp.maximum(i - n_p, 0), 0)),
            _const_spec((1, D_MODEL)),
            _const_spec((D_MODEL, IN_PAD)),
        ],
        out_specs=[pl.BlockSpec((tm, wd), lambda i: (i, 0)) for wd in widths],
        compiler_params=_cparams(("arbitrary",)),
        name="in_proj",
    )(xp, xs, g, w)


def _mixer_front(conv, dtr, dtb, alog, alog_x, rexp, tril, seg_ones):
    xact = conv * jax.nn.sigmoid(conv)
    xs = xact[:, :SSD_WIDTH]
    bm = xact[:, SSD_WIDTH:SSD_WIDTH + 256]
    cm = xact[:, SSD_WIDTH + 256:]
    dt = _softplus(dtr + dtb)
    a = dt * (-jnp.exp(alog))
    dt_x = _sel_right(dt, rexp)
    a_x = dt_x * (-jnp.exp(alog_x))
    acum = _sel_left(tril, a)
    acum_x = _sel_left(tril, a_x)
    if seg_ones is None:
        r = acum_x.shape[0]
        tot_x = jnp.broadcast_to(acum_x[r - 1:r, :], acum_x.shape)
    else:
        tot_x = _sel_left(seg_ones, a_x)
    return xs, bm, cm, dt_x, acum, acum_x, tot_x


def _ssd_intra(cmb, bmb, acum, xdt, mask):
    r = acum.shape[0]
    acum_t = acum.T
    lane = lax.broadcasted_iota(I32, (r, LANES), 1)
    low = lane < HEAD_DIM
    outs = []
    for g in range(SSD_GROUPS):
        sg = _dot_nt(cmb[:, LANES * g:LANES * (g + 1)], bmb[:, LANES * g:LANES * (g + 1)])
        for k in (2 * g, 2 * g + 1):
            parts = []
            for h in (2 * k, 2 * k + 1):
                seg = acum[:, h:h + 1] - acum_t[h:h + 1, :]
                parts.append((sg * jnp.exp(jnp.where(mask, seg, -jnp.inf))).astype(BF16))
            lhs = jnp.concatenate(parts, axis=1)
            xd = xdt[:, LANES * k:LANES * (k + 1)]
            rhs = jnp.concatenate([jnp.where(low, xd, 0.0), jnp.where(low, 0.0, xd)], axis=0).astype(BF16)
            outs.append(_dot(lhs, rhs))
    return jnp.concatenate(outs, axis=1)


def _mixer_back(y, z, u, v, sng, vng, vnb, wsp_ref, bsp, mog):
    r = y.shape[0]
    yg = y * (z * jax.nn.sigmoid(z))
    halves = []
    for g in range(SSD_GROUPS):
        t = yg[:, 256 * g:256 * (g + 1)]
        halves.append(_rms(t))
    yn = jnp.concatenate(halves, axis=1) * sng
    ug = jax.nn.gelu(u)
    vg = jax.nn.gelu(v)
    mu = jnp.mean(vg, axis=-1, keepdims=True)
    var = jnp.mean(jnp.square(vg - mu), axis=-1, keepdims=True)
    v_ln = (vg - mu) * lax.rsqrt(var + EPS) * vng + vnb
    lane = lax.broadcasted_iota(I32, (r, LANES), 1)
    low = lane < HEAD_DIM
    outs = []
    for k in range(MLP_HEADS // 2):
        vd = v_ln[:, LANES * k:LANES * (k + 1)]
        rhs = jnp.concatenate([jnp.where(low, vd, 0.0), jnp.where(low, 0.0, vd)], axis=0).astype(BF16)
        outs.append(_dot(wsp_ref[k], rhs))
    s = jnp.concatenate(outs, axis=1) + bsp
    m = _rms(ug * s) * mog
    return jnp.concatenate([yn, m], axis=1).astype(BF16), v_ln


_MIXER_PARAM_SHAPES = (
    (CONV_W, CONV_DIM), (1, CONV_DIM), (1, LANES), (1, LANES), (1, SSD_WIDTH), (LANES, SSD_WIDTH),
    (CHUNK, CHUNK), (1, SSD_WIDTH), (1, SSD_WIDTH), (1, MLP_WIDTH), (1, MLP_WIDTH),
    (MLP_HEADS // 2, CHUNK, 2 * CHUNK), (CHUNK, MLP_WIDTH), (1, MLP_WIDTH),
)


def _prompt_mixer_body(z_ref, xbc_ref, u_ref, v_ref, dt_ref,
                       cw_ref, cb_ref, dtb_ref, alog_ref, alogx_ref, rexp_ref, tril_ref, dskip_ref,
                       sng_ref, vng_ref, vnb_ref, wsp_ref, bsp_ref, mog_ref,
                       cat_ref, ssm_ref, ext_scr, s_scr):
    c = pl.program_id(1)
    r = CHUNK

    @pl.when(c == 0)
    def _():
        ext_scr[0:8, :] = jnp.zeros((8, CONV_DIM), F32)
        s_scr[...] = jnp.zeros_like(s_scr)

    x = xbc_ref[...]
    ext_scr[8:8 + r, :] = x
    cw = cw_ref[...]
    conv = (cb_ref[...] + cw[3:4] * x + cw[2:3] * ext_scr[7:7 + r, :]
            + cw[1:2] * ext_scr[6:6 + r, :] + cw[0:1] * ext_scr[5:5 + r, :])
    ext_scr[0:8, :] = x[r - 8:r, :]

    xs, bm, cm, dt_x, acum, acum_x, tot_x = _mixer_front(
        conv, dt_ref[...], dtb_ref[...], alog_ref[...], alogx_ref[...], rexp_ref[...], tril_ref[...], None)
    bmb, cmb = bm.astype(BF16), cm.astype(BF16)
    xdt = xs * dt_x
    row = lax.broadcasted_iota(I32, (r, r), 0)
    col = lax.broadcasted_iota(I32, (r, r), 1)
    y_diag = _ssd_intra(cmb, bmb, acum, xdt, row >= col)

    s_prev = s_scr[...]
    s_prev_b = s_prev.astype(BF16)
    y_off = jnp.concatenate(
        [_dot_nt(cmb[:, LANES * g:LANES * (g + 1)], s_prev_b[256 * g:256 * (g + 1), :]) for g in range(SSD_GROUPS)],
        axis=1)
    y = y_diag + y_off * jnp.exp(acum_x) + dskip_ref[...] * xs

    w_t = (xdt * jnp.exp(tot_x - acum_x)).T.astype(BF16)
    states = jnp.concatenate(
        [_dot(w_t[256 * g:256 * (g + 1), :], bmb[:, LANES * g:LANES * (g + 1)]) for g in range(SSD_GROUPS)], axis=0)
    s_new = s_prev * jnp.exp(tot_x).T + states
    s_scr[...] = s_new

    cat, _ = _mixer_back(y, z_ref[...], u_ref[...], v_ref[...], sng_ref[...], vng_ref[...], vnb_ref[...],
                         wsp_ref, bsp_ref[...], mog_ref[...])
    cat_ref[...] = cat

    @pl.when(c == pl.num_programs(1) - 1)
    def _():
        ssm_ref[0] = s_new


def _prompt_mixer_call(z, xbc, u, v, dtr, params, nb, nc):
    row = lambda b, c: (b * nc + c, 0)
    in_specs = [
        pl.BlockSpec((CHUNK, SSD_WIDTH), row), pl.BlockSpec((CHUNK, CONV_DIM), row),
        pl.BlockSpec((CHUNK, MLP_WIDTH), row), pl.BlockSpec((CHUNK, MLP_WIDTH), row),
        pl.BlockSpec((CHUNK, DT_PAD), row),
    ] + [_const_spec(s) for s in _MIXER_PARAM_SHAPES]
    return pl.pallas_call(
        _prompt_mixer_body,
        out_shape=[jax.ShapeDtypeStruct((nb * nc * CHUNK, D_MODEL), BF16),
                   jax.ShapeDtypeStruct((nb, SSD_WIDTH, D_STATE), F32)],
        grid=(nb, nc),
        in_specs=in_specs,
        out_specs=[pl.BlockSpec((CHUNK, D_MODEL), row),
                   pl.BlockSpec((1, SSD_WIDTH, D_STATE), lambda b, c: (b, 0, 0))],
        scratch_shapes=[pltpu.VMEM((CHUNK + 8, CONV_DIM), F32), pltpu.VMEM((SSD_WIDTH, D_STATE), F32)],
        compiler_params=_cparams(("arbitrary", "arbitrary")),
        name="prompt_mixer",
    )(z, xbc, u, v, dtr, *params)


def _sample_mixer_body(seq_len, z_ref, x0_ref, x1_ref, x2_ref, x3_ref, u_ref, v_ref, dt_ref, h_ref,
                       cw_ref, cb_ref, dtb_ref, alog_ref, alogx_ref, rexp_ref, tril_ref, dskip_ref,
                       sng_ref, vng_ref, vnb_ref, wsp_ref, bsp_ref, mog_ref, segones_ref,
                       cat_ref, vout_ref, hout_ref, cm_scr, bm_scr, wt_scr, dtt_scr, yoff_scr):
    r = CHUNK
    shift = seq_len.bit_length() - 1
    cw = cw_ref[...]
    conv = (cb_ref[...] + cw[3:4] * x0_ref[...] + cw[2:3] * x1_ref[...]
            + cw[1:2] * x2_ref[...] + cw[0:1] * x3_ref[...])
    xs, bm, cm, dt_x, acum, acum_x, tot_x = _mixer_front(
        conv, dt_ref[...], dtb_ref[...], alog_ref[...], alogx_ref[...], rexp_ref[...], tril_ref[...],
        segones_ref[...])
    bmb, cmb = bm.astype(BF16), cm.astype(BF16)
    xdt = xs * dt_x
    row = lax.broadcasted_iota(I32, (r, r), 0)
    col = lax.broadcasted_iota(I32, (r, r), 1)
    same = lax.shift_right_logical(row, shift) == lax.shift_right_logical(col, shift)
    y_diag = _ssd_intra(cmb, bmb, acum, xdt, same & (row >= col))

    cm_scr[...] = cm
    bm_scr[...] = bmb
    wt_scr[...] = (xdt * jnp.exp(tot_x - acum_x)).T
    dtt_scr[...] = jnp.exp(tot_x).T
    ones_b = jnp.ones((LANES, LANES), BF16)
    seqs_per_slab = 8 // seq_len

    def slab(j, carry):
        rows = pl.ds(pl.multiple_of(8 * j, 8), 8)
        cms = cm_scr[rows, :].astype(BF16)
        sub = lax.broadcasted_iota(I32, (8, 256), 0)
        lane = lax.broadcasted_iota(I32, (256, LANES), 1)
        for g in range(SSD_GROUPS):
            q_rows = slice(256 * g, 256 * (g + 1))
            acc = jnp.zeros((8, 256), F32)
            for q in range(seqs_per_slab):
                s = seqs_per_slab * j + q
                y_s = _dot_nt(cms[:, LANES * g:LANES * (g + 1)], h_ref[s, q_rows, :].astype(BF16))
                acc = jnp.where(lax.shift_right_logical(sub, shift) == q, y_s, acc)
            yoff_scr[rows, 256 * g:256 * (g + 1)] = acc
            for q in range(seqs_per_slab):
                s = seqs_per_slab * j + q
                w_sel = jnp.where(lax.shift_right_logical(lane, shift) == s, wt_scr[q_rows, :], 0.0).astype(BF16)
                st = _dot(w_sel, bm_scr[:, LANES * g:LANES * (g + 1)])
                d_sel = jnp.where(lane == s * seq_len, dtt_scr[q_rows, :], 0.0)
                hout_ref[s, q_rows, :] = h_ref[s, q_rows, :] * _sel_right(d_sel, ones_b) + st
        return carry

    lax.fori_loop(0, r // 8, slab, 0)

    y = y_diag + yoff_scr[...] * jnp.exp(acum_x) + dskip_ref[...] * xs
    cat, v_ln = _mixer_back(y, z_ref[...], u_ref[...], v_ref[...], sng_ref[...], vng_ref[...], vnb_ref[...],
                            wsp_ref, bsp_ref[...], mog_ref[...])
    cat_ref[...] = cat
    vout_ref[...] = v_ln


def _sample_mixer_call(z, x_shift, u, v, dtr, h0, params, seg_ones, row0, seq_len):
    ts = x_shift[0].shape[0]
    n = ts // CHUNK
    spt = CHUNK // seq_len
    off = lambda i: (row0 + i, 0)
    loc = lambda i: (i, 0)
    st3 = lambda i: (i, 0, 0)
    in_specs = (
        [pl.BlockSpec((CHUNK, SSD_WIDTH), off)]
        + [pl.BlockSpec((CHUNK, CONV_DIM), loc)] * 4
        + [pl.BlockSpec((CHUNK, MLP_WIDTH), off), pl.BlockSpec((CHUNK, MLP_WIDTH), off),
           pl.BlockSpec((CHUNK, DT_PAD), off), pl.BlockSpec((spt, SSD_WIDTH, D_STATE), st3)]
        + [_const_spec(s) for s in _MIXER_PARAM_SHAPES] + [_const_spec((CHUNK, CHUNK))])
    return pl.pallas_call(
        functools.partial(_sample_mixer_body, seq_len),
        out_shape=[jax.ShapeDtypeStruct((ts, D_MODEL), BF16), jax.ShapeDtypeStruct((ts, MLP_WIDTH), F32),
                   jax.ShapeDtypeStruct(h0.shape, F32)],
        grid=(n,),
        in_specs=in_specs,
        out_specs=[pl.BlockSpec((CHUNK, D_MODEL), loc), pl.BlockSpec((CHUNK, MLP_WIDTH), loc),
                   pl.BlockSpec((spt, SSD_WIDTH, D_STATE), st3)],
        scratch_shapes=[pltpu.VMEM((CHUNK, 256), F32), pltpu.VMEM((CHUNK, 256), BF16),
                        pltpu.VMEM((SSD_WIDTH, CHUNK), F32), pltpu.VMEM((SSD_WIDTH, CHUNK), F32),
                        pltpu.VMEM((CHUNK, SSD_WIDTH), F32)],
        compiler_params=_cparams(("arbitrary",)),
        name="sample_mixer",
    )(z, *x_shift, u, v, dtr, h0, *params, seg_ones)


def _out_router_call(cat_p, cat_s, xp, xs, w_out, g_moe, wr_hi, wr_lo, b_r, tm):
    tp, ts = xp.shape[0], xs.shape[0]
    n_p, n_s = tp // tm, ts // tm
    t_all = tp + ts

    def body(cp_ref, cs_ref, xp_ref, xs_ref, wo_ref, g_ref, wh_ref, wl_ref, br_ref,
             h1_ref, m_ref, eid_ref, gate_ref):
        def run(c_ref, x_ref):
            h1 = x_ref[...] + _dot(c_ref[...], wo_ref[...])
            h1_ref[...] = h1
            m = _rms(h1) * g_ref[...]
            for j in range(TOKEN_TILE_ROWS):
                m_ref[pl.ds(j, tm, stride=TOKEN_TILE_ROWS), :] = m[:, LANES * j:LANES * (j + 1)]
            m_hi = m.astype(BF16)
            m_lo = (m - m_hi.astype(F32)).astype(BF16)
            logits = _dot(m_hi, wh_ref[...]) + _dot(m_lo, wh_ref[...]) + _dot(m_hi, wl_ref[...]) + br_ref[...]
            lane = lax.broadcasted_iota(I32, logits.shape, 1).astype(F32)
            work = logits
            vals, ids = [], []
            for _ in range(TOP_K):
                mx = jnp.max(work, axis=-1, keepdims=True)
                idx = jnp.min(jnp.where(work == mx, lane, float(LANES)), axis=-1, keepdims=True)
                vals.append(mx)
                ids.append(idx)
                work = jnp.where(lane == idx, -jnp.inf, work)
            ex = [jnp.exp(vv - vals[0]) for vv in vals]
            den = ex[0] + ex[1] + ex[2] + ex[3]
            eid = jnp.zeros(logits.shape, I32)
            gate = jnp.zeros(logits.shape, F32)
            for k in range(TOP_K):
                eid = jnp.where(lane == k, ids[k].astype(I32), eid)
                gate = jnp.where(lane == k, ex[k] / den, gate)
            eid_ref[...] = eid
            gate_ref[...] = gate

        i = pl.program_id(0)

        @pl.when(i < n_p)
        def _():
            run(cp_ref, xp_ref)

        @pl.when(i >= n_p)
        def _():
            run(cs_ref, xs_ref)

    pmap = lambda i: (jnp.minimum(i, n_p - 1), 0)
    smap = lambda i: (jnp.maximum(i - n_p, 0), 0)
    omap = lambda i: (i, 0)
    return pl.pallas_call(
        body,
        out_shape=[jax.ShapeDtypeStruct((t_all, D_MODEL), F32),
                   jax.ShapeDtypeStruct((t_all * TOKEN_TILE_ROWS, LANES), F32),
                   jax.ShapeDtypeStruct((t_all, LANES), I32), jax.ShapeDtypeStruct((t_all, LANES), F32)],
        grid=(n_p + n_s,),
        in_specs=[pl.BlockSpec((tm, D_MODEL), pmap), pl.BlockSpec((tm, D_MODEL), smap),
                  pl.BlockSpec((tm, D_MODEL), pmap), pl.BlockSpec((tm, D_MODEL), smap),
                  _const_spec((D_MODEL, D_MODEL)), _const_spec((1, D_MODEL)),
                  _const_spec((D_MODEL, LANES)), _const_spec((D_MODEL, LANES)), _const_spec((1, LANES))],
        out_specs=[pl.BlockSpec((tm, D_MODEL), omap), pl.BlockSpec((tm * TOKEN_TILE_ROWS, LANES), omap),
                   pl.BlockSpec((tm, LANES), omap), pl.BlockSpec((tm, LANES), omap)],
        compiler_params=_cparams(("arbitrary",)),
        name="out_router",
    )(cat_p, cat_s, xp, xs, w_out, g_moe, wr_hi, wr_lo, b_r)


def _route(eid, tm, nb):
    t = eid.shape[0]
    tk = t * TOP_K
    flat = eid.reshape(tk)
    _, order = lax.sort((flat, jnp.arange(tk, dtype=I32)), num_keys=1, is_stable=True)
    counts = jnp.sum((flat[:, None] == jnp.arange(N_EXPERTS, dtype=I32)[None, :]).astype(I32), axis=0)
    nblk = (counts + tm - 1) // tm
    bend = jnp.cumsum(nblk)
    bstart = bend - nblk
    start = jnp.cumsum(counts) - counts
    nused = bend[-1]
    blk = jnp.arange(nb, dtype=I32)
    used = blk < nused
    be = jnp.minimum(jnp.sum((jnp.minimum(blk, nused - 1)[:, None] >= bend[None, :]).astype(I32), axis=1),
                     N_EXPERTS - 1)
    sel = (be[:, None] == jnp.arange(N_EXPERTS, dtype=I32)[None, :]).astype(I32)
    pick = lambda v: jnp.sum(sel * v[None, :], axis=1)
    done = (blk - pick(bstart)) * tm
    nval = jnp.where(used, jnp.clip(pick(counts) - done, 0, tm), 0).astype(I32)
    off = jnp.where(used, pick(start) + done, 0).astype(I32)
    pad = (-(-(tk + tm) // LANES) + _id_rows(tm)) * LANES - tk
    tok = jnp.pad(lax.shift_right_logical(order, TOPK_SHIFT) * TOKEN_TILE_ROWS, (0, pad))
    dst = jnp.pad(((order & (TOP_K - 1)) * t + lax.shift_right_logical(order, TOPK_SHIFT)) * TOKEN_TILE_ROWS,
                  (0, pad))
    return be, nval, off, tok, dst


def _id_rows(tm):
    return tm // LANES + 1


def _moe_body(tm, t_all, nb, be_ref, nval_ref, off_ref, tok_hbm, dst_hbm, m_hbm, wup_ref, bup_ref, wdn_ref, bdn_ref,
              perm_ref, y_hbm, gids, sids, xbuf, ybuf, wup_b, wdn_b, isem, gsem, ssem):
    i = pl.program_id(0)
    nv = nval_ref[i]
    slot = i & 1
    prv = jnp.maximum(i - 1, 0)
    nxt = jnp.minimum(i + 1, nb - 1)
    nx2 = jnp.minimum(i + 2, nb - 1)
    has_next = (i + 1 < nb) & (nval_ref[nxt] > 0)
    has_next2 = (i + 2 < nb) & (nval_ref[nx2] > 0)
    n_prev = jnp.where(i > 0, nval_ref[prv], 0)
    win = _id_rows(tm) * LANES
    spare = TOP_K * t_all * TOKEN_TILE_ROWS

    def ids_copies(b):
        start = pl.multiple_of(lax.shift_right_logical(off_ref[b], 7) * LANES, LANES)
        ring = pl.ds(pl.multiple_of((b & 3) * win, LANES), win)
        return (pltpu.make_async_copy(tok_hbm.at[pl.ds(start, win)], gids.at[ring], isem.at[b & 3, 0]),
                pltpu.make_async_copy(dst_hbm.at[pl.ds(start, win)], sids.at[ring], isem.at[b & 3, 1]))

    def id_base(b):
        return (b & 3) * win + (off_ref[b] & (LANES - 1))

    def gather_row(base, s, r):
        src = pl.ds(pl.multiple_of(gids[base + r], TOKEN_TILE_ROWS), TOKEN_TILE_ROWS)
        return pltpu.make_async_copy(m_hbm.at[src], xbuf.at[s, pl.ds(TOKEN_TILE_ROWS * r, TOKEN_TILE_ROWS)],
                                     gsem.at[s])

    def scatter_row(base, s, r, n):
        dest = jnp.where(r < n, sids[base + r], spare + TOKEN_TILE_ROWS * r)
        rows = pl.ds(pl.multiple_of(dest, TOKEN_TILE_ROWS), TOKEN_TILE_ROWS)
        return pltpu.make_async_copy(ybuf.at[s, pl.ds(TOKEN_TILE_ROWS * r, TOKEN_TILE_ROWS)], y_hbm.at[rows],
                                     ssem.at[s])

    def wait_gathers(s):
        pltpu.make_async_copy(m_hbm.at[pl.ds(0, tm * TOKEN_TILE_ROWS)], xbuf.at[s], gsem.at[s]).wait()

    def wait_scatters(s):
        pltpu.make_async_copy(ybuf.at[s], y_hbm.at[pl.ds(0, tm * TOKEN_TILE_ROWS)], ssem.at[s]).wait()

    def for_rows(fn):
        def one(r, c):
            fn(r)
            return c

        lax.fori_loop(0, tm, one, 0)

    @pl.when(nv > 0)
    def _():
        @pl.when(i == 0)
        def _():
            ybuf[...] = jnp.zeros_like(ybuf)
            fill = pltpu.make_async_copy(ybuf.at[0], y_hbm.at[pl.ds(spare, tm * TOKEN_TILE_ROWS)], ssem.at[0])
            fill.start()
            fill.wait()
            for b in range(4):
                for cp in ids_copies(b):
                    cp.start()
                    cp.wait()
            base0 = id_base(0)
            for_rows(lambda r: gather_row(base0, 0, r).start())

        @pl.when(has_next & (i >= 3))
        def _():
            for cp in ids_copies(nxt):
                cp.wait()

        @pl.when(has_next2 & (i >= 2))
        def _():
            for cp in ids_copies(nx2):
                cp.start()

        @pl.when((i == 0) | (be_ref[i] != be_ref[prv]))
        def _():
            for jb in range(2 * D_FF // 256):
                cols = slice(256 * jb, 256 * (jb + 1))
                wup_b[:, cols] = _dot(wup_ref[0, :, cols].astype(BF16), perm_ref[...]).astype(BF16)
            wdn_b[...] = wdn_ref[0].astype(BF16)

        y_cur = lax.rem(i, 3)
        y_prev = lax.rem(i + 2, 3)
        y_prev2 = lax.rem(i + 1, 3)

        def ffn_step(cur, oth):
            wait_gathers(cur)

            @pl.when(i >= 2)
            def _():
                wait_scatters(y_cur)

            g_base = id_base(nxt)
            s_base = id_base(prv)
            for r in range(tm):
                gather_row(g_base, oth, r).start()
                scatter_row(s_base, y_prev, r, n_prev).start()

            x = jnp.concatenate([xbuf[slot, pl.ds(j, tm, stride=TOKEN_TILE_ROWS), :]
                                 for j in range(TOKEN_TILE_ROWS)], axis=1).astype(BF16)
            acts = []
            for jb in range(D_FF // LANES):
                h = _dot(x, wup_b[:, 256 * jb:256 * (jb + 1)]) + bup_ref[0, :, 256 * jb:256 * (jb + 1)]
                gate = jnp.minimum(h[:, :LANES], SWIGLU_LIMIT)
                lin = jnp.clip(h[:, LANES:], -SWIGLU_LIMIT, SWIGLU_LIMIT)
                acts.append((gate * jax.nn.sigmoid(SWIGLU_ALPHA * gate) * (lin + 1.0)).astype(BF16))
            act = jnp.concatenate(acts, axis=1)
            for c in range(D_MODEL // 256):
                yc = _dot(act, wdn_b[:, 256 * c:256 * (c + 1)]) + bdn_ref[0, :, 256 * c:256 * (c + 1)]
                for half in range(2):
                    ybuf[y_cur, pl.ds(2 * c + half, tm, stride=TOKEN_TILE_ROWS), :] = (
                        yc[:, LANES * half:LANES * (half + 1)])

            @pl.when(jnp.logical_not(has_next))
            def _():
                wait_gathers(oth)

                @pl.when(i >= 1)
                def _():
                    wait_scatters(y_prev2)
                wait_scatters(y_prev)
                last_base = id_base(i)
                for_rows(lambda r: scatter_row(last_base, y_cur, r, nv).start())
                wait_scatters(y_cur)

        for parity in range(2):
            pl.when(slot == parity)(functools.partial(ffn_step, parity, 1 - parity))


def _moe_call(m, be, nval, off, tok, dst, w_up, b_up_g, w_down, b_down, perm, tm, nb):
    t = m.shape[0] // TOKEN_TILE_ROWS
    assert nb >= 4
    by_expert = lambda i, be, nv, off: (be[i], 0, 0)
    grid_spec = pltpu.PrefetchScalarGridSpec(
        num_scalar_prefetch=3,
        grid=(nb,),
        in_specs=[
            pl.BlockSpec(memory_space=pl.ANY),
            pl.BlockSpec(memory_space=pl.ANY),
            pl.BlockSpec(memory_space=pl.ANY),
            pl.BlockSpec((1, D_MODEL, 2 * D_FF), by_expert),
            pl.BlockSpec((1, 1, 2 * D_FF), by_expert),
            pl.BlockSpec((1, D_FF, D_MODEL), by_expert),
            pl.BlockSpec((1, 1, D_MODEL), by_expert),
            pl.BlockSpec((256, 256), lambda i, be, nv, off: (0, 0)),
        ],
        out_specs=pl.BlockSpec(memory_space=pl.ANY),
        scratch_shapes=[pltpu.SMEM((4 * _id_rows(tm) * LANES,), I32), pltpu.SMEM((4 * _id_rows(tm) * LANES,), I32),
                        pltpu.VMEM((2, tm * TOKEN_TILE_ROWS, LANES), F32),
                        pltpu.VMEM((3, tm * TOKEN_TILE_ROWS, LANES), F32),
                        pltpu.VMEM((D_MODEL, 2 * D_FF), BF16), pltpu.VMEM((D_FF, D_MODEL), BF16),
                        pltpu.SemaphoreType.DMA((4, 2)), pltpu.SemaphoreType.DMA((2,)),
                        pltpu.SemaphoreType.DMA((3,))],
    )
    return pl.pallas_call(
        functools.partial(_moe_body, tm, t, nb),
        out_shape=jax.ShapeDtypeStruct(((TOP_K * t + tm) * TOKEN_TILE_ROWS, LANES), F32),
        grid_spec=grid_spec,
        compiler_params=_cparams(("arbitrary",)),
        name="moe_experts",
    )(be, nval, off, tok, dst, m, w_up, b_up_g, w_down, b_down, perm)


def _ple_call(h1, y4, gates, pp, ps, g_ple, w_gate, w_proj, g_final, tm):
    tp, ts = pp.shape[0], ps.shape[0]
    n_p, n_s = tp // tm, ts // tm
    ple = pp.shape[1]

    def body(h1_ref, y0_ref, y1_ref, y2_ref, y3_ref, gt_ref, pp_ref, ps_ref, g_ref, wg_ref, wp_ref, gf_ref,
             yp_ref, ys_ref):
        def run(p_ref, o_ref):
            gt = gt_ref[...]
            moe = None
            for k, y_ref in enumerate((y0_ref, y1_ref, y2_ref, y3_ref)):
                y_k = jnp.concatenate([y_ref[pl.ds(j, tm, stride=TOKEN_TILE_ROWS), :]
                                       for j in range(TOKEN_TILE_ROWS)], axis=1)
                moe = gt[:, k:k + 1] * y_k if moe is None else moe + gt[:, k:k + 1] * y_k
            h2 = h1_ref[...] + moe
            a = (_rms(h2) * g_ref[...]).astype(BF16)
            gate = jax.nn.sigmoid(_dot(a, wg_ref[...]))
            pe = _dot(p_ref[...].astype(BF16), wp_ref[...])
            h3 = h2 + pe * gate
            o_ref[...] = _rms(h3) * gf_ref[...]

        i = pl.program_id(0)

        @pl.when(i < n_p)
        def _():
            run(pp_ref, yp_ref)

        @pl.when(i >= n_p)
        def _():
            run(ps_ref, ys_ref)

    pmap = lambda i: (jnp.minimum(i, n_p - 1), 0)
    smap = lambda i: (jnp.maximum(i - n_p, 0), 0)
    omap = lambda i: (i, 0)
    return pl.pallas_call(
        body,
        out_shape=[jax.ShapeDtypeStruct((tp, D_MODEL), F32), jax.ShapeDtypeStruct((ts, D_MODEL), F32)],
        grid=(n_p + n_s,),
        in_specs=[pl.BlockSpec((tm, D_MODEL), omap)]
                 + [pl.BlockSpec((tm * TOKEN_TILE_ROWS, LANES),
                                 functools.partial(lambda k, i: (k * (n_p + n_s) + i, 0), k)) for k in range(TOP_K)]
                 + [pl.BlockSpec((tm, LANES), omap), pl.BlockSpec((tm, ple), pmap), pl.BlockSpec((tm, ple), smap),
                  _const_spec((1, D_MODEL)), _const_spec((D_MODEL, D_MODEL)), _const_spec((ple, D_MODEL)),
                  _const_spec((1, D_MODEL))],
        out_specs=[pl.BlockSpec((tm, D_MODEL), pmap), pl.BlockSpec((tm, D_MODEL), smap)],
        compiler_params=_cparams(("arbitrary",)),
        name="ple_final",
    )(h1, y4, y4, y4, y4, gates, pp, ps, g_ple, w_gate, w_proj, g_final)


def _row(x, width=None):
    x = x.reshape(1, -1).astype(F32)
    if width is not None and x.shape[1] < width:
        x = jnp.pad(x, ((0, 0), (0, width - x.shape[1])))
    return x


def _mixer_params(conv_w, conv_b, dt_bias, a_log, d_skip, ssd_norm_g, v_norm_g, v_norm_b, w_spatial, b_spatial,
                  mlp_out_g, seq_len):
    pos = jnp.arange(CHUNK) % seq_len
    same = (jnp.arange(CHUNK)[:, None] // seq_len) == (jnp.arange(CHUNK)[None, :] // seq_len)
    tril = (same & (jnp.arange(CHUNK)[:, None] >= jnp.arange(CHUNK)[None, :])).astype(BF16)
    rexp = (jnp.arange(LANES)[:, None] == (jnp.arange(SSD_WIDTH)[None, :] // HEAD_DIM)).astype(BF16)
    w_loc = jnp.tril(w_spatial[:, :seq_len, :seq_len])
    onehot = (pos[:, None] == jnp.arange(seq_len)[None, :]).astype(F32)
    tiled = jnp.einsum("iq,hqr,jr->hij", onehot, w_loc.astype(F32), onehot, precision=lax.Precision.HIGHEST)
    w_bd = jnp.where(same[None], tiled, 0.0)
    wsp = (w_bd.reshape(MLP_HEADS // 2, 2, CHUNK, CHUNK).transpose(0, 2, 1, 3)
           .reshape(MLP_HEADS // 2, CHUNK, 2 * CHUNK).astype(BF16))
    bsp = jnp.repeat(b_spatial[:, :seq_len].T[pos], MLP_WIDTH // MLP_HEADS, axis=1)
    params = (
        conv_w.astype(F32), _row(conv_b), _row(dt_bias, LANES), _row(a_log, LANES),
        _row(jnp.repeat(a_log, HEAD_DIM)), rexp, tril, _row(jnp.repeat(d_skip, HEAD_DIM)),
        _row(ssd_norm_g), _row(v_norm_g), _row(v_norm_b), wsp, bsp.astype(F32), _row(mlp_out_g),
    )
    return params, same.astype(BF16)


def _tile_rows(n):
    return 512 if n % 512 == 0 else CHUNK


def kernel(x_prompt, x_sample, state_ssm, state_conv, p_prompt, p_sample, norm_mix_g, w_in, conv_w, conv_b, dt_bias, a_log, d_skip, ssd_norm_g, v_norm_g, v_norm_b, w_spatial, b_spatial, mlp_out_g, w_out, norm_moe_g, w_router, b_router, w_up, b_up, w_down, b_down, norm_ple_g, w_ple_gate, w_ple_proj, norm_final_g):
    depth = norm_mix_g.shape[0]
    bp, lp, d = x_prompt.shape
    bs, ls, _ = x_sample.shape
    tp, ts = bp * lp, bs * ls
    assert depth == 1 and d == D_MODEL and lp % CHUNK == 0 and ts % CHUNK == 0 and 8 % ls == 0
    tm = _tile_rows(tp) if ts % _tile_rows(tp) == 0 else CHUNK
    t_all = tp + ts
    tm_moe = MOE_BLOCK_ROWS
    nb_moe = -(-t_all * TOP_K // tm_moe) + N_EXPERTS

    hp = x_prompt.reshape(tp, d)
    hs = x_sample.reshape(ts, d)
    ssm_p, conv_p, ssm_s, conv_s, v_s = [], [], [], [], []
    o1 = SSD_WIDTH
    o2 = o1 + CONV_DIM
    o3 = o2 + SSD_HEADS
    c = jnp.arange(256)
    src = jnp.where(c < LANES, 2 * c, 2 * (c - LANES) + 1)
    perm = (jnp.arange(256)[:, None] == src[None, :]).astype(BF16)

    for i in range(depth):
        wi = w_in[i]
        w_cat = jnp.concatenate(
            [wi[:, :o2], wi[:, o3:], jnp.pad(wi[:, o2:o3], ((0, 0), (0, DT_PAD - SSD_HEADS)))], axis=1).astype(BF16)
        z, xbc, u, v, dtr = _inproj_call(hp, hs, _row(norm_mix_g[i]), w_cat, tm)

        mix_args = (conv_w[i], conv_b[i], dt_bias[i], a_log[i], d_skip[i], ssd_norm_g[i], v_norm_g[i], v_norm_b[i],
                    w_spatial[i], b_spatial[i], mlp_out_g[i])
        prm_p, _ = _mixer_params(*mix_args, seq_len=CHUNK)
        cat_p, s_p = _prompt_mixer_call(z, xbc, u, v, dtr, prm_p, bp, lp // CHUNK)
        ssm_p.append(s_p.reshape(bp, SSD_HEADS, HEAD_DIM, D_STATE).astype(state_ssm.dtype))
        conv_p.append(jnp.stack([xbc[(b + 1) * lp - (CONV_W - 1):(b + 1) * lp] for b in range(bp)]))

        prm_s, seg_ones = _mixer_params(*mix_args, seq_len=ls)
        xbc_s = xbc[tp:].reshape(bs, ls, CONV_DIM)
        xpad = jnp.concatenate([state_conv[i].astype(F32), xbc_s], axis=1)
        x_shift = [xpad[:, CONV_W - 1 - k:CONV_W - 1 - k + ls].reshape(ts, CONV_DIM) for k in range(CONV_W)]
        h0 = state_ssm[i].astype(F32).reshape(bs, SSD_WIDTH, D_STATE)
        cat_s, v_rows, s_s = _sample_mixer_call(z, x_shift, u, v, dtr, h0, prm_s, seg_ones, tp // CHUNK, ls)
        ssm_s.append(s_s.reshape(bs, SSD_HEADS, HEAD_DIM, D_STATE).astype(state_ssm.dtype))
        conv_s.append(xpad[:, ls:])
        v_s.append(v_rows.reshape(bs, ls, MLP_WIDTH))

        wr = jnp.pad(w_router[i].astype(F32), ((0, 0), (0, LANES - N_EXPERTS)))
        wr_hi = wr.astype(BF16)
        wr_lo = (wr - wr_hi.astype(F32)).astype(BF16)
        b_r = jnp.concatenate([b_router[i].astype(F32), jnp.full((LANES - N_EXPERTS,), -1e30, F32)]).reshape(1, LANES)
        h1, m, eid, gates = _out_router_call(cat_p, cat_s, hp, hs, w_out[i].astype(BF16), _row(norm_moe_g[i]),
                                             wr_hi, wr_lo, b_r, tm)

        be, nval, off, tok, dst = _route(eid[:, :TOP_K], tm_moe, nb_moe)
        b_up_g = (b_up[i].astype(F32).reshape(N_EXPERTS, 2 * D_FF // 256, LANES, 2).transpose(0, 1, 3, 2)
                  .reshape(N_EXPERTS, 1, 2 * D_FF))
        y4 = _moe_call(m, be, nval, off, tok, dst, w_up[i], b_up_g, w_down[i],
                       b_down[i].reshape(N_EXPERTS, 1, D_MODEL), perm, tm_moe, nb_moe)

        hp, hs = _ple_call(h1, y4, gates,
                           p_prompt[i].reshape(tp, -1), p_sample[i].reshape(ts, -1), _row(norm_ple_g[i]),
                           w_ple_gate[i].astype(BF16), w_ple_proj[i].astype(BF16), _row(norm_final_g), tm)

    y_prompt = hp.reshape(bp, lp, d)
    y_sample = hs.reshape(bs, ls, d)
    return (y_prompt, y_sample, jnp.stack(ssm_p), jnp.stack(conv_p), jnp.stack(ssm_s), jnp.stack(conv_s),
            jnp.stack(v_s))
```

```python
import functools

import jax
import jax.numpy as jnp
from jax import lax
from jax.experimental import pallas as pl
from jax.experimental.pallas import tpu as pltpu

F32 = jnp.float32
BF16 = jnp.bfloat16
I32 = jnp.int32

EPS = 1e-6
D_MODEL = 1024
SSD_WIDTH = 512
SSD_HEADS = 8
HEAD_DIM = 64
SSD_GROUPS = 2
D_STATE = 128
CONV_W = 4
CONV_DIM = SSD_WIDTH + 2 * SSD_GROUPS * D_STATE
MLP_WIDTH = 512
MLP_HEADS = 8
N_EXPERTS = 32
TOP_K = 4
D_FF = 1024
SWIGLU_LIMIT = 7.0
SWIGLU_ALPHA = 1.702
TOPK_SHIFT = 2
assert 1 << TOPK_SHIFT == TOP_K
LANES = 128
CHUNK = 128
DT_PAD = LANES
TOKEN_TILE_ROWS = D_MODEL // LANES
MOE_BLOCK_ROWS = 256
IN_PAD = SSD_WIDTH + CONV_DIM + 2 * MLP_WIDTH + DT_PAD
VMEM_LIMIT = 56 * 1024 * 1024


def _cparams(sem):
    return pltpu.CompilerParams(dimension_semantics=sem, vmem_limit_bytes=VMEM_LIMIT)


def _const_spec(shape):
    return pl.BlockSpec(shape, lambda *_: (0,) * len(shape))


def _rms(x):
    return x * lax.rsqrt(jnp.mean(x * x, axis=-1, keepdims=True) + EPS)


def _dot(a, b):
    return jnp.dot(a, b, preferred_element_type=F32)


def _dot_nt(a, b):
    return lax.dot_general(a, b, (((1,), (1,)), ((), ())), preferred_element_type=F32)


def _split3(x):
    hi = x.astype(BF16)
    r = x - hi.astype(F32)
    mid = r.astype(BF16)
    lo = (r - mid.astype(F32)).astype(BF16)
    return hi, mid, lo


def _sel_right(x, m01):
    hi, mid, lo = _split3(x)
    return _dot(hi, m01) + _dot(mid, m01) + _dot(lo, m01)


def _sel_left(m01, x):
    hi, mid, lo = _split3(x)
    return _dot(m01, hi) + _dot(m01, mid) + _dot(m01, lo)


def _softplus(x):
    return jnp.maximum(x, 0.0) + jnp.log1p(jnp.exp(-jnp.abs(x)))


def _inproj_call(xp, xs, g, w, tm):
    tp, ts = xp.shape[0], xs.shape[0]
    n_p, n_s = tp // tm, ts // tm
    t_all = tp + ts
    segs = ((0, 512), (512, 1536), (1536, 2048), (2048, 2560), (2560, IN_PAD))

    def body(xp_ref, xs_ref, g_ref, w_ref, *outs):
        def run(x_ref):
            xn = (_rms(x_ref[...]) * g_ref[...]).astype(BF16)
            for (a, b), o in zip(segs, outs):
                o[...] = _dot(xn, w_ref[:, a:b])

        i = pl.program_id(0)

        @pl.when(i < n_p)
        def _():
            run(xp_ref)

        @pl.when(i >= n_p)
        def _():
            run(xs_ref)

    widths = [b - a for a, b in segs]
    return pl.pallas_call(
        body,
        out_shape=[jax.ShapeDtypeStruct((t_all, wd), F32) for wd in widths],
        grid=(n_p + n_s,),
        in_specs=[
            pl.BlockSpec((tm, D_MODEL), lambda i: (jnp.minimum(i, n_p - 1), 0)),
            pl.BlockSpec((tm, D_MODEL), lambda i: (jnp.maximum(i - n_p, 0), 0)),
            _const_spec((1, D_MODEL)),
            _const_spec((D_MODEL, IN_PAD)),
        ],
        out_specs=[pl.BlockSpec((tm, wd), lambda i: (i, 0)) for wd in widths],
        compiler_params=_cparams(("arbitrary",)),
        name="in_proj",
    )(xp, xs, g, w)


def _mixer_front(conv, dtr, dtb, alog, alog_x, rexp, tril, seg_ones):
    xact = conv * jax.nn.sigmoid(conv)
    xs = xact[:, :SSD_WIDTH]
    bm = xact[:, SSD_WIDTH:SSD_WIDTH + 256]
    cm = xact[:, SSD_WIDTH + 256:]
    dt = _softplus(dtr + dtb)
    a = dt * (-jnp.exp(alog))
    dt_x = _sel_right(dt, rexp)
    a_x = dt_x * (-jnp.exp(alog_x))
    acum = _sel_left(tril, a)
    acum_x = _sel_left(tril, a_x)
    if seg_ones is None:
        r = acum_x.shape[0]
        tot_x = jnp.broadcast_to(acum_x[r - 1:r, :], acum_x.shape)
    else:
        tot_x = _sel_left(seg_ones, a_x)
    return xs, bm, cm, dt_x, acum, acum_x, tot_x


def _ssd_intra(cmb, bmb, acum, xdt, mask):
    r = acum.shape[0]
    acum_t = acum.T
    lane = lax.broadcasted_iota(I32, (r, LANES), 1)
    low = lane < HEAD_DIM
    outs = []
    for g in range(SSD_GROUPS):
        sg = _dot_nt(cmb[:, LANES * g:LANES * (g + 1)], bmb[:, LANES * g:LANES * (g + 1)])
        for k in (2 * g, 2 * g + 1):
            parts = []
            for h in (2 * k, 2 * k + 1):
                seg = acum[:, h:h + 1] - acum_t[h:h + 1, :]
                parts.append((sg * jnp.exp(jnp.where(mask, seg, -jnp.inf))).astype(BF16))
            lhs = jnp.concatenate(parts, axis=1)
            xd = xdt[:, LANES * k:LANES * (k + 1)]
            rhs = jnp.concatenate([jnp.where(low, xd, 0.0), jnp.where(low, 0.0, xd)], axis=0).astype(BF16)
            outs.append(_dot(lhs, rhs))
    return jnp.concatenate(outs, axis=1)


def _mixer_back(y, z, u, v, sng, vng, vnb, wsp_ref, bsp, mog):
    r = y.shape[0]
    yg = y * (z * jax.nn.sigmoid(z))
    halves = []
    for g in range(SSD_GROUPS):
        t = yg[:, 256 * g:256 * (g + 1)]
        halves.append(_rms(t))
    yn = jnp.concatenate(halves, axis=1) * sng
    ug = jax.nn.gelu(u)
    vg = jax.nn.gelu(v)
    mu = jnp.mean(vg, axis=-1, keepdims=True)
    var = jnp.mean(jnp.square(vg - mu), axis=-1, keepdims=True)
    v_ln = (vg - mu) * lax.rsqrt(var + EPS) * vng + vnb
    lane = lax.broadcasted_iota(I32, (r, LANES), 1)
    low = lane < HEAD_DIM
    outs = []
    for k in range(MLP_HEADS // 2):
        vd = v_ln[:, LANES * k:LANES * (k + 1)]
        rhs = jnp.concatenate([jnp.where(low, vd, 0.0), jnp.where(low, 0.0, vd)], axis=0).astype(BF16)
        outs.append(_dot(wsp_ref[k], rhs))
    s = jnp.concatenate(outs, axis=1) + bsp
    m = _rms(ug * s) * mog
    return jnp.concatenate([yn, m], axis=1).astype(BF16), v_ln


_MIXER_PARAM_SHAPES = (
    (CONV_W, CONV_DIM), (1, CONV_DIM), (1, LANES), (1, LANES), (1, SSD_WIDTH), (LANES, SSD_WIDTH),
    (CHUNK, CHUNK), (1, SSD_WIDTH), (1, SSD_WIDTH), (1, MLP_WIDTH), (1, MLP_WIDTH),
    (MLP_HEADS // 2, CHUNK, 2 * CHUNK), (CHUNK, MLP_WIDTH), (1, MLP_WIDTH),
)


def _prompt_mixer_body(z_ref, xbc_ref, u_ref, v_ref, dt_ref,
                       cw_ref, cb_ref, dtb_ref, alog_ref, alogx_ref, rexp_ref, tril_ref, dskip_ref,
                       sng_ref, vng_ref, vnb_ref, wsp_ref, bsp_ref, mog_ref,
                       cat_ref, ssm_ref, ext_scr, s_scr):
    c = pl.program_id(1)
    r = CHUNK

    @pl.when(c == 0)
    def _():
        ext_scr[0:8, :] = jnp.zeros((8, CONV_DIM), F32)
        s_scr[...] = jnp.zeros_like(s_scr)

    x = xbc_ref[...]
    ext_scr[8:8 + r, :] = x
    cw = cw_ref[...]
    conv = (cb_ref[...] + cw[3:4] * x + cw[2:3] * ext_scr[7:7 + r, :]
            + cw[1:2] * ext_scr[6:6 + r, :] + cw[0:1] * ext_scr[5:5 + r, :])
    ext_scr[0:8, :] = x[r - 8:r, :]

    xs, bm, cm, dt_x, acum, acum_x, tot_x = _mixer_front(
        conv, dt_ref[...], dtb_ref[...], alog_ref[...], alogx_ref[...], rexp_ref[...], tril_ref[...], None)
    bmb, cmb = bm.astype(BF16), cm.astype(BF16)
    xdt = xs * dt_x
    row = lax.broadcasted_iota(I32, (r, r), 0)
    col = lax.broadcasted_iota(I32, (r, r), 1)
    y_diag = _ssd_intra(cmb, bmb, acum, xdt, row >= col)

    s_prev = s_scr[...]
    s_prev_b = s_prev.astype(BF16)
    y_off = jnp.concatenate(
        [_dot_nt(cmb[:, LANES * g:LANES * (g + 1)], s_prev_b[256 * g:256 * (g + 1), :]) for g in range(SSD_GROUPS)],
        axis=1)
    y = y_diag + y_off * jnp.exp(acum_x) + dskip_ref[...] * xs

    w_t = (xdt * jnp.exp(tot_x - acum_x)).T.astype(BF16)
    states = jnp.concatenate(
        [_dot(w_t[256 * g:256 * (g + 1), :], bmb[:, LANES * g:LANES * (g + 1)]) for g in range(SSD_GROUPS)], axis=0)
    s_new = s_prev * jnp.exp(tot_x).T + states
    s_scr[...] = s_new

    cat, _ = _mixer_back(y, z_ref[...], u_ref[...], v_ref[...], sng_ref[...], vng_ref[...], vnb_ref[...],
                         wsp_ref, bsp_ref[...], mog_ref[...])
    cat_ref[...] = cat

    @pl.when(c == pl.num_programs(1) - 1)
    def _():
        ssm_ref[0] = s_new


def _prompt_mixer_call(z, xbc, u, v, dtr, params, nb, nc):
    row = lambda b, c: (b * nc + c, 0)
    in_specs = [
        pl.BlockSpec((CHUNK, SSD_WIDTH), row), pl.BlockSpec((CHUNK, CONV_DIM), row),
        pl.BlockSpec((CHUNK, MLP_WIDTH), row), pl.BlockSpec((CHUNK, MLP_WIDTH), row),
        pl.BlockSpec((CHUNK, DT_PAD), row),
    ] + [_const_spec(s) for s in _MIXER_PARAM_SHAPES]
    return pl.pallas_call(
        _prompt_mixer_body,
        out_shape=[jax.ShapeDtypeStruct((nb * nc * CHUNK, D_MODEL), BF16),
                   jax.ShapeDtypeStruct((nb, SSD_WIDTH, D_STATE), F32)],
        grid=(nb, nc),
        in_specs=in_specs,
        out_specs=[pl.BlockSpec((CHUNK, D_MODEL), row),
                   pl.BlockSpec((1, SSD_WIDTH, D_STATE), lambda b, c: (b, 0, 0))],
        scratch_shapes=[pltpu.VMEM((CHUNK + 8, CONV_DIM), F32), pltpu.VMEM((SSD_WIDTH, D_STATE), F32)],
        compiler_params=_cparams(("arbitrary", "arbitrary")),
        name="prompt_mixer",
    )(z, xbc, u, v, dtr, *params)


def _sample_mixer_body(seq_len, z_ref, x0_ref, x1_ref, x2_ref, x3_ref, u_ref, v_ref, dt_ref, h_ref,
                       cw_ref, cb_ref, dtb_ref, alog_ref, alogx_ref, rexp_ref, tril_ref, dskip_ref,
                       sng_ref, vng_ref, vnb_ref, wsp_ref, bsp_ref, mog_ref, segones_ref,
                       cat_ref, vout_ref, hout_ref, cm_scr, bm_scr, wt_scr, dtt_scr, yoff_scr):
    r = CHUNK
    shift = seq_len.bit_length() - 1
    cw = cw_ref[...]
    conv = (cb_ref[...] + cw[3:4] * x0_ref[...] + cw[2:3] * x1_ref[...]
            + cw[1:2] * x2_ref[...] + cw[0:1] * x3_ref[...])
    xs, bm, cm, dt_x, acum, acum_x, tot_x = _mixer_front(
        conv, dt_ref[...], dtb_ref[...], alog_ref[...], alogx_ref[...], rexp_ref[...], tril_ref[...],
        segones_ref[...])
    bmb, cmb = bm.astype(BF16), cm.astype(BF16)
    xdt = xs * dt_x
    row = lax.broadcasted_iota(I32, (r, r), 0)
    col = lax.broadcasted_iota(I32, (r, r), 1)
    same = lax.shift_right_logical(row, shift) == lax.shift_right_logical(col, shift)
    y_diag = _ssd_intra(cmb, bmb, acum, xdt, same & (row >= col))

    cm_scr[...] = cm
    bm_scr[...] = bmb
    wt_scr[...] = (xdt * jnp.exp(tot_x - acum_x)).T
    dtt_scr[...] = jnp.exp(tot_x).T
    ones_b = jnp.ones((LANES, LANES), BF16)
    seqs_per_slab = 8 // seq_len

    def slab(j, carry):
        rows = pl.ds(pl.multiple_of(8 * j, 8), 8)
        cms = cm_scr[rows, :].astype(BF16)
        sub = lax.broadcasted_iota(I32, (8, 256), 0)
        lane = lax.broadcasted_iota(I32, (256, LANES), 1)
        for g in range(SSD_GROUPS):
            q_rows = slice(256 * g, 256 * (g + 1))
            acc = jnp.zeros((8, 256), F32)
            for q in range(seqs_per_slab):
                s = seqs_per_slab * j + q
                y_s = _dot_nt(cms[:, LANES * g:LANES * (g + 1)], h_ref[s, q_rows, :].astype(BF16))
                acc = jnp.where(lax.shift_right_logical(sub, shift) == q, y_s, acc)
            yoff_scr[rows, 256 * g:256 * (g + 1)] = acc
            for q in range(seqs_per_slab):
                s = seqs_per_slab * j + q
                w_sel = jnp.where(lax.shift_right_logical(lane, shift) == s, wt_scr[q_rows, :], 0.0).astype(BF16)
                st = _dot(w_sel, bm_scr[:, LANES * g:LANES * (g + 1)])
                d_sel = jnp.where(lane == s * seq_len, dtt_scr[q_rows, :], 0.0)
                hout_ref[s, q_rows, :] = h_ref[s, q_rows, :] * _sel_right(d_sel, ones_b) + st
        return carry

    lax.fori_loop(0, r // 8, slab, 0)

    y = y_diag + yoff_scr[...] * jnp.exp(acum_x) + dskip_ref[...] * xs
    cat, v_ln = _mixer_back(y, z_ref[...], u_ref[...], v_ref[...], sng_ref[...], vng_ref[...], vnb_ref[...],
                            wsp_ref, bsp_ref[...], mog_ref[...])
    cat_ref[...] = cat
    vout_ref[...] = v_ln


def _sample_mixer_call(z, x_shift, u, v, dtr, h0, params, seg_ones, row0, seq_len):
    ts = x_shift[0].shape[0]
    n = ts // CHUNK
    spt = CHUNK // seq_len
    off = lambda i: (row0 + i, 0)
    loc = lambda i: (i, 0)
    st3 = lambda i: (i, 0, 0)
    in_specs = (
        [pl.BlockSpec((CHUNK, SSD_WIDTH), off)]
        + [pl.BlockSpec((CHUNK, CONV_DIM), loc)] * 4
        + [pl.BlockSpec((CHUNK, MLP_WIDTH), off), pl.BlockSpec((CHUNK, MLP_WIDTH), off),
           pl.BlockSpec((CHUNK, DT_PAD), off), pl.BlockSpec((spt, SSD_WIDTH, D_STATE), st3)]
        + [_const_spec(s) for s in _MIXER_PARAM_SHAPES] + [_const_spec((CHUNK, CHUNK))])
    return pl.pallas_call(
        functools.partial(_sample_mixer_body, seq_len),
        out_shape=[jax.ShapeDtypeStruct((ts, D_MODEL), BF16), jax.ShapeDtypeStruct((ts, MLP_WIDTH), F32),
                   jax.ShapeDtypeStruct(h0.shape, F32)],
        grid=(n,),
        in_specs=in_specs,
        out_specs=[pl.BlockSpec((CHUNK, D_MODEL), loc), pl.BlockSpec((CHUNK, MLP_WIDTH), loc),
                   pl.BlockSpec((spt, SSD_WIDTH, D_STATE), st3)],
        scratch_shapes=[pltpu.VMEM((CHUNK, 256), F32), pltpu.VMEM((CHUNK, 256), BF16),
                        pltpu.VMEM((SSD_WIDTH, CHUNK), F32), pltpu.VMEM((SSD_WIDTH, CHUNK), F32),
                        pltpu.VMEM((CHUNK, SSD_WIDTH), F32)],
        compiler_params=_cparams(("arbitrary",)),
        name="sample_mixer",
    )(z, *x_shift, u, v, dtr, h0, *params, seg_ones)


def _out_router_call(cat_p, cat_s, xp, xs, w_out, g_moe, wr_hi, wr_lo, b_r, tm):
    tp, ts = xp.shape[0], xs.shape[0]
    n_p, n_s = tp // tm, ts // tm
    t_all = tp + ts

    def body(cp_ref, cs_ref, xp_ref, xs_ref, wo_ref, g_ref, wh_ref, wl_ref, br_ref,
             h1_ref, m_ref, eid_ref, gate_ref):
        def run(c_ref, x_ref):
            h1 = x_ref[...] + _dot(c_ref[...], wo_ref[...])
            h1_ref[...] = h1
            m = _rms(h1) * g_ref[...]
            for j in range(TOKEN_TILE_ROWS):
                m_ref[pl.ds(j, tm, stride=TOKEN_TILE_ROWS), :] = m[:, LANES * j:LANES * (j + 1)]
            m_hi = m.astype(BF16)
            m_lo = (m - m_hi.astype(F32)).astype(BF16)
            logits = _dot(m_hi, wh_ref[...]) + _dot(m_lo, wh_ref[...]) + _dot(m_hi, wl_ref[...]) + br_ref[...]
            lane = lax.broadcasted_iota(I32, logits.shape, 1).astype(F32)
            work = logits
            vals, ids = [], []
            for _ in range(TOP_K):
                mx = jnp.max(work, axis=-1, keepdims=True)
                idx = jnp.min(jnp.where(work == mx, lane, float(LANES)), axis=-1, keepdims=True)
                vals.append(mx)
                ids.append(idx)
                work = jnp.where(lane == idx, -jnp.inf, work)
            ex = [jnp.exp(vv - vals[0]) for vv in vals]
            den = ex[0] + ex[1] + ex[2] + ex[3]
            eid = jnp.zeros(logits.shape, I32)
            gate = jnp.zeros(logits.shape, F32)
            for k in range(TOP_K):
                eid = jnp.where(lane == k, ids[k].astype(I32), eid)
                gate = jnp.where(lane == k, ex[k] / den, gate)
            eid_ref[...] = eid
            gate_ref[...] = gate

        i = pl.program_id(0)

        @pl.when(i < n_p)
        def _():
            run(cp_ref, xp_ref)

        @pl.when(i >= n_p)
        def _():
            run(cs_ref, xs_ref)

    pmap = lambda i: (jnp.minimum(i, n_p - 1), 0)
    smap = lambda i: (jnp.maximum(i - n_p, 0), 0)
    omap = lambda i: (i, 0)
    return pl.pallas_call(
        body,
        out_shape=[jax.ShapeDtypeStruct((t_all, D_MODEL), F32),
                   jax.ShapeDtypeStruct((t_all * TOKEN_TILE_ROWS, LANES), F32),
                   jax.ShapeDtypeStruct((t_all, LANES), I32), jax.ShapeDtypeStruct((t_all, LANES), F32)],
        grid=(n_p + n_s,),
        in_specs=[pl.BlockSpec((tm, D_MODEL), pmap), pl.BlockSpec((tm, D_MODEL), smap),
                  pl.BlockSpec((tm, D_MODEL), pmap), pl.BlockSpec((tm, D_MODEL), smap),
                  _const_spec((D_MODEL, D_MODEL)), _const_spec((1, D_MODEL)),
                  _const_spec((D_MODEL, LANES)), _const_spec((D_MODEL, LANES)), _const_spec((1, LANES))],
        out_specs=[pl.BlockSpec((tm, D_MODEL), omap), pl.BlockSpec((tm * TOKEN_TILE_ROWS, LANES), omap),
                   pl.BlockSpec((tm, LANES), omap), pl.BlockSpec((tm, LANES), omap)],
        compiler_params=_cparams(("arbitrary",)),
        name="out_router",
    )(cat_p, cat_s, xp, xs, w_out, g_moe, wr_hi, wr_lo, b_r)


def _route(eid, tm, nb):
    t = eid.shape[0]
    tk = t * TOP_K
    flat = eid.reshape(tk)
    _, order = lax.sort((flat, jnp.arange(tk, dtype=I32)), num_keys=1, is_stable=True)
    counts = jnp.sum((flat[:, None] == jnp.arange(N_EXPERTS, dtype=I32)[None, :]).astype(I32), axis=0)
    nblk = (counts + tm - 1) // tm
    bend = jnp.cumsum(nblk)
    bstart = bend - nblk
    start = jnp.cumsum(counts) - counts
    nused = bend[-1]
    blk = jnp.arange(nb, dtype=I32)
    used = blk < nused
    be = jnp.minimum(jnp.sum((jnp.minimum(blk, nused - 1)[:, None] >= bend[None, :]).astype(I32), axis=1),
                     N_EXPERTS - 1)
    sel = (be[:, None] == jnp.arange(N_EXPERTS, dtype=I32)[None, :]).astype(I32)
    pick = lambda v: jnp.sum(sel * v[None, :], axis=1)
    done = (blk - pick(bstart)) * tm
    nval = jnp.where(used, jnp.clip(pick(counts) - done, 0, tm), 0).astype(I32)
    off = jnp.where(used, pick(start) + done, 0).astype(I32)
    pad = (-(-(tk + tm) // LANES) + _id_rows(tm)) * LANES - tk
    tok = jnp.pad(lax.shift_right_logical(order, TOPK_SHIFT) * TOKEN_TILE_ROWS, (0, pad))
    dst = jnp.pad(((order & (TOP_K - 1)) * t + lax.shift_right_logical(order, TOPK_SHIFT)) * TOKEN_TILE_ROWS,
                  (0, pad))
    return be, nval, off, tok, dst


def _id_rows(tm):
    return tm // LANES + 1


def _moe_body(tm, t_all, nb, be_ref, nval_ref, off_ref, tok_hbm, dst_hbm, m_hbm, wup_ref, bup_ref, wdn_ref, bdn_ref,
              perm_ref, y_hbm, gids, sids, xbuf, ybuf, wup_b, wdn_b, isem, gsem, ssem):
    i = pl.program_id(0)
    nv = nval_ref[i]
    slot = i & 1
    prv = jnp.maximum(i - 1, 0)
    nxt = jnp.minimum(i + 1, nb - 1)
    nx2 = jnp.minimum(i + 2, nb - 1)
    has_next = (i + 1 < nb) & (nval_ref[nxt] > 0)
    has_next2 = (i + 2 < nb) & (nval_ref[nx2] > 0)
    n_prev = jnp.where(i > 0, nval_ref[prv], 0)
    win = _id_rows(tm) * LANES
    spare = TOP_K * t_all * TOKEN_TILE_ROWS

    def ids_copies(b):
        start = pl.multiple_of(lax.shift_right_logical(off_ref[b], 7) * LANES, LANES)
        ring = pl.ds(pl.multiple_of((b & 3) * win, LANES), win)
        return (pltpu.make_async_copy(tok_hbm.at[pl.ds(start, win)], gids.at[ring], isem.at[b & 3, 0]),
                pltpu.make_async_copy(dst_hbm.at[pl.ds(start, win)], sids.at[ring], isem.at[b & 3, 1]))

    def id_base(b):
        return (b & 3) * win + (off_ref[b] & (LANES - 1))

    def gather_row(base, s, r):
        src = pl.ds(pl.multiple_of(gids[base + r], TOKEN_TILE_ROWS), TOKEN_TILE_ROWS)
        return pltpu.make_async_copy(m_hbm.at[src], xbuf.at[s, pl.ds(TOKEN_TILE_ROWS * r, TOKEN_TILE_ROWS)],
                                     gsem.at[s])

    def scatter_row(base, s, r, n):
        dest = jnp.where(r < n, sids[base + r], spare + TOKEN_TILE_ROWS * r)
        rows = pl.ds(pl.multiple_of(dest, TOKEN_TILE_ROWS), TOKEN_TILE_ROWS)
        return pltpu.make_async_copy(ybuf.at[s, pl.ds(TOKEN_TILE_ROWS * r, TOKEN_TILE_ROWS)], y_hbm.at[rows],
                                     ssem.at[s])

    def wait_gathers(s):
        pltpu.make_async_copy(m_hbm.at[pl.ds(0, tm * TOKEN_TILE_ROWS)], xbuf.at[s], gsem.at[s]).wait()

    def wait_scatters(s):
        pltpu.make_async_copy(ybuf.at[s], y_hbm.at[pl.ds(0, tm * TOKEN_TILE_ROWS)], ssem.at[s]).wait()

    def for_rows(fn):
        def one(r, c):
            fn(r)
            return c

        lax.fori_loop(0, tm, one, 0)

    @pl.when(nv > 0)
    def _():
        @pl.when(i == 0)
        def _():
            ybuf[...] = jnp.zeros_like(ybuf)
            fill = pltpu.make_async_copy(ybuf.at[0], y_hbm.at[pl.ds(spare, tm * TOKEN_TILE_ROWS)], ssem.at[0])
            fill.start()
            fill.wait()
            for b in range(4):
                for cp in ids_copies(b):
                    cp.start()
                    cp.wait()
            base0 = id_base(0)
            for_rows(lambda r: gather_row(base0, 0, r).start())

        @pl.when(has_next & (i >= 3))
        def _():
            for cp in ids_copies(nxt):
                cp.wait()

        @pl.when(has_next2 & (i >= 2))
        def _():
            for cp in ids_copies(nx2):
                cp.start()

        @pl.when((i == 0) | (be_ref[i] != be_ref[prv]))
        def _():
            for jb in range(2 * D_FF // 256):
                cols = slice(256 * jb, 256 * (jb + 1))
                wup_b[:, cols] = _dot(wup_ref[0, :, cols].astype(BF16), perm_ref[...]).astype(BF16)
            wdn_b[...] = wdn_ref[0].astype(BF16)

        y_cur = lax.rem(i, 3)
        y_prev = lax.rem(i + 2, 3)
        y_prev2 = lax.rem(i + 1, 3)

        def ffn_step(cur, oth):
            wait_gathers(cur)

            @pl.when(i >= 2)
            def _():
                wait_scatters(y_cur)

            g_base = id_base(nxt)
            s_base = id_base(prv)
            for r in range(tm):
                gather_row(g_base, oth, r).start(priority=1)
                scatter_row(s_base, y_prev, r, n_prev).start()

            x = jnp.concatenate([xbuf[slot, pl.ds(j, tm, stride=TOKEN_TILE_ROWS), :]
                                 for j in range(TOKEN_TILE_ROWS)], axis=1).astype(BF16)
            acts = []
            for jb in range(D_FF // LANES):
                h = _dot(x, wup_b[:, 256 * jb:256 * (jb + 1)]) + bup_ref[0, :, 256 * jb:256 * (jb + 1)]
                gate = jnp.minimum(h[:, :LANES], SWIGLU_LIMIT)
                lin = jnp.clip(h[:, LANES:], -SWIGLU_LIMIT, SWIGLU_LIMIT)
                acts.append((gate * jax.nn.sigmoid(SWIGLU_ALPHA * gate) * (lin + 1.0)).astype(BF16))
            act = jnp.concatenate(acts, axis=1)
            for c in range(D_MODEL // 256):
                yc = _dot(act, wdn_b[:, 256 * c:256 * (c + 1)]) + bdn_ref[0, :, 256 * c:256 * (c + 1)]
                for half in range(2):
                    ybuf[y_cur, pl.ds(2 * c + half, tm, stride=TOKEN_TILE_ROWS), :] = (
                        yc[:, LANES * half:LANES * (half + 1)])

            @pl.when(jnp.logical_not(has_next))
            def _():
                wait_gathers(oth)

                @pl.when(i >= 1)
                def _():
                    wait_scatters(y_prev2)
                wait_scatters(y_prev)
                last_base = id_base(i)
                for_rows(lambda r: scatter_row(last_base, y_cur, r, nv).start())
                wait_scatters(y_cur)

        for parity in range(2):
            pl.when(slot == parity)(functools.partial(ffn_step, parity, 1 - parity))


def _moe_call(m, be, nval, off, tok, dst, w_up, b_up_g, w_down, b_down, perm, tm, nb):
    t = m.shape[0] // TOKEN_TILE_ROWS
    assert nb >= 4
    by_expert = lambda i, be, nv, off: (be[i], 0, 0)
    grid_spec = pltpu.PrefetchScalarGridSpec(
        num_scalar_prefetch=3,
        grid=(nb,),
        in_specs=[
            pl.BlockSpec(memory_space=pl.ANY),
            pl.BlockSpec(memory_space=pl.ANY),
            pl.BlockSpec(memory_space=pl.ANY),
            pl.BlockSpec((1, D_MODEL, 2 * D_FF), by_expert),
            pl.BlockSpec((1, 1, 2 * D_FF), by_expert),
            pl.BlockSpec((1, D_FF, D_MODEL), by_expert),
            pl.BlockSpec((1, 1, D_MODEL), by_expert),
            pl.BlockSpec((256, 256), lambda i, be, nv, off: (0, 0)),
        ],
        out_specs=pl.BlockSpec(memory_space=pl.ANY),
        scratch_shapes=[pltpu.SMEM((4 * _id_rows(tm) * LANES,), I32), pltpu.SMEM((4 * _id_rows(tm) * LANES,), I32),
                        pltpu.VMEM((2, tm * TOKEN_TILE_ROWS, LANES), F32),
                        pltpu.VMEM((3, tm * TOKEN_TILE_ROWS, LANES), F32),
                        pltpu.VMEM((D_MODEL, 2 * D_FF), BF16), pltpu.VMEM((D_FF, D_MODEL), BF16),
                        pltpu.SemaphoreType.DMA((4, 2)), pltpu.SemaphoreType.DMA((2,)),
                        pltpu.SemaphoreType.DMA((3,))],
    )
    return pl.pallas_call(
        functools.partial(_moe_body, tm, t, nb),
        out_shape=jax.ShapeDtypeStruct(((TOP_K * t + tm) * TOKEN_TILE_ROWS, LANES), F32),
        grid_spec=grid_spec,
        compiler_params=_cparams(("arbitrary",)),
        name="moe_experts",
    )(be, nval, off, tok, dst, m, w_up, b_up_g, w_down, b_down, perm)


def _ple_call(h1, y4, gates, pp, ps, g_ple, w_gate, w_proj, g_final, tm):
    tp, ts = pp.shape[0], ps.shape[0]
    n_p, n_s = tp // tm, ts // tm
    ple = pp.shape[1]

    def body(h1_ref, y0_ref, y1_ref, y2_ref, y3_ref, gt_ref, pp_ref, ps_ref, g_ref, wg_ref, wp_ref, gf_ref,
             yp_ref, ys_ref):
        def run(p_ref, o_ref):
            gt = gt_ref[...]
            moe = None
            for k, y_ref in enumerate((y0_ref, y1_ref, y2_ref, y3_ref)):
                y_k = jnp.concatenate([y_ref[pl.ds(j, tm, stride=TOKEN_TILE_ROWS), :]
                                       for j in range(TOKEN_TILE_ROWS)], axis=1)
                moe = gt[:, k:k + 1] * y_k if moe is None else moe + gt[:, k:k + 1] * y_k
            h2 = h1_ref[...] + moe
            a = (_rms(h2) * g_ref[...]).astype(BF16)
            gate = jax.nn.sigmoid(_dot(a, wg_ref[...]))
            pe = _dot(p_ref[...].astype(BF16), wp_ref[...])
            h3 = h2 + pe * gate
            o_ref[...] = _rms(h3) * gf_ref[...]

        i = pl.program_id(0)

        @pl.when(i < n_p)
        def _():
            run(pp_ref, yp_ref)

        @pl.when(i >= n_p)
        def _():
            run(ps_ref, ys_ref)

    pmap = lambda i: (jnp.minimum(i, n_p - 1), 0)
    smap = lambda i: (jnp.maximum(i - n_p, 0), 0)
    omap = lambda i: (i, 0)
    return pl.pallas_call(
        body,
        out_shape=[jax.ShapeDtypeStruct((tp, D_MODEL), F32), jax.ShapeDtypeStruct((ts, D_MODEL), F32)],
        grid=(n_p + n_s,),
        in_specs=[pl.BlockSpec((tm, D_MODEL), omap)]
                 + [pl.BlockSpec((tm * TOKEN_TILE_ROWS, LANES),
                                 functools.partial(lambda k, i: (k * (n_p + n_s) + i, 0), k)) for k in range(TOP_K)]
                 + [pl.BlockSpec((tm, LANES), omap), pl.BlockSpec((tm, ple), pmap), pl.BlockSpec((tm, ple), smap),
                  _const_spec((1, D_MODEL)), _const_spec((D_MODEL, D_MODEL)), _const_spec((ple, D_MODEL)),
                  _const_spec((1, D_MODEL))],
        out_specs=[pl.BlockSpec((tm, D_MODEL), pmap), pl.BlockSpec((tm, D_MODEL), smap)],
        compiler_params=_cparams(("arbitrary",)),
        name="ple_final",
    )(h1, y4, y4, y4, y4, gates, pp, ps, g_ple, w_gate, w_proj, g_final)


def _row(x, width=None):
    x = x.reshape(1, -1).astype(F32)
    if width is not None and x.shape[1] < width:
        x = jnp.pad(x, ((0, 0), (0, width - x.shape[1])))
    return x


def _mixer_params(conv_w, conv_b, dt_bias, a_log, d_skip, ssd_norm_g, v_norm_g, v_norm_b, w_spatial, b_spatial,
                  mlp_out_g, seq_len):
    pos = jnp.arange(CHUNK) % seq_len
    same = (jnp.arange(CHUNK)[:, None] // seq_len) == (jnp.arange(CHUNK)[None, :] // seq_len)
    tril = (same & (jnp.arange(CHUNK)[:, None] >= jnp.arange(CHUNK)[None, :])).astype(BF16)
    rexp = (jnp.arange(LANES)[:, None] == (jnp.arange(SSD_WIDTH)[None, :] // HEAD_DIM)).astype(BF16)
    w_loc = jnp.tril(w_spatial[:, :seq_len, :seq_len])
    onehot = (pos[:, None] == jnp.arange(seq_len)[None, :]).astype(F32)
    tiled = jnp.einsum("iq,hqr,jr->hij", onehot, w_loc.astype(F32), onehot, precision=lax.Precision.HIGHEST)
    w_bd = jnp.where(same[None], tiled, 0.0)
    wsp = (w_bd.reshape(MLP_HEADS // 2, 2, CHUNK, CHUNK).transpose(0, 2, 1, 3)
           .reshape(MLP_HEADS // 2, CHUNK, 2 * CHUNK).astype(BF16))
    bsp = jnp.repeat(b_spatial[:, :seq_len].T[pos], MLP_WIDTH // MLP_HEADS, axis=1)
    params = (
        conv_w.astype(F32), _row(conv_b), _row(dt_bias, LANES), _row(a_log, LANES),
        _row(jnp.repeat(a_log, HEAD_DIM)), rexp, tril, _row(jnp.repeat(d_skip, HEAD_DIM)),
        _row(ssd_norm_g), _row(v_norm_g), _row(v_norm_b), wsp, bsp.astype(F32), _row(mlp_out_g),
    )
    return params, same.astype(BF16)


def _tile_rows(n):
    return 512 if n % 512 == 0 else CHUNK


def kernel(x_prompt, x_sample, state_ssm, state_conv, p_prompt, p_sample, norm_mix_g, w_in, conv_w, conv_b, dt_bias, a_log, d_skip, ssd_norm_g, v_norm_g, v_norm_b, w_spatial, b_spatial, mlp_out_g, w_out, norm_moe_g, w_router, b_router, w_up, b_up, w_down, b_down, norm_ple_g, w_ple_gate, w_ple_proj, norm_final_g):
    depth = norm_mix_g.shape[0]
    bp, lp, d = x_prompt.shape
    bs, ls, _ = x_sample.shape
    tp, ts = bp * lp, bs * ls
    assert depth == 1 and d == D_MODEL and lp % CHUNK == 0 and ts % CHUNK == 0 and 8 % ls == 0
    tm = _tile_rows(tp) if ts % _tile_rows(tp) == 0 else CHUNK
    t_all = tp + ts
    tm_moe = MOE_BLOCK_ROWS
    nb_moe = -(-t_all * TOP_K // tm_moe) + N_EXPERTS

    hp = x_prompt.reshape(tp, d)
    hs = x_sample.reshape(ts, d)
    ssm_p, conv_p, ssm_s, conv_s, v_s = [], [], [], [], []
    o1 = SSD_WIDTH
    o2 = o1 + CONV_DIM
    o3 = o2 + SSD_HEADS
    c = jnp.arange(256)
    src = jnp.where(c < LANES, 2 * c, 2 * (c - LANES) + 1)
    perm = (jnp.arange(256)[:, None] == src[None, :]).astype(BF16)

    for i in range(depth):
        wi = w_in[i]
        w_cat = jnp.concatenate(
            [wi[:, :o2], wi[:, o3:], jnp.pad(wi[:, o2:o3], ((0, 0), (0, DT_PAD - SSD_HEADS)))], axis=1).astype(BF16)
        z, xbc, u, v, dtr = _inproj_call(hp, hs, _row(norm_mix_g[i]), w_cat, tm)

        mix_args = (conv_w[i], conv_b[i], dt_bias[i], a_log[i], d_skip[i], ssd_norm_g[i], v_norm_g[i], v_norm_b[i],
                    w_spatial[i], b_spatial[i], mlp_out_g[i])
        prm_p, _ = _mixer_params(*mix_args, seq_len=CHUNK)
        cat_p, s_p = _prompt_mixer_call(z, xbc, u, v, dtr, prm_p, bp, lp // CHUNK)
        ssm_p.append(s_p.reshape(bp, SSD_HEADS, HEAD_DIM, D_STATE).astype(state_ssm.dtype))
        conv_p.append(jnp.stack([xbc[(b + 1) * lp - (CONV_W - 1):(b + 1) * lp] for b in range(bp)]))

        prm_s, seg_ones = _mixer_params(*mix_args, seq_len=ls)
        xbc_s = xbc[tp:].reshape(bs, ls, CONV_DIM)
        xpad = jnp.concatenate([state_conv[i].astype(F32), xbc_s], axis=1)
        x_shift = [xpad[:, CONV_W - 1 - k:CONV_W - 1 - k + ls].reshape(ts, CONV_DIM) for k in range(CONV_W)]
        h0 = state_ssm[i].astype(F32).reshape(bs, SSD_WIDTH, D_STATE)
        cat_s, v_rows, s_s = _sample_mixer_call(z, x_shift, u, v, dtr, h0, prm_s, seg_ones, tp // CHUNK, ls)
        ssm_s.append(s_s.reshape(bs, SSD_HEADS, HEAD_DIM, D_STATE).astype(state_ssm.dtype))
        conv_s.append(xpad[:, ls:])
        v_s.append(v_rows.reshape(bs, ls, MLP_WIDTH))

        wr = jnp.pad(w_router[i].astype(F32), ((0, 0), (0, LANES - N_EXPERTS)))
        wr_hi = wr.astype(BF16)
        wr_lo = (wr - wr_hi.astype(F32)).astype(BF16)
        b_r = jnp.concatenate([b_router[i].astype(F32), jnp.full((LANES - N_EXPERTS,), -1e30, F32)]).reshape(1, LANES)
        h1, m, eid, gates = _out_router_call(cat_p, cat_s, hp, hs, w_out[i].astype(BF16), _row(norm_moe_g[i]),
                                             wr_hi, wr_lo, b_r, tm)

        be, nval, off, tok, dst = _route(eid[:, :TOP_K], tm_moe, nb_moe)
        b_up_g = (b_up[i].astype(F32).reshape(N_EXPERTS, 2 * D_FF // 256, LANES, 2).transpose(0, 1, 3, 2)
                  .reshape(N_EXPERTS, 1, 2 * D_FF))
        y4 = _moe_call(m, be, nval, off, tok, dst, w_up[i], b_up_g, w_down[i],
                       b_down[i].reshape(N_EXPERTS, 1, D_MODEL), perm, tm_moe, nb_moe)

        hp, hs = _ple_call(h1, y4, gates,
                           p_prompt[i].reshape(tp, -1), p_sample[i].reshape(ts, -1), _row(norm_ple_g[i]),
                           w_ple_gate[i].astype(BF16), w_ple_proj[i].astype(BF16), _row(norm_final_g), tm)

    y_prompt = hp.reshape(bp, lp, d)
    y_sample = hs.reshape(bs, ls, d)
    return (y_prompt, y_sample, jnp.stack(ssm_p), jnp.stack(conv_p), jnp.stack(ssm_s), jnp.stack(conv_s),
            jnp.stack(v_s))
```

```python
import functools

import jax
import jax.numpy as jnp
from jax import lax
from jax.experimental import pallas as pl
from jax.experimental.pallas import tpu as pltpu

F32 = jnp.float32
BF16 = jnp.bfloat16
I32 = jnp.int32

EPS = 1e-6
D_MODEL = 1024
SSD_WIDTH = 512
SSD_HEADS = 8
HEAD_DIM = 64
SSD_GROUPS = 2
D_STATE = 128
CONV_W = 4
CONV_DIM = SSD_WIDTH + 2 * SSD_GROUPS * D_STATE
MLP_WIDTH = 512
MLP_HEADS = 8
N_EXPERTS = 32
TOP_K = 4
D_FF = 1024
SWIGLU_LIMIT = 7.0
SWIGLU_ALPHA = 1.702
TOPK_SHIFT = 2
assert 1 << TOPK_SHIFT == TOP_K
LANES = 128
CHUNK = 128
DT_PAD = LANES
TOKEN_TILE_ROWS = D_MODEL // LANES
MOE_BLOCK_ROWS = 256
IN_PAD = SSD_WIDTH + CONV_DIM + 2 * MLP_WIDTH + DT_PAD
VMEM_LIMIT = 56 * 1024 * 1024


def _cparams(sem):
    return pltpu.CompilerParams(dimension_semantics=sem, vmem_limit_bytes=VMEM_LIMIT)


def _const_spec(shape):
    return pl.BlockSpec(shape, lambda *_: (0,) * len(shape))


def _rms(x):
    return x * lax.rsqrt(jnp.mean(x * x, axis=-1, keepdims=True) + EPS)


def _dot(a, b):
    return jnp.dot(a, b, preferred_element_type=F32)


def _dot_nt(a, b):
    return lax.dot_general(a, b, (((1,), (1,)), ((), ())), preferred_element_type=F32)


def _split3(x):
    hi = x.astype(BF16)
    r = x - hi.astype(F32)
    mid = r.astype(BF16)
    lo = (r - mid.astype(F32)).astype(BF16)
    return hi, mid, lo


def _sel_right(x, m01):
    hi, mid, lo = _split3(x)
    return _dot(hi, m01) + _dot(mid, m01) + _dot(lo, m01)


def _sel_left(m01, x):
    hi, mid, lo = _split3(x)
    return _dot(m01, hi) + _dot(m01, mid) + _dot(m01, lo)


def _softplus(x):
    return jnp.maximum(x, 0.0) + jnp.log1p(jnp.exp(-jnp.abs(x)))


def _inproj_call(xp, xs, g, w, tm):
    tp, ts = xp.shape[0], xs.shape[0]
    n_p, n_s = tp // tm, ts // tm
    t_all = tp + ts
    segs = ((0, 512), (512, 1536), (1536, 2048), (2048, 2560), (2560, IN_PAD))

    def body(xp_ref, xs_ref, g_ref, w_ref, *outs):
        def run(x_ref):
            xn = (_rms(x_ref[...]) * g_ref[...]).astype(BF16)
            for (a, b), o in zip(segs, outs):
                o[...] = _dot(xn, w_ref[:, a:b])

        i = pl.program_id(0)

        @pl.when(i < n_p)
        def _():
            run(xp_ref)

        @pl.when(i >= n_p)
        def _():
            run(xs_ref)

    widths = [b - a for a, b in segs]
    return pl.pallas_call(
        body,
        out_shape=[jax.ShapeDtypeStruct((t_all, wd), F32) for wd in widths],
        grid=(n_p + n_s,),
        in_specs=[
            pl.BlockSpec((tm, D_MODEL), lambda i: (jnp.minimum(i, n_p - 1), 0)),
            pl.BlockSpec((tm, D_MODEL), lambda i: (jnp.maximum(i - n_p, 0), 0)),
            _const_spec((1, D_MODEL)),
            _const_spec((D_MODEL, IN_PAD)),
        ],
        out_specs=[pl.BlockSpec((tm, wd), lambda i: (i, 0)) for wd in widths],
        compiler_params=_cparams(("arbitrary",)),
        name="in_proj",
    )(xp, xs, g, w)


def _mixer_front(conv, dtr, dtb, alog, alog_x, rexp, tril, seg_ones):
    xact = conv * jax.nn.sigmoid(conv)
    xs = xact[:, :SSD_WIDTH]
    bm = xact[:, SSD_WIDTH:SSD_WIDTH + 256]
    cm = xact[:, SSD_WIDTH + 256:]
    dt = _softplus(dtr + dtb)
    a = dt * (-jnp.exp(alog))
    dt_x = _sel_right(dt, rexp)
    a_x = dt_x * (-jnp.exp(alog_x))
    acum = _sel_left(tril, a)
    acum_x = _sel_left(tril, a_x)
    if seg_ones is None:
        r = acum_x.shape[0]
        tot_x = jnp.broadcast_to(acum_x[r - 1:r, :], acum_x.shape)
    else:
        tot_x = _sel_left(seg_ones, a_x)
    return xs, bm, cm, dt_x, acum, acum_x, tot_x


def _ssd_intra(cmb, bmb, acum, xdt, mask):
    r = acum.shape[0]
    acum_t = acum.T
    lane = lax.broadcasted_iota(I32, (r, LANES), 1)
    low = lane < HEAD_DIM
    outs = []
    for g in range(SSD_GROUPS):
        sg = _dot_nt(cmb[:, LANES * g:LANES * (g + 1)], bmb[:, LANES * g:LANES * (g + 1)])
        for k in (2 * g, 2 * g + 1):
            parts = []
            for h in (2 * k, 2 * k + 1):
                seg = acum[:, h:h + 1] - acum_t[h:h + 1, :]
                parts.append((sg * jnp.exp(jnp.where(mask, seg, -jnp.inf))).astype(BF16))
            lhs = jnp.concatenate(parts, axis=1)
            xd = xdt[:, LANES * k:LANES * (k + 1)]
            rhs = jnp.concatenate([jnp.where(low, xd, 0.0), jnp.where(low, 0.0, xd)], axis=0).astype(BF16)
            outs.append(_dot(lhs, rhs))
    return jnp.concatenate(outs, axis=1)


def _mixer_back(y, z, u, v, sng, vng, vnb, wsp_ref, bsp, mog):
    r = y.shape[0]
    yg = y * (z * jax.nn.sigmoid(z))
    halves = []
    for g in range(SSD_GROUPS):
        t = yg[:, 256 * g:256 * (g + 1)]
        halves.append(_rms(t))
    yn = jnp.concatenate(halves, axis=1) * sng
    ug = jax.nn.gelu(u)
    vg = jax.nn.gelu(v)
    mu = jnp.mean(vg, axis=-1, keepdims=True)
    var = jnp.mean(jnp.square(vg - mu), axis=-1, keepdims=True)
    v_ln = (vg - mu) * lax.rsqrt(var + EPS) * vng + vnb
    lane = lax.broadcasted_iota(I32, (r, LANES), 1)
    low = lane < HEAD_DIM
    outs = []
    for k in range(MLP_HEADS // 2):
        vd = v_ln[:, LANES * k:LANES * (k + 1)]
        rhs = jnp.concatenate([jnp.where(low, vd, 0.0), jnp.where(low, 0.0, vd)], axis=0).astype(BF16)
        outs.append(_dot(wsp_ref[k], rhs))
    s = jnp.concatenate(outs, axis=1) + bsp
    m = _rms(ug * s) * mog
    return jnp.concatenate([yn, m], axis=1).astype(BF16), v_ln


_MIXER_PARAM_SHAPES = (
    (CONV_W, CONV_DIM), (1, CONV_DIM), (1, LANES), (1, LANES), (1, SSD_WIDTH), (LANES, SSD_WIDTH),
    (CHUNK, CHUNK), (1, SSD_WIDTH), (1, SSD_WIDTH), (1, MLP_WIDTH), (1, MLP_WIDTH),
    (MLP_HEADS // 2, CHUNK, 2 * CHUNK), (CHUNK, MLP_WIDTH), (1, MLP_WIDTH),
)


def _prompt_mixer_body(z_ref, xbc_ref, u_ref, v_ref, dt_ref,
                       cw_ref, cb_ref, dtb_ref, alog_ref, alogx_ref, rexp_ref, tril_ref, dskip_ref,
                       sng_ref, vng_ref, vnb_ref, wsp_ref, bsp_ref, mog_ref,
                       cat_ref, ssm_ref, ext_scr, s_scr):
    c = pl.program_id(1)
    r = CHUNK

    @pl.when(c == 0)
    def _():
        ext_scr[0:8, :] = jnp.zeros((8, CONV_DIM), F32)
        s_scr[...] = jnp.zeros_like(s_scr)

    x = xbc_ref[...]
    ext_scr[8:8 + r, :] = x
    cw = cw_ref[...]
    conv = (cb_ref[...] + cw[3:4] * x + cw[2:3] * ext_scr[7:7 + r, :]
            + cw[1:2] * ext_scr[6:6 + r, :] + cw[0:1] * ext_scr[5:5 + r, :])
    ext_scr[0:8, :] = x[r - 8:r, :]

    xs, bm, cm, dt_x, acum, acum_x, tot_x = _mixer_front(
        conv, dt_ref[...], dtb_ref[...], alog_ref[...], alogx_ref[...], rexp_ref[...], tril_ref[...], None)
    bmb, cmb = bm.astype(BF16), cm.astype(BF16)
    xdt = xs * dt_x
    row = lax.broadcasted_iota(I32, (r, r), 0)
    col = lax.broadcasted_iota(I32, (r, r), 1)
    y_diag = _ssd_intra(cmb, bmb, acum, xdt, row >= col)

    s_prev = s_scr[...]
    s_prev_b = s_prev.astype(BF16)
    y_off = jnp.concatenate(
        [_dot_nt(cmb[:, LANES * g:LANES * (g + 1)], s_prev_b[256 * g:256 * (g + 1), :]) for g in range(SSD_GROUPS)],
        axis=1)
    y = y_diag + y_off * jnp.exp(acum_x) + dskip_ref[...] * xs

    w_t = (xdt * jnp.exp(tot_x - acum_x)).T.astype(BF16)
    states = jnp.concatenate(
        [_dot(w_t[256 * g:256 * (g + 1), :], bmb[:, LANES * g:LANES * (g + 1)]) for g in range(SSD_GROUPS)], axis=0)
    s_new = s_prev * jnp.exp(tot_x).T + states
    s_scr[...] = s_new

    cat, _ = _mixer_back(y, z_ref[...], u_ref[...], v_ref[...], sng_ref[...], vng_ref[...], vnb_ref[...],
                         wsp_ref, bsp_ref[...], mog_ref[...])
    cat_ref[...] = cat

    @pl.when(c == pl.num_programs(1) - 1)
    def _():
        ssm_ref[0] = s_new


def _prompt_mixer_call(z, xbc, u, v, dtr, params, nb, nc):
    row = lambda b, c: (b * nc + c, 0)
    in_specs = [
        pl.BlockSpec((CHUNK, SSD_WIDTH), row), pl.BlockSpec((CHUNK, CONV_DIM), row),
        pl.BlockSpec((CHUNK, MLP_WIDTH), row), pl.BlockSpec((CHUNK, MLP_WIDTH), row),
        pl.BlockSpec((CHUNK, DT_PAD), row),
    ] + [_const_spec(s) for s in _MIXER_PARAM_SHAPES]
    return pl.pallas_call(
        _prompt_mixer_body,
        out_shape=[jax.ShapeDtypeStruct((nb * nc * CHUNK, D_MODEL), BF16),
                   jax.ShapeDtypeStruct((nb, SSD_WIDTH, D_STATE), F32)],
        grid=(nb, nc),
        in_specs=in_specs,
        out_specs=[pl.BlockSpec((CHUNK, D_MODEL), row),
                   pl.BlockSpec((1, SSD_WIDTH, D_STATE), lambda b, c: (b, 0, 0))],
        scratch_shapes=[pltpu.VMEM((CHUNK + 8, CONV_DIM), F32), pltpu.VMEM((SSD_WIDTH, D_STATE), F32)],
        compiler_params=_cparams(("arbitrary", "arbitrary")),
        name="prompt_mixer",
    )(z, xbc, u, v, dtr, *params)


def _sample_mixer_body(seq_len, z_ref, x0_ref, x1_ref, x2_ref, x3_ref, u_ref, v_ref, dt_ref, h_ref,
                       cw_ref, cb_ref, dtb_ref, alog_ref, alogx_ref, rexp_ref, tril_ref, dskip_ref,
                       sng_ref, vng_ref, vnb_ref, wsp_ref, bsp_ref, mog_ref, segones_ref,
                       cat_ref, vout_ref, hout_ref, cm_scr, bm_scr, wt_scr, dtt_scr, yoff_scr):
    r = CHUNK
    shift = seq_len.bit_length() - 1
    cw = cw_ref[...]
    conv = (cb_ref[...] + cw[3:4] * x0_ref[...] + cw[2:3] * x1_ref[...]
            + cw[1:2] * x2_ref[...] + cw[0:1] * x3_ref[...])
    xs, bm, cm, dt_x, acum, acum_x, tot_x = _mixer_front(
        conv, dt_ref[...], dtb_ref[...], alog_ref[...], alogx_ref[...], rexp_ref[...], tril_ref[...],
        segones_ref[...])
    bmb, cmb = bm.astype(BF16), cm.astype(BF16)
    xdt = xs * dt_x
    row = lax.broadcasted_iota(I32, (r, r), 0)
    col = lax.broadcasted_iota(I32, (r, r), 1)
    same = lax.shift_right_logical(row, shift) == lax.shift_right_logical(col, shift)
    y_diag = _ssd_intra(cmb, bmb, acum, xdt, same & (row >= col))

    cm_scr[...] = cm
    bm_scr[...] = bmb
    wt_scr[...] = (xdt * jnp.exp(tot_x - acum_x)).T
    dtt_scr[...] = jnp.exp(tot_x).T
    ones_b = jnp.ones((LANES, LANES), BF16)
    seqs_per_slab = 8 // seq_len

    def slab(j, carry):
        rows = pl.ds(pl.multiple_of(8 * j, 8), 8)
        cms = cm_scr[rows, :].astype(BF16)
        sub = lax.broadcasted_iota(I32, (8, 256), 0)
        lane = lax.broadcasted_iota(I32, (256, LANES), 1)
        for g in range(SSD_GROUPS):
            q_rows = slice(256 * g, 256 * (g + 1))
            acc = jnp.zeros((8, 256), F32)
            for q in range(seqs_per_slab):
                s = seqs_per_slab * j + q
                y_s = _dot_nt(cms[:, LANES * g:LANES * (g + 1)], h_ref[s, q_rows, :].astype(BF16))
                acc = jnp.where(lax.shift_right_logical(sub, shift) == q, y_s, acc)
            yoff_scr[rows, 256 * g:256 * (g + 1)] = acc
            for q in range(seqs_per_slab):
                s = seqs_per_slab * j + q
                w_sel = jnp.where(lax.shift_right_logical(lane, shift) == s, wt_scr[q_rows, :], 0.0).astype(BF16)
                st = _dot(w_sel, bm_scr[:, LANES * g:LANES * (g + 1)])
                d_sel = jnp.where(lane == s * seq_len, dtt_scr[q_rows, :], 0.0)
                hout_ref[s, q_rows, :] = h_ref[s, q_rows, :] * _sel_right(d_sel, ones_b) + st
        return carry

    lax.fori_loop(0, r // 8, slab, 0)

    y = y_diag + yoff_scr[...] * jnp.exp(acum_x) + dskip_ref[...] * xs
    cat, v_ln = _mixer_back(y, z_ref[...], u_ref[...], v_ref[...], sng_ref[...], vng_ref[...], vnb_ref[...],
                            wsp_ref, bsp_ref[...], mog_ref[...])
    cat_ref[...] = cat
    vout_ref[...] = v_ln


def _sample_mixer_call(z, x_shift, u, v, dtr, h0, params, seg_ones, row0, seq_len):
    ts = x_shift[0].shape[0]
    n = ts // CHUNK
    spt = CHUNK // seq_len
    off = lambda i: (row0 + i, 0)
    loc = lambda i: (i, 0)
    st3 = lambda i: (i, 0, 0)
    in_specs = (
        [pl.BlockSpec((CHUNK, SSD_WIDTH), off)]
        + [pl.BlockSpec((CHUNK, CONV_DIM), loc)] * 4
        + [pl.BlockSpec((CHUNK, MLP_WIDTH), off), pl.BlockSpec((CHUNK, MLP_WIDTH), off),
           pl.BlockSpec((CHUNK, DT_PAD), off), pl.BlockSpec((spt, SSD_WIDTH, D_STATE), st3)]
        + [_const_spec(s) for s in _MIXER_PARAM_SHAPES] + [_const_spec((CHUNK, CHUNK))])
    return pl.pallas_call(
        functools.partial(_sample_mixer_body, seq_len),
        out_shape=[jax.ShapeDtypeStruct((ts, D_MODEL), BF16), jax.ShapeDtypeStruct((ts, MLP_WIDTH), F32),
                   jax.ShapeDtypeStruct(h0.shape, F32)],
        grid=(n,),
        in_specs=in_specs,
        out_specs=[pl.BlockSpec((CHUNK, D_MODEL), loc), pl.BlockSpec((CHUNK, MLP_WIDTH), loc),
                   pl.BlockSpec((spt, SSD_WIDTH, D_STATE), st3)],
        scratch_shapes=[pltpu.VMEM((CHUNK, 256), F32), pltpu.VMEM((CHUNK, 256), BF16),
                        pltpu.VMEM((SSD_WIDTH, CHUNK), F32), pltpu.VMEM((SSD_WIDTH, CHUNK), F32),
                        pltpu.VMEM((CHUNK, SSD_WIDTH), F32)],
        compiler_params=_cparams(("arbitrary",)),
        name="sample_mixer",
    )(z, *x_shift, u, v, dtr, h0, *params, seg_ones)


def _out_router_call(cat_p, cat_s, xp, xs, w_out, g_moe, wr_hi, wr_lo, b_r, tm):
    tp, ts = xp.shape[0], xs.shape[0]
    n_p, n_s = tp // tm, ts // tm
    t_all = tp + ts

    def body(cp_ref, cs_ref, xp_ref, xs_ref, wo_ref, g_ref, wr_ref, br_ref,
             h1_ref, m_ref, eid_ref, gate_ref):
        def run(c_ref, x_ref):
            h1 = x_ref[...] + _dot(c_ref[...], wo_ref[...])
            h1_ref[...] = h1
            m = _rms(h1) * g_ref[...]
            for j in range(TOKEN_TILE_ROWS):
                m_ref[pl.ds(j, tm, stride=TOKEN_TILE_ROWS), :] = m[:, LANES * j:LANES * (j + 1)]
            m_hi = m.astype(BF16)
            m_lo = (m - m_hi.astype(F32)).astype(BF16)
            part = _dot(m_hi, wr_ref[...]) + _dot(m_lo, wr_ref[...])
            logits = part[:, :LANES] + part[:, LANES:] + br_ref[...]
            lane = lax.broadcasted_iota(I32, logits.shape, 1).astype(F32)
            work = logits
            vals, ids = [], []
            for _ in range(TOP_K):
                mx = jnp.max(work, axis=-1, keepdims=True)
                idx = jnp.min(jnp.where(work == mx, lane, float(LANES)), axis=-1, keepdims=True)
                vals.append(mx)
                ids.append(idx)
                work = jnp.where(lane == idx, -jnp.inf, work)
            ex = [jnp.exp(vv - vals[0]) for vv in vals]
            den = ex[0] + ex[1] + ex[2] + ex[3]
            eid = jnp.zeros(logits.shape, I32)
            gate = jnp.zeros(logits.shape, F32)
            for k in range(TOP_K):
                eid = jnp.where(lane == k, ids[k].astype(I32), eid)
                gate = jnp.where(lane == k, ex[k] / den, gate)
            eid_ref[...] = eid
            gate_ref[...] = gate

        i = pl.program_id(0)

        @pl.when(i < n_p)
        def _():
            run(cp_ref, xp_ref)

        @pl.when(i >= n_p)
        def _():
            run(cs_ref, xs_ref)

    pmap = lambda i: (jnp.minimum(i, n_p - 1), 0)
    smap = lambda i: (jnp.maximum(i - n_p, 0), 0)
    omap = lambda i: (i, 0)
    return pl.pallas_call(
        body,
        out_shape=[jax.ShapeDtypeStruct((t_all, D_MODEL), F32),
                   jax.ShapeDtypeStruct((t_all * TOKEN_TILE_ROWS, LANES), F32),
                   jax.ShapeDtypeStruct((t_all, LANES), I32), jax.ShapeDtypeStruct((t_all, LANES), F32)],
        grid=(n_p + n_s,),
        in_specs=[pl.BlockSpec((tm, D_MODEL), pmap), pl.BlockSpec((tm, D_MODEL), smap),
                  pl.BlockSpec((tm, D_MODEL), pmap), pl.BlockSpec((tm, D_MODEL), smap),
                  _const_spec((D_MODEL, D_MODEL)), _const_spec((1, D_MODEL)),
                  _const_spec((D_MODEL, 2 * LANES)), _const_spec((1, LANES))],
        out_specs=[pl.BlockSpec((tm, D_MODEL), omap), pl.BlockSpec((tm * TOKEN_TILE_ROWS, LANES), omap),
                   pl.BlockSpec((tm, LANES), omap), pl.BlockSpec((tm, LANES), omap)],
        compiler_params=_cparams(("arbitrary",)),
        name="out_router",
    )(cat_p, cat_s, xp, xs, w_out, g_moe, jnp.concatenate([wr_hi, wr_lo], axis=1), b_r)


def _route(eid, tm, nb):
    t = eid.shape[0]
    tk = t * TOP_K
    flat = eid.reshape(tk)
    _, order = lax.sort((flat, jnp.arange(tk, dtype=I32)), num_keys=1, is_stable=True)
    counts = jnp.sum((flat[:, None] == jnp.arange(N_EXPERTS, dtype=I32)[None, :]).astype(I32), axis=0)
    nblk = (counts + tm - 1) // tm
    bend = jnp.cumsum(nblk)
    bstart = bend - nblk
    start = jnp.cumsum(counts) - counts
    nused = bend[-1]
    blk = jnp.arange(nb, dtype=I32)
    used = blk < nused
    be = jnp.minimum(jnp.sum((jnp.minimum(blk, nused - 1)[:, None] >= bend[None, :]).astype(I32), axis=1),
                     N_EXPERTS - 1)
    sel = (be[:, None] == jnp.arange(N_EXPERTS, dtype=I32)[None, :]).astype(I32)
    pick = lambda v: jnp.sum(sel * v[None, :], axis=1)
    done = (blk - pick(bstart)) * tm
    nval = jnp.where(used, jnp.clip(pick(counts) - done, 0, tm), 0).astype(I32)
    off = jnp.where(used, pick(start) + done, 0).astype(I32)
    pad = (-(-(tk + tm) // LANES) + _id_rows(tm)) * LANES - tk
    tok = jnp.pad(lax.shift_right_logical(order, TOPK_SHIFT) * TOKEN_TILE_ROWS, (0, pad))
    dst = jnp.pad(((order & (TOP_K - 1)) * t + lax.shift_right_logical(order, TOPK_SHIFT)) * TOKEN_TILE_ROWS,
                  (0, pad))
    return be, nval, off, tok, dst


def _id_rows(tm):
    return tm // LANES + 1


def _moe_body(tm, t_all, nb, be_ref, nval_ref, off_ref, tok_hbm, dst_hbm, m_hbm, wup_ref, bup_ref, wdn_ref, bdn_ref,
              perm_ref, y_hbm, gids, sids, xbuf, ybuf, wup_b, wdn_b, isem, gsem, ssem):
    i = pl.program_id(0)
    nv = nval_ref[i]
    slot = i & 1
    prv = jnp.maximum(i - 1, 0)
    nxt = jnp.minimum(i + 1, nb - 1)
    nx2 = jnp.minimum(i + 2, nb - 1)
    has_next = (i + 1 < nb) & (nval_ref[nxt] > 0)
    has_next2 = (i + 2 < nb) & (nval_ref[nx2] > 0)
    n_prev = jnp.where(i > 0, nval_ref[prv], 0)
    win = _id_rows(tm) * LANES
    spare = TOP_K * t_all * TOKEN_TILE_ROWS

    def ids_copies(b):
        start = pl.multiple_of(lax.shift_right_logical(off_ref[b], 7) * LANES, LANES)
        ring = pl.ds(pl.multiple_of((b & 3) * win, LANES), win)
        return (pltpu.make_async_copy(tok_hbm.at[pl.ds(start, win)], gids.at[ring], isem.at[b & 3, 0]),
                pltpu.make_async_copy(dst_hbm.at[pl.ds(start, win)], sids.at[ring], isem.at[b & 3, 1]))

    def id_base(b):
        return (b & 3) * win + (off_ref[b] & (LANES - 1))

    def gather_row(base, s, r):
        src = pl.ds(pl.multiple_of(gids[base + r], TOKEN_TILE_ROWS), TOKEN_TILE_ROWS)
        return pltpu.make_async_copy(m_hbm.at[src], xbuf.at[s, pl.ds(TOKEN_TILE_ROWS * r, TOKEN_TILE_ROWS)],
                                     gsem.at[s])

    def scatter_row(base, s, r, n):
        dest = jnp.where(r < n, sids[base + r], spare + TOKEN_TILE_ROWS * r)
        rows = pl.ds(pl.multiple_of(dest, TOKEN_TILE_ROWS), TOKEN_TILE_ROWS)
        return pltpu.make_async_copy(ybuf.at[s, pl.ds(TOKEN_TILE_ROWS * r, TOKEN_TILE_ROWS)], y_hbm.at[rows],
                                     ssem.at[s])

    def wait_gathers(s):
        pltpu.make_async_copy(m_hbm.at[pl.ds(0, tm * TOKEN_TILE_ROWS)], xbuf.at[s], gsem.at[s]).wait()

    def wait_scatters(s):
        pltpu.make_async_copy(ybuf.at[s], y_hbm.at[pl.ds(0, tm * TOKEN_TILE_ROWS)], ssem.at[s]).wait()

    def for_rows(fn):
        def one(r, c):
            fn(r)
            return c

        lax.fori_loop(0, tm, one, 0)

    @pl.when(nv > 0)
    def _():
        @pl.when(i == 0)
        def _():
            ybuf[...] = jnp.zeros_like(ybuf)
            fill = pltpu.make_async_copy(ybuf.at[0], y_hbm.at[pl.ds(spare, tm * TOKEN_TILE_ROWS)], ssem.at[0])
            fill.start()
            fill.wait()
            for b in range(4):
                for cp in ids_copies(b):
                    cp.start()
                    cp.wait()
            base0 = id_base(0)
            for_rows(lambda r: gather_row(base0, 0, r).start())

        @pl.when(has_next & (i >= 3))
        def _():
            for cp in ids_copies(nxt):
                cp.wait()

        @pl.when(has_next2 & (i >= 2))
        def _():
            for cp in ids_copies(nx2):
                cp.start()

        @pl.when((i == 0) | (be_ref[i] != be_ref[prv]))
        def _():
            for jb in range(2 * D_FF // 256):
                cols = slice(256 * jb, 256 * (jb + 1))
                wup_b[:, cols] = _dot(wup_ref[0, :, cols].astype(BF16), perm_ref[...]).astype(BF16)
            wdn_b[...] = wdn_ref[0].astype(BF16)

        y_cur = lax.rem(i, 3)
        y_prev = lax.rem(i + 2, 3)
        y_prev2 = lax.rem(i + 1, 3)

        def ffn_step(cur, oth):
            wait_gathers(cur)

            @pl.when(i >= 2)
            def _():
                wait_scatters(y_cur)

            g_base = id_base(nxt)
            s_base = id_base(prv)
            for r in range(tm):
                gather_row(g_base, oth, r).start()
                scatter_row(s_base, y_prev, r, n_prev).start()

            x = jnp.concatenate([xbuf[slot, pl.ds(j, tm, stride=TOKEN_TILE_ROWS), :]
                                 for j in range(TOKEN_TILE_ROWS)], axis=1).astype(BF16)
            acts = []
            for jb in range(D_FF // LANES):
                h = _dot(x, wup_b[:, 256 * jb:256 * (jb + 1)]) + bup_ref[0, :, 256 * jb:256 * (jb + 1)]
                gate = jnp.minimum(h[:, :LANES], SWIGLU_LIMIT)
                lin = jnp.clip(h[:, LANES:], -SWIGLU_LIMIT, SWIGLU_LIMIT)
                acts.append((gate * jax.nn.sigmoid(SWIGLU_ALPHA * gate) * (lin + 1.0)).astype(BF16))
            act = jnp.concatenate(acts, axis=1)
            for c in range(D_MODEL // 256):
                yc = _dot(act, wdn_b[:, 256 * c:256 * (c + 1)]) + bdn_ref[0, :, 256 * c:256 * (c + 1)]
                for half in range(2):
                    ybuf[y_cur, pl.ds(2 * c + half, tm, stride=TOKEN_TILE_ROWS), :] = (
                        yc[:, LANES * half:LANES * (half + 1)])

            @pl.when(jnp.logical_not(has_next))
            def _():
                wait_gathers(oth)

                @pl.when(i >= 1)
                def _():
                    wait_scatters(y_prev2)
                wait_scatters(y_prev)
                last_base = id_base(i)
                for_rows(lambda r: scatter_row(last_base, y_cur, r, nv).start())
                wait_scatters(y_cur)

        for parity in range(2):
            pl.when(slot == parity)(functools.partial(ffn_step, parity, 1 - parity))


def _moe_call(m, be, nval, off, tok, dst, w_up, b_up_g, w_down, b_down, perm, tm, nb):
    t = m.shape[0] // TOKEN_TILE_ROWS
    assert nb >= 4
    by_expert = lambda i, be, nv, off: (be[i], 0, 0)
    grid_spec = pltpu.PrefetchScalarGridSpec(
        num_scalar_prefetch=3,
        grid=(nb,),
        in_specs=[
            pl.BlockSpec(memory_space=pl.ANY),
            pl.BlockSpec(memory_space=pl.ANY),
            pl.BlockSpec(memory_space=pl.ANY),
            pl.BlockSpec((1, D_MODEL, 2 * D_FF), by_expert),
            pl.BlockSpec((1, 1, 2 * D_FF), by_expert),
            pl.BlockSpec((1, D_FF, D_MODEL), by_expert),
            pl.BlockSpec((1, 1, D_MODEL), by_expert),
            pl.BlockSpec((256, 256), lambda i, be, nv, off: (0, 0)),
        ],
        out_specs=pl.BlockSpec(memory_space=pl.ANY),
        scratch_shapes=[pltpu.SMEM((4 * _id_rows(tm) * LANES,), I32), pltpu.SMEM((4 * _id_rows(tm) * LANES,), I32),
                        pltpu.VMEM((2, tm * TOKEN_TILE_ROWS, LANES), F32),
                        pltpu.VMEM((3, tm * TOKEN_TILE_ROWS, LANES), F32),
                        pltpu.VMEM((D_MODEL, 2 * D_FF), BF16), pltpu.VMEM((D_FF, D_MODEL), BF16),
                        pltpu.SemaphoreType.DMA((4, 2)), pltpu.SemaphoreType.DMA((2,)),
                        pltpu.SemaphoreType.DMA((3,))],
    )
    return pl.pallas_call(
        functools.partial(_moe_body, tm, t, nb),
        out_shape=jax.ShapeDtypeStruct(((TOP_K * t + tm) * TOKEN_TILE_ROWS, LANES), F32),
        grid_spec=grid_spec,
        compiler_params=_cparams(("arbitrary",)),
        name="moe_experts",
    )(be, nval, off, tok, dst, m, w_up, b_up_g, w_down, b_down, perm)


def _ple_call(h1, y4, gates, pp, ps, g_ple, w_gate, w_proj, g_final, tm):
    tp, ts = pp.shape[0], ps.shape[0]
    n_p, n_s = tp // tm, ts // tm
    ple = pp.shape[1]

    def body(h1_ref, y0_ref, y1_ref, y2_ref, y3_ref, gt_ref, pp_ref, ps_ref, g_ref, wg_ref, wp_ref, gf_ref,
             yp_ref, ys_ref):
        def run(p_ref, o_ref):
            gt = gt_ref[...]
            moe = None
            for k, y_ref in enumerate((y0_ref, y1_ref, y2_ref, y3_ref)):
                y_k = jnp.concatenate([y_ref[pl.ds(j, tm, stride=TOKEN_TILE_ROWS), :]
                                       for j in range(TOKEN_TILE_ROWS)], axis=1)
                moe = gt[:, k:k + 1] * y_k if moe is None else moe + gt[:, k:k + 1] * y_k
            h2 = h1_ref[...] + moe
            a = (_rms(h2) * g_ref[...]).astype(BF16)
            gate = jax.nn.sigmoid(_dot(a, wg_ref[...]))
            pe = _dot(p_ref[...].astype(BF16), wp_ref[...])
            h3 = h2 + pe * gate
            o_ref[...] = _rms(h3) * gf_ref[...]

        i = pl.program_id(0)

        @pl.when(i < n_p)
        def _():
            run(pp_ref, yp_ref)

        @pl.when(i >= n_p)
        def _():
            run(ps_ref, ys_ref)

    pmap = lambda i: (jnp.minimum(i, n_p - 1), 0)
    smap = lambda i: (jnp.maximum(i - n_p, 0), 0)
    omap = lambda i: (i, 0)
    return pl.pallas_call(
        body,
        out_shape=[jax.ShapeDtypeStruct((tp, D_MODEL), F32), jax.ShapeDtypeStruct((ts, D_MODEL), F32)],
        grid=(n_p + n_s,),
        in_specs=[pl.BlockSpec((tm, D_MODEL), omap)]
                 + [pl.BlockSpec((tm * TOKEN_TILE_ROWS, LANES),
                                 functools.partial(lambda k, i: (k * (n_p + n_s) + i, 0), k)) for k in range(TOP_K)]
                 + [pl.BlockSpec((tm, LANES), omap), pl.BlockSpec((tm, ple), pmap), pl.BlockSpec((tm, ple), smap),
                  _const_spec((1, D_MODEL)), _const_spec((D_MODEL, D_MODEL)), _const_spec((ple, D_MODEL)),
                  _const_spec((1, D_MODEL))],
        out_specs=[pl.BlockSpec((tm, D_MODEL), pmap), pl.BlockSpec((tm, D_MODEL), smap)],
        compiler_params=_cparams(("arbitrary",)),
        name="ple_final",
    )(h1, y4, y4, y4, y4, gates, pp, ps, g_ple, w_gate, w_proj, g_final)


def _row(x, width=None):
    x = x.reshape(1, -1).astype(F32)
    if width is not None and x.shape[1] < width:
        x = jnp.pad(x, ((0, 0), (0, width - x.shape[1])))
    return x


def _mixer_params(conv_w, conv_b, dt_bias, a_log, d_skip, ssd_norm_g, v_norm_g, v_norm_b, w_spatial, b_spatial,
                  mlp_out_g, seq_len):
    pos = jnp.arange(CHUNK) % seq_len
    same = (jnp.arange(CHUNK)[:, None] // seq_len) == (jnp.arange(CHUNK)[None, :] // seq_len)
    tril = (same & (jnp.arange(CHUNK)[:, None] >= jnp.arange(CHUNK)[None, :])).astype(BF16)
    rexp = (jnp.arange(LANES)[:, None] == (jnp.arange(SSD_WIDTH)[None, :] // HEAD_DIM)).astype(BF16)
    w_loc = jnp.tril(w_spatial[:, :seq_len, :seq_len])
    onehot = (pos[:, None] == jnp.arange(seq_len)[None, :]).astype(F32)
    tiled = jnp.einsum("iq,hqr,jr->hij", onehot, w_loc.astype(F32), onehot, precision=lax.Precision.HIGHEST)
    w_bd = jnp.where(same[None], tiled, 0.0)
    wsp = (w_bd.reshape(MLP_HEADS // 2, 2, CHUNK, CHUNK).transpose(0, 2, 1, 3)
           .reshape(MLP_HEADS // 2, CHUNK, 2 * CHUNK).astype(BF16))
    bsp = jnp.repeat(b_spatial[:, :seq_len].T[pos], MLP_WIDTH // MLP_HEADS, axis=1)
    params = (
        conv_w.astype(F32), _row(conv_b), _row(dt_bias, LANES), _row(a_log, LANES),
        _row(jnp.repeat(a_log, HEAD_DIM)), rexp, tril, _row(jnp.repeat(d_skip, HEAD_DIM)),
        _row(ssd_norm_g), _row(v_norm_g), _row(v_norm_b), wsp, bsp.astype(F32), _row(mlp_out_g),
    )
    return params, same.astype(BF16)


def _tile_rows(n):
    return 512 if n % 512 == 0 else CHUNK


def kernel(x_prompt, x_sample, state_ssm, state_conv, p_prompt, p_sample, norm_mix_g, w_in, conv_w, conv_b, dt_bias, a_log, d_skip, ssd_norm_g, v_norm_g, v_norm_b, w_spatial, b_spatial, mlp_out_g, w_out, norm_moe_g, w_router, b_router, w_up, b_up, w_down, b_down, norm_ple_g, w_ple_gate, w_ple_proj, norm_final_g):
    depth = norm_mix_g.shape[0]
    bp, lp, d = x_prompt.shape
    bs, ls, _ = x_sample.shape
    tp, ts = bp * lp, bs * ls
    assert depth == 1 and d == D_MODEL and lp % CHUNK == 0 and ts % CHUNK == 0 and 8 % ls == 0
    tm = _tile_rows(tp) if ts % _tile_rows(tp) == 0 else CHUNK
    t_all = tp + ts
    tm_moe = MOE_BLOCK_ROWS
    nb_moe = -(-t_all * TOP_K // tm_moe) + N_EXPERTS

    hp = x_prompt.reshape(tp, d)
    hs = x_sample.reshape(ts, d)
    ssm_p, conv_p, ssm_s, conv_s, v_s = [], [], [], [], []
    o1 = SSD_WIDTH
    o2 = o1 + CONV_DIM
    o3 = o2 + SSD_HEADS
    c = jnp.arange(256)
    src = jnp.where(c < LANES, 2 * c, 2 * (c - LANES) + 1)
    perm = (jnp.arange(256)[:, None] == src[None, :]).astype(BF16)

    for i in range(depth):
        wi = w_in[i]
        w_cat = jnp.concatenate(
            [wi[:, :o2], wi[:, o3:], jnp.pad(wi[:, o2:o3], ((0, 0), (0, DT_PAD - SSD_HEADS)))], axis=1).astype(BF16)
        z, xbc, u, v, dtr = _inproj_call(hp, hs, _row(norm_mix_g[i]), w_cat, tm)

        mix_args = (conv_w[i], conv_b[i], dt_bias[i], a_log[i], d_skip[i], ssd_norm_g[i], v_norm_g[i], v_norm_b[i],
                    w_spatial[i], b_spatial[i], mlp_out_g[i])
        prm_p, _ = _mixer_params(*mix_args, seq_len=CHUNK)
        cat_p, s_p = _prompt_mixer_call(z, xbc, u, v, dtr, prm_p, bp, lp // CHUNK)
        ssm_p.append(s_p.reshape(bp, SSD_HEADS, HEAD_DIM, D_STATE).astype(state_ssm.dtype))
        conv_p.append(jnp.stack([xbc[(b + 1) * lp - (CONV_W - 1):(b + 1) * lp] for b in range(bp)]))

        prm_s, seg_ones = _mixer_params(*mix_args, seq_len=ls)
        xbc_s = xbc[tp:].reshape(bs, ls, CONV_DIM)
        xpad = jnp.concatenate([state_conv[i].astype(F32), xbc_s], axis=1)
        x_shift = [xpad[:, CONV_W - 1 - k:CONV_W - 1 - k + ls].reshape(ts, CONV_DIM) for k in range(CONV_W)]
        h0 = state_ssm[i].astype(F32).reshape(bs, SSD_WIDTH, D_STATE)
        cat_s, v_rows, s_s = _sample_mixer_call(z, x_shift, u, v, dtr, h0, prm_s, seg_ones, tp // CHUNK, ls)
        ssm_s.append(s_s.reshape(bs, SSD_HEADS, HEAD_DIM, D_STATE).astype(state_ssm.dtype))
        conv_s.append(xpad[:, ls:])
        v_s.append(v_rows.reshape(bs, ls, MLP_WIDTH))

        wr = jnp.pad(w_router[i].astype(F32), ((0, 0), (0, LANES - N_EXPERTS)))
        wr_hi = wr.astype(BF16)
        wr_lo = (wr - wr_hi.astype(F32)).astype(BF16)
        b_r = jnp.concatenate([b_router[i].astype(F32), jnp.full((LANES - N_EXPERTS,), -1e30, F32)]).reshape(1, LANES)
        h1, m, eid, gates = _out_router_call(cat_p, cat_s, hp, hs, w_out[i].astype(BF16), _row(norm_moe_g[i]),
                                             wr_hi, wr_lo, b_r, tm)

        be, nval, off, tok, dst = _route(eid[:, :TOP_K], tm_moe, nb_moe)
        b_up_g = (b_up[i].astype(F32).reshape(N_EXPERTS, 2 * D_FF // 256, LANES, 2).transpose(0, 1, 3, 2)
                  .reshape(N_EXPERTS, 1, 2 * D_FF))
        y4 = _moe_call(m, be, nval, off, tok, dst, w_up[i], b_up_g, w_down[i],
                       b_down[i].reshape(N_EXPERTS, 1, D_MODEL), perm, tm_moe, nb_moe)

        hp, hs = _ple_call(h1, y4, gates,
                           p_prompt[i].reshape(tp, -1), p_sample[i].reshape(ts, -1), _row(norm_ple_g[i]),
                           w_ple_gate[i].astype(BF16), w_ple_proj[i].astype(BF16), _row(norm_final_g), tm)

    y_prompt = hp.reshape(bp, lp, d)
    y_sample = hs.reshape(bs, ls, d)
    return (y_prompt, y_sample, jnp.stack(ssm_p), jnp.stack(conv_p), jnp.stack(ssm_s), jnp.stack(conv_s),
            jnp.stack(v_s))
```

```python
import functools

import jax
import jax.numpy as jnp
from jax import lax
from jax.experimental import pallas as pl
from jax.experimental.pallas import tpu as pltpu

F32 = jnp.float32
BF16 = jnp.bfloat16
I32 = jnp.int32

EPS = 1e-6
D_MODEL = 1024
SSD_WIDTH = 512
SSD_HEADS = 8
HEAD_DIM = 64
SSD_GROUPS = 2
D_STATE = 128
CONV_W = 4
CONV_DIM = SSD_WIDTH + 2 * SSD_GROUPS * D_STATE
MLP_WIDTH = 512
MLP_HEADS = 8
N_EXPERTS = 32
TOP_K = 4
D_FF = 1024
SWIGLU_LIMIT = 7.0
SWIGLU_ALPHA = 1.702
TOPK_SHIFT = 2
assert 1 << TOPK_SHIFT == TOP_K
LANES = 128
CHUNK = 128
DT_PAD = LANES
TOKEN_TILE_ROWS = D_MODEL // LANES
MOE_BLOCK_ROWS = 256
IN_PAD = SSD_WIDTH + CONV_DIM + 2 * MLP_WIDTH + DT_PAD
VMEM_LIMIT = 56 * 1024 * 1024


def _cparams(sem):
    return pltpu.CompilerParams(dimension_semantics=sem, vmem_limit_bytes=VMEM_LIMIT)


def _const_spec(shape):
    return pl.BlockSpec(shape, lambda *_: (0,) * len(shape))


def _rms(x):
    return x * lax.rsqrt(jnp.mean(x * x, axis=-1, keepdims=True) + EPS)


def _dot(a, b):
    return jnp.dot(a, b, preferred_element_type=F32)


def _dot_nt(a, b):
    return lax.dot_general(a, b, (((1,), (1,)), ((), ())), preferred_element_type=F32)


def _split3(x):
    hi = x.astype(BF16)
    r = x - hi.astype(F32)
    mid = r.astype(BF16)
    lo = (r - mid.astype(F32)).astype(BF16)
    return hi, mid, lo


def _sel_right(x, m01):
    hi, mid, lo = _split3(x)
    return _dot(hi, m01) + _dot(mid, m01) + _dot(lo, m01)


def _sel_left(m01, x):
    hi, mid, lo = _split3(x)
    return _dot(m01, hi) + _dot(m01, mid) + _dot(m01, lo)


def _softplus(x):
    return jnp.maximum(x, 0.0) + jnp.log1p(jnp.exp(-jnp.abs(x)))


def _inproj_call(xp, xs, g, w, tm):
    tp, ts = xp.shape[0], xs.shape[0]
    n_p, n_s = tp // tm, ts // tm
    t_all = tp + ts
    segs = ((0, 512), (512, 1536), (1536, 2048), (2048, 2560), (2560, IN_PAD))

    def body(xp_ref, xs_ref, g_ref, w_ref, *outs):
        def run(x_ref):
            xn = (_rms(x_ref[...]) * g_ref[...]).astype(BF16)
            for (a, b), o in zip(segs, outs):
                o[...] = _dot(xn, w_ref[:, a:b])

        i = pl.program_id(0)

        @pl.when(i < n_p)
        def _():
            run(xp_ref)

        @pl.when(i >= n_p)
        def _():
            run(xs_ref)

    widths = [b - a for a, b in segs]
    return pl.pallas_call(
        body,
        out_shape=[jax.ShapeDtypeStruct((t_all, wd), F32) for wd in widths],
        grid=(n_p + n_s,),
        in_specs=[
            pl.BlockSpec((tm, D_MODEL), lambda i: (jnp.minimum(i, n_p - 1), 0)),
            pl.BlockSpec((tm, D_MODEL), lambda i: (jnp.maximum(i - n_p, 0), 0)),
            _const_spec((1, D_MODEL)),
            _const_spec((D_MODEL, IN_PAD)),
        ],
        out_specs=[pl.BlockSpec((tm, wd), lambda i: (i, 0)) for wd in widths],
        compiler_params=_cparams(("arbitrary",)),
        name="in_proj",
    )(xp, xs, g, w)


def _mixer_front(conv, dtr, dtb, alog, alog_x, rexp, tril, seg_ones):
    xact = conv * jax.nn.sigmoid(conv)
    xs = xact[:, :SSD_WIDTH]
    bm = xact[:, SSD_WIDTH:SSD_WIDTH + 256]
    cm = xact[:, SSD_WIDTH + 256:]
    dt = _softplus(dtr + dtb)
    a = dt * (-jnp.exp(alog))
    dt_x = _sel_right(dt, rexp)
    a_x = dt_x * (-jnp.exp(alog_x))
    acum = _sel_left(tril, a)
    acum_x = _sel_left(tril, a_x)
    if seg_ones is None:
        r = acum_x.shape[0]
        tot_x = jnp.broadcast_to(acum_x[r - 1:r, :], acum_x.shape)
    else:
        tot_x = _sel_left(seg_ones, a_x)
    return xs, bm, cm, dt_x, acum, acum_x, tot_x


def _ssd_intra(cmb, bmb, acum, xdt, mask):
    r = acum.shape[0]
    acum_t = acum.T
    lane = lax.broadcasted_iota(I32, (r, LANES), 1)
    low = lane < HEAD_DIM
    outs = []
    for g in range(SSD_GROUPS):
        sg = _dot_nt(cmb[:, LANES * g:LANES * (g + 1)], bmb[:, LANES * g:LANES * (g + 1)])
        for k in (2 * g, 2 * g + 1):
            parts = []
            for h in (2 * k, 2 * k + 1):
                seg = acum[:, h:h + 1] - acum_t[h:h + 1, :]
                parts.append((sg * jnp.exp(jnp.where(mask, seg, -jnp.inf))).astype(BF16))
            lhs = jnp.concatenate(parts, axis=1)
            xd = xdt[:, LANES * k:LANES * (k + 1)]
            rhs = jnp.concatenate([jnp.where(low, xd, 0.0), jnp.where(low, 0.0, xd)], axis=0).astype(BF16)
            outs.append(_dot(lhs, rhs))
    return jnp.concatenate(outs, axis=1)


def _mixer_back(y, z, u, v, sng, vng, vnb, wsp_ref, bsp, mog):
    r = y.shape[0]
    yg = y * (z * jax.nn.sigmoid(z))
    halves = []
    for g in range(SSD_GROUPS):
        t = yg[:, 256 * g:256 * (g + 1)]
        halves.append(_rms(t))
    yn = jnp.concatenate(halves, axis=1) * sng
    ug = jax.nn.gelu(u)
    vg = jax.nn.gelu(v)
    mu = jnp.mean(vg, axis=-1, keepdims=True)
    var = jnp.mean(jnp.square(vg - mu), axis=-1, keepdims=True)
    v_ln = (vg - mu) * lax.rsqrt(var + EPS) * vng + vnb
    lane = lax.broadcasted_iota(I32, (r, LANES), 1)
    low = lane < HEAD_DIM
    outs = []
    for k in range(MLP_HEADS // 2):
        vd = v_ln[:, LANES * k:LANES * (k + 1)]
        rhs = jnp.concatenate([jnp.where(low, vd, 0.0), jnp.where(low, 0.0, vd)], axis=0).astype(BF16)
        outs.append(_dot(wsp_ref[k], rhs))
    s = jnp.concatenate(outs, axis=1) + bsp
    m = _rms(ug * s) * mog
    return jnp.concatenate([yn, m], axis=1).astype(BF16), v_ln


_MIXER_PARAM_SHAPES = (
    (CONV_W, CONV_DIM), (1, CONV_DIM), (1, LANES), (1, LANES), (1, SSD_WIDTH), (LANES, SSD_WIDTH),
    (CHUNK, CHUNK), (1, SSD_WIDTH), (1, SSD_WIDTH), (1, MLP_WIDTH), (1, MLP_WIDTH),
    (MLP_HEADS // 2, CHUNK, 2 * CHUNK), (CHUNK, MLP_WIDTH), (1, MLP_WIDTH),
)


def _prompt_mixer_body(z_ref, xbc_ref, u_ref, v_ref, dt_ref,
                       cw_ref, cb_ref, dtb_ref, alog_ref, alogx_ref, rexp_ref, tril_ref, dskip_ref,
                       sng_ref, vng_ref, vnb_ref, wsp_ref, bsp_ref, mog_ref,
                       cat_ref, ssm_ref, ext_scr, s_scr):
    c = pl.program_id(1)
    r = CHUNK

    @pl.when(c == 0)
    def _():
        ext_scr[0:8, :] = jnp.zeros((8, CONV_DIM), F32)
        s_scr[...] = jnp.zeros_like(s_scr)

    x = xbc_ref[...]
    ext_scr[8:8 + r, :] = x
    cw = cw_ref[...]
    conv = (cb_ref[...] + cw[3:4] * x + cw[2:3] * ext_scr[7:7 + r, :]
            + cw[1:2] * ext_scr[6:6 + r, :] + cw[0:1] * ext_scr[5:5 + r, :])
    ext_scr[0:8, :] = x[r - 8:r, :]

    xs, bm, cm, dt_x, acum, acum_x, tot_x = _mixer_front(
        conv, dt_ref[...], dtb_ref[...], alog_ref[...], alogx_ref[...], rexp_ref[...], tril_ref[...], None)
    bmb, cmb = bm.astype(BF16), cm.astype(BF16)
    xdt = xs * dt_x
    row = lax.broadcasted_iota(I32, (r, r), 0)
    col = lax.broadcasted_iota(I32, (r, r), 1)
    y_diag = _ssd_intra(cmb, bmb, acum, xdt, row >= col)

    s_prev = s_scr[...]
    s_prev_b = s_prev.astype(BF16)
    y_off = jnp.concatenate(
        [_dot_nt(cmb[:, LANES * g:LANES * (g + 1)], s_prev_b[256 * g:256 * (g + 1), :]) for g in range(SSD_GROUPS)],
        axis=1)
    y = y_diag + y_off * jnp.exp(acum_x) + dskip_ref[...] * xs

    w_t = (xdt * jnp.exp(tot_x - acum_x)).T.astype(BF16)
    states = jnp.concatenate(
        [_dot(w_t[256 * g:256 * (g + 1), :], bmb[:, LANES * g:LANES * (g + 1)]) for g in range(SSD_GROUPS)], axis=0)
    s_new = s_prev * jnp.exp(tot_x).T + states
    s_scr[...] = s_new

    cat, _ = _mixer_back(y, z_ref[...], u_ref[...], v_ref[...], sng_ref[...], vng_ref[...], vnb_ref[...],
                         wsp_ref, bsp_ref[...], mog_ref[...])
    cat_ref[...] = cat

    @pl.when(c == pl.num_programs(1) - 1)
    def _():
        ssm_ref[0] = s_new


def _prompt_mixer_call(z, xbc, u, v, dtr, params, nb, nc):
    row = lambda b, c: (b * nc + c, 0)
    in_specs = [
        pl.BlockSpec((CHUNK, SSD_WIDTH), row), pl.BlockSpec((CHUNK, CONV_DIM), row),
        pl.BlockSpec((CHUNK, MLP_WIDTH), row), pl.BlockSpec((CHUNK, MLP_WIDTH), row),
        pl.BlockSpec((CHUNK, DT_PAD), row),
    ] + [_const_spec(s) for s in _MIXER_PARAM_SHAPES]
    return pl.pallas_call(
        _prompt_mixer_body,
        out_shape=[jax.ShapeDtypeStruct((nb * nc * CHUNK, D_MODEL), BF16),
                   jax.ShapeDtypeStruct((nb, SSD_WIDTH, D_STATE), F32)],
        grid=(nb, nc),
        in_specs=in_specs,
        out_specs=[pl.BlockSpec((CHUNK, D_MODEL), row),
                   pl.BlockSpec((1, SSD_WIDTH, D_STATE), lambda b, c: (b, 0, 0))],
        scratch_shapes=[pltpu.VMEM((CHUNK + 8, CONV_DIM), F32), pltpu.VMEM((SSD_WIDTH, D_STATE), F32)],
        compiler_params=_cparams(("arbitrary", "arbitrary")),
        name="prompt_mixer",
    )(z, xbc, u, v, dtr, *params)


def _sample_mixer_body(seq_len, z_ref, x0_ref, x1_ref, x2_ref, x3_ref, u_ref, v_ref, dt_ref, h_ref,
                       cw_ref, cb_ref, dtb_ref, alog_ref, alogx_ref, rexp_ref, tril_ref, dskip_ref,
                       sng_ref, vng_ref, vnb_ref, wsp_ref, bsp_ref, mog_ref, segones_ref,
                       cat_ref, vout_ref, hout_ref, cm_scr, bm_scr, wt_scr, dtt_scr, yoff_scr):
    r = CHUNK
    shift = seq_len.bit_length() - 1
    cw = cw_ref[...]
    conv = (cb_ref[...] + cw[3:4] * x0_ref[...] + cw[2:3] * x1_ref[...]
            + cw[1:2] * x2_ref[...] + cw[0:1] * x3_ref[...])
    xs, bm, cm, dt_x, acum, acum_x, tot_x = _mixer_front(
        conv, dt_ref[...], dtb_ref[...], alog_ref[...], alogx_ref[...], rexp_ref[...], tril_ref[...],
        segones_ref[...])
    bmb, cmb = bm.astype(BF16), cm.astype(BF16)
    xdt = xs * dt_x
    row = lax.broadcasted_iota(I32, (r, r), 0)
    col = lax.broadcasted_iota(I32, (r, r), 1)
    same = lax.shift_right_logical(row, shift) == lax.shift_right_logical(col, shift)
    y_diag = _ssd_intra(cmb, bmb, acum, xdt, same & (row >= col))

    cm_scr[...] = cm
    bm_scr[...] = bmb
    wt_scr[...] = (xdt * jnp.exp(tot_x - acum_x)).T
    dtt_scr[...] = jnp.exp(tot_x).T
    ones_b = jnp.ones((LANES, LANES), BF16)
    seqs_per_slab = 8 // seq_len

    def slab(j, carry):
        rows = pl.ds(pl.multiple_of(8 * j, 8), 8)
        cms = cm_scr[rows, :].astype(BF16)
        sub = lax.broadcasted_iota(I32, (8, 256), 0)
        lane = lax.broadcasted_iota(I32, (256, LANES), 1)
        for g in range(SSD_GROUPS):
            q_rows = slice(256 * g, 256 * (g + 1))
            acc = jnp.zeros((8, 256), F32)
            for q in range(seqs_per_slab):
                s = seqs_per_slab * j + q
                y_s = _dot_nt(cms[:, LANES * g:LANES * (g + 1)], h_ref[s, q_rows, :].astype(BF16))
                acc = jnp.where(lax.shift_right_logical(sub, shift) == q, y_s, acc)
            yoff_scr[rows, 256 * g:256 * (g + 1)] = acc
            for q in range(seqs_per_slab):
                s = seqs_per_slab * j + q
                w_sel = jnp.where(lax.shift_right_logical(lane, shift) == s, wt_scr[q_rows, :], 0.0).astype(BF16)
                st = _dot(w_sel, bm_scr[:, LANES * g:LANES * (g + 1)])
                d_sel = jnp.where(lane == s * seq_len, dtt_scr[q_rows, :], 0.0)
                hout_ref[s, q_rows, :] = h_ref[s, q_rows, :] * _sel_right(d_sel, ones_b) + st
        return carry

    lax.fori_loop(0, r // 8, slab, 0, unroll=2)

    y = y_diag + yoff_scr[...] * jnp.exp(acum_x) + dskip_ref[...] * xs
    cat, v_ln = _mixer_back(y, z_ref[...], u_ref[...], v_ref[...], sng_ref[...], vng_ref[...], vnb_ref[...],
                            wsp_ref, bsp_ref[...], mog_ref[...])
    cat_ref[...] = cat
    vout_ref[...] = v_ln


def _sample_mixer_call(z, x_shift, u, v, dtr, h0, params, seg_ones, row0, seq_len):
    ts = x_shift[0].shape[0]
    n = ts // CHUNK
    spt = CHUNK // seq_len
    off = lambda i: (row0 + i, 0)
    loc = lambda i: (i, 0)
    st3 = lambda i: (i, 0, 0)
    in_specs = (
        [pl.BlockSpec((CHUNK, SSD_WIDTH), off)]
        + [pl.BlockSpec((CHUNK, CONV_DIM), loc)] * 4
        + [pl.BlockSpec((CHUNK, MLP_WIDTH), off), pl.BlockSpec((CHUNK, MLP_WIDTH), off),
           pl.BlockSpec((CHUNK, DT_PAD), off), pl.BlockSpec((spt, SSD_WIDTH, D_STATE), st3)]
        + [_const_spec(s) for s in _MIXER_PARAM_SHAPES] + [_const_spec((CHUNK, CHUNK))])
    return pl.pallas_call(
        functools.partial(_sample_mixer_body, seq_len),
        out_shape=[jax.ShapeDtypeStruct((ts, D_MODEL), BF16), jax.ShapeDtypeStruct((ts, MLP_WIDTH), F32),
                   jax.ShapeDtypeStruct(h0.shape, F32)],
        grid=(n,),
        in_specs=in_specs,
        out_specs=[pl.BlockSpec((CHUNK, D_MODEL), loc), pl.BlockSpec((CHUNK, MLP_WIDTH), loc),
                   pl.BlockSpec((spt, SSD_WIDTH, D_STATE), st3)],
        scratch_shapes=[pltpu.VMEM((CHUNK, 256), F32), pltpu.VMEM((CHUNK, 256), BF16),
                        pltpu.VMEM((SSD_WIDTH, CHUNK), F32), pltpu.VMEM((SSD_WIDTH, CHUNK), F32),
                        pltpu.VMEM((CHUNK, SSD_WIDTH), F32)],
        compiler_params=_cparams(("arbitrary",)),
        name="sample_mixer",
    )(z, *x_shift, u, v, dtr, h0, *params, seg_ones)


def _out_router_call(cat_p, cat_s, xp, xs, w_out, g_moe, wr_hi, wr_lo, b_r, tm):
    tp, ts = xp.shape[0], xs.shape[0]
    n_p, n_s = tp // tm, ts // tm
    t_all = tp + ts

    def body(cp_ref, cs_ref, xp_ref, xs_ref, wo_ref, g_ref, wh_ref, wl_ref, br_ref,
             h1_ref, m_ref, eid_ref, gate_ref):
        def run(c_ref, x_ref):
            h1 = x_ref[...] + _dot(c_ref[...], wo_ref[...])
            h1_ref[...] = h1
            m = _rms(h1) * g_ref[...]
            for j in range(TOKEN_TILE_ROWS):
                m_ref[pl.ds(j, tm, stride=TOKEN_TILE_ROWS), :] = m[:, LANES * j:LANES * (j + 1)]
            m_hi = m.astype(BF16)
            m_lo = (m - m_hi.astype(F32)).astype(BF16)
            logits = _dot(m_hi, wh_ref[...]) + _dot(m_lo, wh_ref[...]) + _dot(m_hi, wl_ref[...]) + br_ref[...]
            lane = lax.broadcasted_iota(I32, logits.shape, 1).astype(F32)
            work = logits
            vals, ids = [], []
            for _ in range(TOP_K):
                mx = jnp.max(work, axis=-1, keepdims=True)
                idx = jnp.min(jnp.where(work == mx, lane, float(LANES)), axis=-1, keepdims=True)
                vals.append(mx)
                ids.append(idx)
                work = jnp.where(lane == idx, -jnp.inf, work)
            ex = [jnp.exp(vv - vals[0]) for vv in vals]
            den = ex[0] + ex[1] + ex[2] + ex[3]
            eid = jnp.zeros(logits.shape, I32)
            gate = jnp.zeros(logits.shape, F32)
            for k in range(TOP_K):
                eid = jnp.where(lane == k, ids[k].astype(I32), eid)
                gate = jnp.where(lane == k, ex[k] / den, gate)
            eid_ref[...] = eid
            gate_ref[...] = gate

        i = pl.program_id(0)

        @pl.when(i < n_p)
        def _():
            run(cp_ref, xp_ref)

        @pl.when(i >= n_p)
        def _():
            run(cs_ref, xs_ref)

    pmap = lambda i: (jnp.minimum(i, n_p - 1), 0)
    smap = lambda i: (jnp.maximum(i - n_p, 0), 0)
    omap = lambda i: (i, 0)
    return pl.pallas_call(
        body,
        out_shape=[jax.ShapeDtypeStruct((t_all, D_MODEL), F32),
                   jax.ShapeDtypeStruct((t_all * TOKEN_TILE_ROWS, LANES), F32),
                   jax.ShapeDtypeStruct((t_all, LANES), I32), jax.ShapeDtypeStruct((t_all, LANES), F32)],
        grid=(n_p + n_s,),
        in_specs=[pl.BlockSpec((tm, D_MODEL), pmap), pl.BlockSpec((tm, D_MODEL), smap),
                  pl.BlockSpec((tm, D_MODEL), pmap), pl.BlockSpec((tm, D_MODEL), smap),
                  _const_spec((D_MODEL, D_MODEL)), _const_spec((1, D_MODEL)),
                  _const_spec((D_MODEL, LANES)), _const_spec((D_MODEL, LANES)), _const_spec((1, LANES))],
        out_specs=[pl.BlockSpec((tm, D_MODEL), omap), pl.BlockSpec((tm * TOKEN_TILE_ROWS, LANES), omap),
                   pl.BlockSpec((tm, LANES), omap), pl.BlockSpec((tm, LANES), omap)],
        compiler_params=_cparams(("arbitrary",)),
        name="out_router",
    )(cat_p, cat_s, xp, xs, w_out, g_moe, wr_hi, wr_lo, b_r)


def _route(eid, tm, nb):
    t = eid.shape[0]
    tk = t * TOP_K
    flat = eid.reshape(tk)
    _, order = lax.sort((flat, jnp.arange(tk, dtype=I32)), num_keys=1, is_stable=True)
    counts = jnp.sum((flat[:, None] == jnp.arange(N_EXPERTS, dtype=I32)[None, :]).astype(I32), axis=0)
    nblk = (counts + tm - 1) // tm
    bend = jnp.cumsum(nblk)
    bstart = bend - nblk
    start = jnp.cumsum(counts) - counts
    nused = bend[-1]
    blk = jnp.arange(nb, dtype=I32)
    used = blk < nused
    be = jnp.minimum(jnp.sum((jnp.minimum(blk, nused - 1)[:, None] >= bend[None, :]).astype(I32), axis=1),
                     N_EXPERTS - 1)
    sel = (be[:, None] == jnp.arange(N_EXPERTS, dtype=I32)[None, :]).astype(I32)
    pick = lambda v: jnp.sum(sel * v[None, :], axis=1)
    done = (blk - pick(bstart)) * tm
    nval = jnp.where(used, jnp.clip(pick(counts) - done, 0, tm), 0).astype(I32)
    off = jnp.where(used, pick(start) + done, 0).astype(I32)
    pad = (-(-(tk + tm) // LANES) + _id_rows(tm)) * LANES - tk
    tok = jnp.pad(lax.shift_right_logical(order, TOPK_SHIFT) * TOKEN_TILE_ROWS, (0, pad))
    dst = jnp.pad(((order & (TOP_K - 1)) * t + lax.shift_right_logical(order, TOPK_SHIFT)) * TOKEN_TILE_ROWS,
                  (0, pad))
    return be, nval, off, tok, dst


def _id_rows(tm):
    return tm // LANES + 1


def _moe_body(tm, t_all, nb, be_ref, nval_ref, off_ref, tok_hbm, dst_hbm, m_hbm, wup_ref, bup_ref, wdn_ref, bdn_ref,
              perm_ref, y_hbm, gids, sids, xbuf, ybuf, wup_b, wdn_b, isem, gsem, ssem):
    i = pl.program_id(0)
    nv = nval_ref[i]
    slot = i & 1
    prv = jnp.maximum(i - 1, 0)
    nxt = jnp.minimum(i + 1, nb - 1)
    nx2 = jnp.minimum(i + 2, nb - 1)
    has_next = (i + 1 < nb) & (nval_ref[nxt] > 0)
    has_next2 = (i + 2 < nb) & (nval_ref[nx2] > 0)
    n_prev = jnp.where(i > 0, nval_ref[prv], 0)
    win = _id_rows(tm) * LANES
    spare = TOP_K * t_all * TOKEN_TILE_ROWS

    def ids_copies(b):
        start = pl.multiple_of(lax.shift_right_logical(off_ref[b], 7) * LANES, LANES)
        ring = pl.ds(pl.multiple_of((b & 3) * win, LANES), win)
        return (pltpu.make_async_copy(tok_hbm.at[pl.ds(start, win)], gids.at[ring], isem.at[b & 3, 0]),
                pltpu.make_async_copy(dst_hbm.at[pl.ds(start, win)], sids.at[ring], isem.at[b & 3, 1]))

    def id_base(b):
        return (b & 3) * win + (off_ref[b] & (LANES - 1))

    def gather_row(base, s, r):
        src = pl.ds(pl.multiple_of(gids[base + r], TOKEN_TILE_ROWS), TOKEN_TILE_ROWS)
        return pltpu.make_async_copy(m_hbm.at[src], xbuf.at[s, pl.ds(TOKEN_TILE_ROWS * r, TOKEN_TILE_ROWS)],
                                     gsem.at[s])

    def scatter_row(base, s, r, n):
        dest = jnp.where(r < n, sids[base + r], spare + TOKEN_TILE_ROWS * r)
        rows = pl.ds(pl.multiple_of(dest, TOKEN_TILE_ROWS), TOKEN_TILE_ROWS)
        return pltpu.make_async_copy(ybuf.at[s, pl.ds(TOKEN_TILE_ROWS * r, TOKEN_TILE_ROWS)], y_hbm.at[rows],
                                     ssem.at[s])

    def wait_gathers(s):
        pltpu.make_async_copy(m_hbm.at[pl.ds(0, tm * TOKEN_TILE_ROWS)], xbuf.at[s], gsem.at[s]).wait()

    def wait_scatters(s):
        pltpu.make_async_copy(ybuf.at[s], y_hbm.at[pl.ds(0, tm * TOKEN_TILE_ROWS)], ssem.at[s]).wait()

    def for_rows(fn):
        def one(r, c):
            fn(r)
            return c

        lax.fori_loop(0, tm, one, 0)

    @pl.when(nv > 0)
    def _():
        @pl.when(i == 0)
        def _():
            ybuf[...] = jnp.zeros_like(ybuf)
            fill = pltpu.make_async_copy(ybuf.at[0], y_hbm.at[pl.ds(spare, tm * TOKEN_TILE_ROWS)], ssem.at[0])
            fill.start()
            fill.wait()
            for b in range(4):
                for cp in ids_copies(b):
                    cp.start()
                    cp.wait()
            base0 = id_base(0)
            for_rows(lambda r: gather_row(base0, 0, r).start())

        @pl.when(has_next & (i >= 3))
        def _():
            for cp in ids_copies(nxt):
                cp.wait()

        @pl.when(has_next2 & (i >= 2))
        def _():
            for cp in ids_copies(nx2):
                cp.start()

        @pl.when((i == 0) | (be_ref[i] != be_ref[prv]))
        def _():
            for jb in range(2 * D_FF // 256):
                cols = slice(256 * jb, 256 * (jb + 1))
                wup_b[:, cols] = _dot(wup_ref[0, :, cols].astype(BF16), perm_ref[...]).astype(BF16)
            wdn_b[...] = wdn_ref[0].astype(BF16)

        y_cur = lax.rem(i, 3)
        y_prev = lax.rem(i + 2, 3)
        y_prev2 = lax.rem(i + 1, 3)

        def ffn_step(cur, oth):
            wait_gathers(cur)

            @pl.when(i >= 2)
            def _():
                wait_scatters(y_cur)

            g_base = id_base(nxt)
            s_base = id_base(prv)
            for r in range(tm):
                gather_row(g_base, oth, r).start()
                scatter_row(s_base, y_prev, r, n_prev).start()

            x = jnp.concatenate([xbuf[slot, pl.ds(j, tm, stride=TOKEN_TILE_ROWS), :]
                                 for j in range(TOKEN_TILE_ROWS)], axis=1).astype(BF16)
            acts = []
            for jb in range(D_FF // LANES):
                h = _dot(x, wup_b[:, 256 * jb:256 * (jb + 1)]) + bup_ref[0, :, 256 * jb:256 * (jb + 1)]
                gate = jnp.minimum(h[:, :LANES], SWIGLU_LIMIT)
                lin = jnp.clip(h[:, LANES:], -SWIGLU_LIMIT, SWIGLU_LIMIT)
                acts.append((gate * jax.nn.sigmoid(SWIGLU_ALPHA * gate) * (lin + 1.0)).astype(BF16))
            act = jnp.concatenate(acts, axis=1)
            for c in range(D_MODEL // 256):
                yc = _dot(act, wdn_b[:, 256 * c:256 * (c + 1)]) + bdn_ref[0, :, 256 * c:256 * (c + 1)]
                for half in range(2):
                    ybuf[y_cur, pl.ds(2 * c + half, tm, stride=TOKEN_TILE_ROWS), :] = (
                        yc[:, LANES * half:LANES * (half + 1)])

            @pl.when(jnp.logical_not(has_next))
            def _():
                wait_gathers(oth)

                @pl.when(i >= 1)
                def _():
                    wait_scatters(y_prev2)
                wait_scatters(y_prev)
                last_base = id_base(i)
                for_rows(lambda r: scatter_row(last_base, y_cur, r, nv).start())
                wait_scatters(y_cur)

        for parity in range(2):
            pl.when(slot == parity)(functools.partial(ffn_step, parity, 1 - parity))


def _moe_call(m, be, nval, off, tok, dst, w_up, b_up_g, w_down, b_down, perm, tm, nb):
    t = m.shape[0] // TOKEN_TILE_ROWS
    assert nb >= 4
    by_expert = lambda i, be, nv, off: (be[i], 0, 0)
    grid_spec = pltpu.PrefetchScalarGridSpec(
        num_scalar_prefetch=3,
        grid=(nb,),
        in_specs=[
            pl.BlockSpec(memory_space=pl.ANY),
            pl.BlockSpec(memory_space=pl.ANY),
            pl.BlockSpec(memory_space=pl.ANY),
            pl.BlockSpec((1, D_MODEL, 2 * D_FF), by_expert),
            pl.BlockSpec((1, 1, 2 * D_FF), by_expert),
            pl.BlockSpec((1, D_FF, D_MODEL), by_expert),
            pl.BlockSpec((1, 1, D_MODEL), by_expert),
            pl.BlockSpec((256, 256), lambda i, be, nv, off: (0, 0)),
        ],
        out_specs=pl.BlockSpec(memory_space=pl.ANY),
        scratch_shapes=[pltpu.SMEM((4 * _id_rows(tm) * LANES,), I32), pltpu.SMEM((4 * _id_rows(tm) * LANES,), I32),
                        pltpu.VMEM((2, tm * TOKEN_TILE_ROWS, LANES), F32),
                        pltpu.VMEM((3, tm * TOKEN_TILE_ROWS, LANES), F32),
                        pltpu.VMEM((D_MODEL, 2 * D_FF), BF16), pltpu.VMEM((D_FF, D_MODEL), BF16),
                        pltpu.SemaphoreType.DMA((4, 2)), pltpu.SemaphoreType.DMA((2,)),
                        pltpu.SemaphoreType.DMA((3,))],
    )
    return pl.pallas_call(
        functools.partial(_moe_body, tm, t, nb),
        out_shape=jax.ShapeDtypeStruct(((TOP_K * t + tm) * TOKEN_TILE_ROWS, LANES), F32),
        grid_spec=grid_spec,
        compiler_params=_cparams(("arbitrary",)),
        name="moe_experts",
    )(be, nval, off, tok, dst, m, w_up, b_up_g, w_down, b_down, perm)


def _ple_call(h1, y4, gates, pp, ps, g_ple, w_gate, w_proj, g_final, tm):
    tp, ts = pp.shape[0], ps.shape[0]
    n_p, n_s = tp // tm, ts // tm
    ple = pp.shape[1]

    def body(h1_ref, y0_ref, y1_ref, y2_ref, y3_ref, gt_ref, pp_ref, ps_ref, g_ref, wg_ref, wp_ref, gf_ref,
             yp_ref, ys_ref):
        def run(p_ref, o_ref):
            gt = gt_ref[...]
            moe = None
            for k, y_ref in enumerate((y0_ref, y1_ref, y2_ref, y3_ref)):
                y_k = jnp.concatenate([y_ref[pl.ds(j, tm, stride=TOKEN_TILE_ROWS), :]
                                       for j in range(TOKEN_TILE_ROWS)], axis=1)
                moe = gt[:, k:k + 1] * y_k if moe is None else moe + gt[:, k:k + 1] * y_k
            h2 = h1_ref[...] + moe
            a = (_rms(h2) * g_ref[...]).astype(BF16)
            gate = jax.nn.sigmoid(_dot(a, wg_ref[...]))
            pe = _dot(p_ref[...].astype(BF16), wp_ref[...])
            h3 = h2 + pe * gate
            o_ref[...] = _rms(h3) * gf_ref[...]

        i = pl.program_id(0)

        @pl.when(i < n_p)
        def _():
            run(pp_ref, yp_ref)

        @pl.when(i >= n_p)
        def _():
            run(ps_ref, ys_ref)

    pmap = lambda i: (jnp.minimum(i, n_p - 1), 0)
    smap = lambda i: (jnp.maximum(i - n_p, 0), 0)
    omap = lambda i: (i, 0)
    return pl.pallas_call(
        body,
        out_shape=[jax.ShapeDtypeStruct((tp, D_MODEL), F32), jax.ShapeDtypeStruct((ts, D_MODEL), F32)],
        grid=(n_p + n_s,),
        in_specs=[pl.BlockSpec((tm, D_MODEL), omap)]
                 + [pl.BlockSpec((tm * TOKEN_TILE_ROWS, LANES),
                                 functools.partial(lambda k, i: (k * (n_p + n_s) + i, 0), k)) for k in range(TOP_K)]
                 + [pl.BlockSpec((tm, LANES), omap), pl.BlockSpec((tm, ple), pmap), pl.BlockSpec((tm, ple), smap),
                  _const_spec((1, D_MODEL)), _const_spec((D_MODEL, D_MODEL)), _const_spec((ple, D_MODEL)),
                  _const_spec((1, D_MODEL))],
        out_specs=[pl.BlockSpec((tm, D_MODEL), pmap), pl.BlockSpec((tm, D_MODEL), smap)],
        compiler_params=_cparams(("arbitrary",)),
        name="ple_final",
    )(h1, y4, y4, y4, y4, gates, pp, ps, g_ple, w_gate, w_proj, g_final)


def _row(x, width=None):
    x = x.reshape(1, -1).astype(F32)
    if width is not None and x.shape[1] < width:
        x = jnp.pad(x, ((0, 0), (0, width - x.shape[1])))
    return x


def _mixer_params(conv_w, conv_b, dt_bias, a_log, d_skip, ssd_norm_g, v_norm_g, v_norm_b, w_spatial, b_spatial,
                  mlp_out_g, seq_len):
    pos = jnp.arange(CHUNK) % seq_len
    same = (jnp.arange(CHUNK)[:, None] // seq_len) == (jnp.arange(CHUNK)[None, :] // seq_len)
    tril = (same & (jnp.arange(CHUNK)[:, None] >= jnp.arange(CHUNK)[None, :])).astype(BF16)
    rexp = (jnp.arange(LANES)[:, None] == (jnp.arange(SSD_WIDTH)[None, :] // HEAD_DIM)).astype(BF16)
    w_loc = jnp.tril(w_spatial[:, :seq_len, :seq_len])
    onehot = (pos[:, None] == jnp.arange(seq_len)[None, :]).astype(F32)
    tiled = jnp.einsum("iq,hqr,jr->hij", onehot, w_loc.astype(F32), onehot, precision=lax.Precision.HIGHEST)
    w_bd = jnp.where(same[None], tiled, 0.0)
    wsp = (w_bd.reshape(MLP_HEADS // 2, 2, CHUNK, CHUNK).transpose(0, 2, 1, 3)
           .reshape(MLP_HEADS // 2, CHUNK, 2 * CHUNK).astype(BF16))
    bsp = jnp.repeat(b_spatial[:, :seq_len].T[pos], MLP_WIDTH // MLP_HEADS, axis=1)
    params = (
        conv_w.astype(F32), _row(conv_b), _row(dt_bias, LANES), _row(a_log, LANES),
        _row(jnp.repeat(a_log, HEAD_DIM)), rexp, tril, _row(jnp.repeat(d_skip, HEAD_DIM)),
        _row(ssd_norm_g), _row(v_norm_g), _row(v_norm_b), wsp, bsp.astype(F32), _row(mlp_out_g),
    )
    return params, same.astype(BF16)


def _tile_rows(n):
    return 512 if n % 512 == 0 else CHUNK


def kernel(x_prompt, x_sample, state_ssm, state_conv, p_prompt, p_sample, norm_mix_g, w_in, conv_w, conv_b, dt_bias, a_log, d_skip, ssd_norm_g, v_norm_g, v_norm_b, w_spatial, b_spatial, mlp_out_g, w_out, norm_moe_g, w_router, b_router, w_up, b_up, w_down, b_down, norm_ple_g, w_ple_gate, w_ple_proj, norm_final_g):
    depth = norm_mix_g.shape[0]
    bp, lp, d = x_prompt.shape
    bs, ls, _ = x_sample.shape
    tp, ts = bp * lp, bs * ls
    assert depth == 1 and d == D_MODEL and lp % CHUNK == 0 and ts % CHUNK == 0 and 8 % ls == 0
    tm = _tile_rows(tp) if ts % _tile_rows(tp) == 0 else CHUNK
    t_all = tp + ts
    tm_moe = MOE_BLOCK_ROWS
    nb_moe = -(-t_all * TOP_K // tm_moe) + N_EXPERTS

    hp = x_prompt.reshape(tp, d)
    hs = x_sample.reshape(ts, d)
    ssm_p, conv_p, ssm_s, conv_s, v_s = [], [], [], [], []
    o1 = SSD_WIDTH
    o2 = o1 + CONV_DIM
    o3 = o2 + SSD_HEADS
    c = jnp.arange(256)
    src = jnp.where(c < LANES, 2 * c, 2 * (c - LANES) + 1)
    perm = (jnp.arange(256)[:, None] == src[None, :]).astype(BF16)

    for i in range(depth):
        wi = w_in[i]
        w_cat = jnp.concatenate(
            [wi[:, :o2], wi[:, o3:], jnp.pad(wi[:, o2:o3], ((0, 0), (0, DT_PAD - SSD_HEADS)))], axis=1).astype(BF16)
        z, xbc, u, v, dtr = _inproj_call(hp, hs, _row(norm_mix_g[i]), w_cat, tm)

        mix_args = (conv_w[i], conv_b[i], dt_bias[i], a_log[i], d_skip[i], ssd_norm_g[i], v_norm_g[i], v_norm_b[i],
                    w_spatial[i], b_spatial[i], mlp_out_g[i])
        prm_p, _ = _mixer_params(*mix_args, seq_len=CHUNK)
        cat_p, s_p = _prompt_mixer_call(z, xbc, u, v, dtr, prm_p, bp, lp // CHUNK)
        ssm_p.append(s_p.reshape(bp, SSD_HEADS, HEAD_DIM, D_STATE).astype(state_ssm.dtype))
        conv_p.append(jnp.stack([xbc[(b + 1) * lp - (CONV_W - 1):(b + 1) * lp] for b in range(bp)]))

        prm_s, seg_ones = _mixer_params(*mix_args, seq_len=ls)
        xbc_s = xbc[tp:].reshape(bs, ls, CONV_DIM)
        xpad = jnp.concatenate([state_conv[i].astype(F32), xbc_s], axis=1)
        x_shift = [xpad[:, CONV_W - 1 - k:CONV_W - 1 - k + ls].reshape(ts, CONV_DIM) for k in range(CONV_W)]
        h0 = state_ssm[i].astype(F32).reshape(bs, SSD_WIDTH, D_STATE)
        cat_s, v_rows, s_s = _sample_mixer_call(z, x_shift, u, v, dtr, h0, prm_s, seg_ones, tp // CHUNK, ls)
        ssm_s.append(s_s.reshape(bs, SSD_HEADS, HEAD_DIM, D_STATE).astype(state_ssm.dtype))
        conv_s.append(xpad[:, ls:])
        v_s.append(v_rows.reshape(bs, ls, MLP_WIDTH))

        wr = jnp.pad(w_router[i].astype(F32), ((0, 0), (0, LANES - N_EXPERTS)))
        wr_hi = wr.astype(BF16)
        wr_lo = (wr - wr_hi.astype(F32)).astype(BF16)
        b_r = jnp.concatenate([b_router[i].astype(F32), jnp.full((LANES - N_EXPERTS,), -1e30, F32)]).reshape(1, LANES)
        h1, m, eid, gates = _out_router_call(cat_p, cat_s, hp, hs, w_out[i].astype(BF16), _row(norm_moe_g[i]),
                                             wr_hi, wr_lo, b_r, tm)

        be, nval, off, tok, dst = _route(eid[:, :TOP_K], tm_moe, nb_moe)
        b_up_g = (b_up[i].astype(F32).reshape(N_EXPERTS, 2 * D_FF // 256, LANES, 2).transpose(0, 1, 3, 2)
                  .reshape(N_EXPERTS, 1, 2 * D_FF))
        y4 = _moe_call(m, be, nval, off, tok, dst, w_up[i], b_up_g, w_down[i],
                       b_down[i].reshape(N_EXPERTS, 1, D_MODEL), perm, tm_moe, nb_moe)

        hp, hs = _ple_call(h1, y4, gates,
                           p_prompt[i].reshape(tp, -1), p_sample[i].reshape(ts, -1), _row(norm_ple_g[i]),
                           w_ple_gate[i].astype(BF16), w_ple_proj[i].astype(BF16), _row(norm_final_g), tm)

    y_prompt = hp.reshape(bp, lp, d)
    y_sample = hs.reshape(bs, ls, d)
    return (y_prompt, y_sample, jnp.stack(ssm_p), jnp.stack(conv_p), jnp.stack(ssm_s), jnp.stack(conv_s),
            jnp.stack(v_s))
```

```python
import functools

import jax
import jax.numpy as jnp
from jax import lax
from jax.experimental import pallas as pl
from jax.experimental.pallas import tpu as pltpu

F32 = jnp.float32
BF16 = jnp.bfloat16
I32 = jnp.int32

EPS = 1e-6
D_MODEL = 1024
SSD_WIDTH = 512
SSD_HEADS = 8
HEAD_DIM = 64
SSD_GROUPS = 2
D_STATE = 128
CONV_W = 4
CONV_DIM = SSD_WIDTH + 2 * SSD_GROUPS * D_STATE
MLP_WIDTH = 512
MLP_HEADS = 8
N_EXPERTS = 32
TOP_K = 4
D_FF = 1024
SWIGLU_LIMIT = 7.0
SWIGLU_ALPHA = 1.702
TOPK_SHIFT = 2
assert 1 << TOPK_SHIFT == TOP_K
LANES = 128
CHUNK = 128
DT_PAD = LANES
TOKEN_TILE_ROWS = D_MODEL // LANES
MOE_BLOCK_ROWS = 256
IN_PAD = SSD_WIDTH + CONV_DIM + 2 * MLP_WIDTH + DT_PAD
VMEM_LIMIT = 56 * 1024 * 1024


def _cparams(sem):
    return pltpu.CompilerParams(dimension_semantics=sem, vmem_limit_bytes=VMEM_LIMIT)


def _const_spec(shape):
    return pl.BlockSpec(shape, lambda *_: (0,) * len(shape))


def _rms(x):
    return x * lax.rsqrt(jnp.mean(x * x, axis=-1, keepdims=True) + EPS)


def _dot(a, b):
    return jnp.dot(a, b, preferred_element_type=F32)


def _dot_nt(a, b):
    return lax.dot_general(a, b, (((1,), (1,)), ((), ())), preferred_element_type=F32)


def _split3(x):
    hi = x.astype(BF16)
    r = x - hi.astype(F32)
    mid = r.astype(BF16)
    lo = (r - mid.astype(F32)).astype(BF16)
    return hi, mid, lo


def _sel_right(x, m01):
    hi, mid, lo = _split3(x)
    return _dot(hi, m01) + _dot(mid, m01) + _dot(lo, m01)


def _sel_left(m01, x):
    hi, mid, lo = _split3(x)
    return _dot(m01, hi) + _dot(m01, mid) + _dot(m01, lo)


def _softplus(x):
    return jnp.maximum(x, 0.0) + jnp.log1p(jnp.exp(-jnp.abs(x)))


def _inproj_call(xp, xs, g, w, tm):
    tp, ts = xp.shape[0], xs.shape[0]
    n_p, n_s = tp // tm, ts // tm
    t_all = tp + ts
    segs = ((0, 512), (512, 1536), (1536, 2048), (2048, 2560), (2560, IN_PAD))

    def body(xp_ref, xs_ref, g_ref, w_ref, *outs):
        def run(x_ref):
            xn = (_rms(x_ref[...]) * g_ref[...]).astype(BF16)
            for (a, b), o in zip(segs, outs):
                o[...] = _dot(xn, w_ref[:, a:b])

        i = pl.program_id(0)

        @pl.when(i < n_p)
        def _():
            run(xp_ref)

        @pl.when(i >= n_p)
        def _():
            run(xs_ref)

    widths = [b - a for a, b in segs]
    return pl.pallas_call(
        body,
        out_shape=[jax.ShapeDtypeStruct((t_all, wd), F32) for wd in widths],
        grid=(n_p + n_s,),
        in_specs=[
            pl.BlockSpec((tm, D_MODEL), lambda i: (jnp.minimum(i, n_p - 1), 0)),
            pl.BlockSpec((tm, D_MODEL), lambda i: (jnp.maximum(i - n_p, 0), 0)),
            _const_spec((1, D_MODEL)),
            _const_spec((D_MODEL, IN_PAD)),
        ],
        out_specs=[pl.BlockSpec((tm, wd), lambda i: (i, 0)) for wd in widths],
        compiler_params=_cparams(("arbitrary",)),
        name="in_proj",
    )(xp, xs, g, w)


def _mixer_front(conv, dtr, dtb, alog, alog_x, rexp, tril, seg_ones):
    xact = conv * jax.nn.sigmoid(conv)
    xs = xact[:, :SSD_WIDTH]
    bm = xact[:, SSD_WIDTH:SSD_WIDTH + 256]
    cm = xact[:, SSD_WIDTH + 256:]
    dt = _softplus(dtr + dtb)
    a = dt * (-jnp.exp(alog))
    dt_x = _sel_right(dt, rexp)
    a_x = dt_x * (-jnp.exp(alog_x))
    acum = _sel_left(tril, a)
    acum_x = _sel_left(tril, a_x)
    if seg_ones is None:
        r = acum_x.shape[0]
        tot_x = jnp.broadcast_to(acum_x[r - 1:r, :], acum_x.shape)
    else:
        tot_x = _sel_left(seg_ones, a_x)
    return xs, bm, cm, dt_x, acum, acum_x, tot_x


def _ssd_intra(cmb, bmb, acum, xdt, mask):
    r = acum.shape[0]
    acum_t = acum.T
    lane = lax.broadcasted_iota(I32, (r, LANES), 1)
    low = lane < HEAD_DIM
    outs = []
    for g in range(SSD_GROUPS):
        sg = _dot_nt(cmb[:, LANES * g:LANES * (g + 1)], bmb[:, LANES * g:LANES * (g + 1)])
        for k in (2 * g, 2 * g + 1):
            parts = []
            for h in (2 * k, 2 * k + 1):
                seg = acum[:, h:h + 1] - acum_t[h:h + 1, :]
                parts.append((sg * jnp.exp(jnp.where(mask, seg, -jnp.inf))).astype(BF16))
            lhs = jnp.concatenate(parts, axis=1)
            xd = xdt[:, LANES * k:LANES * (k + 1)]
            rhs = jnp.concatenate([jnp.where(low, xd, 0.0), jnp.where(low, 0.0, xd)], axis=0).astype(BF16)
            outs.append(_dot(lhs, rhs))
    return jnp.concatenate(outs, axis=1)


def _mixer_back(y, z, u, v, sng, vng, vnb, wsp_ref, bsp, mog):
    r = y.shape[0]
    yg = y * (z * jax.nn.sigmoid(z))
    halves = []
    for g in range(SSD_GROUPS):
        t = yg[:, 256 * g:256 * (g + 1)]
        halves.append(_rms(t))
    yn = jnp.concatenate(halves, axis=1) * sng
    ug = jax.nn.gelu(u)
    vg = jax.nn.gelu(v)
    mu = jnp.mean(vg, axis=-1, keepdims=True)
    var = jnp.mean(jnp.square(vg - mu), axis=-1, keepdims=True)
    v_ln = (vg - mu) * lax.rsqrt(var + EPS) * vng + vnb
    lane = lax.broadcasted_iota(I32, (r, LANES), 1)
    low = lane < HEAD_DIM
    outs = []
    for k in range(MLP_HEADS // 2):
        vd = v_ln[:, LANES * k:LANES * (k + 1)]
        rhs = jnp.concatenate([jnp.where(low, vd, 0.0), jnp.where(low, 0.0, vd)], axis=0).astype(BF16)
        outs.append(_dot(wsp_ref[k], rhs))
    s = jnp.concatenate(outs, axis=1) + bsp
    m = _rms(ug * s) * mog
    return jnp.concatenate([yn, m], axis=1).astype(BF16), v_ln


_MIXER_PARAM_SHAPES = (
    (CONV_W, CONV_DIM), (1, CONV_DIM), (1, LANES), (1, LANES), (1, SSD_WIDTH), (LANES, SSD_WIDTH),
    (CHUNK, CHUNK), (1, SSD_WIDTH), (1, SSD_WIDTH), (1, MLP_WIDTH), (1, MLP_WIDTH),
    (MLP_HEADS // 2, CHUNK, 2 * CHUNK), (CHUNK, MLP_WIDTH), (1, MLP_WIDTH),
)


def _prompt_mixer_body(z_ref, xbc_ref, u_ref, v_ref, dt_ref,
                       cw_ref, cb_ref, dtb_ref, alog_ref, alogx_ref, rexp_ref, tril_ref, dskip_ref,
                       sng_ref, vng_ref, vnb_ref, wsp_ref, bsp_ref, mog_ref,
                       cat_ref, ssm_ref, ext_scr, s_scr):
    c = pl.program_id(1)
    r = CHUNK

    @pl.when(c == 0)
    def _():
        ext_scr[0:8, :] = jnp.zeros((8, CONV_DIM), F32)
        s_scr[...] = jnp.zeros_like(s_scr)

    x = xbc_ref[...]
    ext_scr[8:8 + r, :] = x
    cw = cw_ref[...]
    conv = (cb_ref[...] + cw[3:4] * x + cw[2:3] * ext_scr[7:7 + r, :]
            + cw[1:2] * ext_scr[6:6 + r, :] + cw[0:1] * ext_scr[5:5 + r, :])
    ext_scr[0:8, :] = x[r - 8:r, :]

    xs, bm, cm, dt_x, acum, acum_x, tot_x = _mixer_front(
        conv, dt_ref[...], dtb_ref[...], alog_ref[...], alogx_ref[...], rexp_ref[...], tril_ref[...], None)
    bmb, cmb = bm.astype(BF16), cm.astype(BF16)
    xdt = xs * dt_x
    row = lax.broadcasted_iota(I32, (r, r), 0)
    col = lax.broadcasted_iota(I32, (r, r), 1)
    y_diag = _ssd_intra(cmb, bmb, acum, xdt, row >= col)

    s_prev = s_scr[...]
    s_prev_b = s_prev.astype(BF16)
    y_off = jnp.concatenate(
        [_dot_nt(cmb[:, LANES * g:LANES * (g + 1)], s_prev_b[256 * g:256 * (g + 1), :]) for g in range(SSD_GROUPS)],
        axis=1)
    y = y_diag + y_off * jnp.exp(acum_x) + dskip_ref[...] * xs

    w_t = (xdt * jnp.exp(tot_x - acum_x)).T.astype(BF16)
    states = jnp.concatenate(
        [_dot(w_t[256 * g:256 * (g + 1), :], bmb[:, LANES * g:LANES * (g + 1)]) for g in range(SSD_GROUPS)], axis=0)
    s_new = s_prev * jnp.exp(tot_x).T + states
    s_scr[...] = s_new

    cat, _ = _mixer_back(y, z_ref[...], u_ref[...], v_ref[...], sng_ref[...], vng_ref[...], vnb_ref[...],
                         wsp_ref, bsp_ref[...], mog_ref[...])
    cat_ref[...] = cat

    @pl.when(c == pl.num_programs(1) - 1)
    def _():
        ssm_ref[0] = s_new


def _prompt_mixer_call(z, xbc, u, v, dtr, params, nb, nc):
    row = lambda b, c: (b * nc + c, 0)
    in_specs = [
        pl.BlockSpec((CHUNK, SSD_WIDTH), row), pl.BlockSpec((CHUNK, CONV_DIM), row),
        pl.BlockSpec((CHUNK, MLP_WIDTH), row), pl.BlockSpec((CHUNK, MLP_WIDTH), row),
        pl.BlockSpec((CHUNK, DT_PAD), row),
    ] + [_const_spec(s) for s in _MIXER_PARAM_SHAPES]
    return pl.pallas_call(
        _prompt_mixer_body,
        out_shape=[jax.ShapeDtypeStruct((nb * nc * CHUNK, D_MODEL), BF16),
                   jax.ShapeDtypeStruct((nb, SSD_WIDTH, D_STATE), F32)],
        grid=(nb, nc),
        in_specs=in_specs,
        out_specs=[pl.BlockSpec((CHUNK, D_MODEL), row),
                   pl.BlockSpec((1, SSD_WIDTH, D_STATE), lambda b, c: (b, 0, 0))],
        scratch_shapes=[pltpu.VMEM((CHUNK + 8, CONV_DIM), F32), pltpu.VMEM((SSD_WIDTH, D_STATE), F32)],
        compiler_params=_cparams(("arbitrary", "arbitrary")),
        name="prompt_mixer",
    )(z, xbc, u, v, dtr, *params)


def _sample_mixer_body(seq_len, z_ref, x0_ref, x1_ref, x2_ref, x3_ref, u_ref, v_ref, dt_ref, h_ref,
                       cw_ref, cb_ref, dtb_ref, alog_ref, alogx_ref, rexp_ref, tril_ref, dskip_ref,
                       sng_ref, vng_ref, vnb_ref, wsp_ref, bsp_ref, mog_ref, segones_ref,
                       cat_ref, vout_ref, hout_ref, cm_scr, bm_scr, wt_scr, dtt_scr, yoff_scr):
    r = CHUNK
    shift = seq_len.bit_length() - 1
    cw = cw_ref[...]
    conv = (cb_ref[...] + cw[3:4] * x0_ref[...] + cw[2:3] * x1_ref[...]
            + cw[1:2] * x2_ref[...] + cw[0:1] * x3_ref[...])
    xs, bm, cm, dt_x, acum, acum_x, tot_x = _mixer_front(
        conv, dt_ref[...], dtb_ref[...], alog_ref[...], alogx_ref[...], rexp_ref[...], tril_ref[...],
        segones_ref[...])
    bmb, cmb = bm.astype(BF16), cm.astype(BF16)
    xdt = xs * dt_x
    row = lax.broadcasted_iota(I32, (r, r), 0)
    col = lax.broadcasted_iota(I32, (r, r), 1)
    same = lax.shift_right_logical(row, shift) == lax.shift_right_logical(col, shift)
    y_diag = _ssd_intra(cmb, bmb, acum, xdt, same & (row >= col))

    cm_scr[...] = cm
    bm_scr[...] = bmb
    wt_scr[...] = (xdt * jnp.exp(tot_x - acum_x)).T
    dtt_scr[...] = jnp.exp(tot_x).T
    ones_b = jnp.ones((LANES, LANES), BF16)
    seqs_per_slab = 8 // seq_len

    def slab(j, carry):
        rows = pl.ds(pl.multiple_of(8 * j, 8), 8)
        cms = cm_scr[rows, :].astype(BF16)
        sub = lax.broadcasted_iota(I32, (8, 256), 0)
        lane = lax.broadcasted_iota(I32, (256, LANES), 1)
        for g in range(SSD_GROUPS):
            q_rows = slice(256 * g, 256 * (g + 1))
            acc = jnp.zeros((8, 256), F32)
            for q in range(seqs_per_slab):
                s = seqs_per_slab * j + q
                y_s = _dot_nt(cms[:, LANES * g:LANES * (g + 1)], h_ref[s, q_rows, :].astype(BF16))
                acc = jnp.where(lax.shift_right_logical(sub, shift) == q, y_s, acc)
            yoff_scr[rows, 256 * g:256 * (g + 1)] = acc
            for q in range(seqs_per_slab):
                s = seqs_per_slab * j + q
                w_sel = jnp.where(lax.shift_right_logical(lane, shift) == s, wt_scr[q_rows, :], 0.0).astype(BF16)
                st = _dot(w_sel, bm_scr[:, LANES * g:LANES * (g + 1)])
                d_sel = jnp.where(lane == s * seq_len, dtt_scr[q_rows, :], 0.0)
                hout_ref[s, q_rows, :] = h_ref[s, q_rows, :] * _sel_right(d_sel, ones_b) + st
        return carry

    lax.fori_loop(0, r // 8, slab, 0, unroll=4)

    y = y_diag + yoff_scr[...] * jnp.exp(acum_x) + dskip_ref[...] * xs
    cat, v_ln = _mixer_back(y, z_ref[...], u_ref[...], v_ref[...], sng_ref[...], vng_ref[...], vnb_ref[...],
                            wsp_ref, bsp_ref[...], mog_ref[...])
    cat_ref[...] = cat
    vout_ref[...] = v_ln


def _sample_mixer_call(z, x_shift, u, v, dtr, h0, params, seg_ones, row0, seq_len):
    ts = x_shift[0].shape[0]
    n = ts // CHUNK
    spt = CHUNK // seq_len
    off = lambda i: (row0 + i, 0)
    loc = lambda i: (i, 0)
    st3 = lambda i: (i, 0, 0)
    in_specs = (
        [pl.BlockSpec((CHUNK, SSD_WIDTH), off)]
        + [pl.BlockSpec((CHUNK, CONV_DIM), loc)] * 4
        + [pl.BlockSpec((CHUNK, MLP_WIDTH), off), pl.BlockSpec((CHUNK, MLP_WIDTH), off),
           pl.BlockSpec((CHUNK, DT_PAD), off), pl.BlockSpec((spt, SSD_WIDTH, D_STATE), st3)]
        + [_const_spec(s) for s in _MIXER_PARAM_SHAPES] + [_const_spec((CHUNK, CHUNK))])
    return pl.pallas_call(
        functools.partial(_sample_mixer_body, seq_len),
        out_shape=[jax.ShapeDtypeStruct((ts, D_MODEL), BF16), jax.ShapeDtypeStruct((ts, MLP_WIDTH), F32),
                   jax.ShapeDtypeStruct(h0.shape, F32)],
        grid=(n,),
        in_specs=in_specs,
        out_specs=[pl.BlockSpec((CHUNK, D_MODEL), loc), pl.BlockSpec((CHUNK, MLP_WIDTH), loc),
                   pl.BlockSpec((spt, SSD_WIDTH, D_STATE), st3)],
        scratch_shapes=[pltpu.VMEM((CHUNK, 256), F32), pltpu.VMEM((CHUNK, 256), BF16),
                        pltpu.VMEM((SSD_WIDTH, CHUNK), F32), pltpu.VMEM((SSD_WIDTH, CHUNK), F32),
                        pltpu.VMEM((CHUNK, SSD_WIDTH), F32)],
        compiler_params=_cparams(("arbitrary",)),
        name="sample_mixer",
    )(z, *x_shift, u, v, dtr, h0, *params, seg_ones)


def _out_router_call(cat_p, cat_s, xp, xs, w_out, g_moe, wr_hi, wr_lo, b_r, tm):
    tp, ts = xp.shape[0], xs.shape[0]
    n_p, n_s = tp // tm, ts // tm
    t_all = tp + ts

    def body(cp_ref, cs_ref, xp_ref, xs_ref, wo_ref, g_ref, wh_ref, wl_ref, br_ref,
             h1_ref, m_ref, eid_ref, gate_ref):
        def run(c_ref, x_ref):
            h1 = x_ref[...] + _dot(c_ref[...], wo_ref[...])
            h1_ref[...] = h1
            m = _rms(h1) * g_ref[...]
            for j in range(TOKEN_TILE_ROWS):
                m_ref[pl.ds(j, tm, stride=TOKEN_TILE_ROWS), :] = m[:, LANES * j:LANES * (j + 1)]
            m_hi = m.astype(BF16)
            m_lo = (m - m_hi.astype(F32)).astype(BF16)
            logits = _dot(m_hi, wh_ref[...]) + _dot(m_lo, wh_ref[...]) + _dot(m_hi, wl_ref[...]) + br_ref[...]
            lane = lax.broadcasted_iota(I32, logits.shape, 1).astype(F32)
            work = logits
            vals, ids = [], []
            for _ in range(TOP_K):
                mx = jnp.max(work, axis=-1, keepdims=True)
                idx = jnp.min(jnp.where(work == mx, lane, float(LANES)), axis=-1, keepdims=True)
                vals.append(mx)
                ids.append(idx)
                work = jnp.where(lane == idx, -jnp.inf, work)
            ex = [jnp.exp(vv - vals[0]) for vv in vals]
            den = ex[0] + ex[1] + ex[2] + ex[3]
            eid = jnp.zeros(logits.shape, I32)
            gate = jnp.zeros(logits.shape, F32)
            for k in range(TOP_K):
                eid = jnp.where(lane == k, ids[k].astype(I32), eid)
                gate = jnp.where(lane == k, ex[k] / den, gate)
            eid_ref[...] = eid
            gate_ref[...] = gate

        i = pl.program_id(0)

        @pl.when(i < n_p)
        def _():
            run(cp_ref, xp_ref)

        @pl.when(i >= n_p)
        def _():
            run(cs_ref, xs_ref)

    pmap = lambda i: (jnp.minimum(i, n_p - 1), 0)
    smap = lambda i: (jnp.maximum(i - n_p, 0), 0)
    omap = lambda i: (i, 0)
    return pl.pallas_call(
        body,
        out_shape=[jax.ShapeDtypeStruct((t_all, D_MODEL), F32),
                   jax.ShapeDtypeStruct((t_all * TOKEN_TILE_ROWS, LANES), F32),
                   jax.ShapeDtypeStruct((t_all, LANES), I32), jax.ShapeDtypeStruct((t_all, LANES), F32)],
        grid=(n_p + n_s,),
        in_specs=[pl.BlockSpec((tm, D_MODEL), pmap), pl.BlockSpec((tm, D_MODEL), smap),
                  pl.BlockSpec((tm, D_MODEL), pmap), pl.BlockSpec((tm, D_MODEL), smap),
                  _const_spec((D_MODEL, D_MODEL)), _const_spec((1, D_MODEL)),
                  _const_spec((D_MODEL, LANES)), _const_spec((D_MODEL, LANES)), _const_spec((1, LANES))],
        out_specs=[pl.BlockSpec((tm, D_MODEL), omap), pl.BlockSpec((tm * TOKEN_TILE_ROWS, LANES), omap),
                   pl.BlockSpec((tm, LANES), omap), pl.BlockSpec((tm, LANES), omap)],
        compiler_params=_cparams(("arbitrary",)),
        name="out_router",
    )(cat_p, cat_s, xp, xs, w_out, g_moe, wr_hi, wr_lo, b_r)


def _route(eid, tm, nb):
    t = eid.shape[0]
    tk = t * TOP_K
    flat = eid.reshape(tk)
    _, order = lax.sort((flat, jnp.arange(tk, dtype=I32)), num_keys=1, is_stable=True)
    counts = jnp.sum((flat[:, None] == jnp.arange(N_EXPERTS, dtype=I32)[None, :]).astype(I32), axis=0)
    nblk = (counts + tm - 1) // tm
    bend = jnp.cumsum(nblk)
    bstart = bend - nblk
    start = jnp.cumsum(counts) - counts
    nused = bend[-1]
    blk = jnp.arange(nb, dtype=I32)
    used = blk < nused
    be = jnp.minimum(jnp.sum((jnp.minimum(blk, nused - 1)[:, None] >= bend[None, :]).astype(I32), axis=1),
                     N_EXPERTS - 1)
    sel = (be[:, None] == jnp.arange(N_EXPERTS, dtype=I32)[None, :]).astype(I32)
    pick = lambda v: jnp.sum(sel * v[None, :], axis=1)
    done = (blk - pick(bstart)) * tm
    nval = jnp.where(used, jnp.clip(pick(counts) - done, 0, tm), 0).astype(I32)
    off = jnp.where(used, pick(start) + done, 0).astype(I32)
    pad = (-(-(tk + tm) // LANES) + _id_rows(tm)) * LANES - tk
    tok = jnp.pad(lax.shift_right_logical(order, TOPK_SHIFT) * TOKEN_TILE_ROWS, (0, pad))
    dst = jnp.pad(((order & (TOP_K - 1)) * t + lax.shift_right_logical(order, TOPK_SHIFT)) * TOKEN_TILE_ROWS,
                  (0, pad))
    return be, nval, off, tok, dst


def _id_rows(tm):
    return tm // LANES + 1


def _moe_body(tm, t_all, nb, be_ref, nval_ref, off_ref, tok_hbm, dst_hbm, m_hbm, wup_ref, bup_ref, wdn_ref, bdn_ref,
              perm_ref, y_hbm, gids, sids, xbuf, ybuf, wup_b, wdn_b, isem, gsem, ssem):
    i = pl.program_id(0)
    nv = nval_ref[i]
    slot = i & 1
    prv = jnp.maximum(i - 1, 0)
    nxt = jnp.minimum(i + 1, nb - 1)
    nx2 = jnp.minimum(i + 2, nb - 1)
    has_next = (i + 1 < nb) & (nval_ref[nxt] > 0)
    has_next2 = (i + 2 < nb) & (nval_ref[nx2] > 0)
    n_prev = jnp.where(i > 0, nval_ref[prv], 0)
    win = _id_rows(tm) * LANES
    spare = TOP_K * t_all * TOKEN_TILE_ROWS

    def ids_copies(b):
        start = pl.multiple_of(lax.shift_right_logical(off_ref[b], 7) * LANES, LANES)
        ring = pl.ds(pl.multiple_of((b & 3) * win, LANES), win)
        return (pltpu.make_async_copy(tok_hbm.at[pl.ds(start, win)], gids.at[ring], isem.at[b & 3, 0]),
                pltpu.make_async_copy(dst_hbm.at[pl.ds(start, win)], sids.at[ring], isem.at[b & 3, 1]))

    def id_base(b):
        return (b & 3) * win + (off_ref[b] & (LANES - 1))

    def gather_row(base, s, r):
        src = pl.ds(pl.multiple_of(gids[base + r], TOKEN_TILE_ROWS), TOKEN_TILE_ROWS)
        return pltpu.make_async_copy(m_hbm.at[src], xbuf.at[s, pl.ds(TOKEN_TILE_ROWS * r, TOKEN_TILE_ROWS)],
                                     gsem.at[s])

    def scatter_row(base, s, r, n):
        dest = jnp.where(r < n, sids[base + r], spare + TOKEN_TILE_ROWS * r)
        rows = pl.ds(pl.multiple_of(dest, TOKEN_TILE_ROWS), TOKEN_TILE_ROWS)
        return pltpu.make_async_copy(ybuf.at[s, pl.ds(TOKEN_TILE_ROWS * r, TOKEN_TILE_ROWS)], y_hbm.at[rows],
                                     ssem.at[s])

    def wait_gathers(s):
        pltpu.make_async_copy(m_hbm.at[pl.ds(0, tm * TOKEN_TILE_ROWS)], xbuf.at[s], gsem.at[s]).wait()

    def wait_scatters(s):
        pltpu.make_async_copy(ybuf.at[s], y_hbm.at[pl.ds(0, tm * TOKEN_TILE_ROWS)], ssem.at[s]).wait()

    def for_rows(fn):
        def one(r, c):
            fn(r)
            return c

        lax.fori_loop(0, tm, one, 0)

    @pl.when(nv > 0)
    def _():
        @pl.when(i == 0)
        def _():
            ybuf[...] = jnp.zeros_like(ybuf)
            fill = pltpu.make_async_copy(ybuf.at[0], y_hbm.at[pl.ds(spare, tm * TOKEN_TILE_ROWS)], ssem.at[0])
            fill.start()
            fill.wait()
            for b in range(4):
                for cp in ids_copies(b):
                    cp.start()
                    cp.wait()
            base0 = id_base(0)
            for_rows(lambda r: gather_row(base0, 0, r).start())

        @pl.when(has_next & (i >= 3))
        def _():
            for cp in ids_copies(nxt):
                cp.wait()

        @pl.when(has_next2 & (i >= 2))
        def _():
            for cp in ids_copies(nx2):
                cp.start()

        @pl.when((i == 0) | (be_ref[i] != be_ref[prv]))
        def _():
            for jb in range(2 * D_FF // 256):
                cols = slice(256 * jb, 256 * (jb + 1))
                wup_b[:, cols] = _dot(wup_ref[0, :, cols].astype(BF16), perm_ref[...]).astype(BF16)
            wdn_b[...] = wdn_ref[0].astype(BF16)

        y_cur = lax.rem(i, 3)
        y_prev = lax.rem(i + 2, 3)
        y_prev2 = lax.rem(i + 1, 3)

        def ffn_step(cur, oth):
            wait_gathers(cur)

            @pl.when(i >= 2)
            def _():
                wait_scatters(y_cur)

            g_base = id_base(nxt)
            s_base = id_base(prv)
            for r in range(tm):
                gather_row(g_base, oth, r).start()
                scatter_row(s_base, y_prev, r, n_prev).start()

            x = jnp.concatenate([xbuf[slot, pl.ds(j, tm, stride=TOKEN_TILE_ROWS), :]
                                 for j in range(TOKEN_TILE_ROWS)], axis=1).astype(BF16)
            acts = []
            for jb in range(D_FF // LANES):
                h = _dot(x, wup_b[:, 256 * jb:256 * (jb + 1)]) + bup_ref[0, :, 256 * jb:256 * (jb + 1)]
                gate = jnp.minimum(h[:, :LANES], SWIGLU_LIMIT)
                lin = jnp.clip(h[:, LANES:], -SWIGLU_LIMIT, SWIGLU_LIMIT)
                acts.append((gate * jax.nn.sigmoid(SWIGLU_ALPHA * gate) * (lin + 1.0)).astype(BF16))
            act = jnp.concatenate(acts, axis=1)
            for c in range(D_MODEL // 256):
                yc = _dot(act, wdn_b[:, 256 * c:256 * (c + 1)]) + bdn_ref[0, :, 256 * c:256 * (c + 1)]
                for half in range(2):
                    ybuf[y_cur, pl.ds(2 * c + half, tm, stride=TOKEN_TILE_ROWS), :] = (
                        yc[:, LANES * half:LANES * (half + 1)])

            @pl.when(jnp.logical_not(has_next))
            def _():
                wait_gathers(oth)

                @pl.when(i >= 1)
                def _():
                    wait_scatters(y_prev2)
                wait_scatters(y_prev)
                last_base = id_base(i)
                for_rows(lambda r: scatter_row(last_base, y_cur, r, nv).start())
                wait_scatters(y_cur)

        for parity in range(2):
            pl.when(slot == parity)(functools.partial(ffn_step, parity, 1 - parity))


def _moe_call(m, be, nval, off, tok, dst, w_up, b_up_g, w_down, b_down, perm, tm, nb):
    t = m.shape[0] // TOKEN_TILE_ROWS
    assert nb >= 4
    by_expert = lambda i, be, nv, off: (be[i], 0, 0)
    grid_spec = pltpu.PrefetchScalarGridSpec(
        num_scalar_prefetch=3,
        grid=(nb,),
        in_specs=[
            pl.BlockSpec(memory_space=pl.ANY),
            pl.BlockSpec(memory_space=pl.ANY),
            pl.BlockSpec(memory_space=pl.ANY),
            pl.BlockSpec((1, D_MODEL, 2 * D_FF), by_expert),
            pl.BlockSpec((1, 1, 2 * D_FF), by_expert),
            pl.BlockSpec((1, D_FF, D_MODEL), by_expert),
            pl.BlockSpec((1, 1, D_MODEL), by_expert),
            pl.BlockSpec((256, 256), lambda i, be, nv, off: (0, 0)),
        ],
        out_specs=pl.BlockSpec(memory_space=pl.ANY),
        scratch_shapes=[pltpu.SMEM((4 * _id_rows(tm) * LANES,), I32), pltpu.SMEM((4 * _id_rows(tm) * LANES,), I32),
                        pltpu.VMEM((2, tm * TOKEN_TILE_ROWS, LANES), F32),
                        pltpu.VMEM((3, tm * TOKEN_TILE_ROWS, LANES), F32),
                        pltpu.VMEM((D_MODEL, 2 * D_FF), BF16), pltpu.VMEM((D_FF, D_MODEL), BF16),
                        pltpu.SemaphoreType.DMA((4, 2)), pltpu.SemaphoreType.DMA((2,)),
                        pltpu.SemaphoreType.DMA((3,))],
    )
    return pl.pallas_call(
        functools.partial(_moe_body, tm, t, nb),
        out_shape=jax.ShapeDtypeStruct(((TOP_K * t + tm) * TOKEN_TILE_ROWS, LANES), F32),
        grid_spec=grid_spec,
        compiler_params=_cparams(("arbitrary",)),
        name="moe_experts",
    )(be, nval, off, tok, dst, m, w_up, b_up_g, w_down, b_down, perm)


def _ple_call(h1, y4, gates, pp, ps, g_ple, w_gate, w_proj, g_final, tm):
    tp, ts = pp.shape[0], ps.shape[0]
    n_p, n_s = tp // tm, ts // tm
    ple = pp.shape[1]

    def body(h1_ref, y0_ref, y1_ref, y2_ref, y3_ref, gt_ref, pp_ref, ps_ref, g_ref, wg_ref, wp_ref, gf_ref,
             yp_ref, ys_ref):
        def run(p_ref, o_ref):
            gt = gt_ref[...]
            moe = None
            for k, y_ref in enumerate((y0_ref, y1_ref, y2_ref, y3_ref)):
                y_k = jnp.concatenate([y_ref[pl.ds(j, tm, stride=TOKEN_TILE_ROWS), :]
                                       for j in range(TOKEN_TILE_ROWS)], axis=1)
                moe = gt[:, k:k + 1] * y_k if moe is None else moe + gt[:, k:k + 1] * y_k
            h2 = h1_ref[...] + moe
            a = (_rms(h2) * g_ref[...]).astype(BF16)
            gate = jax.nn.sigmoid(_dot(a, wg_ref[...]))
            pe = _dot(p_ref[...].astype(BF16), wp_ref[...])
            h3 = h2 + pe * gate
            o_ref[...] = _rms(h3) * gf_ref[...]

        i = pl.program_id(0)

        @pl.when(i < n_p)
        def _():
            run(pp_ref, yp_ref)

        @pl.when(i >= n_p)
        def _():
            run(ps_ref, ys_ref)

    pmap = lambda i: (jnp.minimum(i, n_p - 1), 0)
    smap = lambda i: (jnp.maximum(i - n_p, 0), 0)
    omap = lambda i: (i, 0)
    return pl.pallas_call(
        body,
        out_shape=[jax.ShapeDtypeStruct((tp, D_MODEL), F32), jax.ShapeDtypeStruct((ts, D_MODEL), F32)],
        grid=(n_p + n_s,),
        in_specs=[pl.BlockSpec((tm, D_MODEL), omap)]
                 + [pl.BlockSpec((tm * TOKEN_TILE_ROWS, LANES),
                                 functools.partial(lambda k, i: (k * (n_p + n_s) + i, 0), k)) for k in range(TOP_K)]
                 + [pl.BlockSpec((tm, LANES), omap), pl.BlockSpec((tm, ple), pmap), pl.BlockSpec((tm, ple), smap),
                  _const_spec((1, D_MODEL)), _const_spec((D_MODEL, D_MODEL)), _const_spec((ple, D_MODEL)),
                  _const_spec((1, D_MODEL))],
        out_specs=[pl.BlockSpec((tm, D_MODEL), pmap), pl.BlockSpec((tm, D_MODEL), smap)],
        compiler_params=_cparams(("arbitrary",)),
        name="ple_final",
    )(h1, y4, y4, y4, y4, gates, pp, ps, g_ple, w_gate, w_proj, g_final)


def _row(x, width=None):
    x = x.reshape(1, -1).astype(F32)
    if width is not None and x.shape[1] < width:
        x = jnp.pad(x, ((0, 0), (0, width - x.shape[1])))
    return x


def _mixer_params(conv_w, conv_b, dt_bias, a_log, d_skip, ssd_norm_g, v_norm_g, v_norm_b, w_spatial, b_spatial,
                  mlp_out_g, seq_len):
    pos = jnp.arange(CHUNK) % seq_len
    same = (jnp.arange(CHUNK)[:, None] // seq_len) == (jnp.arange(CHUNK)[None, :] // seq_len)
    tril = (same & (jnp.arange(CHUNK)[:, None] >= jnp.arange(CHUNK)[None, :])).astype(BF16)
    rexp = (jnp.arange(LANES)[:, None] == (jnp.arange(SSD_WIDTH)[None, :] // HEAD_DIM)).astype(BF16)
    w_loc = jnp.tril(w_spatial[:, :seq_len, :seq_len])
    onehot = (pos[:, None] == jnp.arange(seq_len)[None, :]).astype(F32)
    tiled = jnp.einsum("iq,hqr,jr->hij", onehot, w_loc.astype(F32), onehot, precision=lax.Precision.HIGHEST)
    w_bd = jnp.where(same[None], tiled, 0.0)
    wsp = (w_bd.reshape(MLP_HEADS // 2, 2, CHUNK, CHUNK).transpose(0, 2, 1, 3)
           .reshape(MLP_HEADS // 2, CHUNK, 2 * CHUNK).astype(BF16))
    bsp = jnp.repeat(b_spatial[:, :seq_len].T[pos], MLP_WIDTH // MLP_HEADS, axis=1)
    params = (
        conv_w.astype(F32), _row(conv_b), _row(dt_bias, LANES), _row(a_log, LANES),
        _row(jnp.repeat(a_log, HEAD_DIM)), rexp, tril, _row(jnp.repeat(d_skip, HEAD_DIM)),
        _row(ssd_norm_g), _row(v_norm_g), _row(v_norm_b), wsp, bsp.astype(F32), _row(mlp_out_g),
    )
    return params, same.astype(BF16)


def _tile_rows(n):
    return 512 if n % 512 == 0 else CHUNK


def kernel(x_prompt, x_sample, state_ssm, state_conv, p_prompt, p_sample, norm_mix_g, w_in, conv_w, conv_b, dt_bias, a_log, d_skip, ssd_norm_g, v_norm_g, v_norm_b, w_spatial, b_spatial, mlp_out_g, w_out, norm_moe_g, w_router, b_router, w_up, b_up, w_down, b_down, norm_ple_g, w_ple_gate, w_ple_proj, norm_final_g):
    depth = norm_mix_g.shape[0]
    bp, lp, d = x_prompt.shape
    bs, ls, _ = x_sample.shape
    tp, ts = bp * lp, bs * ls
    assert depth == 1 and d == D_MODEL and lp % CHUNK == 0 and ts % CHUNK == 0 and 8 % ls == 0
    tm = _tile_rows(tp) if ts % _tile_rows(tp) == 0 else CHUNK
    t_all = tp + ts
    tm_moe = MOE_BLOCK_ROWS
    nb_moe = -(-t_all * TOP_K // tm_moe) + N_EXPERTS

    hp = x_prompt.reshape(tp, d)
    hs = x_sample.reshape(ts, d)
    ssm_p, conv_p, ssm_s, conv_s, v_s = [], [], [], [], []
    o1 = SSD_WIDTH
    o2 = o1 + CONV_DIM
    o3 = o2 + SSD_HEADS
    c = jnp.arange(256)
    src = jnp.where(c < LANES, 2 * c, 2 * (c - LANES) + 1)
    perm = (jnp.arange(256)[:, None] == src[None, :]).astype(BF16)

    for i in range(depth):
        wi = w_in[i]
        w_cat = jnp.concatenate(
            [wi[:, :o2], wi[:, o3:], jnp.pad(wi[:, o2:o3], ((0, 0), (0, DT_PAD - SSD_HEADS)))], axis=1).astype(BF16)
        z, xbc, u, v, dtr = _inproj_call(hp, hs, _row(norm_mix_g[i]), w_cat, tm)

        mix_args = (conv_w[i], conv_b[i], dt_bias[i], a_log[i], d_skip[i], ssd_norm_g[i], v_norm_g[i], v_norm_b[i],
                    w_spatial[i], b_spatial[i], mlp_out_g[i])
        prm_p, _ = _mixer_params(*mix_args, seq_len=CHUNK)
        cat_p, s_p = _prompt_mixer_call(z, xbc, u, v, dtr, prm_p, bp, lp // CHUNK)
        ssm_p.append(s_p.reshape(bp, SSD_HEADS, HEAD_DIM, D_STATE).astype(state_ssm.dtype))
        conv_p.append(jnp.stack([xbc[(b + 1) * lp - (CONV_W - 1):(b + 1) * lp] for b in range(bp)]))

        prm_s, seg_ones = _mixer_params(*mix_args, seq_len=ls)
        xbc_s = xbc[tp:].reshape(bs, ls, CONV_DIM)
        xpad = jnp.concatenate([state_conv[i].astype(F32), xbc_s], axis=1)
        x_shift = [xpad[:, CONV_W - 1 - k:CONV_W - 1 - k + ls].reshape(ts, CONV_DIM) for k in range(CONV_W)]
        h0 = state_ssm[i].astype(F32).reshape(bs, SSD_WIDTH, D_STATE)
        cat_s, v_rows, s_s = _sample_mixer_call(z, x_shift, u, v, dtr, h0, prm_s, seg_ones, tp // CHUNK, ls)
        ssm_s.append(s_s.reshape(bs, SSD_HEADS, HEAD_DIM, D_STATE).astype(state_ssm.dtype))
        conv_s.append(xpad[:, ls:])
        v_s.append(v_rows.reshape(bs, ls, MLP_WIDTH))

        wr = jnp.pad(w_router[i].astype(F32), ((0, 0), (0, LANES - N_EXPERTS)))
        wr_hi = wr.astype(BF16)
        wr_lo = (wr - wr_hi.astype(F32)).astype(BF16)
        b_r = jnp.concatenate([b_router[i].astype(F32), jnp.full((LANES - N_EXPERTS,), -1e30, F32)]).reshape(1, LANES)
        h1, m, eid, gates = _out_router_call(cat_p, cat_s, hp, hs, w_out[i].astype(BF16), _row(norm_moe_g[i]),
                                             wr_hi, wr_lo, b_r, tm)

        be, nval, off, tok, dst = _route(eid[:, :TOP_K], tm_moe, nb_moe)
        b_up_g = (b_up[i].astype(F32).reshape(N_EXPERTS, 2 * D_FF // 256, LANES, 2).transpose(0, 1, 3, 2)
                  .reshape(N_EXPERTS, 1, 2 * D_FF))
        y4 = _moe_call(m, be, nval, off, tok, dst, w_up[i], b_up_g, w_down[i],
                       b_down[i].reshape(N_EXPERTS, 1, D_MODEL), perm, tm_moe, nb_moe)

        hp, hs = _ple_call(h1, y4, gates,
                           p_prompt[i].reshape(tp, -1), p_sample[i].reshape(ts, -1), _row(norm_ple_g[i]),
                           w_ple_gate[i].astype(BF16), w_ple_proj[i].astype(BF16), _row(norm_final_g), tm)

    y_prompt = hp.reshape(bp, lp, d)
    y_sample = hs.reshape(bs, ls, d)
    return (y_prompt, y_sample, jnp.stack(ssm_p), jnp.stack(conv_p), jnp.stack(ssm_s), jnp.stack(conv_s),
            jnp.stack(v_s))
```
